```python
import math
import functools
import jax
import jax.numpy as jnp
from jax import lax
import numpy as np

D_MODEL = 1024
BATCH = 4
SEQ = 4096
DEPTH = 1
DEC_BATCH = 32
DEC_SEQ = 1
PAST_LEN = 8192
PAGE_SIZE = 128

H_A = 8
DH_A = 64
W_A = H_A * DH_A
H_IDX = 8
DH_IDX = 64
TOPK_MAX = 256
Q_BLOCK = 128
NUM_BUCKETS = 32
MAX_DISTANCE = 128
H_R = 4
DK_R = 128
DV_R = 128
W_R = H_R * DV_R
RET_CHUNK = 128
ROPE_BASE = 10000.0
MIX_WIDTH = W_A + W_R
D_FF = 2816
ALPHA = (2 * DEPTH) ** 0.25
BETA = (8 * DEPTH) ** -0.25
LN_EPS = 1e-5
GN_EPS = 1e-5
IN_SIZES = (W_A, W_A, W_A, H_IDX * DH_IDX, DH_IDX, H_IDX, H_R * DK_R, H_R * DK_R, W_R, W_R)
N_IN = 3 * W_A + H_IDX * DH_IDX + DH_IDX + H_IDX + 2 * H_R * DK_R + 2 * W_R

kernel_name = "hymba_dsa_retnet_macaron_deepnorm_step"


def _split_points():
    pts, acc = [], 0
    for s in IN_SIZES[:-1]:
        acc += s
        pts.append(acc)
    return pts


def layer_norm(x, g, b):
    xf = x.astype(jnp.float32)
    mu = xf.mean(-1, keepdims=True)
    var = jnp.square(xf - mu).mean(-1, keepdims=True)
    return ((xf - mu) * lax.rsqrt(var + LN_EPS)).astype(x.dtype) * g + b


def swiglu(x, wg, wu, wd):
    return (jax.nn.silu(x @ wg) * (x @ wu)) @ wd


def rotary(x, pos):
    half = x.shape[-1] // 2
    freqs = ROPE_BASE ** (-jnp.arange(half, dtype=jnp.float32) / half)
    ang = pos.astype(jnp.float32)[:, None] * freqs[None, :]
    cos = jnp.cos(ang)[None, :, None, :]
    sin = jnp.sin(ang)[None, :, None, :]
    xf = x.astype(jnp.float32)
    x1, x2 = xf[..., :half], xf[..., half:]
    return jnp.concatenate([x1 * cos - x2 * sin, x1 * sin + x2 * cos], axis=-1)


def t5_bucket(rel):
    n = jnp.maximum(rel, 0)
    max_exact = NUM_BUCKETS // 2
    nf = jnp.maximum(n, 1).astype(jnp.float32)
    large = max_exact + (jnp.log(nf / max_exact) / math.log(MAX_DISTANCE / max_exact)
                         * (NUM_BUCKETS - max_exact)).astype(jnp.int32)
    large = jnp.minimum(large, NUM_BUCKETS - 1)
    return jnp.where(n < max_exact, n, large)


def take_rows(arr, idx):
    return jax.vmap(lambda a, i: a[i])(arr, idx)


def indexer_scores(qi, ki, wi, qpos, kpos):
    dots = jnp.einsum('bqhd,bld->bqhl', qi, ki).astype(jnp.float32) * DH_IDX ** -0.5
    s = jnp.einsum('bqh,bqhl->bql', wi.astype(jnp.float32) * H_IDX ** -0.5, jax.nn.relu(dots))
    return jnp.where(kpos[None, None, :] <= qpos[None, :, None], s, -jnp.inf)


def sparse_attend(q, qpos, scores, topk, gather_kv, rel_bias):
    _, idx = lax.top_k(scores, topk)
    valid = idx <= qpos[None, :, None]
    k_sel, v_sel = gather_kv(idx)
    logits = jnp.einsum('bqhd,bqkhd->bhqk', q, k_sel).astype(jnp.float32) * DH_A ** -0.5
    bias = rel_bias[t5_bucket(qpos[None, :, None] - idx)].astype(jnp.float32)
    logits = logits + bias.transpose(0, 3, 1, 2)
    logits = jnp.where(valid[:, None], logits, -jnp.inf)
    p = jax.nn.softmax(logits, axis=-1)
    return jnp.einsum('bhqk,bqkhd->bqhd', p.astype(v_sel.dtype), v_sel).astype(q.dtype)


def prompt_attention(q, k, v, qi, ki, wi, rel_bias):
    B, S = q.shape[:2]
    topk = min(TOPK_MAX, S // 4)
    kpos = jnp.arange(S, dtype=jnp.int32)

    def gather_kv(idx):
        return take_rows(k, idx), take_rows(v, idx)

    def block(start):
        qb = lax.dynamic_slice_in_dim(q, start, Q_BLOCK, axis=1)
        qib = lax.dynamic_slice_in_dim(qi, start, Q_BLOCK, axis=1)
        wib = lax.dynamic_slice_in_dim(wi, start, Q_BLOCK, axis=1)
        qpos = start + jnp.arange(Q_BLOCK, dtype=jnp.int32)
        scores = indexer_scores(qib, ki, wib, qpos, kpos)
        return sparse_attend(qb, qpos, scores, topk, gather_kv, rel_bias)

    out = lax.map(block, jnp.arange(0, S, Q_BLOCK, dtype=jnp.int32))
    return out.transpose(1, 0, 2, 3, 4).reshape(B, S, H_A, DH_A)


def sample_attention(q, k_new, v_new, qi, ki_new, wi, rel_bias, cache_k, cache_v, cache_kidx, page_table, layer):
    DB, DS = q.shape[:2]
    n_pages = page_table.shape[1]
    past = n_pages * PAGE_SIZE
    L = past + DS
    topk = min(TOPK_MAX, L // 4)
    qpos = past + jnp.arange(DS, dtype=jnp.int32)
    kpos = jnp.arange(L, dtype=jnp.int32)
    ki_past = cache_kidx[layer, page_table].reshape(DB, past, DH_IDX)
    ki_all = jnp.concatenate([ki_past.astype(ki_new.dtype), ki_new], axis=1)
    scores = indexer_scores(qi, ki_all, wi, qpos, kpos)

    def gather_kv(idx):
        is_new = (idx >= past)[..., None, None]
        page = take_rows(page_table, jnp.minimum(idx // PAGE_SIZE, n_pages - 1))
        off = idx % PAGE_SIZE
        new_i = jnp.clip(idx - past, 0, DS - 1)
        k_sel = jnp.where(is_new, take_rows(k_new, new_i), cache_k[layer, page, off].astype(k_new.dtype))
        v_sel = jnp.where(is_new, take_rows(v_new, new_i), cache_v[layer, page, off].astype(v_new.dtype))
        return k_sel, v_sel

    return sparse_attend(q, qpos, scores, topk, gather_kv, rel_bias)


def retention(q, k, v, s0):
    B, S = q.shape[:2]
    C = RET_CHUNK if S % RET_CHUNK == 0 else S
    nc = S // C
    lg = jnp.log1p(-jnp.exp2(-5.0 - jnp.arange(H_R, dtype=jnp.float32)))
    i = jnp.arange(C, dtype=jnp.float32)
    diff = i[:, None] - i[None, :]
    causal = diff >= 0
    intra = jnp.where(causal[None], jnp.exp(jnp.where(causal, diff, 0.0)[None] * lg[:, None, None]), 0.0)
    cross_decay = jnp.exp((i[:, None] + 1.0) * lg[None, :])
    kv_decay = jnp.exp((C - 1.0 - i)[:, None] * lg[None, :])
    chunk_decay = jnp.exp(C * lg)

    def to_chunks(a):
        return a.reshape(B, nc, C, H_R, a.shape[-1]).transpose(1, 0, 2, 3, 4)

    def step(s, inp):
        qc, kc, vc = inp
        att = jnp.einsum('bihd,bjhd->bhij', qc, kc) * intra[None]
        o = (jnp.einsum('bhij,bjhv->bihv', att, vc)
             + jnp.einsum('bihd,bhdv->bihv', qc, s) * cross_decay[None, :, :, None])
        s_new = chunk_decay[None, :, None, None] * s + jnp.einsum('bjhd,jh,bjhv->bhdv', kc, kv_decay, vc)
        return s_new, o

    s_fin, o = lax.scan(step, s0, (to_chunks(q), to_chunks(k), to_chunks(v)))
    return o.transpose(1, 0, 2, 3, 4).reshape(B, S, H_R, DV_R), s_fin


def decoder_layer(x, pos, attend, s0, ffn1_wg, ffn1_wu, ffn1_wd, ln1_g, ln1_b, w_in, ret_gn_g, w_out,
                  ln2_g, ln2_b, ffn2_wg, ffn2_wu, ffn2_wd, ln3_g, ln3_b):
    B, S, _ = x.shape
    h = layer_norm(ALPHA * x + 0.5 * swiglu(x, ffn1_wg, ffn1_wu, ffn1_wd), ln1_g, ln1_b)
    q_a, k_a, v_a, q_i, k_i, w_i, q_r, k_r, v_r, g_r = jnp.split(h @ w_in, _split_points(), axis=-1)
    q_a = q_a.reshape(B, S, H_A, DH_A)
    k_a = k_a.reshape(B, S, H_A, DH_A)
    v_a = v_a.reshape(B, S, H_A, DH_A)
    q_i = q_i.reshape(B, S, H_IDX, DH_IDX)
    o_a = attend(q_a, k_a, v_a, q_i, k_i, w_i)
    qr = rotary(q_r.reshape(B, S, H_R, DK_R), pos)
    kr = rotary(k_r.reshape(B, S, H_R, DK_R), pos) * DK_R ** -0.5
    vr = v_r.reshape(B, S, H_R, DV_R).astype(jnp.float32)
    o_r, s_new = retention(qr, kr, vr, s0.astype(jnp.float32))
    mu = o_r.mean(-1, keepdims=True)
    var = jnp.square(o_r - mu).mean(-1, keepdims=True)
    y_r = ((o_r - mu) * lax.rsqrt(var + GN_EPS)).reshape(B, S, W_R).astype(x.dtype) * ret_gn_g
    y_r = jax.nn.silu(g_r) * y_r
    mix = jnp.concatenate([o_a.reshape(B, S, W_A), y_r], axis=-1) @ w_out
    h = layer_norm(ALPHA * h + mix, ln2_g, ln2_b)
    h = layer_norm(ALPHA * h + 0.5 * swiglu(h, ffn2_wg, ffn2_wu, ffn2_wd), ln3_g, ln3_b)
    return h, k_a, v_a, k_i, s_new


def setup_inputs(seed: int = 0) -> dict:
    key = jax.random.key(seed)
    ks = jax.random.split(key, 32)
    f32 = jnp.float32
    nrm = lambda k, shp, sc: jax.random.normal(k, shp, f32) * sc
    n_pages = PAST_LEN // PAGE_SIZE
    n_used = DEC_BATCH * n_pages
    n_pool = n_used + n_used // 4
    perm = jax.random.permutation(ks[0], n_pool)
    page_table = perm[:n_used].reshape(DEC_BATCH, n_pages).astype(jnp.int32)

    w_in = nrm(ks[8], (DEPTH, D_MODEL, N_IN), D_MODEL ** -0.5)
    w_in = w_in.at[..., 2 * W_A:3 * W_A].multiply(BETA).at[..., N_IN - 2 * W_R:N_IN - W_R].multiply(BETA)

    return {
        "x_prompt": nrm(ks[1], (BATCH, SEQ, D_MODEL), 1.0),
        "x_sample": nrm(ks[2], (DEC_BATCH, DEC_SEQ, D_MODEL), 1.0),
        "cache_k": nrm(ks[3], (DEPTH, n_pool, PAGE_SIZE, H_A, DH_A), 1.0),
        "cache_v": nrm(ks[4], (DEPTH, n_pool, PAGE_SIZE, H_A, DH_A), BETA),
        "cache_kidx": nrm(ks[5], (DEPTH, n_pool, PAGE_SIZE, DH_IDX), 1.0),
        "state_ret": nrm(ks[6], (DEPTH, DEC_BATCH, H_R, DK_R, DV_R), 0.5),
        "page_table": page_table,
        "rel_bias": nrm(ks[7], (NUM_BUCKETS, H_A), 0.5),
        "ffn1_wg": nrm(ks[9], (DEPTH, D_MODEL, D_FF), D_MODEL ** -0.5),
        "ffn1_wu": nrm(ks[10], (DEPTH, D_MODEL, D_FF), D_MODEL ** -0.5),
        "ffn1_wd": nrm(ks[11], (DEPTH, D_FF, D_MODEL), BETA * D_FF ** -0.5),
        "ln1_g": 1.0 + nrm(ks[12], (DEPTH, D_MODEL), 0.05),
        "ln1_b": nrm(ks[13], (DEPTH, D_MODEL), 0.02),
        "w_in": w_in,
        "ret_gn_g": 1.0 + nrm(ks[14], (DEPTH, W_R), 0.05),
        "w_out": nrm(ks[15], (DEPTH, MIX_WIDTH, D_MODEL), BETA * MIX_WIDTH ** -0.5),
        "ln2_g": 1.0 + nrm(ks[16], (DEPTH, D_MODEL), 0.05),
        "ln2_b": nrm(ks[17], (DEPTH, D_MODEL), 0.02),
        "ffn2_wg": nrm(ks[18], (DEPTH, D_MODEL, D_FF), D_MODEL ** -0.5),
        "ffn2_wu": nrm(ks[19], (DEPTH, D_MODEL, D_FF), D_MODEL ** -0.5),
        "ffn2_wd": nrm(ks[20], (DEPTH, D_FF, D_MODEL), BETA * D_FF ** -0.5),
        "ln3_g": 1.0 + nrm(ks[21], (DEPTH, D_MODEL), 0.05),
        "ln3_b": nrm(ks[22], (DEPTH, D_MODEL), 0.02),
    }


def reference(x_prompt, x_sample, cache_k, cache_v, cache_kidx, state_ret, page_table, rel_bias,
              ffn1_wg, ffn1_wu, ffn1_wd, ln1_g, ln1_b, w_in, ret_gn_g, w_out,
              ln2_g, ln2_b, ffn2_wg, ffn2_wu, ffn2_wd, ln3_g, ln3_b):
    B, S, _ = x_prompt.shape
    DB, DS, _ = x_sample.shape
    past = page_table.shape[1] * PAGE_SIZE
    pos_p = jnp.arange(S, dtype=jnp.int32)
    pos_s = past + jnp.arange(DS, dtype=jnp.int32)
    s0_p = jnp.zeros((B, H_R, DK_R, DV_R), jnp.float32)
    attend_p = functools.partial(prompt_attention, rel_bias=rel_bias)

    hp, hs = x_prompt, x_sample
    kp_l, vp_l, kip_l, sp_l = [], [], [], []
    ks_l, vs_l, kis_l, ss_l = [], [], [], []
    for l in range(DEPTH):
        lw = (ffn1_wg[l], ffn1_wu[l], ffn1_wd[l], ln1_g[l], ln1_b[l], w_in[l], ret_gn_g[l], w_out[l],
              ln2_g[l], ln2_b[l], ffn2_wg[l], ffn2_wu[l], ffn2_wd[l], ln3_g[l], ln3_b[l])
        attend_s = functools.partial(sample_attention, rel_bias=rel_bias, cache_k=cache_k, cache_v=cache_v,
                                     cache_kidx=cache_kidx, page_table=page_table, layer=l)
        hp, kp, vp, kip, sp = decoder_layer(hp, pos_p, attend_p, s0_p, *lw)
        hs, ksn, vsn, kisn, ssn = decoder_layer(hs, pos_s, attend_s, state_ret[l], *lw)
        kp_l.append(kp); vp_l.append(vp); kip_l.append(kip); sp_l.append(sp)
        ks_l.append(ksn); vs_l.append(vsn); kis_l.append(kisn); ss_l.append(ssn)

    k_prompt = jnp.stack(kp_l)
    v_prompt = jnp.stack(vp_l)
    kidx_prompt = jnp.stack(kip_l)
    ret_state_prompt = jnp.stack(sp_l)
    k_sample = jnp.stack(ks_l)
    v_sample = jnp.stack(vs_l)
    kidx_sample = jnp.stack(kis_l)
    ret_state_sample = jnp.stack(ss_l)
    return (hp, hs, k_prompt, v_prompt, kidx_prompt, ret_state_prompt,
            k_sample, v_sample, kidx_sample, ret_state_sample)
```

```python
import functools
import math

import numpy as np
import jax
import jax.numpy as jnp
from jax import lax
from jax.experimental import pallas as pl
from jax.experimental.pallas import tpu as pltpu

D_MODEL = 1024
D_FF = 2816
PAGE_SIZE = 128
H_A = 8
DH_A = 64
W_A = H_A * DH_A
H_IDX = 8
DH_IDX = 64
TOPK_MAX = 256
NUM_BUCKETS = 32
MAX_DISTANCE = 128
H_R = 4
DK_R = 128
DV_R = 128
W_R = H_R * DV_R
ROPE_BASE = 10000.0
LN_EPS = 1e-5
GN_EPS = 1e-5
IN_SIZES = (W_A, W_A, W_A, H_IDX * DH_IDX, DH_IDX, H_IDX, H_R * DK_R, H_R * DK_R, W_R, W_R)

LANES = 128
MIB = 1024 * 1024
NEG = -1e30

FF_CHUNK = 512
PRE_TM = 256
POST_TM = 512
ATT_TQ = 256
RET_CHUNK = 256
BISECT_ITERS = 16
SAMPLE_PAGE_GROUP = 16

C_QA, C_KA, C_VA, C_QI, C_KI2, C_WI, C_QR, C_KR, C_VR, C_GR, C_END = (
    0, 512, 1024, 1536, 2048, 2176, 2304, 2816, 3328, 3840, 4352)

f32 = jnp.float32
bf16 = jnp.bfloat16
NT_DIMS = (((1,), (1,)), ((), ()))


def _const_spec(shape):
    nd = len(shape)
    return pl.BlockSpec(shape, lambda *_: (0,) * nd, pipeline_mode=pl.Buffered(1))


def _ln(x, g, b):
    mu = jnp.mean(x, axis=-1, keepdims=True)
    xc = x - mu
    var = jnp.mean(xc * xc, axis=-1, keepdims=True)
    return xc * lax.rsqrt(var + LN_EPS) * g + b


def _ffn(xb, wg_ref, wu_ref, wd_ref):
    acc = None
    for c0 in range(0, D_FF, FF_CHUNK):
        c1 = min(c0 + FF_CHUNK, D_FF)
        g = jnp.dot(xb, wg_ref[:, c0:c1], preferred_element_type=f32)
        u = jnp.dot(xb, wu_ref[:, c0:c1], preferred_element_type=f32)
        a = (jax.nn.silu(g) * u).astype(bf16)
        part = jnp.dot(a, wd_ref[c0:c1, :], preferred_element_type=f32)
        acc = part if acc is None else acc + part
    return acc


def _pre_kernel(alpha, x_ref, wg_ref, wu_ref, wd_ref, lng_ref, lnb_ref, win_ref, rc_ref, rs_ref,
                h_ref, qa_ref, ka_ref, va_ref, kab_ref, vab_ref, qi_ref, ki_ref, ki2_ref, wi_ref,
                qr_ref, kr_ref, vr_ref, gr_ref):
    act = qa_ref.dtype
    x = x_ref[...]
    f = _ffn(x.astype(bf16), wg_ref, wu_ref, wd_ref)
    h = _ln(alpha * x + 0.5 * f, lng_ref[...], lnb_ref[...])
    h_ref[...] = h
    hb = h.astype(bf16)

    def proj(c0, c1):
        return jnp.dot(hb, win_ref[:, c0:c1], preferred_element_type=f32)

    qa_ref[...] = (proj(C_QA, C_KA) * DH_A ** -0.5).astype(act)
    ka = proj(C_KA, C_VA)
    ka_ref[...] = ka
    kab_ref[...] = ka.astype(act)
    va = proj(C_VA, C_QI)
    va_ref[...] = va
    vab_ref[...] = va.astype(act)
    qi_ref[...] = (proj(C_QI, C_KI2) * DH_IDX ** -0.5).astype(act)
    kk = proj(C_KI2, C_WI)
    ki_ref[...] = kk[:, :DH_IDX]
    ki2_ref[...] = kk.astype(act)
    wi_ref[...] = proj(C_WI, C_QR)
    qr = proj(C_QR, C_KR)
    kr = proj(C_KR, C_VR)
    c = rc_ref[...]
    s = rs_ref[...]
    for hh in range(H_R):
        sl = slice(DK_R * hh, DK_R * (hh + 1))
        qh = qr[:, sl]
        kh = kr[:, sl]
        qr_ref[:, sl] = (qh * c + pltpu.roll(qh, DK_R // 2, 1) * s).astype(act)
        kr_ref[:, sl] = ((kh * c + pltpu.roll(kh, DK_R // 2, 1) * s) * DK_R ** -0.5).astype(act)
    vr_ref[...] = proj(C_VR, C_GR).astype(act)
    gr_ref[...] = proj(C_GR, C_END)


def _pre_call(x, w, rot_c, rot_s, alpha, tm, act, name):
    n = x.shape[0]
    grid = (pl.cdiv(n, tm),)
    row = lambda width: pl.BlockSpec((tm, width), lambda i: (i, 0))
    rot_blocks = rot_c.shape[0] // tm
    rot = pl.BlockSpec((tm, LANES), lambda i: (i % rot_blocks, 0))
    in_specs = [
        row(D_MODEL),
        _const_spec((D_MODEL, D_FF)), _const_spec((D_MODEL, D_FF)), _const_spec((D_FF, D_MODEL)),
        _const_spec((1, D_MODEL)), _const_spec((1, D_MODEL)),
        _const_spec((D_MODEL, C_END)),
        rot, rot,
    ]
    outs = [
        ("h", D_MODEL, f32), ("qa", W_A, act), ("ka", W_A, f32), ("va", W_A, f32), ("kab", W_A, act),
        ("vab", W_A, act), ("qi", W_A, act), ("ki", DH_IDX, f32), ("ki2", LANES, act), ("wi", LANES, f32),
        ("qr", W_R, act), ("kr", W_R, act), ("vr", W_R, act), ("gr", W_R, f32),
    ]
    out_shape = [jax.ShapeDtypeStruct((n, wd), dt) for _, wd, dt in outs]
    out_specs = [row(wd) for _, wd, _ in outs]
    res = pl.pallas_call(
        functools.partial(_pre_kernel, alpha),
        grid=grid, in_specs=in_specs, out_specs=out_specs, out_shape=out_shape,
        compiler_params=pltpu.CompilerParams(dimension_semantics=("arbitrary",), vmem_limit_bytes=52 * MIB),
        name=name,
    )(x, w["ffn1_wg"], w["ffn1_wu"], w["ffn1_wd"], w["ln1_g"], w["ln1_b"], w["w_in"], rot_c, rot_s)
    return {k: v for (k, _, _), v in zip(outs, res)}


def _post_kernel(alpha, h_ref, oa_ref, yr_ref, wo_ref, l2g_ref, l2b_ref, wg_ref, wu_ref, wd_ref,
                 l3g_ref, l3b_ref, out_ref):
    h = h_ref[...]
    mix = (jnp.dot(oa_ref[...].astype(bf16), wo_ref[0:W_A, :], preferred_element_type=f32)
           + jnp.dot(yr_ref[...].astype(bf16), wo_ref[W_A:W_A + W_R, :], preferred_element_type=f32))
    h2 = _ln(alpha * h + mix, l2g_ref[...], l2b_ref[...])
    f = _ffn(h2.astype(bf16), wg_ref, wu_ref, wd_ref)
    out_ref[...] = _ln(alpha * h2 + 0.5 * f, l3g_ref[...], l3b_ref[...])


def _post_call(h, oa, yr, w, alpha, tm, name):
    n = h.shape[0]
    row = lambda width: pl.BlockSpec((tm, width), lambda i: (i, 0))
    in_specs = [
        row(D_MODEL), row(W_A), row(W_R),
        _const_spec((W_A + W_R, D_MODEL)), _const_spec((1, D_MODEL)), _const_spec((1, D_MODEL)),
        _const_spec((D_MODEL, D_FF)), _const_spec((D_MODEL, D_FF)), _const_spec((D_FF, D_MODEL)),
        _const_spec((1, D_MODEL)), _const_spec((1, D_MODEL)),
    ]
    return pl.pallas_call(
        functools.partial(_post_kernel, alpha),
        grid=(pl.cdiv(n, tm),), in_specs=in_specs, out_specs=row(D_MODEL),
        out_shape=jax.ShapeDtypeStruct((n, D_MODEL), f32),
        compiler_params=pltpu.CompilerParams(dimension_semantics=("arbitrary",), vmem_limit_bytes=48 * MIB),
        name=name,
    )(h, oa, yr, w["w_out"], w["ln2_g"], w["ln2_b"], w["ffn2_wg"], w["ffn2_wu"], w["ffn2_wd"],
      w["ln3_g"], w["ln3_b"])


def _attn_kernel(topk, qa_ref, qi_ref, wi_ref, k_ref, v_ref, ki2_ref, bias_ref, oa_ref,
                 sc_ref, slog_ref, qap_ref, qip_ref, wb_ref, mrun_ref, lrun_ref, acc_ref, osel_ref):
    tq = qa_ref.shape[0]
    ncb = tq // LANES
    i = pl.program_id(1)
    nj = i + 1
    kf = float(topk)

    lane = lax.broadcasted_iota(jnp.int32, (tq, LANES), 1)
    lo_half = lane < DH_A
    for h in range(H_A):
        p = h // 2
        keep = lo_half if h % 2 == 0 else jnp.logical_not(lo_half)
        blk = slice(LANES * p, LANES * (p + 1))
        qip_ref[h] = jnp.where(keep, qi_ref[:, blk], jnp.zeros((), qi_ref.dtype))
        qap_ref[h] = jnp.where(keep, qa_ref[:, blk], jnp.zeros((), qa_ref.dtype))
        wb_ref[h] = jnp.broadcast_to(wi_ref[:, h:h + 1] * H_IDX ** -0.5, (tq, LANES))

    def score_chunk(j, carry):
        off = pl.multiple_of(j * tq, tq)
        kj = ki2_ref[pl.ds(off, tq), :]
        accs = [jnp.zeros((tq, LANES), f32) for _ in range(ncb)]
        for h in range(H_IDX):
            d = lax.dot_general(qip_ref[h], kj, NT_DIMS, preferred_element_type=f32)
            wb = wb_ref[h]
            for cb in range(ncb):
                accs[cb] = accs[cb] + wb * jnp.maximum(d[:, cb * LANES:(cb + 1) * LANES], 0.0)
        for cb in range(ncb):
            sc_ref[j, :, cb * LANES:(cb + 1) * LANES] = accs[cb]
        return carry

    lax.fori_loop(0, nj, score_chunk, 0)

    row = lax.broadcasted_iota(jnp.int32, (tq, tq), 0)
    col = lax.broadcasted_iota(jnp.int32, (tq, tq), 1)
    causal = col <= row
    sd = sc_ref[i]
    mn_diag = jnp.min(jnp.where(causal, sd, jnp.inf), axis=1, keepdims=True)
    sc_ref[i] = jnp.where(causal, sd, -jnp.inf)

    def fold(fn, init, comb, n_chunks):
        def body(j, acc):
            v = fn(sc_ref[j])
            for cb in range(ncb):
                acc = comb(acc, v[:, cb * LANES:(cb + 1) * LANES])
            return acc
        return lax.fori_loop(0, n_chunks, body, jnp.full((tq, LANES), init, f32))

    def tile_lanes(x):
        return jnp.concatenate([x] * ncb, axis=1) if ncb > 1 else x

    def lane_sum(x):
        return jnp.broadcast_to(jnp.sum(x, axis=1, keepdims=True), (tq, LANES))

    def lane_max(x):
        return jnp.broadcast_to(jnp.max(x, axis=1, keepdims=True), (tq, LANES))

    def count_ge(t):
        tt = tile_lanes(t)
        return lane_sum(fold(lambda s: jnp.where(s >= tt, 1.0, 0.0), 0.0, jnp.add, nj))

    def count_gt(t):
        tt = tile_lanes(t)
        return lane_sum(fold(lambda s: jnp.where(s > tt, 1.0, 0.0), 0.0, jnp.add, nj))

    def max_below(t):
        tt = tile_lanes(t)
        return lane_max(fold(lambda s: jnp.where(s < tt, s, -jnp.inf), -jnp.inf, jnp.maximum, nj))

    n_keys = (lax.broadcasted_iota(jnp.int32, (tq, LANES), 0) + (i * tq + 1)).astype(f32)
    take_all = n_keys <= kf

    mx = lane_max(fold(lambda s: s, -jnp.inf, jnp.maximum, nj))
    mn_far = -lane_max(-fold(lambda s: s, jnp.inf, jnp.minimum, i))
    mn = jnp.minimum(mn_far, jnp.broadcast_to(mn_diag, (tq, LANES)))

    @pl.when((i + 1) * tq <= topk)
    def _():
        def mk(j, carry):
            sc_ref[j] = jnp.zeros((tq, tq), f32)
            return carry
        lax.fori_loop(0, nj, mk, 0)

    @pl.when((i + 1) * tq > topk)
    def _():
        c_max = count_ge(mx)
        done0 = jnp.logical_or(take_all, c_max >= kf)
        thr0 = jnp.where(take_all, -jnp.inf, mx)
        cge0 = jnp.where(take_all, n_keys, c_max)

        def bis(_, st):
            lo, hi = st
            mid = 0.5 * (lo + hi)
            ge = count_ge(mid) >= kf
            return jnp.where(ge, mid, lo), jnp.where(ge, hi, mid)

        lo, hi = lax.fori_loop(0, BISECT_ITERS, bis, (mn, mx))

        def snap_cond(st):
            return st[0] > 0.0

        def snap_body(st):
            _, hi, thr, cge, done = st
            m = max_below(hi)
            c = count_ge(m)
            ok = c >= kf
            newly = jnp.logical_and(ok, done < 0.5)
            thr = jnp.where(newly, m, thr)
            cge = jnp.where(newly, c, cge)
            hi = jnp.where(jnp.logical_or(done > 0.5, ok), hi, m)
            done = jnp.where(ok, 1.0, done)
            left = jnp.sum(1.0 - done[:, 0:1])
            return left, hi, thr, cge, done

        done_f = jnp.where(done0, 1.0, 0.0)
        left0 = jnp.sum(1.0 - done_f[:, 0:1])
        _, _, thr, cge, _ = lax.while_loop(snap_cond, snap_body, (left0, hi, thr0, cge0, done_f))

        excess = jnp.sum(jnp.where(cge[:, 0:1] > kf, 1.0, 0.0)) > 0.0
        thr_t = tile_lanes(thr)

        @pl.when(jnp.logical_not(excess))
        def _():
            def mk(j, carry):
                sc_ref[j] = jnp.where(sc_ref[j] >= thr_t, 0.0, NEG)
                return carry
            lax.fori_loop(0, nj, mk, 0)

        @pl.when(excess)
        def _():
            need = tile_lanes(kf - count_gt(thr))
            upper = jnp.where(row <= col, 1.0, 0.0).astype(bf16)

            def mk(j, seen):
                s = sc_ref[j]
                tie = s == thr_t
                tie_f = jnp.where(tie, 1.0, 0.0)
                rank = seen + jnp.dot(tie_f.astype(bf16), upper, preferred_element_type=f32)
                sel = jnp.logical_or(s > thr_t, jnp.logical_and(tie, rank <= need))
                sc_ref[j] = jnp.where(sel, 0.0, NEG)
                part = tie_f[:, 0:LANES]
                for cb in range(1, ncb):
                    part = part + tie_f[:, cb * LANES:(cb + 1) * LANES]
                return seen + tile_lanes(lane_sum(part))
            lax.fori_loop(0, nj, mk, jnp.zeros((tq, tq), f32))

    sc_ref[i] = jnp.where(causal, sc_ref[i], NEG)


    for h in range(H_A):
        p = h // 2
        blk = slice(LANES * p, LANES * (p + 1))
        qh = qap_ref[h]

        def logits(j, bias):
            off = pl.multiple_of(j * tq, tq)
            s = lax.dot_general(qh, k_ref[pl.ds(off, tq), blk], NT_DIMS, preferred_element_type=f32)
            s = s + sc_ref[j]
            if bias is not None:
                s = s + bias
            slog_ref[j] = s
            m = mrun_ref[...]
            for cb in range(ncb):
                m = jnp.maximum(m, s[:, cb * LANES:(cb + 1) * LANES])
            mrun_ref[...] = m

        mrun_ref[...] = jnp.full((tq, LANES), -jnp.inf, f32)

        def far(j, carry):
            logits(j, None)
            return carry
        lax.fori_loop(0, i - 1, far, 0)

        @pl.when(i >= 1)
        def _():
            logits(i - 1, bias_ref[h, :, 0:tq])
        logits(i, bias_ref[h, :, tq:2 * tq])

        m_row = tile_lanes(lane_max(mrun_ref[...]))
        lrun_ref[...] = jnp.zeros((tq, LANES), f32)
        acc_ref[...] = jnp.zeros((tq, LANES), f32)

        def pv(j, carry):
            off = pl.multiple_of(j * tq, tq)
            e = jnp.exp(slog_ref[j] - m_row)
            l = lrun_ref[...]
            for cb in range(ncb):
                l = l + e[:, cb * LANES:(cb + 1) * LANES]
            lrun_ref[...] = l
            acc_ref[...] += jnp.dot(e.astype(bf16), v_ref[pl.ds(off, tq), blk], preferred_element_type=f32)
            return carry
        lax.fori_loop(0, nj, pv, 0)

        o = acc_ref[...] / lane_sum(lrun_ref[...])
        if h % 2 == 0:
            osel_ref[...] = o
        else:
            oa_ref[:, blk] = jnp.where(lo_half, osel_ref[...], o).astype(oa_ref.dtype)


def _attn_call(qa, qi, wi, kab, vab, ki2, bias_tiles, topk, tq):
    b, s, _ = qa.shape
    nq = s // tq
    qspec = lambda width: pl.BlockSpec((None, tq, width), lambda bi, i: (bi, i, 0))
    kvspec = lambda width: pl.BlockSpec((None, s, width), lambda bi, i: (bi, 0, 0))
    scratch = [
        pltpu.VMEM((nq, tq, tq), f32),
        pltpu.VMEM((nq, tq, tq), f32),
        pltpu.VMEM((H_A, tq, LANES), qa.dtype),
        pltpu.VMEM((H_IDX, tq, LANES), qi.dtype),
        pltpu.VMEM((H_IDX, tq, LANES), f32),
        pltpu.VMEM((tq, LANES), f32),
        pltpu.VMEM((tq, LANES), f32),
        pltpu.VMEM((tq, LANES), f32),
        pltpu.VMEM((tq, LANES), f32),
    ]
    return pl.pallas_call(
        functools.partial(_attn_kernel, topk),
        grid=(b, nq),
        in_specs=[qspec(W_A), qspec(W_A), qspec(LANES), kvspec(W_A), kvspec(W_A), kvspec(LANES),
                  _const_spec((H_A, tq, 2 * tq))],
        out_specs=qspec(W_A),
        out_shape=jax.ShapeDtypeStruct((b, s, W_A), bf16),
        scratch_shapes=scratch,
        compiler_params=pltpu.CompilerParams(dimension_semantics=("arbitrary", "arbitrary"),
                                             vmem_limit_bytes=48 * MIB),
        name="attn_prompt",
    )(qa, qi, wi, kab, vab, ki2, bias_tiles)


def _ret_kernel(qr_ref, kr_ref, vr_ref, gr_ref, gng_ref, dmat_ref, cross_ref, kvd_ref, cdec_ref,
                yr_ref, st_ref):
    c = pl.program_id(1)

    @pl.when(c == 0)
    def _():
        st_ref[...] = jnp.zeros(st_ref.shape, f32)

    for h in range(H_R):
        sl = slice(DK_R * h, DK_R * (h + 1))
        q = qr_ref[:, sl]
        k = kr_ref[:, sl]
        v = vr_ref[:, sl]
        att = lax.dot_general(q, k, NT_DIMS, preferred_element_type=f32) * dmat_ref[h]
        st = st_ref[h]
        o = (jnp.dot(att.astype(bf16), v, preferred_element_type=f32)
             + jnp.dot(q, st.astype(bf16), preferred_element_type=f32) * cross_ref[h])
        kd = (k.astype(f32) * kvd_ref[h]).T.astype(bf16)
        st_ref[h] = cdec_ref[h, 0:1, :] * st + jnp.dot(kd, v, preferred_element_type=f32)
        mu = jnp.mean(o, axis=-1, keepdims=True)
        oc = o - mu
        var = jnp.mean(oc * oc, axis=-1, keepdims=True)
        yn = oc * lax.rsqrt(var + GN_EPS) * gng_ref[:, sl]
        yr_ref[:, sl] = (jax.nn.silu(gr_ref[:, sl]) * yn).astype(yr_ref.dtype)


def _ret_call(qr, kr, vr, gr, gng, dec, chunk):
    b, s, _ = qr.shape
    nc = s // chunk
    rspec = pl.BlockSpec((None, chunk, W_R), lambda bi, c: (bi, c, 0))
    return pl.pallas_call(
        _ret_kernel,
        grid=(b, nc),
        in_specs=[rspec, rspec, rspec, rspec, _const_spec((1, W_R)),
                  _const_spec((H_R, chunk, chunk)), _const_spec((H_R, chunk, LANES)),
                  _const_spec((H_R, chunk, LANES)), _const_spec((H_R, 8, LANES))],
        out_specs=[rspec, pl.BlockSpec((None, H_R, DK_R, DV_R), lambda bi, c: (bi, 0, 0, 0))],
        out_shape=[jax.ShapeDtypeStruct((b, s, W_R), bf16), jax.ShapeDtypeStruct((b, H_R, DK_R, DV_R), f32)],
        compiler_params=pltpu.CompilerParams(dimension_semantics=("arbitrary", "arbitrary"),
                                             vmem_limit_bytes=32 * MIB),
        name="ret_prompt",
    )(qr, kr, vr, gr, gng, dec["intra"], dec["cross"], dec["kv"], dec["chunk"])


def _ssel_kernel(layer, npages, topk, pt_ref, ck_ref, q_ref, w_ref, kn_ref, mask_ref, meta_ref,
                 kbuf, sem, sc_ref):
    b = pl.program_id(0)
    nb = pl.num_programs(0)
    slot = b % 2
    kf = float(topk)

    def page_copy(seq, p, sl):
        page = pt_ref[seq * npages + p]
        return pltpu.make_async_copy(ck_ref.at[layer, page], kbuf.at[sl, p], sem.at[sl])

    def start_all(seq, sl):
        def body(p, carry):
            page_copy(seq, p, sl).start()
            return carry
        lax.fori_loop(0, npages, body, 0)

    @pl.when(b == 0)
    def _():
        start_all(0, 0)

    @pl.when(b + 1 < nb)
    def _():
        start_all(b + 1, 1 - slot)

    def wait_body(p, carry):
        page_copy(b, p, slot).wait()
        return carry
    lax.fori_loop(0, npages, wait_body, 0)

    q = q_ref[...].astype(bf16)
    w = w_ref[...]

    def score_page(p, carry):
        kp = kbuf[slot, p].astype(bf16)
        d = jnp.dot(q, kp, preferred_element_type=f32)
        sc_ref[pl.ds(p, 1), :] = jnp.sum(w * jnp.maximum(d, 0.0), axis=0, keepdims=True)
        return carry
    lax.fori_loop(0, npages, score_page, 0)

    kn = kn_ref[...].astype(bf16).astype(f32)
    dn = jnp.sum(q.astype(f32) * kn, axis=1, keepdims=True)
    s_new = jnp.sum(w * jnp.maximum(dn, 0.0), axis=0, keepdims=True)
    sc = sc_ref[...]

    def tot(x):
        return jnp.sum(jnp.sum(x, axis=0, keepdims=True), axis=1, keepdims=True)

    def count(cmp, t):
        return tot(jnp.where(cmp(sc, t), 1.0, 0.0)) + jnp.where(cmp(s_new, t), 1.0, 0.0)

    ge = lambda a, t: a >= t
    gt = lambda a, t: a > t
    mx = jnp.maximum(jnp.max(jnp.max(sc, axis=0, keepdims=True), axis=1, keepdims=True), s_new)
    mn = jnp.minimum(jnp.min(jnp.min(sc, axis=0, keepdims=True), axis=1, keepdims=True), s_new)
    c_max = count(ge, mx)
    done0 = jnp.where(c_max >= kf, 1.0, 0.0)

    def bis(_, st):
        lo, hi = st
        mid = 0.5 * (lo + hi)
        ok = count(ge, mid) >= kf
        return jnp.where(ok, mid, lo), jnp.where(ok, hi, mid)
    lo, hi = lax.fori_loop(0, BISECT_ITERS, bis, (mn, mx))

    def snap_cond(st):
        return st[0] > 0.0

    def snap_body(st):
        _, hi, thr, done = st
        below = jnp.maximum(
            jnp.max(jnp.max(jnp.where(sc < hi, sc, -jnp.inf), axis=0, keepdims=True), axis=1, keepdims=True),
            jnp.where(s_new < hi, s_new, -jnp.inf))
        ok = count(ge, below) >= kf
        newly = jnp.logical_and(ok, done < 0.5)
        thr = jnp.where(newly, below, thr)
        hi = jnp.where(jnp.logical_or(done > 0.5, ok), hi, below)
        done = jnp.where(ok, 1.0, done)
        return jnp.sum(1.0 - done), hi, thr, done
    _, _, thr, _ = lax.while_loop(snap_cond, snap_body, (jnp.sum(1.0 - done0), hi, mx, done0))

    need = kf - count(gt, thr)
    r128 = lax.broadcasted_iota(jnp.int32, (LANES, LANES), 0)
    c128 = lax.broadcasted_iota(jnp.int32, (LANES, LANES), 1)
    upper = jnp.where(r128 <= c128, 1.0, 0.0).astype(bf16)
    rp = lax.broadcasted_iota(jnp.int32, (npages, npages), 0)
    cp = lax.broadcasted_iota(jnp.int32, (npages, npages), 1)
    before = jnp.where(cp < rp, 1.0, 0.0).astype(bf16)

    def prefix(flags):
        within = jnp.dot(flags.astype(bf16), upper, preferred_element_type=f32)
        rows = jnp.broadcast_to(within[:, LANES - 1:LANES], (npages, LANES)).astype(bf16)
        return jnp.dot(before, rows, preferred_element_type=f32) + within

    tie = sc == thr
    tie_f = jnp.where(tie, 1.0, 0.0)
    sel = jnp.logical_or(sc > thr, jnp.logical_and(tie, prefix(tie_f) <= need))
    sel_new = jnp.logical_or(s_new > thr, jnp.logical_and(s_new == thr, tot(tie_f) + 1.0 <= need))
    mask_ref[...] = jnp.where(sel, 0.0, NEG)
    meta_ref[...] = jnp.broadcast_to(jnp.where(sel_new, 1.0, 0.0), (1, LANES))


def _ssel_call(page_table_flat, cache_kidx_t, q16, w16, kn, layer, npages, topk):
    db = q16.shape[0]
    grid_spec = pltpu.PrefetchScalarGridSpec(
        num_scalar_prefetch=1,
        grid=(db,),
        in_specs=[
            pl.BlockSpec(memory_space=pl.ANY),
            pl.BlockSpec((None, 16, DH_IDX), lambda b, pt: (b, 0, 0)),
            pl.BlockSpec((None, 16, 1), lambda b, pt: (b, 0, 0)),
            pl.BlockSpec((None, 1, DH_IDX), lambda b, pt: (b, 0, 0)),
        ],
        out_specs=[pl.BlockSpec((None, npages, PAGE_SIZE), lambda b, pt: (b, 0, 0)),
                   pl.BlockSpec((None, 1, LANES), lambda b, pt: (b, 0, 0))],
        scratch_shapes=[
            pltpu.VMEM((2, npages, DH_IDX, PAGE_SIZE), f32),
            pltpu.SemaphoreType.DMA((2,)),
            pltpu.VMEM((npages, PAGE_SIZE), f32),
        ],
    )
    return pl.pallas_call(
        functools.partial(_ssel_kernel, layer, npages, topk),
        grid_spec=grid_spec,
        out_shape=[jax.ShapeDtypeStruct((db, npages, PAGE_SIZE), f32),
                   jax.ShapeDtypeStruct((db, 1, LANES), f32)],
        compiler_params=pltpu.CompilerParams(dimension_semantics=("arbitrary",), vmem_limit_bytes=32 * MIB),
        name="sample_select",
    )(page_table_flat, cache_kidx_t, q16, w16, kn)


def _satt_kernel(layer, npages, grp, pt_ref,
                 ckk_ref, ckv_ref, qt_ref, knt_ref, vnt_ref, bias_ref, bias0_ref, mask_ref, snew_ref,
                 qr_ref, kr_ref, vr_ref, gr_ref, gng_ref, gam_ref, s0_ref,
                 ot_ref, yr_ref, sn_ref,
                 kbuf, vbuf, sem, m_ref, l_ref, acc_ref):
    b = pl.program_id(0)
    g = pl.program_id(1)
    nb = pl.num_programs(0)
    ng = pl.num_programs(1)
    t = b * ng + g
    slot = t % 2

    def page_copies(seq, gi, p, sl):
        page = pt_ref[seq * npages + gi * grp + p]
        return (pltpu.make_async_copy(ckk_ref.at[layer, page], kbuf.at[sl, p], sem.at[sl, 0]),
                pltpu.make_async_copy(ckv_ref.at[layer, page], vbuf.at[sl, p], sem.at[sl, 1]))

    def start_all(seq, gi, sl):
        def body(p, carry):
            ck, cv = page_copies(seq, gi, p, sl)
            ck.start()
            cv.start()
            return carry
        lax.fori_loop(0, grp, body, 0)

    @pl.when(t == 0)
    def _():
        start_all(0, 0, 0)

    @pl.when(t + 1 < nb * ng)
    def _():
        wrap = g + 1 == ng
        start_all(jnp.where(wrap, b + 1, b), jnp.where(wrap, 0, g + 1), 1 - slot)

    @pl.when(g == 0)
    def _():
        m_ref[...] = jnp.full(m_ref.shape, -jnp.inf, f32)
        l_ref[...] = jnp.zeros(l_ref.shape, f32)
        acc_ref[...] = jnp.zeros(acc_ref.shape, f32)

    def wait_body(p, carry):
        ck, cv = page_copies(b, g, p, slot)
        ck.wait()
        cv.wait()
        return carry
    lax.fori_loop(0, grp, wait_body, 0)

    mask = mask_ref[...]
    for h in range(H_A):
        qc = qt_ref[:, h:h + 1]
        kh = kbuf[slot, :, h]
        lg = jnp.sum(kh * qc[None], axis=1, keepdims=True) + bias_ref[h] + mask
        m_old = m_ref[h]
        m_blk = jnp.max(jnp.max(lg, axis=0), axis=1, keepdims=True)
        m_new = jnp.maximum(m_old, m_blk)
        a = jnp.exp(m_old - m_new)
        e = jnp.exp(lg - m_new[None])
        l_ref[h] = l_ref[h] * a + jnp.sum(e, axis=0)
        acc_ref[h] = acc_ref[h] * a + jnp.sum(vbuf[slot, :, h] * e, axis=0)
        m_ref[h] = m_new

    @pl.when(g == ng - 1)
    def _():
        lg_new = jnp.sum(qt_ref[...] * knt_ref[...], axis=0, keepdims=True) + bias0_ref[...]
        lg_new = jnp.where(snew_ref[0:1, 0:1] > 0.5, lg_new, NEG)
        for h in range(H_A):
            m_c = m_ref[h][:, 0:1]
            lg_h = lg_new[:, h:h + 1]
            m_f = jnp.maximum(m_c, lg_h)
            a = jnp.exp(m_c - m_f)
            e_new = jnp.exp(lg_h - m_f)
            den = jnp.sum(l_ref[h], axis=1, keepdims=True) * a + e_new
            num = jnp.sum(acc_ref[h], axis=1, keepdims=True) * a + e_new * vnt_ref[:, h:h + 1]
            ot_ref[:, h:h + 1] = num / den

    @pl.when(g == 0)
    def _():
        r_i = lax.broadcasted_iota(jnp.int32, (DK_R, DK_R), 0)
        c_i = lax.broadcasted_iota(jnp.int32, (DK_R, DK_R), 1)
        eye = jnp.where(r_i == c_i, 1.0, 0.0)
        for h in range(H_R):
            qrow = qr_ref[h:h + 1, :]
            krow = kr_ref[h:h + 1, :]
            vrow = vr_ref[h:h + 1, :]
            gam = gam_ref[h:h + 1, :]
            qcol = jnp.sum(eye * qrow, axis=1, keepdims=True)
            kcol = jnp.sum(eye * krow, axis=1, keepdims=True)
            st = s0_ref[h]
            qk = jnp.sum(qrow * krow, axis=1, keepdims=True)
            o = qk * vrow + gam * jnp.sum(qcol * st, axis=0, keepdims=True)
            sn_ref[h] = gam * st + kcol * vrow
            mu = jnp.mean(o, axis=1, keepdims=True)
            oc = o - mu
            var = jnp.mean(oc * oc, axis=1, keepdims=True)
            yn = oc * lax.rsqrt(var + GN_EPS) * gng_ref[h:h + 1, :]
            yr_ref[h:h + 1, :] = jax.nn.silu(gr_ref[h:h + 1, :]) * yn


def _satt_call(pt_flat, cache_k_t, cache_v_t, qt, knt, vnt, bias_pos, bias0, mask, snew, qr, kr, vr, gr, gng, gam,
               state, layer, npages, grp):
    db = qt.shape[0]
    ng = npages // grp
    per_seq = lambda d0, d1: pl.BlockSpec((None, d0, d1), lambda b, g, pt: (b, 0, 0))
    grid_spec = pltpu.PrefetchScalarGridSpec(
        num_scalar_prefetch=1,
        grid=(db, ng),
        in_specs=[
            pl.BlockSpec(memory_space=pl.ANY), pl.BlockSpec(memory_space=pl.ANY),
            per_seq(DH_A, H_A), per_seq(DH_A, H_A), per_seq(DH_A, H_A),
            pl.BlockSpec((H_A, grp, 1, PAGE_SIZE), lambda b, g, pt: (0, g, 0, 0)),
            pl.BlockSpec((1, H_A), lambda b, g, pt: (0, 0)),
            pl.BlockSpec((None, grp, 1, PAGE_SIZE), lambda b, g, pt: (b, g, 0, 0)),
            per_seq(1, LANES),
            per_seq(H_R, DK_R), per_seq(H_R, DK_R), per_seq(H_R, DV_R), per_seq(H_R, DV_R),
            pl.BlockSpec((H_R, DV_R), lambda b, g, pt: (0, 0)),
            pl.BlockSpec((H_R, DV_R), lambda b, g, pt: (0, 0)),
            pl.BlockSpec((None, None, H_R, DK_R, DV_R), lambda b, g, pt: (layer, b, 0, 0, 0)),
        ],
        out_specs=[per_seq(DH_A, H_A), per_seq(H_R, DV_R),
                   pl.BlockSpec((None, H_R, DK_R, DV_R), lambda b, g, pt: (b, 0, 0, 0))],
        scratch_shapes=[
            pltpu.VMEM((2, grp, H_A, DH_A, PAGE_SIZE), f32),
            pltpu.VMEM((2, grp, H_A, DH_A, PAGE_SIZE), f32),
            pltpu.SemaphoreType.DMA((2, 2)),
            pltpu.VMEM((H_A, 1, PAGE_SIZE), f32),
            pltpu.VMEM((H_A, 1, PAGE_SIZE), f32),
            pltpu.VMEM((H_A, DH_A, PAGE_SIZE), f32),
        ],
    )
    return pl.pallas_call(
        functools.partial(_satt_kernel, layer, npages, grp),
        grid_spec=grid_spec,
        out_shape=[jax.ShapeDtypeStruct((db, DH_A, H_A), f32), jax.ShapeDtypeStruct((db, H_R, DV_R), f32),
                   jax.ShapeDtypeStruct((db, H_R, DK_R, DV_R), f32)],
        compiler_params=pltpu.CompilerParams(dimension_semantics=("arbitrary", "arbitrary"),
                                             vmem_limit_bytes=40 * MIB),
        name="sample_attend",
    )(pt_flat, cache_k_t, cache_v_t, qt, knt, vnt, bias_pos, bias0, mask, snew, qr, kr, vr, gr, gng, gam, state)


def _t5_bucket(rel):
    n = jnp.maximum(rel, 0)
    max_exact = NUM_BUCKETS // 2
    nf = jnp.maximum(n, 1).astype(f32)
    large = max_exact + (jnp.log(nf / max_exact) / math.log(MAX_DISTANCE / max_exact)
                         * (NUM_BUCKETS - max_exact)).astype(jnp.int32)
    large = jnp.minimum(large, NUM_BUCKETS - 1)
    return jnp.where(n < max_exact, n, large)


def _rotary_tables(pos):
    half = DK_R // 2
    freqs = ROPE_BASE ** (-jnp.arange(half, dtype=f32) / half)
    ang = pos.astype(f32)[:, None] * freqs[None, :]
    cos, sin = jnp.cos(ang), jnp.sin(ang)
    return jnp.concatenate([cos, cos], axis=1), jnp.concatenate([-sin, sin], axis=1)


def _decay_tables(chunk):
    lg = jnp.log1p(-jnp.exp2(-5.0 - jnp.arange(H_R, dtype=f32)))
    i = jnp.arange(chunk, dtype=f32)
    diff = i[:, None] - i[None, :]
    causal = diff >= 0
    intra = jnp.where(causal[None], jnp.exp(jnp.where(causal, diff, 0.0)[None] * lg[:, None, None]), 0.0)
    cross = jnp.exp((i[None, :] + 1.0) * lg[:, None])
    kv = jnp.exp((chunk - 1.0 - i)[None, :] * lg[:, None])
    cdec = jnp.exp(chunk * lg)
    bc = lambda a: jnp.broadcast_to(a[:, :, None], (H_R, chunk, LANES))
    return {"intra": intra, "cross": bc(cross), "kv": bc(kv),
            "chunk": jnp.broadcast_to(cdec[:, None, None], (H_R, 8, LANES))}


def _bias_tiles(rel_bias, tq):
    far_n = np.float32(tq + 1)
    far_bucket = 16 + int(np.float32(np.log(far_n / np.float32(16.0))) / math.log(8.0) * 16)
    assert far_bucket >= NUM_BUCKETS - 1, "keys beyond the previous chunk must share the last bucket"
    r = jnp.arange(tq, dtype=jnp.int32)[:, None]
    c = jnp.arange(2 * tq, dtype=jnp.int32)[None, :]
    b = rel_bias[_t5_bucket(r + tq - c)].astype(f32)
    b = b - rel_bias[NUM_BUCKETS - 1].astype(f32)[None, None, :]
    return b.transpose(2, 0, 1)


def _layer_weights(l, ffn1_wg, ffn1_wu, ffn1_wd, ln1_g, ln1_b, w_in, ret_gn_g, w_out, ln2_g, ln2_b,
                   ffn2_wg, ffn2_wu, ffn2_wd, ln3_g, ln3_b):
    pts = np.cumsum((0,) + IN_SIZES)
    wi = w_in[l]
    col = lambda k: wi[:, pts[k]:pts[k + 1]]
    zeros = jnp.zeros((D_MODEL, LANES - H_IDX), wi.dtype)
    w_in2 = jnp.concatenate([col(0), col(1), col(2), col(3), col(4), col(4), col(5), zeros,
                             col(6), col(7), col(8), col(9)], axis=1).astype(bf16)
    r2 = lambda a: a[l].reshape(1, -1).astype(f32)
    return {
        "ffn1_wg": ffn1_wg[l].astype(bf16), "ffn1_wu": ffn1_wu[l].astype(bf16), "ffn1_wd": ffn1_wd[l].astype(bf16),
        "ln1_g": r2(ln1_g), "ln1_b": r2(ln1_b), "w_in": w_in2, "gng": r2(ret_gn_g),
        "w_out": w_out[l].astype(bf16), "ln2_g": r2(ln2_g), "ln2_b": r2(ln2_b),
        "ffn2_wg": ffn2_wg[l].astype(bf16), "ffn2_wu": ffn2_wu[l].astype(bf16), "ffn2_wd": ffn2_wd[l].astype(bf16),
        "ln3_g": r2(ln3_g), "ln3_b": r2(ln3_b),
    }


def kernel(x_prompt, x_sample, cache_k, cache_v, cache_kidx, state_ret, page_table, rel_bias,
           ffn1_wg, ffn1_wu, ffn1_wd, ln1_g, ln1_b, w_in, ret_gn_g, w_out,
           ln2_g, ln2_b, ffn2_wg, ffn2_wu, ffn2_wd, ln3_g, ln3_b):
    b, s, _ = x_prompt.shape
    db, ds, _ = x_sample.shape
    depth = w_in.shape[0]
    npages = page_table.shape[1]
    past = npages * PAGE_SIZE
    assert ds == 1, "the sample group decodes one token per sequence"
    alpha = (2 * depth) ** 0.25

    tq = min(ATT_TQ, s)
    chunk = min(RET_CHUNK, s)
    assert s % tq == 0 and s % chunk == 0 and tq % LANES == 0
    topk_p = min(TOPK_MAX, s // 4)
    topk_s = min(TOPK_MAX, (past + ds) // 4)

    rot_p = _rotary_tables(jnp.arange(s, dtype=jnp.int32))
    rot_s = _rotary_tables(jnp.full((db * ds,), past, jnp.int32))
    dec_p = _decay_tables(chunk)
    gam = jnp.broadcast_to(_decay_tables(1)["chunk"][:, 0, :], (H_R, LANES))
    bias_tiles = _bias_tiles(rel_bias, tq)
    dist = past - jnp.arange(past, dtype=jnp.int32)
    bias_pos = rel_bias[_t5_bucket(dist)].astype(f32).T.reshape(H_A, npages, 1, PAGE_SIZE)
    bias0 = rel_bias[_t5_bucket(jnp.zeros((1,), jnp.int32))].astype(f32)
    pt_flat = page_table.reshape(-1).astype(jnp.int32)
    grp = min(SAMPLE_PAGE_GROUP, npages)
    assert npages % grp == 0
    ckidx_t = jnp.transpose(cache_kidx, (0, 1, 3, 2))
    ck_t = jnp.transpose(cache_k, (0, 1, 3, 4, 2))
    cv_t = jnp.transpose(cache_v, (0, 1, 3, 4, 2))

    hp = x_prompt.reshape(b * s, D_MODEL)
    hs = x_sample.reshape(db * ds, D_MODEL)
    outs = {k: [] for k in ("kp", "vp", "kip", "sp", "ks", "vs", "kis", "ss")}
    for l in range(depth):
        w = _layer_weights(l, ffn1_wg, ffn1_wu, ffn1_wd, ln1_g, ln1_b, w_in, ret_gn_g, w_out, ln2_g, ln2_b,
                           ffn2_wg, ffn2_wu, ffn2_wd, ln3_g, ln3_b)
        pp = _pre_call(hp, w, rot_p[0], rot_p[1], alpha, min(PRE_TM, b * s), bf16, "pre_prompt")
        r3 = lambda a: a.reshape(b, s, a.shape[-1])
        oa = _attn_call(r3(pp["qa"]), r3(pp["qi"]), r3(pp["wi"]), r3(pp["kab"]), r3(pp["vab"]), r3(pp["ki2"]),
                        bias_tiles, topk_p, tq)
        yr, st_p = _ret_call(r3(pp["qr"]), r3(pp["kr"]), r3(pp["vr"]), r3(pp["gr"]), w["gng"], dec_p, chunk)
        hp = _post_call(pp["h"], oa.reshape(b * s, W_A), yr.reshape(b * s, W_R), w, alpha,
                        min(POST_TM, b * s), "post_prompt")
        outs["kp"].append(pp["ka"].reshape(b, s, H_A, DH_A))
        outs["vp"].append(pp["va"].reshape(b, s, H_A, DH_A))
        outs["kip"].append(pp["ki"].reshape(b, s, DH_IDX))
        outs["sp"].append(st_p)

        ps = _pre_call(hs, w, rot_s[0], rot_s[1], alpha, db * ds, f32, "pre_sample")
        q16 = jnp.pad(ps["qi"].reshape(db, H_IDX, DH_IDX), ((0, 0), (0, 16 - H_IDX), (0, 0)))
        w16 = jnp.pad((ps["wi"][:, :H_IDX] * H_IDX ** -0.5).reshape(db, H_IDX, 1), ((0, 0), (0, 16 - H_IDX), (0, 0)))
        mask, snew = _ssel_call(pt_flat, ckidx_t, q16, w16, ps["ki"].reshape(db, 1, DH_IDX), l, npages, topk_s)
        t8 = lambda a: a.reshape(db, H_A, DH_A).transpose(0, 2, 1)
        r4 = lambda a: a.reshape(db, H_R, DK_R)
        ot_s, yr_s, st_s = _satt_call(
            pt_flat, ck_t, cv_t, t8(ps["qa"]), t8(ps["ka"]), t8(ps["va"]), bias_pos, bias0,
            mask.reshape(db, npages, 1, PAGE_SIZE), snew,
            r4(ps["qr"]), r4(ps["kr"]), r4(ps["vr"]), r4(ps["gr"]), w["gng"].reshape(H_R, DV_R), gam,
            state_ret, l, npages, grp)
        oa_s = ot_s.transpose(0, 2, 1).reshape(db, W_A)
        hs = _post_call(ps["h"], oa_s, yr_s.reshape(db, W_R), w, alpha, db * ds, "post_sample")
        outs["ks"].append(ps["ka"].reshape(db, ds, H_A, DH_A))
        outs["vs"].append(ps["va"].reshape(db, ds, H_A, DH_A))
        outs["kis"].append(ps["ki"].reshape(db, ds, DH_IDX))
        outs["ss"].append(st_s)

    stack = lambda k: jnp.stack(outs[k])
    return (hp.reshape(b, s, D_MODEL), hs.reshape(db, ds, D_MODEL),
            stack("kp"), stack("vp"), stack("kip"), stack("sp"),
            stack("ks"), stack("vs"), stack("kis"), stack("ss"))
```

```python
import functools
import math

import numpy as np
import jax
import jax.numpy as jnp
from jax import lax
from jax.experimental import pallas as pl
from jax.experimental.pallas import tpu as pltpu

D_MODEL = 1024
D_FF = 2816
PAGE_SIZE = 128
H_A = 8
DH_A = 64
W_A = H_A * DH_A
H_IDX = 8
DH_IDX = 64
TOPK_MAX = 256
NUM_BUCKETS = 32
MAX_DISTANCE = 128
H_R = 4
DK_R = 128
DV_R = 128
W_R = H_R * DV_R
ROPE_BASE = 10000.0
LN_EPS = 1e-5
GN_EPS = 1e-5
IN_SIZES = (W_A, W_A, W_A, H_IDX * DH_IDX, DH_IDX, H_IDX, H_R * DK_R, H_R * DK_R, W_R, W_R)

LANES = 128
MIB = 1024 * 1024
NEG = -1e30

FF_CHUNK = 512
PRE_TM = 256
POST_TM = 512
ATT_TQ = 256
RET_CHUNK = 256
BISECT_ITERS = 16
SAMPLE_PAGE_GROUP = 16

C_QA, C_KA, C_VA, C_QI, C_KI2, C_WI, C_QR, C_KR, C_VR, C_GR, C_END = (
    0, 512, 1024, 1536, 2048, 2176, 2304, 2816, 3328, 3840, 4352)

f32 = jnp.float32
bf16 = jnp.bfloat16
NT_DIMS = (((1,), (1,)), ((), ()))


def _const_spec(shape):
    nd = len(shape)
    return pl.BlockSpec(shape, lambda *_: (0,) * nd, pipeline_mode=pl.Buffered(1))


def _ln(x, g, b):
    mu = jnp.mean(x, axis=-1, keepdims=True)
    xc = x - mu
    var = jnp.mean(xc * xc, axis=-1, keepdims=True)
    return xc * lax.rsqrt(var + LN_EPS) * g + b


def _ffn(xb, wg_ref, wu_ref, wd_ref):
    acc = None
    for c0 in range(0, D_FF, FF_CHUNK):
        c1 = min(c0 + FF_CHUNK, D_FF)
        g = jnp.dot(xb, wg_ref[:, c0:c1], preferred_element_type=f32)
        u = jnp.dot(xb, wu_ref[:, c0:c1], preferred_element_type=f32)
        a = (jax.nn.silu(g) * u).astype(bf16)
        part = jnp.dot(a, wd_ref[c0:c1, :], preferred_element_type=f32)
        acc = part if acc is None else acc + part
    return acc


def _pre_kernel(alpha, x_ref, wg_ref, wu_ref, wd_ref, lng_ref, lnb_ref, win_ref, rc_ref, rs_ref,
                h_ref, qa_ref, ka_ref, va_ref, kab_ref, vab_ref, qi_ref, ki_ref, ki2_ref, wi_ref,
                qr_ref, kr_ref, vr_ref, gr_ref):
    act = qa_ref.dtype
    x = x_ref[...]
    f = _ffn(x.astype(bf16), wg_ref, wu_ref, wd_ref)
    h = _ln(alpha * x + 0.5 * f, lng_ref[...], lnb_ref[...])
    h_ref[...] = h
    hb = h.astype(bf16)

    def proj(c0, c1):
        return jnp.dot(hb, win_ref[:, c0:c1], preferred_element_type=f32)

    qa_ref[...] = (proj(C_QA, C_KA) * DH_A ** -0.5).astype(act)
    ka = proj(C_KA, C_VA)
    ka_ref[...] = ka
    kab_ref[...] = ka.astype(act)
    va = proj(C_VA, C_QI)
    va_ref[...] = va
    vab_ref[...] = va.astype(act)
    qi_ref[...] = (proj(C_QI, C_KI2) * DH_IDX ** -0.5).astype(act)
    kk = proj(C_KI2, C_WI)
    ki_ref[...] = kk[:, :DH_IDX]
    ki2_ref[...] = kk.astype(act)
    wi_ref[...] = proj(C_WI, C_QR)
    qr = proj(C_QR, C_KR)
    kr = proj(C_KR, C_VR)
    c = rc_ref[...]
    s = rs_ref[...]
    for hh in range(H_R):
        sl = slice(DK_R * hh, DK_R * (hh + 1))
        qh = qr[:, sl]
        kh = kr[:, sl]
        qr_ref[:, sl] = (qh * c + pltpu.roll(qh, DK_R // 2, 1) * s).astype(act)
        kr_ref[:, sl] = ((kh * c + pltpu.roll(kh, DK_R // 2, 1) * s) * DK_R ** -0.5).astype(act)
    vr_ref[...] = proj(C_VR, C_GR).astype(act)
    gr_ref[...] = proj(C_GR, C_END)


def _pre_call(x, w, rot_c, rot_s, alpha, tm, act, name):
    n = x.shape[0]
    grid = (pl.cdiv(n, tm),)
    row = lambda width: pl.BlockSpec((tm, width), lambda i: (i, 0))
    rot_blocks = rot_c.shape[0] // tm
    rot = pl.BlockSpec((tm, LANES), lambda i: (i % rot_blocks, 0))
    in_specs = [
        row(D_MODEL),
        _const_spec((D_MODEL, D_FF)), _const_spec((D_MODEL, D_FF)), _const_spec((D_FF, D_MODEL)),
        _const_spec((1, D_MODEL)), _const_spec((1, D_MODEL)),
        _const_spec((D_MODEL, C_END)),
        rot, rot,
    ]
    outs = [
        ("h", D_MODEL, f32), ("qa", W_A, act), ("ka", W_A, f32), ("va", W_A, f32), ("kab", W_A, act),
        ("vab", W_A, act), ("qi", W_A, act), ("ki", DH_IDX, f32), ("ki2", LANES, act), ("wi", LANES, f32),
        ("qr", W_R, act), ("kr", W_R, act), ("vr", W_R, act), ("gr", W_R, f32),
    ]
    out_shape = [jax.ShapeDtypeStruct((n, wd), dt) for _, wd, dt in outs]
    out_specs = [row(wd) for _, wd, _ in outs]
    res = pl.pallas_call(
        functools.partial(_pre_kernel, alpha),
        grid=grid, in_specs=in_specs, out_specs=out_specs, out_shape=out_shape,
        compiler_params=pltpu.CompilerParams(dimension_semantics=("arbitrary",), vmem_limit_bytes=52 * MIB),
        name=name,
    )(x, w["ffn1_wg"], w["ffn1_wu"], w["ffn1_wd"], w["ln1_g"], w["ln1_b"], w["w_in"], rot_c, rot_s)
    return {k: v for (k, _, _), v in zip(outs, res)}


def _post_kernel(alpha, h_ref, oa_ref, yr_ref, wo_ref, l2g_ref, l2b_ref, wg_ref, wu_ref, wd_ref,
                 l3g_ref, l3b_ref, out_ref):
    h = h_ref[...]
    mix = (jnp.dot(oa_ref[...].astype(bf16), wo_ref[0:W_A, :], preferred_element_type=f32)
           + jnp.dot(yr_ref[...].astype(bf16), wo_ref[W_A:W_A + W_R, :], preferred_element_type=f32))
    h2 = _ln(alpha * h + mix, l2g_ref[...], l2b_ref[...])
    f = _ffn(h2.astype(bf16), wg_ref, wu_ref, wd_ref)
    out_ref[...] = _ln(alpha * h2 + 0.5 * f, l3g_ref[...], l3b_ref[...])


def _post_call(h, oa, yr, w, alpha, tm, name):
    n = h.shape[0]
    row = lambda width: pl.BlockSpec((tm, width), lambda i: (i, 0))
    in_specs = [
        row(D_MODEL), row(W_A), row(W_R),
        _const_spec((W_A + W_R, D_MODEL)), _const_spec((1, D_MODEL)), _const_spec((1, D_MODEL)),
        _const_spec((D_MODEL, D_FF)), _const_spec((D_MODEL, D_FF)), _const_spec((D_FF, D_MODEL)),
        _const_spec((1, D_MODEL)), _const_spec((1, D_MODEL)),
    ]
    return pl.pallas_call(
        functools.partial(_post_kernel, alpha),
        grid=(pl.cdiv(n, tm),), in_specs=in_specs, out_specs=row(D_MODEL),
        out_shape=jax.ShapeDtypeStruct((n, D_MODEL), f32),
        compiler_params=pltpu.CompilerParams(dimension_semantics=("arbitrary",), vmem_limit_bytes=48 * MIB),
        name=name,
    )(h, oa, yr, w["w_out"], w["ln2_g"], w["ln2_b"], w["ffn2_wg"], w["ffn2_wu"], w["ffn2_wd"],
      w["ln3_g"], w["ln3_b"])


def _attn_kernel(topk, qa_ref, qi_ref, wi_ref, k_ref, v_ref, ki2_ref, bias_ref, oa_ref,
                 sc_ref, qap_ref, qip_ref, wb_ref, m_ref, l_ref, acc_ref):
    tq = qa_ref.shape[0]
    ncb = tq // LANES
    i = pl.program_id(1)
    nj = i + 1
    kf = float(topk)

    lane = lax.broadcasted_iota(jnp.int32, (tq, LANES), 1)
    lo_half = lane < DH_A
    for h in range(H_A):
        p = h // 2
        keep = lo_half if h % 2 == 0 else jnp.logical_not(lo_half)
        blk = slice(LANES * p, LANES * (p + 1))
        qip_ref[h] = jnp.where(keep, qi_ref[:, blk], jnp.zeros((), qi_ref.dtype))
        qap_ref[h] = jnp.where(keep, qa_ref[:, blk], jnp.zeros((), qa_ref.dtype))
        wb_ref[h] = jnp.broadcast_to(wi_ref[:, h:h + 1] * H_IDX ** -0.5, (tq, LANES))

    def score_chunk(j, carry):
        off = pl.multiple_of(j * tq, tq)
        kj = ki2_ref[pl.ds(off, tq), :]
        accs = [jnp.zeros((tq, LANES), f32) for _ in range(ncb)]
        for h in range(H_IDX):
            d = lax.dot_general(qip_ref[h], kj, NT_DIMS, preferred_element_type=f32)
            wb = wb_ref[h]
            for cb in range(ncb):
                accs[cb] = accs[cb] + wb * jnp.maximum(d[:, cb * LANES:(cb + 1) * LANES], 0.0)
        for cb in range(ncb):
            sc_ref[j, :, cb * LANES:(cb + 1) * LANES] = accs[cb]
        return carry

    lax.fori_loop(0, nj, score_chunk, 0)

    row = lax.broadcasted_iota(jnp.int32, (tq, tq), 0)
    col = lax.broadcasted_iota(jnp.int32, (tq, tq), 1)
    causal = col <= row
    sd = sc_ref[i]
    mn_diag = jnp.min(jnp.where(causal, sd, jnp.inf), axis=1, keepdims=True)
    sc_ref[i] = jnp.where(causal, sd, -jnp.inf)

    def fold(fn, init, comb, n_chunks):
        def body(j, acc):
            v = fn(sc_ref[j])
            for cb in range(ncb):
                acc = comb(acc, v[:, cb * LANES:(cb + 1) * LANES])
            return acc
        return lax.fori_loop(0, n_chunks, body, jnp.full((tq, LANES), init, f32))

    def tile_lanes(x):
        return jnp.concatenate([x] * ncb, axis=1) if ncb > 1 else x

    def lane_sum(x):
        return jnp.broadcast_to(jnp.sum(x, axis=1, keepdims=True), (tq, LANES))

    def lane_max(x):
        return jnp.broadcast_to(jnp.max(x, axis=1, keepdims=True), (tq, LANES))

    def count_ge(t):
        tt = tile_lanes(t)
        return lane_sum(fold(lambda s: jnp.where(s >= tt, 1.0, 0.0), 0.0, jnp.add, nj))

    def count_gt(t):
        tt = tile_lanes(t)
        return lane_sum(fold(lambda s: jnp.where(s > tt, 1.0, 0.0), 0.0, jnp.add, nj))

    def max_below(t):
        tt = tile_lanes(t)
        return lane_max(fold(lambda s: jnp.where(s < tt, s, -jnp.inf), -jnp.inf, jnp.maximum, nj))

    n_keys = (lax.broadcasted_iota(jnp.int32, (tq, LANES), 0) + (i * tq + 1)).astype(f32)
    take_all = n_keys <= kf

    mx = lane_max(fold(lambda s: s, -jnp.inf, jnp.maximum, nj))
    mn_far = -lane_max(-fold(lambda s: s, jnp.inf, jnp.minimum, i))
    mn = jnp.minimum(mn_far, jnp.broadcast_to(mn_diag, (tq, LANES)))

    @pl.when((i + 1) * tq <= topk)
    def _():
        def mk(j, carry):
            sc_ref[j] = jnp.zeros((tq, tq), f32)
            return carry
        lax.fori_loop(0, nj, mk, 0)

    @pl.when((i + 1) * tq > topk)
    def _():
        c_max = count_ge(mx)
        done0 = jnp.logical_or(take_all, c_max >= kf)
        thr0 = jnp.where(take_all, -jnp.inf, mx)
        cge0 = jnp.where(take_all, n_keys, c_max)

        def bis(_, st):
            lo, hi = st
            mid = 0.5 * (lo + hi)
            ge = count_ge(mid) >= kf
            return jnp.where(ge, mid, lo), jnp.where(ge, hi, mid)

        lo, hi = lax.fori_loop(0, BISECT_ITERS, bis, (mn, mx))

        def snap_cond(st):
            return st[0] > 0.0

        def snap_body(st):
            _, hi, thr, cge, done = st
            m = max_below(hi)
            c = count_ge(m)
            ok = c >= kf
            newly = jnp.logical_and(ok, done < 0.5)
            thr = jnp.where(newly, m, thr)
            cge = jnp.where(newly, c, cge)
            hi = jnp.where(jnp.logical_or(done > 0.5, ok), hi, m)
            done = jnp.where(ok, 1.0, done)
            left = jnp.sum(1.0 - done[:, 0:1])
            return left, hi, thr, cge, done

        done_f = jnp.where(done0, 1.0, 0.0)
        left0 = jnp.sum(1.0 - done_f[:, 0:1])
        _, _, thr, cge, _ = lax.while_loop(snap_cond, snap_body, (left0, hi, thr0, cge0, done_f))

        excess = jnp.sum(jnp.where(cge[:, 0:1] > kf, 1.0, 0.0)) > 0.0
        thr_t = tile_lanes(thr)

        @pl.when(jnp.logical_not(excess))
        def _():
            def mk(j, carry):
                sc_ref[j] = jnp.where(sc_ref[j] >= thr_t, 0.0, NEG)
                return carry
            lax.fori_loop(0, nj, mk, 0)

        @pl.when(excess)
        def _():
            need = tile_lanes(kf - count_gt(thr))
            upper = jnp.where(row <= col, 1.0, 0.0).astype(bf16)

            def mk(j, seen):
                s = sc_ref[j]
                tie = s == thr_t
                tie_f = jnp.where(tie, 1.0, 0.0)
                rank = seen + jnp.dot(tie_f.astype(bf16), upper, preferred_element_type=f32)
                sel = jnp.logical_or(s > thr_t, jnp.logical_and(tie, rank <= need))
                sc_ref[j] = jnp.where(sel, 0.0, NEG)
                part = tie_f[:, 0:LANES]
                for cb in range(1, ncb):
                    part = part + tie_f[:, cb * LANES:(cb + 1) * LANES]
                return seen + tile_lanes(lane_sum(part))
            lax.fori_loop(0, nj, mk, jnp.zeros((tq, tq), f32))

    sc_ref[i] = jnp.where(causal, sc_ref[i], NEG)


    m_ref[...] = jnp.full(m_ref.shape, -jnp.inf, f32)
    l_ref[...] = jnp.zeros(l_ref.shape, f32)
    acc_ref[...] = jnp.zeros(acc_ref.shape, f32)

    def attend_chunk(j, bias_cols):
        off = pl.multiple_of(j * tq, tq)
        sel_mask = sc_ref[j]
        for h in range(H_A):
            blk = slice(LANES * (h // 2), LANES * (h // 2 + 1))
            s = lax.dot_general(qap_ref[h], k_ref[pl.ds(off, tq), blk], NT_DIMS, preferred_element_type=f32)
            s = s + sel_mask
            if bias_cols is not None:
                s = s + bias_ref[h, :, bias_cols]
            m_old = m_ref[h]
            cmax = s[:, 0:LANES]
            for cb in range(1, ncb):
                cmax = jnp.maximum(cmax, s[:, cb * LANES:(cb + 1) * LANES])
            m_new = jnp.maximum(m_old, lane_max(cmax))
            a = jnp.exp(m_old - m_new)
            e = jnp.exp(s - tile_lanes(m_new))
            esum = e[:, 0:LANES]
            for cb in range(1, ncb):
                esum = esum + e[:, cb * LANES:(cb + 1) * LANES]
            l_ref[h] = l_ref[h] * a + esum
            acc_ref[h] = acc_ref[h] * a + jnp.dot(e.astype(bf16), v_ref[pl.ds(off, tq), blk],
                                                  preferred_element_type=f32)
            m_ref[h] = m_new

    def far(j, carry):
        attend_chunk(j, None)
        return carry
    lax.fori_loop(0, i - 1, far, 0)

    @pl.when(i >= 1)
    def _():
        attend_chunk(i - 1, slice(0, tq))
    attend_chunk(i, slice(tq, 2 * tq))

    for p in range(H_A // 2):
        o_even = acc_ref[2 * p] / lane_sum(l_ref[2 * p])
        o_odd = acc_ref[2 * p + 1] / lane_sum(l_ref[2 * p + 1])
        oa_ref[:, LANES * p:LANES * (p + 1)] = jnp.where(lo_half, o_even, o_odd).astype(oa_ref.dtype)


def _attn_call(qa, qi, wi, kab, vab, ki2, bias_tiles, topk, tq):
    b, s, _ = qa.shape
    nq = s // tq
    qspec = lambda width: pl.BlockSpec((None, tq, width), lambda bi, i: (bi, i, 0))
    kvspec = lambda width: pl.BlockSpec((None, s, width), lambda bi, i: (bi, 0, 0), pipeline_mode=pl.Buffered(1))
    scratch = [
        pltpu.VMEM((nq, tq, tq), f32),
        pltpu.VMEM((H_A, tq, LANES), qa.dtype),
        pltpu.VMEM((H_IDX, tq, LANES), qi.dtype),
        pltpu.VMEM((H_IDX, tq, LANES), f32),
        pltpu.VMEM((H_A, tq, LANES), f32),
        pltpu.VMEM((H_A, tq, LANES), f32),
        pltpu.VMEM((H_A, tq, LANES), f32),
    ]
    return pl.pallas_call(
        functools.partial(_attn_kernel, topk),
        grid=(b, nq),
        in_specs=[qspec(W_A), qspec(W_A), qspec(LANES), kvspec(W_A), kvspec(W_A), kvspec(LANES),
                  _const_spec((H_A, tq, 2 * tq))],
        out_specs=qspec(W_A),
        out_shape=jax.ShapeDtypeStruct((b, s, W_A), bf16),
        scratch_shapes=scratch,
        compiler_params=pltpu.CompilerParams(dimension_semantics=("arbitrary", "arbitrary"),
                                             vmem_limit_bytes=48 * MIB),
        name="attn_prompt",
    )(qa, qi, wi, kab, vab, ki2, bias_tiles)


def _ret_kernel(qr_ref, kr_ref, vr_ref, gr_ref, gng_ref, dmat_ref, cross_ref, kvd_ref, cdec_ref,
                yr_ref, st_ref):
    c = pl.program_id(1)

    @pl.when(c == 0)
    def _():
        st_ref[...] = jnp.zeros(st_ref.shape, f32)

    for h in range(H_R):
        sl = slice(DK_R * h, DK_R * (h + 1))
        q = qr_ref[:, sl]
        k = kr_ref[:, sl]
        v = vr_ref[:, sl]
        att = lax.dot_general(q, k, NT_DIMS, preferred_element_type=f32) * dmat_ref[h]
        st = st_ref[h]
        o = (jnp.dot(att.astype(bf16), v, preferred_element_type=f32)
             + jnp.dot(q, st.astype(bf16), preferred_element_type=f32) * cross_ref[h])
        kd = (k.astype(f32) * kvd_ref[h]).T.astype(bf16)
        st_ref[h] = cdec_ref[h, 0:1, :] * st + jnp.dot(kd, v, preferred_element_type=f32)
        mu = jnp.mean(o, axis=-1, keepdims=True)
        oc = o - mu
        var = jnp.mean(oc * oc, axis=-1, keepdims=True)
        yn = oc * lax.rsqrt(var + GN_EPS) * gng_ref[:, sl]
        yr_ref[:, sl] = (jax.nn.silu(gr_ref[:, sl]) * yn).astype(yr_ref.dtype)


def _ret_call(qr, kr, vr, gr, gng, dec, chunk):
    b, s, _ = qr.shape
    nc = s // chunk
    rspec = pl.BlockSpec((None, chunk, W_R), lambda bi, c: (bi, c, 0))
    return pl.pallas_call(
        _ret_kernel,
        grid=(b, nc),
        in_specs=[rspec, rspec, rspec, rspec, _const_spec((1, W_R)),
                  _const_spec((H_R, chunk, chunk)), _const_spec((H_R, chunk, LANES)),
                  _const_spec((H_R, chunk, LANES)), _const_spec((H_R, 8, LANES))],
        out_specs=[rspec, pl.BlockSpec((None, H_R, DK_R, DV_R), lambda bi, c: (bi, 0, 0, 0))],
        out_shape=[jax.ShapeDtypeStruct((b, s, W_R), bf16), jax.ShapeDtypeStruct((b, H_R, DK_R, DV_R), f32)],
        compiler_params=pltpu.CompilerParams(dimension_semantics=("arbitrary", "arbitrary"),
                                             vmem_limit_bytes=32 * MIB),
        name="ret_prompt",
    )(qr, kr, vr, gr, gng, dec["intra"], dec["cross"], dec["kv"], dec["chunk"])


def _sscore_kernel(layer, npages, pt_ref, ck_ref, q_ref, w_ref, kn_ref, sc_ref, snew_ref, kbuf, sem):
    b = pl.program_id(0)
    nb = pl.num_programs(0)
    slot = b % 2

    def page_copy(seq, p, sl):
        page = pt_ref[seq * npages + p]
        return pltpu.make_async_copy(ck_ref.at[layer, page], kbuf.at[sl, p], sem.at[sl])

    def start_all(seq, sl):
        def body(p, carry):
            page_copy(seq, p, sl).start()
            return carry
        lax.fori_loop(0, npages, body, 0)

    @pl.when(b == 0)
    def _():
        start_all(0, 0)

    @pl.when(b + 1 < nb)
    def _():
        start_all(b + 1, 1 - slot)

    def wait_body(p, carry):
        page_copy(b, p, slot).wait()
        return carry
    lax.fori_loop(0, npages, wait_body, 0)

    q = q_ref[...].astype(bf16)
    w = w_ref[...]

    pages_per_dot = min(8, npages)
    for p0 in range(0, npages, pages_per_dot):
        kw = jnp.concatenate([kbuf[slot, p] for p in range(p0, p0 + pages_per_dot)], axis=1).astype(bf16)
        d = jnp.dot(q, kw, preferred_element_type=f32)
        sc_ref[:, p0 * PAGE_SIZE:(p0 + pages_per_dot) * PAGE_SIZE] = jnp.sum(
            w * jnp.maximum(d, 0.0), axis=0, keepdims=True)

    kn = kn_ref[...].astype(bf16).astype(f32)
    dn = jnp.sum(q.astype(f32) * kn, axis=1, keepdims=True)
    s_new = jnp.sum(w * jnp.maximum(dn, 0.0), axis=0, keepdims=True)
    snew_ref[...] = jnp.broadcast_to(s_new, (1, LANES))


def _sscore_call(page_table_flat, cache_kidx_t, q16, w16, kn, layer, npages):
    db = q16.shape[0]
    assert npages % min(8, npages) == 0
    grid_spec = pltpu.PrefetchScalarGridSpec(
        num_scalar_prefetch=1,
        grid=(db,),
        in_specs=[
            pl.BlockSpec(memory_space=pl.ANY),
            pl.BlockSpec((None, 16, DH_IDX), lambda b, pt: (b, 0, 0)),
            pl.BlockSpec((None, 16, 1), lambda b, pt: (b, 0, 0)),
            pl.BlockSpec((None, 1, DH_IDX), lambda b, pt: (b, 0, 0)),
        ],
        out_specs=[pl.BlockSpec((None, 1, npages * PAGE_SIZE), lambda b, pt: (b, 0, 0)),
                   pl.BlockSpec((None, 1, LANES), lambda b, pt: (b, 0, 0))],
        scratch_shapes=[
            pltpu.VMEM((2, npages, DH_IDX, PAGE_SIZE), f32),
            pltpu.SemaphoreType.DMA((2,)),
        ],
    )
    return pl.pallas_call(
        functools.partial(_sscore_kernel, layer, npages),
        grid_spec=grid_spec,
        out_shape=[jax.ShapeDtypeStruct((db, 1, npages * PAGE_SIZE), f32),
                   jax.ShapeDtypeStruct((db, 1, LANES), f32)],
        compiler_params=pltpu.CompilerParams(dimension_semantics=("arbitrary",), vmem_limit_bytes=32 * MIB),
        name="sample_scores",
    )(page_table_flat, cache_kidx_t, q16, w16, kn)


PREFIX_CHUNK = 256


def _ssel_kernel(topk, sc_ref, sn_ref, mask_ref, selnew_ref):
    kf = float(topk)
    db, length = sc_ref.shape
    sc = sc_ref[...]
    s_new = sn_ref[:, 0:1]

    def rsum(x):
        return jnp.sum(x, axis=1, keepdims=True)

    def count(cmp, t):
        return rsum(jnp.where(cmp(sc, t), 1.0, 0.0)) + jnp.where(cmp(s_new, t), 1.0, 0.0)

    ge = lambda a, t: a >= t
    gt = lambda a, t: a > t
    mx = jnp.maximum(jnp.max(sc, axis=1, keepdims=True), s_new)
    mn = jnp.minimum(jnp.min(sc, axis=1, keepdims=True), s_new)
    c_max = count(ge, mx)
    done0 = jnp.where(c_max >= kf, 1.0, 0.0)

    def bis(_, st):
        lo, hi = st
        mid = 0.5 * (lo + hi)
        ok = count(ge, mid) >= kf
        return jnp.where(ok, mid, lo), jnp.where(ok, hi, mid)
    lo, hi = lax.fori_loop(0, BISECT_ITERS, bis, (mn, mx))

    def snap_cond(st):
        return st[0] > 0.0

    def snap_body(st):
        _, hi, thr, done = st
        below = jnp.maximum(jnp.max(jnp.where(sc < hi, sc, -jnp.inf), axis=1, keepdims=True),
                            jnp.where(s_new < hi, s_new, -jnp.inf))
        ok = count(ge, below) >= kf
        newly = jnp.logical_and(ok, done < 0.5)
        thr = jnp.where(newly, below, thr)
        hi = jnp.where(jnp.logical_or(done > 0.5, ok), hi, below)
        done = jnp.where(ok, 1.0, done)
        return jnp.sum(1.0 - done), hi, thr, done
    _, _, thr, _ = lax.while_loop(snap_cond, snap_body, (jnp.sum(1.0 - done0), hi, mx, done0))

    need = kf - count(gt, thr)
    pc = min(PREFIX_CHUNK, length)
    r_i = lax.broadcasted_iota(jnp.int32, (pc, pc), 0)
    c_i = lax.broadcasted_iota(jnp.int32, (pc, pc), 1)
    upper = jnp.where(r_i <= c_i, 1.0, 0.0).astype(bf16)
    seen = jnp.zeros((db, 1), f32)
    for c0 in range(0, length, pc):
        s_c = sc[:, c0:c0 + pc]
        tie = s_c == thr
        rank = seen + jnp.dot(jnp.where(tie, 1.0, 0.0).astype(bf16), upper, preferred_element_type=f32)
        sel = jnp.logical_or(s_c > thr, jnp.logical_and(tie, rank <= need))
        mask_ref[:, c0:c0 + pc] = jnp.where(sel, 0.0, NEG)
        seen = rank[:, pc - 1:pc]
    sel_new = jnp.logical_or(s_new > thr, jnp.logical_and(s_new == thr, seen + 1.0 <= need))
    selnew_ref[...] = jnp.broadcast_to(jnp.where(sel_new, 1.0, 0.0), (db, LANES))


def _ssel_call(scores, s_new, topk):
    db, length = scores.shape
    assert length % min(PREFIX_CHUNK, length) == 0
    return pl.pallas_call(
        functools.partial(_ssel_kernel, topk),
        out_shape=[jax.ShapeDtypeStruct((db, length), f32), jax.ShapeDtypeStruct((db, LANES), f32)],
        compiler_params=pltpu.CompilerParams(vmem_limit_bytes=32 * MIB),
        name="sample_select",
    )(scores, s_new)


def _satt_kernel(layer, npages, grp, pt_ref,
                 ckk_ref, ckv_ref, qt_ref, knt_ref, vnt_ref, bias_ref, bias0_ref, mask_ref, snew_ref,
                 qr_ref, kr_ref, vr_ref, gr_ref, gng_ref, gam_ref, s0_ref,
                 ot_ref, yr_ref, sn_ref,
                 kbuf, vbuf, sem, m_ref, l_ref, acc_ref):
    b = pl.program_id(0)
    g = pl.program_id(1)
    nb = pl.num_programs(0)
    ng = pl.num_programs(1)
    t = b * ng + g
    slot = t % 2

    def page_copies(seq, gi, p, sl):
        page = pt_ref[seq * npages + gi * grp + p]
        return (pltpu.make_async_copy(ckk_ref.at[layer, page], kbuf.at[sl, p], sem.at[sl, 0]),
                pltpu.make_async_copy(ckv_ref.at[layer, page], vbuf.at[sl, p], sem.at[sl, 1]))

    def start_all(seq, gi, sl):
        def body(p, carry):
            ck, cv = page_copies(seq, gi, p, sl)
            ck.start()
            cv.start()
            return carry
        lax.fori_loop(0, grp, body, 0)

    @pl.when(t == 0)
    def _():
        start_all(0, 0, 0)

    @pl.when(t + 1 < nb * ng)
    def _():
        wrap = g + 1 == ng
        start_all(jnp.where(wrap, b + 1, b), jnp.where(wrap, 0, g + 1), 1 - slot)

    @pl.when(g == 0)
    def _():
        m_ref[...] = jnp.full(m_ref.shape, -jnp.inf, f32)
        l_ref[...] = jnp.zeros(l_ref.shape, f32)
        acc_ref[...] = jnp.zeros(acc_ref.shape, f32)

    def wait_body(p, carry):
        ck, cv = page_copies(b, g, p, slot)
        ck.wait()
        cv.wait()
        return carry
    lax.fori_loop(0, grp, wait_body, 0)

    mask = mask_ref[...]
    for h in range(H_A):
        qc = qt_ref[:, h:h + 1]
        kh = kbuf[slot, :, h]
        lg = jnp.sum(kh * qc[None], axis=1, keepdims=True) + bias_ref[h] + mask
        m_old = m_ref[h]
        m_blk = jnp.max(jnp.max(lg, axis=0), axis=1, keepdims=True)
        m_new = jnp.maximum(m_old, m_blk)
        a = jnp.exp(m_old - m_new)
        e = jnp.exp(lg - m_new[None])
        l_ref[h] = l_ref[h] * a + jnp.sum(e, axis=0)
        acc_ref[h] = acc_ref[h] * a + jnp.sum(vbuf[slot, :, h] * e, axis=0)
        m_ref[h] = m_new

    @pl.when(g == ng - 1)
    def _():
        lg_new = jnp.sum(qt_ref[...] * knt_ref[...], axis=0, keepdims=True) + bias0_ref[...]
        lg_new = jnp.where(snew_ref[0:1, 0:1] > 0.5, lg_new, NEG)
        for h in range(H_A):
            m_c = m_ref[h][:, 0:1]
            lg_h = lg_new[:, h:h + 1]
            m_f = jnp.maximum(m_c, lg_h)
            a = jnp.exp(m_c - m_f)
            e_new = jnp.exp(lg_h - m_f)
            den = jnp.sum(l_ref[h], axis=1, keepdims=True) * a + e_new
            num = jnp.sum(acc_ref[h], axis=1, keepdims=True) * a + e_new * vnt_ref[:, h:h + 1]
            ot_ref[:, h:h + 1] = num / den

    @pl.when(g == 0)
    def _():
        r_i = lax.broadcasted_iota(jnp.int32, (DK_R, DK_R), 0)
        c_i = lax.broadcasted_iota(jnp.int32, (DK_R, DK_R), 1)
        eye = jnp.where(r_i == c_i, 1.0, 0.0)
        for h in range(H_R):
            qrow = qr_ref[h:h + 1, :]
            krow = kr_ref[h:h + 1, :]
            vrow = vr_ref[h:h + 1, :]
            gam = gam_ref[h:h + 1, :]
            qcol = jnp.sum(eye * qrow, axis=1, keepdims=True)
            kcol = jnp.sum(eye * krow, axis=1, keepdims=True)
            st = s0_ref[h]
            qk = jnp.sum(qrow * krow, axis=1, keepdims=True)
            o = qk * vrow + gam * jnp.sum(qcol * st, axis=0, keepdims=True)
            sn_ref[h] = gam * st + kcol * vrow
            mu = jnp.mean(o, axis=1, keepdims=True)
            oc = o - mu
            var = jnp.mean(oc * oc, axis=1, keepdims=True)
            yn = oc * lax.rsqrt(var + GN_EPS) * gng_ref[h:h + 1, :]
            yr_ref[h:h + 1, :] = jax.nn.silu(gr_ref[h:h + 1, :]) * yn


def _satt_call(pt_flat, cache_k_t, cache_v_t, qt, knt, vnt, bias_pos, bias0, mask, snew, qr, kr, vr, gr, gng, gam,
               state, layer, npages, grp):
    db = qt.shape[0]
    ng = npages // grp
    per_seq = lambda d0, d1: pl.BlockSpec((None, d0, d1), lambda b, g, pt: (b, 0, 0))
    grid_spec = pltpu.PrefetchScalarGridSpec(
        num_scalar_prefetch=1,
        grid=(db, ng),
        in_specs=[
            pl.BlockSpec(memory_space=pl.ANY), pl.BlockSpec(memory_space=pl.ANY),
            per_seq(DH_A, H_A), per_seq(DH_A, H_A), per_seq(DH_A, H_A),
            pl.BlockSpec((H_A, grp, 1, PAGE_SIZE), lambda b, g, pt: (0, g, 0, 0)),
            pl.BlockSpec((1, H_A), lambda b, g, pt: (0, 0)),
            pl.BlockSpec((None, grp, 1, PAGE_SIZE), lambda b, g, pt: (b, g, 0, 0)),
            per_seq(1, LANES),
            per_seq(H_R, DK_R), per_seq(H_R, DK_R), per_seq(H_R, DV_R), per_seq(H_R, DV_R),
            pl.BlockSpec((H_R, DV_R), lambda b, g, pt: (0, 0)),
            pl.BlockSpec((H_R, DV_R), lambda b, g, pt: (0, 0)),
            pl.BlockSpec((None, None, H_R, DK_R, DV_R), lambda b, g, pt: (layer, b, 0, 0, 0)),
        ],
        out_specs=[per_seq(DH_A, H_A), per_seq(H_R, DV_R),
                   pl.BlockSpec((None, H_R, DK_R, DV_R), lambda b, g, pt: (b, 0, 0, 0))],
        scratch_shapes=[
            pltpu.VMEM((2, grp, H_A, DH_A, PAGE_SIZE), f32),
            pltpu.VMEM((2, grp, H_A, DH_A, PAGE_SIZE), f32),
            pltpu.SemaphoreType.DMA((2, 2)),
            pltpu.VMEM((H_A, 1, PAGE_SIZE), f32),
            pltpu.VMEM((H_A, 1, PAGE_SIZE), f32),
            pltpu.VMEM((H_A, DH_A, PAGE_SIZE), f32),
        ],
    )
    return pl.pallas_call(
        functools.partial(_satt_kernel, layer, npages, grp),
        grid_spec=grid_spec,
        out_shape=[jax.ShapeDtypeStruct((db, DH_A, H_A), f32), jax.ShapeDtypeStruct((db, H_R, DV_R), f32),
                   jax.ShapeDtypeStruct((db, H_R, DK_R, DV_R), f32)],
        compiler_params=pltpu.CompilerParams(dimension_semantics=("arbitrary", "arbitrary"),
                                             vmem_limit_bytes=40 * MIB),
        name="sample_attend",
    )(pt_flat, cache_k_t, cache_v_t, qt, knt, vnt, bias_pos, bias0, mask, snew, qr, kr, vr, gr, gng, gam, state)


def _t5_bucket(rel):
    n = jnp.maximum(rel, 0)
    max_exact = NUM_BUCKETS // 2
    nf = jnp.maximum(n, 1).astype(f32)
    large = max_exact + (jnp.log(nf / max_exact) / math.log(MAX_DISTANCE / max_exact)
                         * (NUM_BUCKETS - max_exact)).astype(jnp.int32)
    large = jnp.minimum(large, NUM_BUCKETS - 1)
    return jnp.where(n < max_exact, n, large)


def _rotary_tables(pos):
    half = DK_R // 2
    freqs = ROPE_BASE ** (-jnp.arange(half, dtype=f32) / half)
    ang = pos.astype(f32)[:, None] * freqs[None, :]
    cos, sin = jnp.cos(ang), jnp.sin(ang)
    return jnp.concatenate([cos, cos], axis=1), jnp.concatenate([-sin, sin], axis=1)


def _decay_tables(chunk):
    lg = jnp.log1p(-jnp.exp2(-5.0 - jnp.arange(H_R, dtype=f32)))
    i = jnp.arange(chunk, dtype=f32)
    diff = i[:, None] - i[None, :]
    causal = diff >= 0
    intra = jnp.where(causal[None], jnp.exp(jnp.where(causal, diff, 0.0)[None] * lg[:, None, None]), 0.0)
    cross = jnp.exp((i[None, :] + 1.0) * lg[:, None])
    kv = jnp.exp((chunk - 1.0 - i)[None, :] * lg[:, None])
    cdec = jnp.exp(chunk * lg)
    bc = lambda a: jnp.broadcast_to(a[:, :, None], (H_R, chunk, LANES))
    return {"intra": intra, "cross": bc(cross), "kv": bc(kv),
            "chunk": jnp.broadcast_to(cdec[:, None, None], (H_R, 8, LANES))}


def _bias_tiles(rel_bias, tq):
    far_n = np.float32(tq + 1)
    far_bucket = 16 + int(np.float32(np.log(far_n / np.float32(16.0))) / math.log(8.0) * 16)
    assert far_bucket >= NUM_BUCKETS - 1, "keys beyond the previous chunk must share the last bucket"
    r = jnp.arange(tq, dtype=jnp.int32)[:, None]
    c = jnp.arange(2 * tq, dtype=jnp.int32)[None, :]
    rb = rel_bias.astype(f32) - rel_bias[NUM_BUCKETS - 1].astype(f32)[None, :]
    return _bucket_lookup(rb, _t5_bucket(r + tq - c))


def _bucket_lookup(table, bucket):
    out = jnp.zeros((table.shape[1],) + bucket.shape, f32)
    for k in range(table.shape[0]):
        out = jnp.where(bucket[None] == k, table[k].reshape((-1,) + (1,) * bucket.ndim), out)
    return out


def _layer_weights(l, ffn1_wg, ffn1_wu, ffn1_wd, ln1_g, ln1_b, w_in, ret_gn_g, w_out, ln2_g, ln2_b,
                   ffn2_wg, ffn2_wu, ffn2_wd, ln3_g, ln3_b):
    pts = np.cumsum((0,) + IN_SIZES)
    wi = w_in[l]
    col = lambda k: wi[:, pts[k]:pts[k + 1]]
    zeros = jnp.zeros((D_MODEL, LANES - H_IDX), wi.dtype)
    w_in2 = jnp.concatenate([col(0), col(1), col(2), col(3), col(4), col(4), col(5), zeros,
                             col(6), col(7), col(8), col(9)], axis=1).astype(bf16)
    r2 = lambda a: a[l].reshape(1, -1).astype(f32)
    return {
        "ffn1_wg": ffn1_wg[l].astype(bf16), "ffn1_wu": ffn1_wu[l].astype(bf16), "ffn1_wd": ffn1_wd[l].astype(bf16),
        "ln1_g": r2(ln1_g), "ln1_b": r2(ln1_b), "w_in": w_in2, "gng": r2(ret_gn_g),
        "w_out": w_out[l].astype(bf16), "ln2_g": r2(ln2_g), "ln2_b": r2(ln2_b),
        "ffn2_wg": ffn2_wg[l].astype(bf16), "ffn2_wu": ffn2_wu[l].astype(bf16), "ffn2_wd": ffn2_wd[l].astype(bf16),
        "ln3_g": r2(ln3_g), "ln3_b": r2(ln3_b),
    }


def kernel(x_prompt, x_sample, cache_k, cache_v, cache_kidx, state_ret, page_table, rel_bias,
           ffn1_wg, ffn1_wu, ffn1_wd, ln1_g, ln1_b, w_in, ret_gn_g, w_out,
           ln2_g, ln2_b, ffn2_wg, ffn2_wu, ffn2_wd, ln3_g, ln3_b):
    b, s, _ = x_prompt.shape
    db, ds, _ = x_sample.shape
    depth = w_in.shape[0]
    npages = page_table.shape[1]
    past = npages * PAGE_SIZE
    assert ds == 1, "the sample group decodes one token per sequence"
    alpha = (2 * depth) ** 0.25

    tq = min(ATT_TQ, s)
    chunk = min(RET_CHUNK, s)
    assert s % tq == 0 and s % chunk == 0 and tq % LANES == 0
    topk_p = min(TOPK_MAX, s // 4)
    topk_s = min(TOPK_MAX, (past + ds) // 4)

    rot_p = _rotary_tables(jnp.arange(s, dtype=jnp.int32))
    rot_s = _rotary_tables(jnp.full((db * ds,), past, jnp.int32))
    dec_p = _decay_tables(chunk)
    gam = jnp.broadcast_to(_decay_tables(1)["chunk"][:, 0, :], (H_R, LANES))
    bias_tiles = _bias_tiles(rel_bias, tq)
    dist = past - jnp.arange(past, dtype=jnp.int32)
    bias_pos = _bucket_lookup(rel_bias.astype(f32), _t5_bucket(dist)).reshape(H_A, npages, 1, PAGE_SIZE)
    bias0 = rel_bias[0:1].astype(f32)
    pt_flat = page_table.reshape(-1).astype(jnp.int32)
    grp = min(SAMPLE_PAGE_GROUP, npages)
    assert npages % grp == 0
    ckidx_t = jnp.transpose(cache_kidx, (0, 1, 3, 2))
    ck_t = jnp.transpose(cache_k, (0, 1, 3, 4, 2))
    cv_t = jnp.transpose(cache_v, (0, 1, 3, 4, 2))

    hp = x_prompt.reshape(b * s, D_MODEL)
    hs = x_sample.reshape(db * ds, D_MODEL)
    outs = {k: [] for k in ("kp", "vp", "kip", "sp", "ks", "vs", "kis", "ss")}
    for l in range(depth):
        w = _layer_weights(l, ffn1_wg, ffn1_wu, ffn1_wd, ln1_g, ln1_b, w_in, ret_gn_g, w_out, ln2_g, ln2_b,
                           ffn2_wg, ffn2_wu, ffn2_wd, ln3_g, ln3_b)
        pp = _pre_call(hp, w, rot_p[0], rot_p[1], alpha, min(PRE_TM, b * s), bf16, "pre_prompt")
        r3 = lambda a: a.reshape(b, s, a.shape[-1])
        oa = _attn_call(r3(pp["qa"]), r3(pp["qi"]), r3(pp["wi"]), r3(pp["kab"]), r3(pp["vab"]), r3(pp["ki2"]),
                        bias_tiles, topk_p, tq)
        yr, st_p = _ret_call(r3(pp["qr"]), r3(pp["kr"]), r3(pp["vr"]), r3(pp["gr"]), w["gng"], dec_p, chunk)
        hp = _post_call(pp["h"], oa.reshape(b * s, W_A), yr.reshape(b * s, W_R), w, alpha,
                        min(POST_TM, b * s), "post_prompt")
        outs["kp"].append(pp["ka"].reshape(b, s, H_A, DH_A))
        outs["vp"].append(pp["va"].reshape(b, s, H_A, DH_A))
        outs["kip"].append(pp["ki"].reshape(b, s, DH_IDX))
        outs["sp"].append(st_p)

        ps = _pre_call(hs, w, rot_s[0], rot_s[1], alpha, db * ds, f32, "pre_sample")
        q16 = jnp.pad(ps["qi"].reshape(db, H_IDX, DH_IDX), ((0, 0), (0, 16 - H_IDX), (0, 0)))
        w16 = jnp.pad((ps["wi"][:, :H_IDX] * H_IDX ** -0.5).reshape(db, H_IDX, 1), ((0, 0), (0, 16 - H_IDX), (0, 0)))
        sc_s, sn_s = _sscore_call(pt_flat, ckidx_t, q16, w16, ps["ki"].reshape(db, 1, DH_IDX), l, npages)
        mask, snew = _ssel_call(sc_s.reshape(db, past), sn_s.reshape(db, LANES), topk_s)
        t8 = lambda a: a.reshape(db, H_A, DH_A).transpose(0, 2, 1)
        r4 = lambda a: a.reshape(db, H_R, DK_R)
        ot_s, yr_s, st_s = _satt_call(
            pt_flat, ck_t, cv_t, t8(ps["qa"]), t8(ps["ka"]), t8(ps["va"]), bias_pos, bias0,
            mask.reshape(db, npages, 1, PAGE_SIZE), snew.reshape(db, 1, LANES),
            r4(ps["qr"]), r4(ps["kr"]), r4(ps["vr"]), r4(ps["gr"]), w["gng"].reshape(H_R, DV_R), gam,
            state_ret, l, npages, grp)
        oa_s = ot_s.transpose(0, 2, 1).reshape(db, W_A)
        hs = _post_call(ps["h"], oa_s, yr_s.reshape(db, W_R), w, alpha, db * ds, "post_sample")
        outs["ks"].append(ps["ka"].reshape(db, ds, H_A, DH_A))
        outs["vs"].append(ps["va"].reshape(db, ds, H_A, DH_A))
        outs["kis"].append(ps["ki"].reshape(db, ds, DH_IDX))
        outs["ss"].append(st_s)

    stack = lambda k: jnp.stack(outs[k])
    return (hp.reshape(b, s, D_MODEL), hs.reshape(db, ds, D_MODEL),
            stack("kp"), stack("vp"), stack("kip"), stack("sp"),
            stack("ks"), stack("vs"), stack("kis"), stack("ss"))
```

```python
import functools
import math

import numpy as np
import jax
import jax.numpy as jnp
from jax import lax
from jax.experimental import pallas as pl
from jax.experimental.pallas import tpu as pltpu

D_MODEL = 1024
D_FF = 2816
PAGE_SIZE = 128
H_A = 8
DH_A = 64
W_A = H_A * DH_A
H_IDX = 8
DH_IDX = 64
TOPK_MAX = 256
NUM_BUCKETS = 32
MAX_DISTANCE = 128
H_R = 4
DK_R = 128
DV_R = 128
W_R = H_R * DV_R
ROPE_BASE = 10000.0
LN_EPS = 1e-5
GN_EPS = 1e-5
IN_SIZES = (W_A, W_A, W_A, H_IDX * DH_IDX, DH_IDX, H_IDX, H_R * DK_R, H_R * DK_R, W_R, W_R)

LANES = 128
MIB = 1024 * 1024
NEG = -1e30
LOG2E = math.log2(math.e)

FF_CHUNK = 512
PRE_TM = 256
POST_TM = 512
ATT_TQ = 256
RET_CHUNK = 256
BISECT_ITERS = 16
SAMPLE_PAGE_GROUP = 16
SELECT_ROWS = 128

C_QA, C_KA, C_VA, C_QI, C_KI2, C_WI, C_QR, C_KR, C_VR, C_GR, C_END = (
    0, 512, 1024, 1536, 2048, 2176, 2304, 2816, 3328, 3840, 4352)

f32 = jnp.float32
bf16 = jnp.bfloat16
NT_DIMS = (((1,), (1,)), ((), ()))


def _const_spec(shape):
    nd = len(shape)
    return pl.BlockSpec(shape, lambda *_: (0,) * nd, pipeline_mode=pl.Buffered(1))


def _ln(x, g, b):
    mu = jnp.mean(x, axis=-1, keepdims=True)
    xc = x - mu
    var = jnp.mean(xc * xc, axis=-1, keepdims=True)
    return xc * lax.rsqrt(var + LN_EPS) * g + b


def _ffn(xb, wg_ref, wu_ref, wd_ref):
    acc = None
    for c0 in range(0, D_FF, FF_CHUNK):
        c1 = min(c0 + FF_CHUNK, D_FF)
        g = jnp.dot(xb, wg_ref[:, c0:c1], preferred_element_type=f32)
        u = jnp.dot(xb, wu_ref[:, c0:c1], preferred_element_type=f32)
        a = (jax.nn.silu(g) * u).astype(bf16)
        part = jnp.dot(a, wd_ref[c0:c1, :], preferred_element_type=f32)
        acc = part if acc is None else acc + part
    return acc


def _pre_kernel(alpha, q_scale, feature_major, x_ref, wg_ref, wu_ref, wd_ref, lng_ref, lnb_ref, win_ref,
                rc_ref, rs_ref,
                h_ref, qa_ref, ka_ref, va_ref, kab_ref, vab_ref, qi_ref, ki_ref, ki2_ref, wi_ref,
                qr_ref, kr_ref, vr_ref, gr_ref):
    act = qa_ref.dtype
    x = x_ref[...]
    f = _ffn(x.astype(bf16), wg_ref, wu_ref, wd_ref)
    h = _ln(alpha * x + 0.5 * f, lng_ref[...], lnb_ref[...])
    h_ref[...] = h
    hb = h.astype(bf16)

    def proj(c0, c1):
        return jnp.dot(hb, win_ref[:, c0:c1], preferred_element_type=f32)

    qa_ref[...] = (proj(C_QA, C_KA) * q_scale).astype(act)
    ka = proj(C_KA, C_VA)
    kab_ref[...] = ka.astype(act)
    va = proj(C_VA, C_QI)
    vab_ref[...] = va.astype(act)
    qi_ref[...] = (proj(C_QI, C_KI2) * DH_IDX ** -0.5).astype(act)
    kk = proj(C_KI2, C_WI)
    if feature_major:
        ka_ref[...] = ka.T
        va_ref[...] = va.T
        ki_ref[...] = kk.T[:DH_IDX, :]
    else:
        ka_ref[...] = ka
        va_ref[...] = va
        ki_ref[...] = kk[:, :DH_IDX]
    ki2_ref[...] = kk.astype(act)
    wi_ref[...] = proj(C_WI, C_QR)
    qr = proj(C_QR, C_KR)
    kr = proj(C_KR, C_VR)
    c = rc_ref[...]
    s = rs_ref[...]
    for hh in range(H_R):
        sl = slice(DK_R * hh, DK_R * (hh + 1))
        qh = qr[:, sl]
        kh = kr[:, sl]
        qr_ref[:, sl] = (qh * c + pltpu.roll(qh, DK_R // 2, 1) * s).astype(act)
        kr_ref[:, sl] = ((kh * c + pltpu.roll(kh, DK_R // 2, 1) * s) * DK_R ** -0.5).astype(act)
    vr_ref[...] = proj(C_VR, C_GR).astype(act)
    gr_ref[...] = proj(C_GR, C_END)


def _pre_call(x, w, rot_c, rot_s, alpha, q_scale, tm, act, seq_len, name):
    n = x.shape[0]
    grid = (pl.cdiv(n, tm),)
    row = lambda width: pl.BlockSpec((tm, width), lambda i: (i, 0))
    rot_blocks = rot_c.shape[0] // tm
    rot = pl.BlockSpec((tm, LANES), lambda i: (i % rot_blocks, 0))
    feature_major = seq_len is not None
    if feature_major:
        assert seq_len % tm == 0 and n % seq_len == 0
        seq_blocks = seq_len // tm
        kv_shape = lambda width: jax.ShapeDtypeStruct((n // seq_len, width, seq_len), f32)
        kv_spec = lambda width: pl.BlockSpec((None, width, tm), lambda i: (i // seq_blocks, 0, i % seq_blocks))
    else:
        kv_shape = lambda width: jax.ShapeDtypeStruct((n, width), f32)
        kv_spec = row
    in_specs = [
        row(D_MODEL),
        _const_spec((D_MODEL, D_FF)), _const_spec((D_MODEL, D_FF)), _const_spec((D_FF, D_MODEL)),
        _const_spec((1, D_MODEL)), _const_spec((1, D_MODEL)),
        _const_spec((D_MODEL, C_END)),
        rot, rot,
    ]
    outs = [
        ("h", D_MODEL, f32), ("qa", W_A, act), ("ka", W_A, f32), ("va", W_A, f32), ("kab", W_A, act),
        ("vab", W_A, act), ("qi", W_A, act), ("ki", DH_IDX, f32), ("ki2", LANES, act), ("wi", LANES, f32),
        ("qr", W_R, act), ("kr", W_R, act), ("vr", W_R, act), ("gr", W_R, f32),
    ]
    kv_names = ("ka", "va", "ki")
    out_shape = [kv_shape(wd) if k in kv_names else jax.ShapeDtypeStruct((n, wd), dt) for k, wd, dt in outs]
    out_specs = [kv_spec(wd) if k in kv_names else row(wd) for k, wd, _ in outs]
    res = pl.pallas_call(
        functools.partial(_pre_kernel, alpha, q_scale, feature_major),
        grid=grid, in_specs=in_specs, out_specs=out_specs, out_shape=out_shape,
        compiler_params=pltpu.CompilerParams(dimension_semantics=("arbitrary",), vmem_limit_bytes=52 * MIB),
        name=name,
    )(x, w["ffn1_wg"], w["ffn1_wu"], w["ffn1_wd"], w["ln1_g"], w["ln1_b"], w["w_in"], rot_c, rot_s)
    return {k: v for (k, _, _), v in zip(outs, res)}


def _post_kernel(alpha, h_ref, oa_ref, yr_ref, wo_ref, l2g_ref, l2b_ref, wg_ref, wu_ref, wd_ref,
                 l3g_ref, l3b_ref, out_ref):
    h = h_ref[...]
    mix = (jnp.dot(oa_ref[...].astype(bf16), wo_ref[0:W_A, :], preferred_element_type=f32)
           + jnp.dot(yr_ref[...].astype(bf16), wo_ref[W_A:W_A + W_R, :], preferred_element_type=f32))
    h2 = _ln(alpha * h + mix, l2g_ref[...], l2b_ref[...])
    f = _ffn(h2.astype(bf16), wg_ref, wu_ref, wd_ref)
    out_ref[...] = _ln(alpha * h2 + 0.5 * f, l3g_ref[...], l3b_ref[...])


def _post_call(h, oa, yr, w, alpha, tm, name):
    n = h.shape[0]
    row = lambda width: pl.BlockSpec((tm, width), lambda i: (i, 0))
    in_specs = [
        row(D_MODEL), row(W_A), row(W_R),
        _const_spec((W_A + W_R, D_MODEL)), _const_spec((1, D_MODEL)), _const_spec((1, D_MODEL)),
        _const_spec((D_MODEL, D_FF)), _const_spec((D_MODEL, D_FF)), _const_spec((D_FF, D_MODEL)),
        _const_spec((1, D_MODEL)), _const_spec((1, D_MODEL)),
    ]
    return pl.pallas_call(
        functools.partial(_post_kernel, alpha),
        grid=(pl.cdiv(n, tm),), in_specs=in_specs, out_specs=row(D_MODEL),
        out_shape=jax.ShapeDtypeStruct((n, D_MODEL), f32),
        compiler_params=pltpu.CompilerParams(dimension_semantics=("arbitrary",), vmem_limit_bytes=48 * MIB),
        name=name,
    )(h, oa, yr, w["w_out"], w["ln2_g"], w["ln2_b"], w["ffn2_wg"], w["ffn2_wu"], w["ffn2_wd"],
      w["ln3_g"], w["ln3_b"])


def _attn_kernel(topk, qa_ref, qi_ref, wi_ref, k_ref, v_ref, ki2_ref, bias_ref, oa_ref,
                 sc_ref, qap_ref, qip_ref, wb_ref, m_ref, l_ref, acc_ref):
    tq = qa_ref.shape[0]
    ncb = tq // LANES
    i = pl.program_id(1)
    nj = i + 1
    kf = float(topk)

    lane = lax.broadcasted_iota(jnp.int32, (tq, LANES), 1)
    lo_half = lane < DH_A
    for h in range(H_A):
        p = h // 2
        keep = lo_half if h % 2 == 0 else jnp.logical_not(lo_half)
        blk = slice(LANES * p, LANES * (p + 1))
        qip_ref[h] = jnp.where(keep, qi_ref[:, blk], jnp.zeros((), qi_ref.dtype))
        qap_ref[h] = jnp.where(keep, qa_ref[:, blk], jnp.zeros((), qa_ref.dtype))
        wb_ref[h] = jnp.broadcast_to(wi_ref[:, h:h + 1] * H_IDX ** -0.5, (tq, LANES))

    def score_chunk(j, carry):
        off = pl.multiple_of(j * tq, tq)
        kj = ki2_ref[pl.ds(off, tq), :]
        accs = [jnp.zeros((tq, LANES), f32) for _ in range(ncb)]
        for h in range(H_IDX):
            d = lax.dot_general(qip_ref[h], kj, NT_DIMS, preferred_element_type=f32)
            wb = wb_ref[h]
            for cb in range(ncb):
                accs[cb] = accs[cb] + wb * jnp.maximum(d[:, cb * LANES:(cb + 1) * LANES], 0.0)
        for cb in range(ncb):
            sc_ref[j, :, cb * LANES:(cb + 1) * LANES] = accs[cb]
        return carry

    lax.fori_loop(0, nj, score_chunk, 0)

    row = lax.broadcasted_iota(jnp.int32, (tq, tq), 0)
    col = lax.broadcasted_iota(jnp.int32, (tq, tq), 1)
    causal = col <= row
    sd = sc_ref[i]
    mn_diag = jnp.min(jnp.where(causal, sd, jnp.inf), axis=1, keepdims=True)
    sc_ref[i] = jnp.where(causal, sd, -jnp.inf)

    rb = min(SELECT_ROWS, tq)

    def fold(fn, init, comb, n_chunks, t=None):
        outs = []
        for r0 in range(0, tq, rb):
            t_blk = None if t is None else t[r0:r0 + rb]

            def body(j, acc, r0=r0, t_blk=t_blk):
                for cb in range(ncb):
                    acc = comb(acc, fn(sc_ref[j, r0:r0 + rb, cb * LANES:(cb + 1) * LANES], t_blk))
                return acc
            outs.append(lax.fori_loop(0, n_chunks, body, jnp.full((rb, LANES), init, f32)))
        return jnp.concatenate(outs, axis=0) if len(outs) > 1 else outs[0]

    def tile_lanes(x):
        return jnp.concatenate([x] * ncb, axis=1) if ncb > 1 else x

    def lane_sum(x):
        return jnp.broadcast_to(jnp.sum(x, axis=1, keepdims=True), (tq, LANES))

    def lane_max(x):
        return jnp.broadcast_to(jnp.max(x, axis=1, keepdims=True), (tq, LANES))

    def count_ge(t):
        return lane_sum(fold(lambda s, tb: jnp.where(s >= tb, 1.0, 0.0), 0.0, jnp.add, nj, t))

    def count_gt(t):
        return lane_sum(fold(lambda s, tb: jnp.where(s > tb, 1.0, 0.0), 0.0, jnp.add, nj, t))

    def max_below(t):
        return lane_max(fold(lambda s, tb: jnp.where(s < tb, s, -jnp.inf), -jnp.inf, jnp.maximum, nj, t))

    n_keys = (lax.broadcasted_iota(jnp.int32, (tq, LANES), 0) + (i * tq + 1)).astype(f32)
    take_all = n_keys <= kf

    mx = lane_max(fold(lambda s, _: s, -jnp.inf, jnp.maximum, nj))
    mn_far = -lane_max(-fold(lambda s, _: s, jnp.inf, jnp.minimum, i))
    mn = jnp.minimum(mn_far, jnp.broadcast_to(mn_diag, (tq, LANES)))

    @pl.when((i + 1) * tq <= topk)
    def _():
        def mk(j, carry):
            sc_ref[j] = jnp.zeros((tq, tq), f32)
            return carry
        lax.fori_loop(0, nj, mk, 0)

    @pl.when((i + 1) * tq > topk)
    def _():
        c_max = count_ge(mx)
        done0 = jnp.logical_or(take_all, c_max >= kf)
        thr0 = jnp.where(take_all, -jnp.inf, mx)
        cge0 = jnp.where(take_all, n_keys, c_max)

        def bis(_, st):
            lo, hi = st
            mid = 0.5 * (lo + hi)
            ge = count_ge(mid) >= kf
            return jnp.where(ge, mid, lo), jnp.where(ge, hi, mid)

        lo, hi = lax.fori_loop(0, BISECT_ITERS, bis, (mn, mx))

        def snap_cond(st):
            return st[0] > 0.0

        def snap_body(st):
            _, hi, thr, cge, done = st
            m = max_below(hi)
            c = count_ge(m)
            ok = c >= kf
            newly = jnp.logical_and(ok, done < 0.5)
            thr = jnp.where(newly, m, thr)
            cge = jnp.where(newly, c, cge)
            hi = jnp.where(jnp.logical_or(done > 0.5, ok), hi, m)
            done = jnp.where(ok, 1.0, done)
            left = jnp.sum(1.0 - done[:, 0:1])
            return left, hi, thr, cge, done

        done_f = jnp.where(done0, 1.0, 0.0)
        left0 = jnp.sum(1.0 - done_f[:, 0:1])
        _, _, thr, cge, _ = lax.while_loop(snap_cond, snap_body, (left0, hi, thr0, cge0, done_f))

        excess = jnp.sum(jnp.where(cge[:, 0:1] > kf, 1.0, 0.0)) > 0.0
        thr_t = tile_lanes(thr)

        @pl.when(jnp.logical_not(excess))
        def _():
            def mk(j, carry):
                sc_ref[j] = jnp.where(sc_ref[j] >= thr_t, 0.0, NEG)
                return carry
            lax.fori_loop(0, nj, mk, 0)

        @pl.when(excess)
        def _():
            need = tile_lanes(kf - count_gt(thr))
            upper = jnp.where(row <= col, 1.0, 0.0).astype(bf16)

            def mk(j, seen):
                s = sc_ref[j]
                tie = s == thr_t
                tie_f = jnp.where(tie, 1.0, 0.0)
                rank = seen + jnp.dot(tie_f.astype(bf16), upper, preferred_element_type=f32)
                sel = jnp.logical_or(s > thr_t, jnp.logical_and(tie, rank <= need))
                sc_ref[j] = jnp.where(sel, 0.0, NEG)
                part = tie_f[:, 0:LANES]
                for cb in range(1, ncb):
                    part = part + tie_f[:, cb * LANES:(cb + 1) * LANES]
                return seen + tile_lanes(lane_sum(part))
            lax.fori_loop(0, nj, mk, jnp.zeros((tq, tq), f32))

    sc_ref[i] = jnp.where(causal, sc_ref[i], NEG)


    m_ref[...] = jnp.full(m_ref.shape, -jnp.inf, f32)
    l_ref[...] = jnp.zeros(l_ref.shape, f32)
    acc_ref[...] = jnp.zeros(acc_ref.shape, f32)

    def attend_chunk(j, bias_cols):
        off = pl.multiple_of(j * tq, tq)
        sel_mask = sc_ref[j]
        for h in range(H_A):
            blk = slice(LANES * (h // 2), LANES * (h // 2 + 1))
            s = lax.dot_general(qap_ref[h], k_ref[pl.ds(off, tq), blk], NT_DIMS, preferred_element_type=f32)
            s = s + sel_mask
            if bias_cols is not None:
                s = s + bias_ref[h, :, bias_cols]
            m_old = m_ref[h]
            cmax = s[:, 0:LANES]
            for cb in range(1, ncb):
                cmax = jnp.maximum(cmax, s[:, cb * LANES:(cb + 1) * LANES])
            m_new = jnp.maximum(m_old, lane_max(cmax))
            a = jnp.exp2(m_old - m_new)
            e = jnp.exp2(s - tile_lanes(m_new))
            esum = e[:, 0:LANES]
            for cb in range(1, ncb):
                esum = esum + e[:, cb * LANES:(cb + 1) * LANES]
            l_ref[h] = l_ref[h] * a + esum
            acc_ref[h] = acc_ref[h] * a + jnp.dot(e.astype(bf16), v_ref[pl.ds(off, tq), blk],
                                                  preferred_element_type=f32)
            m_ref[h] = m_new

    def far(j, carry):
        attend_chunk(j, None)
        return carry
    lax.fori_loop(0, i - 1, far, 0)

    @pl.when(i >= 1)
    def _():
        attend_chunk(i - 1, slice(0, tq))
    attend_chunk(i, slice(tq, 2 * tq))

    for p in range(H_A // 2):
        o_even = acc_ref[2 * p] / lane_sum(l_ref[2 * p])
        o_odd = acc_ref[2 * p + 1] / lane_sum(l_ref[2 * p + 1])
        oa_ref[:, LANES * p:LANES * (p + 1)] = jnp.where(lo_half, o_even, o_odd).astype(oa_ref.dtype)


def _attn_call(qa, qi, wi, kab, vab, ki2, bias_tiles, topk, tq):
    b, s, _ = qa.shape
    nq = s // tq
    qspec = lambda width: pl.BlockSpec((None, tq, width), lambda bi, i: (bi, i, 0))
    kvspec = lambda width: pl.BlockSpec((None, s, width), lambda bi, i: (bi, 0, 0), pipeline_mode=pl.Buffered(1))
    scratch = [
        pltpu.VMEM((nq, tq, tq), f32),
        pltpu.VMEM((H_A, tq, LANES), qa.dtype),
        pltpu.VMEM((H_IDX, tq, LANES), qi.dtype),
        pltpu.VMEM((H_IDX, tq, LANES), f32),
        pltpu.VMEM((H_A, tq, LANES), f32),
        pltpu.VMEM((H_A, tq, LANES), f32),
        pltpu.VMEM((H_A, tq, LANES), f32),
    ]
    return pl.pallas_call(
        functools.partial(_attn_kernel, topk),
        grid=(b, nq),
        in_specs=[qspec(W_A), qspec(W_A), qspec(LANES), kvspec(W_A), kvspec(W_A), kvspec(LANES),
                  _const_spec((H_A, tq, 2 * tq))],
        out_specs=qspec(W_A),
        out_shape=jax.ShapeDtypeStruct((b, s, W_A), bf16),
        scratch_shapes=scratch,
        compiler_params=pltpu.CompilerParams(dimension_semantics=("arbitrary", "arbitrary"),
                                             vmem_limit_bytes=48 * MIB),
        name="attn_prompt",
    )(qa, qi, wi, kab, vab, ki2, bias_tiles)


def _ret_kernel(qr_ref, kr_ref, vr_ref, gr_ref, gng_ref, dmat_ref, cross_ref, kvd_ref, cdec_ref,
                yr_ref, st_ref):
    c = pl.program_id(1)

    @pl.when(c == 0)
    def _():
        st_ref[...] = jnp.zeros(st_ref.shape, f32)

    for h in range(H_R):
        sl = slice(DK_R * h, DK_R * (h + 1))
        q = qr_ref[:, sl]
        k = kr_ref[:, sl]
        v = vr_ref[:, sl]
        att = lax.dot_general(q, k, NT_DIMS, preferred_element_type=f32) * dmat_ref[h]
        st = st_ref[h]
        o = (jnp.dot(att.astype(bf16), v, preferred_element_type=f32)
             + jnp.dot(q, st.astype(bf16), preferred_element_type=f32) * cross_ref[h])
        kd = (k.astype(f32) * kvd_ref[h]).T.astype(bf16)
        st_ref[h] = cdec_ref[h, 0:1, :] * st + jnp.dot(kd, v, preferred_element_type=f32)
        mu = jnp.mean(o, axis=-1, keepdims=True)
        oc = o - mu
        var = jnp.mean(oc * oc, axis=-1, keepdims=True)
        yn = oc * lax.rsqrt(var + GN_EPS) * gng_ref[:, sl]
        yr_ref[:, sl] = (jax.nn.silu(gr_ref[:, sl]) * yn).astype(yr_ref.dtype)


def _ret_call(qr, kr, vr, gr, gng, dec, chunk):
    b, s, _ = qr.shape
    nc = s // chunk
    rspec = pl.BlockSpec((None, chunk, W_R), lambda bi, c: (bi, c, 0))
    return pl.pallas_call(
        _ret_kernel,
        grid=(b, nc),
        in_specs=[rspec, rspec, rspec, rspec, _const_spec((1, W_R)),
                  _const_spec((H_R, chunk, chunk)), _const_spec((H_R, chunk, LANES)),
                  _const_spec((H_R, chunk, LANES)), _const_spec((H_R, 8, LANES))],
        out_specs=[rspec, pl.BlockSpec((None, H_R, DK_R, DV_R), lambda bi, c: (bi, 0, 0, 0))],
        out_shape=[jax.ShapeDtypeStruct((b, s, W_R), bf16), jax.ShapeDtypeStruct((b, H_R, DK_R, DV_R), f32)],
        compiler_params=pltpu.CompilerParams(dimension_semantics=("arbitrary", "arbitrary"),
                                             vmem_limit_bytes=32 * MIB),
        name="ret_prompt",
    )(qr, kr, vr, gr, gng, dec["intra"], dec["cross"], dec["kv"], dec["chunk"])


def _sscore_kernel(layer, npages, pt_ref, ck_ref, q_ref, w_ref, kn_ref, sc_ref, snew_ref, kbuf, sem):
    b = pl.program_id(0)
    nb = pl.num_programs(0)
    slot = b % 2

    def page_copy(seq, p, sl):
        page = pt_ref[seq * npages + p]
        return pltpu.make_async_copy(ck_ref.at[layer, page], kbuf.at[sl, p], sem.at[sl])

    def start_all(seq, sl):
        def body(p, carry):
            page_copy(seq, p, sl).start()
            return carry
        lax.fori_loop(0, npages, body, 0)

    @pl.when(b == 0)
    def _():
        start_all(0, 0)

    @pl.when(b + 1 < nb)
    def _():
        start_all(b + 1, 1 - slot)

    def wait_body(p, carry):
        page_copy(b, p, slot).wait()
        return carry
    lax.fori_loop(0, npages, wait_body, 0)

    q = q_ref[...].astype(bf16)
    w = w_ref[...]

    pages_per_dot = min(8, npages)
    for p0 in range(0, npages, pages_per_dot):
        kw = jnp.concatenate([kbuf[slot, p] for p in range(p0, p0 + pages_per_dot)], axis=1).astype(bf16)
        d = jnp.dot(q, kw, preferred_element_type=f32)
        sc_ref[:, p0 * PAGE_SIZE:(p0 + pages_per_dot) * PAGE_SIZE] = jnp.sum(
            w * jnp.maximum(d, 0.0), axis=0, keepdims=True)

    kn = kn_ref[...].astype(bf16).astype(f32)
    dn = jnp.sum(q.astype(f32) * kn, axis=1, keepdims=True)
    s_new = jnp.sum(w * jnp.maximum(dn, 0.0), axis=0, keepdims=True)
    snew_ref[...] = jnp.broadcast_to(s_new, (1, LANES))


def _sscore_call(page_table_flat, cache_kidx_t, q16, w16, kn, layer, npages):
    db = q16.shape[0]
    assert npages % min(8, npages) == 0
    grid_spec = pltpu.PrefetchScalarGridSpec(
        num_scalar_prefetch=1,
        grid=(db,),
        in_specs=[
            pl.BlockSpec(memory_space=pl.ANY),
            pl.BlockSpec((None, 16, DH_IDX), lambda b, pt: (b, 0, 0)),
            pl.BlockSpec((None, 16, 1), lambda b, pt: (b, 0, 0)),
            pl.BlockSpec((None, 1, DH_IDX), lambda b, pt: (b, 0, 0)),
        ],
        out_specs=[pl.BlockSpec((None, 1, npages * PAGE_SIZE), lambda b, pt: (b, 0, 0)),
                   pl.BlockSpec((None, 1, LANES), lambda b, pt: (b, 0, 0))],
        scratch_shapes=[
            pltpu.VMEM((2, npages, DH_IDX, PAGE_SIZE), f32),
            pltpu.SemaphoreType.DMA((2,)),
        ],
    )
    return pl.pallas_call(
        functools.partial(_sscore_kernel, layer, npages),
        grid_spec=grid_spec,
        out_shape=[jax.ShapeDtypeStruct((db, 1, npages * PAGE_SIZE), f32),
                   jax.ShapeDtypeStruct((db, 1, LANES), f32)],
        compiler_params=pltpu.CompilerParams(dimension_semantics=("arbitrary",), vmem_limit_bytes=32 * MIB),
        name="sample_scores",
    )(page_table_flat, cache_kidx_t, q16, w16, kn)


PREFIX_CHUNK = 256


def _ssel_kernel(topk, sc_ref, sn_ref, mask_ref, selnew_ref):
    kf = float(topk)
    db, length = sc_ref.shape
    sc = sc_ref[...]
    s_new = sn_ref[:, 0:1]

    def rsum(x):
        return jnp.sum(x, axis=1, keepdims=True)

    def count(cmp, t):
        return rsum(jnp.where(cmp(sc, t), 1.0, 0.0)) + jnp.where(cmp(s_new, t), 1.0, 0.0)

    ge = lambda a, t: a >= t
    gt = lambda a, t: a > t
    mx = jnp.maximum(jnp.max(sc, axis=1, keepdims=True), s_new)
    mn = jnp.minimum(jnp.min(sc, axis=1, keepdims=True), s_new)
    c_max = count(ge, mx)
    done0 = jnp.where(c_max >= kf, 1.0, 0.0)

    def bis(_, st):
        lo, hi = st
        mid = 0.5 * (lo + hi)
        ok = count(ge, mid) >= kf
        return jnp.where(ok, mid, lo), jnp.where(ok, hi, mid)
    lo, hi = lax.fori_loop(0, BISECT_ITERS, bis, (mn, mx))

    def snap_cond(st):
        return st[0] > 0.0

    def snap_body(st):
        _, hi, thr, done = st
        below = jnp.maximum(jnp.max(jnp.where(sc < hi, sc, -jnp.inf), axis=1, keepdims=True),
                            jnp.where(s_new < hi, s_new, -jnp.inf))
        ok = count(ge, below) >= kf
        newly = jnp.logical_and(ok, done < 0.5)
        thr = jnp.where(newly, below, thr)
        hi = jnp.where(jnp.logical_or(done > 0.5, ok), hi, below)
        done = jnp.where(ok, 1.0, done)
        return jnp.sum(1.0 - done), hi, thr, done
    _, _, thr, _ = lax.while_loop(snap_cond, snap_body, (jnp.sum(1.0 - done0), hi, mx, done0))

    need = kf - count(gt, thr)
    pc = min(PREFIX_CHUNK, length)
    r_i = lax.broadcasted_iota(jnp.int32, (pc, pc), 0)
    c_i = lax.broadcasted_iota(jnp.int32, (pc, pc), 1)
    upper = jnp.where(r_i <= c_i, 1.0, 0.0).astype(bf16)
    seen = jnp.zeros((db, 1), f32)
    for c0 in range(0, length, pc):
        s_c = sc[:, c0:c0 + pc]
        tie = s_c == thr
        rank = seen + jnp.dot(jnp.where(tie, 1.0, 0.0).astype(bf16), upper, preferred_element_type=f32)
        sel = jnp.logical_or(s_c > thr, jnp.logical_and(tie, rank <= need))
        mask_ref[:, c0:c0 + pc] = jnp.where(sel, 0.0, NEG)
        seen = rank[:, pc - 1:pc]
    sel_new = jnp.logical_or(s_new > thr, jnp.logical_and(s_new == thr, seen + 1.0 <= need))
    selnew_ref[...] = jnp.broadcast_to(jnp.where(sel_new, 1.0, 0.0), (db, LANES))


def _ssel_call(scores, s_new, topk):
    db, length = scores.shape
    assert length % min(PREFIX_CHUNK, length) == 0
    return pl.pallas_call(
        functools.partial(_ssel_kernel, topk),
        out_shape=[jax.ShapeDtypeStruct((db, length), f32), jax.ShapeDtypeStruct((db, LANES), f32)],
        compiler_params=pltpu.CompilerParams(vmem_limit_bytes=32 * MIB),
        name="sample_select",
    )(scores, s_new)


def _satt_kernel(layer, npages, grp, pt_ref,
                 ckk_ref, ckv_ref, qt_ref, knt_ref, vnt_ref, bias_ref, bias0_ref, mask_ref, snew_ref,
                 qr_ref, kr_ref, vr_ref, gr_ref, gng_ref, gam_ref, s0_ref,
                 ot_ref, yr_ref, sn_ref,
                 kbuf, vbuf, sem, m_ref, l_ref, acc_ref):
    b = pl.program_id(0)
    g = pl.program_id(1)
    nb = pl.num_programs(0)
    ng = pl.num_programs(1)
    t = b * ng + g
    slot = t % 2

    def page_copies(seq, gi, p, sl):
        page = pt_ref[seq * npages + gi * grp + p]
        return (pltpu.make_async_copy(ckk_ref.at[layer, page], kbuf.at[sl, p], sem.at[sl, 0]),
                pltpu.make_async_copy(ckv_ref.at[layer, page], vbuf.at[sl, p], sem.at[sl, 1]))

    def start_all(seq, gi, sl):
        def body(p, carry):
            ck, cv = page_copies(seq, gi, p, sl)
            ck.start()
            cv.start()
            return carry
        lax.fori_loop(0, grp, body, 0)

    @pl.when(t == 0)
    def _():
        start_all(0, 0, 0)

    @pl.when(t + 1 < nb * ng)
    def _():
        wrap = g + 1 == ng
        start_all(jnp.where(wrap, b + 1, b), jnp.where(wrap, 0, g + 1), 1 - slot)

    @pl.when(g == 0)
    def _():
        m_ref[...] = jnp.full(m_ref.shape, -jnp.inf, f32)
        l_ref[...] = jnp.zeros(l_ref.shape, f32)
        acc_ref[...] = jnp.zeros(acc_ref.shape, f32)

    def wait_body(p, carry):
        ck, cv = page_copies(b, g, p, slot)
        ck.wait()
        cv.wait()
        return carry
    lax.fori_loop(0, grp, wait_body, 0)

    mask = mask_ref[...]
    for h in range(H_A):
        qc = qt_ref[:, h:h + 1]
        kh = kbuf[slot, :, h]
        lg = jnp.sum(kh * qc[None], axis=1, keepdims=True) + bias_ref[h] + mask
        m_old = m_ref[h]
        m_blk = jnp.max(jnp.max(lg, axis=0), axis=1, keepdims=True)
        m_new = jnp.maximum(m_old, m_blk)
        a = jnp.exp(m_old - m_new)
        e = jnp.exp(lg - m_new[None])
        l_ref[h] = l_ref[h] * a + jnp.sum(e, axis=0)
        acc_ref[h] = acc_ref[h] * a + jnp.sum(vbuf[slot, :, h] * e, axis=0)
        m_ref[h] = m_new

    @pl.when(g == ng - 1)
    def _():
        lg_new = jnp.sum(qt_ref[...] * knt_ref[...], axis=0, keepdims=True) + bias0_ref[...]
        lg_new = jnp.where(snew_ref[0:1, 0:1] > 0.5, lg_new, NEG)
        for h in range(H_A):
            m_c = m_ref[h][:, 0:1]
            lg_h = lg_new[:, h:h + 1]
            m_f = jnp.maximum(m_c, lg_h)
            a = jnp.exp(m_c - m_f)
            e_new = jnp.exp(lg_h - m_f)
            den = jnp.sum(l_ref[h], axis=1, keepdims=True) * a + e_new
            num = jnp.sum(acc_ref[h], axis=1, keepdims=True) * a + e_new * vnt_ref[:, h:h + 1]
            ot_ref[:, h:h + 1] = num / den

    @pl.when(g == 0)
    def _():
        r_i = lax.broadcasted_iota(jnp.int32, (DK_R, DK_R), 0)
        c_i = lax.broadcasted_iota(jnp.int32, (DK_R, DK_R), 1)
        eye = jnp.where(r_i == c_i, 1.0, 0.0)
        for h in range(H_R):
            qrow = qr_ref[h:h + 1, :]
            krow = kr_ref[h:h + 1, :]
            vrow = vr_ref[h:h + 1, :]
            gam = gam_ref[h:h + 1, :]
            qcol = jnp.sum(eye * qrow, axis=1, keepdims=True)
            kcol = jnp.sum(eye * krow, axis=1, keepdims=True)
            st = s0_ref[h]
            qk = jnp.sum(qrow * krow, axis=1, keepdims=True)
            o = qk * vrow + gam * jnp.sum(qcol * st, axis=0, keepdims=True)
            sn_ref[h] = gam * st + kcol * vrow
            mu = jnp.mean(o, axis=1, keepdims=True)
            oc = o - mu
            var = jnp.mean(oc * oc, axis=1, keepdims=True)
            yn = oc * lax.rsqrt(var + GN_EPS) * gng_ref[h:h + 1, :]
            yr_ref[h:h + 1, :] = jax.nn.silu(gr_ref[h:h + 1, :]) * yn


def _satt_call(pt_flat, cache_k_t, cache_v_t, qt, knt, vnt, bias_pos, bias0, mask, snew, qr, kr, vr, gr, gng, gam,
               state, layer, npages, grp):
    db = qt.shape[0]
    ng = npages // grp
    per_seq = lambda d0, d1: pl.BlockSpec((None, d0, d1), lambda b, g, pt: (b, 0, 0))
    grid_spec = pltpu.PrefetchScalarGridSpec(
        num_scalar_prefetch=1,
        grid=(db, ng),
        in_specs=[
            pl.BlockSpec(memory_space=pl.ANY), pl.BlockSpec(memory_space=pl.ANY),
            per_seq(DH_A, H_A), per_seq(DH_A, H_A), per_seq(DH_A, H_A),
            pl.BlockSpec((H_A, grp, 1, PAGE_SIZE), lambda b, g, pt: (0, g, 0, 0)),
            pl.BlockSpec((1, H_A), lambda b, g, pt: (0, 0)),
            pl.BlockSpec((None, grp, 1, PAGE_SIZE), lambda b, g, pt: (b, g, 0, 0)),
            per_seq(1, LANES),
            per_seq(H_R, DK_R), per_seq(H_R, DK_R), per_seq(H_R, DV_R), per_seq(H_R, DV_R),
            pl.BlockSpec((H_R, DV_R), lambda b, g, pt: (0, 0)),
            pl.BlockSpec((H_R, DV_R), lambda b, g, pt: (0, 0)),
            pl.BlockSpec((None, None, H_R, DK_R, DV_R), lambda b, g, pt: (layer, b, 0, 0, 0)),
        ],
        out_specs=[per_seq(DH_A, H_A), per_seq(H_R, DV_R),
                   pl.BlockSpec((None, H_R, DK_R, DV_R), lambda b, g, pt: (b, 0, 0, 0))],
        scratch_shapes=[
            pltpu.VMEM((2, grp, H_A, DH_A, PAGE_SIZE), f32),
            pltpu.VMEM((2, grp, H_A, DH_A, PAGE_SIZE), f32),
            pltpu.SemaphoreType.DMA((2, 2)),
            pltpu.VMEM((H_A, 1, PAGE_SIZE), f32),
            pltpu.VMEM((H_A, 1, PAGE_SIZE), f32),
            pltpu.VMEM((H_A, DH_A, PAGE_SIZE), f32),
        ],
    )
    return pl.pallas_call(
        functools.partial(_satt_kernel, layer, npages, grp),
        grid_spec=grid_spec,
        out_shape=[jax.ShapeDtypeStruct((db, DH_A, H_A), f32), jax.ShapeDtypeStruct((db, H_R, DV_R), f32),
                   jax.ShapeDtypeStruct((db, H_R, DK_R, DV_R), f32)],
        compiler_params=pltpu.CompilerParams(dimension_semantics=("arbitrary", "arbitrary"),
                                             vmem_limit_bytes=40 * MIB),
        name="sample_attend",
    )(pt_flat, cache_k_t, cache_v_t, qt, knt, vnt, bias_pos, bias0, mask, snew, qr, kr, vr, gr, gng, gam, state)


def _t5_bucket(rel):
    n = jnp.maximum(rel, 0)
    max_exact = NUM_BUCKETS // 2
    nf = jnp.maximum(n, 1).astype(f32)
    large = max_exact + (jnp.log(nf / max_exact) / math.log(MAX_DISTANCE / max_exact)
                         * (NUM_BUCKETS - max_exact)).astype(jnp.int32)
    large = jnp.minimum(large, NUM_BUCKETS - 1)
    return jnp.where(n < max_exact, n, large)


def _rotary_tables(pos):
    half = DK_R // 2
    freqs = ROPE_BASE ** (-jnp.arange(half, dtype=f32) / half)
    ang = pos.astype(f32)[:, None] * freqs[None, :]
    cos, sin = jnp.cos(ang), jnp.sin(ang)
    return jnp.concatenate([cos, cos], axis=1), jnp.concatenate([-sin, sin], axis=1)


def _decay_tables(chunk):
    lg = jnp.log1p(-jnp.exp2(-5.0 - jnp.arange(H_R, dtype=f32)))
    i = jnp.arange(chunk, dtype=f32)
    diff = i[:, None] - i[None, :]
    causal = diff >= 0
    intra = jnp.where(causal[None], jnp.exp(jnp.where(causal, diff, 0.0)[None] * lg[:, None, None]), 0.0)
    cross = jnp.exp((i[None, :] + 1.0) * lg[:, None])
    kv = jnp.exp((chunk - 1.0 - i)[None, :] * lg[:, None])
    cdec = jnp.exp(chunk * lg)
    bc = lambda a: jnp.broadcast_to(a[:, :, None], (H_R, chunk, LANES))
    return {"intra": intra, "cross": bc(cross), "kv": bc(kv),
            "chunk": jnp.broadcast_to(cdec[:, None, None], (H_R, 8, LANES))}


def _bias_tiles(rel_bias, tq):
    far_n = np.float32(tq + 1)
    far_bucket = 16 + int(np.float32(np.log(far_n / np.float32(16.0))) / math.log(8.0) * 16)
    assert far_bucket >= NUM_BUCKETS - 1, "keys beyond the previous chunk must share the last bucket"
    r = jnp.arange(tq, dtype=jnp.int32)[:, None]
    c = jnp.arange(2 * tq, dtype=jnp.int32)[None, :]
    rb = rel_bias.astype(f32) - rel_bias[NUM_BUCKETS - 1].astype(f32)[None, :]
    return _bucket_lookup(rb, _t5_bucket(r + tq - c))


def _bucket_lookup(table, bucket):
    out = jnp.zeros((table.shape[1],) + bucket.shape, f32)
    for k in range(table.shape[0]):
        out = jnp.where(bucket[None] == k, table[k].reshape((-1,) + (1,) * bucket.ndim), out)
    return out


def _layer_weights(l, ffn1_wg, ffn1_wu, ffn1_wd, ln1_g, ln1_b, w_in, ret_gn_g, w_out, ln2_g, ln2_b,
                   ffn2_wg, ffn2_wu, ffn2_wd, ln3_g, ln3_b):
    pts = np.cumsum((0,) + IN_SIZES)
    wi = w_in[l]
    col = lambda k: wi[:, pts[k]:pts[k + 1]]
    zeros = jnp.zeros((D_MODEL, LANES - H_IDX), wi.dtype)
    w_in2 = jnp.concatenate([col(0), col(1), col(2), col(3), col(4), col(4), col(5), zeros,
                             col(6), col(7), col(8), col(9)], axis=1).astype(bf16)
    r2 = lambda a: a[l].reshape(1, -1).astype(f32)
    return {
        "ffn1_wg": ffn1_wg[l].astype(bf16), "ffn1_wu": ffn1_wu[l].astype(bf16), "ffn1_wd": ffn1_wd[l].astype(bf16),
        "ln1_g": r2(ln1_g), "ln1_b": r2(ln1_b), "w_in": w_in2, "gng": r2(ret_gn_g),
        "w_out": w_out[l].astype(bf16), "ln2_g": r2(ln2_g), "ln2_b": r2(ln2_b),
        "ffn2_wg": ffn2_wg[l].astype(bf16), "ffn2_wu": ffn2_wu[l].astype(bf16), "ffn2_wd": ffn2_wd[l].astype(bf16),
        "ln3_g": r2(ln3_g), "ln3_b": r2(ln3_b),
    }


def kernel(x_prompt, x_sample, cache_k, cache_v, cache_kidx, state_ret, page_table, rel_bias,
           ffn1_wg, ffn1_wu, ffn1_wd, ln1_g, ln1_b, w_in, ret_gn_g, w_out,
           ln2_g, ln2_b, ffn2_wg, ffn2_wu, ffn2_wd, ln3_g, ln3_b):
    b, s, _ = x_prompt.shape
    db, ds, _ = x_sample.shape
    depth = w_in.shape[0]
    npages = page_table.shape[1]
    past = npages * PAGE_SIZE
    assert ds == 1, "the sample group decodes one token per sequence"
    alpha = (2 * depth) ** 0.25

    tq = min(ATT_TQ, s)
    chunk = min(RET_CHUNK, s)
    assert s % tq == 0 and s % chunk == 0 and tq % LANES == 0
    topk_p = min(TOPK_MAX, s // 4)
    topk_s = min(TOPK_MAX, (past + ds) // 4)

    rot_p = _rotary_tables(jnp.arange(s, dtype=jnp.int32))
    rot_s = _rotary_tables(jnp.full((db * ds,), past, jnp.int32))
    dec_p = _decay_tables(chunk)
    gam = jnp.broadcast_to(_decay_tables(1)["chunk"][:, 0, :], (H_R, LANES))
    bias_tiles = _bias_tiles(rel_bias, tq)
    dist = past - jnp.arange(past, dtype=jnp.int32)
    bias_pos = _bucket_lookup(rel_bias.astype(f32), _t5_bucket(dist)).reshape(H_A, npages, 1, PAGE_SIZE)
    bias0 = rel_bias[0:1].astype(f32)
    pt_flat = page_table.reshape(-1).astype(jnp.int32)
    grp = min(SAMPLE_PAGE_GROUP, npages)
    assert npages % grp == 0
    ckidx_t = jnp.transpose(cache_kidx, (0, 1, 3, 2))
    ck_t = jnp.transpose(cache_k, (0, 1, 3, 4, 2))
    cv_t = jnp.transpose(cache_v, (0, 1, 3, 4, 2))

    hp = x_prompt.reshape(b * s, D_MODEL)
    hs = x_sample.reshape(db * ds, D_MODEL)
    outs = {k: [] for k in ("kp", "vp", "kip", "sp", "ks", "vs", "kis", "ss")}
    for l in range(depth):
        w = _layer_weights(l, ffn1_wg, ffn1_wu, ffn1_wd, ln1_g, ln1_b, w_in, ret_gn_g, w_out, ln2_g, ln2_b,
                           ffn2_wg, ffn2_wu, ffn2_wd, ln3_g, ln3_b)
        pp = _pre_call(hp, w, rot_p[0], rot_p[1], alpha, DH_A ** -0.5 * LOG2E, min(PRE_TM, s), bf16, s,
                       "pre_prompt")
        r3 = lambda a: a.reshape(b, s, a.shape[-1])
        oa = _attn_call(r3(pp["qa"]), r3(pp["qi"]), r3(pp["wi"]), r3(pp["kab"]), r3(pp["vab"]), r3(pp["ki2"]),
                        bias_tiles * LOG2E, topk_p, tq)
        yr, st_p = _ret_call(r3(pp["qr"]), r3(pp["kr"]), r3(pp["vr"]), r3(pp["gr"]), w["gng"], dec_p, chunk)
        hp = _post_call(pp["h"], oa.reshape(b * s, W_A), yr.reshape(b * s, W_R), w, alpha,
                        min(POST_TM, b * s), "post_prompt")
        outs["kp"].append(pp["ka"].reshape(b, H_A, DH_A, s).transpose(0, 3, 1, 2))
        outs["vp"].append(pp["va"].reshape(b, H_A, DH_A, s).transpose(0, 3, 1, 2))
        outs["kip"].append(pp["ki"].transpose(0, 2, 1))
        outs["sp"].append(st_p)

        ps = _pre_call(hs, w, rot_s[0], rot_s[1], alpha, DH_A ** -0.5, db * ds, f32, None, "pre_sample")
        q16 = jnp.pad(ps["qi"].reshape(db, H_IDX, DH_IDX), ((0, 0), (0, 16 - H_IDX), (0, 0)))
        w16 = jnp.pad((ps["wi"][:, :H_IDX] * H_IDX ** -0.5).reshape(db, H_IDX, 1), ((0, 0), (0, 16 - H_IDX), (0, 0)))
        sc_s, sn_s = _sscore_call(pt_flat, ckidx_t, q16, w16, ps["ki"].reshape(db, 1, DH_IDX), l, npages)
        mask, snew = _ssel_call(sc_s.reshape(db, past), sn_s.reshape(db, LANES), topk_s)
        t8 = lambda a: a.reshape(db, H_A, DH_A).transpose(0, 2, 1)
        r4 = lambda a: a.reshape(db, H_R, DK_R)
        ot_s, yr_s, st_s = _satt_call(
            pt_flat, ck_t, cv_t, t8(ps["qa"]), t8(ps["ka"]), t8(ps["va"]), bias_pos, bias0,
            mask.reshape(db, npages, 1, PAGE_SIZE), snew.reshape(db, 1, LANES),
            r4(ps["qr"]), r4(ps["kr"]), r4(ps["vr"]), r4(ps["gr"]), w["gng"].reshape(H_R, DV_R), gam,
            state_ret, l, npages, grp)
        oa_s = ot_s.transpose(0, 2, 1).reshape(db, W_A)
        hs = _post_call(ps["h"], oa_s, yr_s.reshape(db, W_R), w, alpha, db * ds, "post_sample")
        outs["ks"].append(ps["ka"].reshape(db, ds, H_A, DH_A))
        outs["vs"].append(ps["va"].reshape(db, ds, H_A, DH_A))
        outs["kis"].append(ps["ki"].reshape(db, ds, DH_IDX))
        outs["ss"].append(st_s)

    stack = lambda k: jnp.stack(outs[k])
    return (hp.reshape(b, s, D_MODEL), hs.reshape(db, ds, D_MODEL),
            stack("kp"), stack("vp"), stack("kip"), stack("sp"),
            stack("ks"), stack("vs"), stack("kis"), stack("ss"))
```

```python
import functools
import math

import numpy as np
import jax
import jax.numpy as jnp
from jax import lax
from jax.experimental import pallas as pl
from jax.experimental.pallas import tpu as pltpu

D_MODEL = 1024
D_FF = 2816
PAGE_SIZE = 128
H_A = 8
DH_A = 64
W_A = H_A * DH_A
H_IDX = 8
DH_IDX = 64
TOPK_MAX = 256
NUM_BUCKETS = 32
MAX_DISTANCE = 128
H_R = 4
DK_R = 128
DV_R = 128
W_R = H_R * DV_R
ROPE_BASE = 10000.0
LN_EPS = 1e-5
GN_EPS = 1e-5
IN_SIZES = (W_A, W_A, W_A, H_IDX * DH_IDX, DH_IDX, H_IDX, H_R * DK_R, H_R * DK_R, W_R, W_R)

LANES = 128
MIB = 1024 * 1024
NEG = -1e30
LOG2E = math.log2(math.e)

FF_CHUNK = 512
PRE_TM = 256
POST_TM = 512
ATT_TQ = 256
RET_CHUNK = 256
BISECT_ITERS = 16
SAMPLE_PAGE_GROUP = 16
SELECT_ROWS = 128
ATT_FAR_WIDTH = 2

C_QA, C_KA, C_VA, C_QI, C_KI2, C_WI, C_QR, C_KR, C_VR, C_GR, C_END = (
    0, 512, 1024, 1536, 2048, 2176, 2304, 2816, 3328, 3840, 4352)

f32 = jnp.float32
bf16 = jnp.bfloat16
NT_DIMS = (((1,), (1,)), ((), ()))


def _const_spec(shape):
    nd = len(shape)
    return pl.BlockSpec(shape, lambda *_: (0,) * nd, pipeline_mode=pl.Buffered(1))


def _ln(x, g, b):
    mu = jnp.mean(x, axis=-1, keepdims=True)
    xc = x - mu
    var = jnp.mean(xc * xc, axis=-1, keepdims=True)
    return xc * lax.rsqrt(var + LN_EPS) * g + b


def _ffn(xb, wg_ref, wu_ref, wd_ref):
    acc = None
    for c0 in range(0, D_FF, FF_CHUNK):
        c1 = min(c0 + FF_CHUNK, D_FF)
        g = jnp.dot(xb, wg_ref[:, c0:c1], preferred_element_type=f32)
        u = jnp.dot(xb, wu_ref[:, c0:c1], preferred_element_type=f32)
        a = (jax.nn.silu(g) * u).astype(bf16)
        part = jnp.dot(a, wd_ref[c0:c1, :], preferred_element_type=f32)
        acc = part if acc is None else acc + part
    return acc


def _pre_kernel(alpha, q_scale, feature_major, x_ref, wg_ref, wu_ref, wd_ref, lng_ref, lnb_ref, win_ref,
                rc_ref, rs_ref,
                h_ref, qa_ref, ka_ref, va_ref, kab_ref, vab_ref, qi_ref, ki_ref, ki2_ref, wi_ref,
                qr_ref, kr_ref, vr_ref, gr_ref):
    act = qa_ref.dtype
    x = x_ref[...]
    f = _ffn(x.astype(bf16), wg_ref, wu_ref, wd_ref)
    h = _ln(alpha * x + 0.5 * f, lng_ref[...], lnb_ref[...])
    h_ref[...] = h
    hb = h.astype(bf16)

    def proj(c0, c1):
        return jnp.dot(hb, win_ref[:, c0:c1], preferred_element_type=f32)

    qa_ref[...] = (proj(C_QA, C_KA) * q_scale).astype(act)
    ka = proj(C_KA, C_VA)
    kab_ref[...] = ka.astype(act)
    va = proj(C_VA, C_QI)
    qi_ref[...] = (proj(C_QI, C_KI2) * DH_IDX ** -0.5).astype(act)
    kk = proj(C_KI2, C_WI)
    wi_ref[...] = proj(C_WI, C_QR)
    vab_ref[...] = va.astype(act)
    if feature_major:
        ka_ref[...] = ka.T
        va_ref[...] = va.T
        ki_ref[...] = kk.T[:DH_IDX, :]
    else:
        ka_ref[...] = ka
        va_ref[...] = va
        ki_ref[...] = kk[:, :DH_IDX]
    ki2_ref[...] = kk.astype(act)
    qr = proj(C_QR, C_KR)
    kr = proj(C_KR, C_VR)
    c = rc_ref[...]
    s = rs_ref[...]
    for hh in range(H_R):
        sl = slice(DK_R * hh, DK_R * (hh + 1))
        qh = qr[:, sl]
        kh = kr[:, sl]
        qr_ref[:, sl] = (qh * c + pltpu.roll(qh, DK_R // 2, 1) * s).astype(act)
        kr_ref[:, sl] = ((kh * c + pltpu.roll(kh, DK_R // 2, 1) * s) * DK_R ** -0.5).astype(act)
    vr_ref[...] = proj(C_VR, C_GR).astype(act)
    gr_ref[...] = proj(C_GR, C_END)


def _pre_call(x, w, rot_c, rot_s, alpha, q_scale, tm, act, seq_len, name):
    n = x.shape[0]
    grid = (pl.cdiv(n, tm),)
    row = lambda width: pl.BlockSpec((tm, width), lambda i: (i, 0))
    rot_blocks = rot_c.shape[0] // tm
    rot = pl.BlockSpec((tm, LANES), lambda i: (i % rot_blocks, 0))
    feature_major = seq_len is not None
    if feature_major:
        assert seq_len % tm == 0 and n % seq_len == 0
        seq_blocks = seq_len // tm
        kv_shape = lambda width: jax.ShapeDtypeStruct((n // seq_len, width, seq_len), f32)
        kv_spec = lambda width: pl.BlockSpec((None, width, tm), lambda i: (i // seq_blocks, 0, i % seq_blocks))
    else:
        kv_shape = lambda width: jax.ShapeDtypeStruct((n, width), f32)
        kv_spec = row
    in_specs = [
        row(D_MODEL),
        _const_spec((D_MODEL, D_FF)), _const_spec((D_MODEL, D_FF)), _const_spec((D_FF, D_MODEL)),
        _const_spec((1, D_MODEL)), _const_spec((1, D_MODEL)),
        _const_spec((D_MODEL, C_END)),
        rot, rot,
    ]
    outs = [
        ("h", D_MODEL, f32), ("qa", W_A, act), ("ka", W_A, f32), ("va", W_A, f32), ("kab", W_A, act),
        ("vab", W_A, act), ("qi", W_A, act), ("ki", DH_IDX, f32), ("ki2", LANES, act), ("wi", LANES, f32),
        ("qr", W_R, act), ("kr", W_R, act), ("vr", W_R, act), ("gr", W_R, f32),
    ]
    kv_names = ("ka", "va", "ki")
    out_shape = [kv_shape(wd) if k in kv_names else jax.ShapeDtypeStruct((n, wd), dt) for k, wd, dt in outs]
    out_specs = [kv_spec(wd) if k in kv_names else row(wd) for k, wd, _ in outs]
    res = pl.pallas_call(
        functools.partial(_pre_kernel, alpha, q_scale, feature_major),
        grid=grid, in_specs=in_specs, out_specs=out_specs, out_shape=out_shape,
        compiler_params=pltpu.CompilerParams(dimension_semantics=("arbitrary",), vmem_limit_bytes=52 * MIB),
        name=name,
    )(x, w["ffn1_wg"], w["ffn1_wu"], w["ffn1_wd"], w["ln1_g"], w["ln1_b"], w["w_in"], rot_c, rot_s)
    return {k: v for (k, _, _), v in zip(outs, res)}


def _post_kernel(alpha, h_ref, oa_ref, yr_ref, wo_ref, l2g_ref, l2b_ref, wg_ref, wu_ref, wd_ref,
                 l3g_ref, l3b_ref, out_ref):
    h = h_ref[...]
    mix = (jnp.dot(oa_ref[...].astype(bf16), wo_ref[0:W_A, :], preferred_element_type=f32)
           + jnp.dot(yr_ref[...].astype(bf16), wo_ref[W_A:W_A + W_R, :], preferred_element_type=f32))
    h2 = _ln(alpha * h + mix, l2g_ref[...], l2b_ref[...])
    f = _ffn(h2.astype(bf16), wg_ref, wu_ref, wd_ref)
    out_ref[...] = _ln(alpha * h2 + 0.5 * f, l3g_ref[...], l3b_ref[...])


def _post_call(h, oa, yr, w, alpha, tm, name):
    n = h.shape[0]
    row = lambda width: pl.BlockSpec((tm, width), lambda i: (i, 0))
    in_specs = [
        row(D_MODEL), row(W_A), row(W_R),
        _const_spec((W_A + W_R, D_MODEL)), _const_spec((1, D_MODEL)), _const_spec((1, D_MODEL)),
        _const_spec((D_MODEL, D_FF)), _const_spec((D_MODEL, D_FF)), _const_spec((D_FF, D_MODEL)),
        _const_spec((1, D_MODEL)), _const_spec((1, D_MODEL)),
    ]
    return pl.pallas_call(
        functools.partial(_post_kernel, alpha),
        grid=(pl.cdiv(n, tm),), in_specs=in_specs, out_specs=row(D_MODEL),
        out_shape=jax.ShapeDtypeStruct((n, D_MODEL), f32),
        compiler_params=pltpu.CompilerParams(dimension_semantics=("arbitrary",), vmem_limit_bytes=48 * MIB),
        name=name,
    )(h, oa, yr, w["w_out"], w["ln2_g"], w["ln2_b"], w["ffn2_wg"], w["ffn2_wu"], w["ffn2_wd"],
      w["ln3_g"], w["ln3_b"])


def _attn_kernel(topk, qa_ref, qi_ref, wi_ref, k_ref, v_ref, ki2_ref, bias_ref, oa_ref,
                 sc_ref, qap_ref, qip_ref, wb_ref, m_ref, l_ref, acc_ref):
    tq = qa_ref.shape[0]
    ncb = tq // LANES
    i = pl.program_id(1)
    nj = i + 1
    kf = float(topk)

    lane = lax.broadcasted_iota(jnp.int32, (tq, LANES), 1)
    lo_half = lane < DH_A
    for h in range(H_A):
        p = h // 2
        keep = lo_half if h % 2 == 0 else jnp.logical_not(lo_half)
        blk = slice(LANES * p, LANES * (p + 1))
        qip_ref[h] = jnp.where(keep, qi_ref[:, blk], jnp.zeros((), qi_ref.dtype))
        qap_ref[h] = jnp.where(keep, qa_ref[:, blk], jnp.zeros((), qa_ref.dtype))
        wb_ref[h] = jnp.broadcast_to(wi_ref[:, h:h + 1] * H_IDX ** -0.5, (tq, LANES))

    def score_chunk(j, carry):
        off = pl.multiple_of(j * tq, tq)
        kj = ki2_ref[pl.ds(off, tq), :]
        accs = [jnp.zeros((tq, LANES), f32) for _ in range(ncb)]
        for h in range(H_IDX):
            d = lax.dot_general(qip_ref[h], kj, NT_DIMS, preferred_element_type=f32)
            wb = wb_ref[h]
            for cb in range(ncb):
                accs[cb] = accs[cb] + wb * jnp.maximum(d[:, cb * LANES:(cb + 1) * LANES], 0.0)
        for cb in range(ncb):
            sc_ref[j, :, cb * LANES:(cb + 1) * LANES] = accs[cb]
        return carry

    lax.fori_loop(0, nj, score_chunk, 0)

    row = lax.broadcasted_iota(jnp.int32, (tq, tq), 0)
    col = lax.broadcasted_iota(jnp.int32, (tq, tq), 1)
    causal = col <= row
    sd = sc_ref[i]
    mn_diag = jnp.min(jnp.where(causal, sd, jnp.inf), axis=1, keepdims=True)
    sc_ref[i] = jnp.where(causal, sd, -jnp.inf)

    rb = min(SELECT_ROWS, tq)

    def fold(fn, init, comb, n_chunks, t=None):
        outs = []
        for r0 in range(0, tq, rb):
            t_blk = None if t is None else t[r0:r0 + rb]

            def body(j, acc, r0=r0, t_blk=t_blk):
                for cb in range(ncb):
                    acc = comb(acc, fn(sc_ref[j, r0:r0 + rb, cb * LANES:(cb + 1) * LANES], t_blk))
                return acc
            outs.append(lax.fori_loop(0, n_chunks, body, jnp.full((rb, LANES), init, f32)))
        return jnp.concatenate(outs, axis=0) if len(outs) > 1 else outs[0]

    def tile_lanes(x):
        return jnp.concatenate([x] * ncb, axis=1) if ncb > 1 else x

    def lane_sum(x):
        return jnp.broadcast_to(jnp.sum(x, axis=1, keepdims=True), (tq, LANES))

    def lane_max(x):
        return jnp.broadcast_to(jnp.max(x, axis=1, keepdims=True), (tq, LANES))

    def count_ge(t):
        return lane_sum(fold(lambda s, tb: jnp.where(s >= tb, 1.0, 0.0), 0.0, jnp.add, nj, t))

    def count_gt(t):
        return lane_sum(fold(lambda s, tb: jnp.where(s > tb, 1.0, 0.0), 0.0, jnp.add, nj, t))

    def max_below(t):
        return lane_max(fold(lambda s, tb: jnp.where(s < tb, s, -jnp.inf), -jnp.inf, jnp.maximum, nj, t))

    n_keys = (lax.broadcasted_iota(jnp.int32, (tq, LANES), 0) + (i * tq + 1)).astype(f32)
    take_all = n_keys <= kf

    mx = lane_max(fold(lambda s, _: s, -jnp.inf, jnp.maximum, nj))
    mn_far = -lane_max(-fold(lambda s, _: s, jnp.inf, jnp.minimum, i))
    mn = jnp.minimum(mn_far, jnp.broadcast_to(mn_diag, (tq, LANES)))

    @pl.when((i + 1) * tq <= topk)
    def _():
        def mk(j, carry):
            sc_ref[j] = jnp.zeros((tq, tq), f32)
            return carry
        lax.fori_loop(0, nj, mk, 0)

    @pl.when((i + 1) * tq > topk)
    def _():
        c_max = count_ge(mx)
        done0 = jnp.logical_or(take_all, c_max >= kf)
        thr0 = jnp.where(take_all, -jnp.inf, mx)
        cge0 = jnp.where(take_all, n_keys, c_max)

        def bis(_, st):
            lo, hi = st
            mid = 0.5 * (lo + hi)
            ge = count_ge(mid) >= kf
            return jnp.where(ge, mid, lo), jnp.where(ge, hi, mid)

        lo, hi = lax.fori_loop(0, BISECT_ITERS, bis, (mn, mx))

        def snap_cond(st):
            return st[0] > 0.0

        def snap_body(st):
            _, hi, thr, cge, done = st
            m = max_below(hi)
            c = count_ge(m)
            ok = c >= kf
            newly = jnp.logical_and(ok, done < 0.5)
            thr = jnp.where(newly, m, thr)
            cge = jnp.where(newly, c, cge)
            hi = jnp.where(jnp.logical_or(done > 0.5, ok), hi, m)
            done = jnp.where(ok, 1.0, done)
            left = jnp.sum(1.0 - done[:, 0:1])
            return left, hi, thr, cge, done

        done_f = jnp.where(done0, 1.0, 0.0)
        left0 = jnp.sum(1.0 - done_f[:, 0:1])
        _, _, thr, cge, _ = lax.while_loop(snap_cond, snap_body, (left0, hi, thr0, cge0, done_f))

        excess = jnp.sum(jnp.where(cge[:, 0:1] > kf, 1.0, 0.0)) > 0.0
        thr_t = tile_lanes(thr)

        @pl.when(jnp.logical_not(excess))
        def _():
            def mk(j, carry):
                sc_ref[j] = jnp.where(sc_ref[j] >= thr_t, 0.0, NEG)
                return carry
            lax.fori_loop(0, nj, mk, 0)

        @pl.when(excess)
        def _():
            need = tile_lanes(kf - count_gt(thr))
            upper = jnp.where(row <= col, 1.0, 0.0).astype(bf16)

            def mk(j, seen):
                s = sc_ref[j]
                tie = s == thr_t
                tie_f = jnp.where(tie, 1.0, 0.0)
                rank = seen + jnp.dot(tie_f.astype(bf16), upper, preferred_element_type=f32)
                sel = jnp.logical_or(s > thr_t, jnp.logical_and(tie, rank <= need))
                sc_ref[j] = jnp.where(sel, 0.0, NEG)
                part = tie_f[:, 0:LANES]
                for cb in range(1, ncb):
                    part = part + tie_f[:, cb * LANES:(cb + 1) * LANES]
                return seen + tile_lanes(lane_sum(part))
            lax.fori_loop(0, nj, mk, jnp.zeros((tq, tq), f32))

    sc_ref[i] = jnp.where(causal, sc_ref[i], NEG)


    m_ref[...] = jnp.full(m_ref.shape, -jnp.inf, f32)
    l_ref[...] = jnp.zeros(l_ref.shape, f32)
    acc_ref[...] = jnp.zeros(acc_ref.shape, f32)

    def attend_chunk(j, bias_cols):
        off = pl.multiple_of(j * tq, tq)
        sel_mask = sc_ref[j]
        for h in range(H_A):
            blk = slice(LANES * (h // 2), LANES * (h // 2 + 1))
            s = lax.dot_general(qap_ref[h], k_ref[pl.ds(off, tq), blk], NT_DIMS, preferred_element_type=f32)
            s = s + sel_mask
            if bias_cols is not None:
                s = s + bias_ref[h, :, bias_cols]
            m_old = m_ref[h]
            cmax = s[:, 0:LANES]
            for cb in range(1, ncb):
                cmax = jnp.maximum(cmax, s[:, cb * LANES:(cb + 1) * LANES])
            m_new = jnp.maximum(m_old, lane_max(cmax))
            a = jnp.exp2(m_old - m_new)
            e = jnp.exp2(s - tile_lanes(m_new))
            esum = e[:, 0:LANES]
            for cb in range(1, ncb):
                esum = esum + e[:, cb * LANES:(cb + 1) * LANES]
            l_ref[h] = l_ref[h] * a + esum
            acc_ref[h] = acc_ref[h] * a + jnp.dot(e.astype(bf16), v_ref[pl.ds(off, tq), blk],
                                                  preferred_element_type=f32)
            m_ref[h] = m_new

    def far(j, carry):
        attend_chunk(j, None)
        return carry
    lax.fori_loop(0, i - 1, far, 0)

    @pl.when(i >= 1)
    def _():
        attend_chunk(i - 1, slice(0, tq))
    attend_chunk(i, slice(tq, 2 * tq))

    for p in range(H_A // 2):
        o_even = acc_ref[2 * p] / lane_sum(l_ref[2 * p])
        o_odd = acc_ref[2 * p + 1] / lane_sum(l_ref[2 * p + 1])
        oa_ref[:, LANES * p:LANES * (p + 1)] = jnp.where(lo_half, o_even, o_odd).astype(oa_ref.dtype)


def _attn_call(qa, qi, wi, kab, vab, ki2, bias_tiles, topk, tq):
    b, s, _ = qa.shape
    nq = s // tq
    qspec = lambda width: pl.BlockSpec((None, tq, width), lambda bi, i: (bi, i, 0))
    kvspec = lambda width: pl.BlockSpec((None, s, width), lambda bi, i: (bi, 0, 0), pipeline_mode=pl.Buffered(1))
    scratch = [
        pltpu.VMEM((nq, tq, tq), f32),
        pltpu.VMEM((H_A, tq, LANES), qa.dtype),
        pltpu.VMEM((H_IDX, tq, LANES), qi.dtype),
        pltpu.VMEM((H_IDX, tq, LANES), f32),
        pltpu.VMEM((H_A, tq, LANES), f32),
        pltpu.VMEM((H_A, tq, LANES), f32),
        pltpu.VMEM((H_A, tq, LANES), f32),
    ]
    return pl.pallas_call(
        functools.partial(_attn_kernel, topk),
        grid=(b, nq),
        in_specs=[qspec(W_A), qspec(W_A), qspec(LANES), kvspec(W_A), kvspec(W_A), kvspec(LANES),
                  _const_spec((H_A, tq, 2 * tq))],
        out_specs=qspec(W_A),
        out_shape=jax.ShapeDtypeStruct((b, s, W_A), bf16),
        scratch_shapes=scratch,
        compiler_params=pltpu.CompilerParams(dimension_semantics=("arbitrary", "arbitrary"),
                                             vmem_limit_bytes=48 * MIB),
        name="attn_prompt",
    )(qa, qi, wi, kab, vab, ki2, bias_tiles)


def _attn_km_kernel(topk, qa_ref, qi_ref, wi_ref, k_ref, v_ref, ki2_ref, bias_ref, oa_ref,
                    sc_ref, qap_ref, qip_ref, m_ref, l_ref, acc_ref):
    tq = qa_ref.shape[0]
    ngrp = tq // 8
    i = pl.program_id(1)
    nj = i + 1
    kf = float(topk)

    lane = lax.broadcasted_iota(jnp.int32, (tq, LANES), 1)
    lo_half = lane < DH_A
    for h in range(H_A):
        p, par = h // 2, h % 2
        keep = lo_half if par == 0 else jnp.logical_not(lo_half)
        blk = slice(LANES * p, LANES * (p + 1))
        qip_ref[h] = jnp.where(keep, qi_ref[:, blk], jnp.zeros((), qi_ref.dtype))
        qap_ref[p, par * tq:(par + 1) * tq, :] = jnp.where(keep, qa_ref[:, blk], jnp.zeros((), qa_ref.dtype))
    w_heads = wi_ref[...].T[0:H_IDX, :] * H_IDX ** -0.5

    def row_groups(x, comb):
        parts = [x[8 * r:8 * r + 8] for r in range(ngrp)]
        while len(parts) > 1:
            parts = [comb(parts[k], parts[k + 1]) for k in range(0, len(parts) - 1, 2)] + (
                [parts[-1]] if len(parts) % 2 else [])
        return parts[0]

    def score_chunk(j, carry):
        off = pl.multiple_of(j * tq, tq)
        kj = ki2_ref[pl.ds(off, tq), :]
        acc = jnp.zeros((tq, tq), f32)
        for h in range(H_IDX):
            d = lax.dot_general(kj, qip_ref[h], NT_DIMS, preferred_element_type=f32)
            acc = acc + w_heads[h:h + 1, :] * jnp.maximum(d, 0.0)
        sc_ref[j] = acc
        return carry

    lax.fori_loop(0, nj, score_chunk, 0)

    krow = lax.broadcasted_iota(jnp.int32, (tq, tq), 0)
    qcol = lax.broadcasted_iota(jnp.int32, (tq, tq), 1)
    causal = krow <= qcol
    sd = sc_ref[i]
    mn_diag = jnp.min(jnp.where(causal, sd, jnp.inf), axis=0, keepdims=True)
    sc_ref[i] = jnp.where(causal, sd, -jnp.inf)

    def fold(fn, init, comb, n_chunks, t=None):
        t8 = None if t is None else jnp.broadcast_to(t, (8, tq))

        nacc = 4

        def body(j, accs):
            accs = list(accs)
            for r in range(ngrp):
                accs[r % nacc] = comb(accs[r % nacc], fn(sc_ref[j, 8 * r:8 * r + 8, :], t8))
            return tuple(accs)
        accs = lax.fori_loop(0, n_chunks, body, tuple(jnp.full((8, tq), init, f32) for _ in range(nacc)))
        acc = comb(comb(accs[0], accs[1]), comb(accs[2], accs[3]))
        if comb is jnp.add:
            return jnp.sum(acc, axis=0, keepdims=True)
        if comb is jnp.maximum:
            return jnp.max(acc, axis=0, keepdims=True)
        return jnp.min(acc, axis=0, keepdims=True)

    def count_ge(t):
        return fold(lambda s, t8: jnp.where(s >= t8, 1.0, 0.0), 0.0, jnp.add, nj, t)

    def count_gt(t):
        return fold(lambda s, t8: jnp.where(s > t8, 1.0, 0.0), 0.0, jnp.add, nj, t)

    def max_below(t):
        return fold(lambda s, t8: jnp.where(s < t8, s, -jnp.inf), -jnp.inf, jnp.maximum, nj, t)

    n_keys = (lax.broadcasted_iota(jnp.int32, (1, tq), 1) + (i * tq + 1)).astype(f32)
    take_all = n_keys <= kf

    @pl.when((i + 1) * tq <= topk)
    def _():
        def mk(j, carry):
            sc_ref[j] = jnp.zeros((tq, tq), f32)
            return carry
        lax.fori_loop(0, nj, mk, 0)

    @pl.when((i + 1) * tq > topk)
    def _():
        mx = fold(lambda s, _: s, -jnp.inf, jnp.maximum, nj)
        mn = jnp.minimum(fold(lambda s, _: s, jnp.inf, jnp.minimum, i), mn_diag)
        c_max = count_ge(mx)
        done0 = jnp.logical_or(take_all, c_max >= kf)
        thr0 = jnp.where(take_all, -jnp.inf, mx)
        cge0 = jnp.where(take_all, n_keys, c_max)

        def bis(_, st):
            lo, hi = st
            mid = 0.5 * (lo + hi)
            ge = count_ge(mid) >= kf
            return jnp.where(ge, mid, lo), jnp.where(ge, hi, mid)

        lo, hi = lax.fori_loop(0, BISECT_ITERS, bis, (mn, mx))

        def snap_cond(st):
            return st[0] > 0.0

        def snap_body(st):
            _, hi, thr, cge, done = st
            m = max_below(hi)
            c = count_ge(m)
            ok = c >= kf
            newly = jnp.logical_and(ok, done < 0.5)
            thr = jnp.where(newly, m, thr)
            cge = jnp.where(newly, c, cge)
            hi = jnp.where(jnp.logical_or(done > 0.5, ok), hi, m)
            done = jnp.where(ok, 1.0, done)
            return jnp.sum(1.0 - done), hi, thr, cge, done

        done_f = jnp.where(done0, 1.0, 0.0)
        _, _, thr, cge, _ = lax.while_loop(snap_cond, snap_body, (jnp.sum(1.0 - done_f), hi, thr0, cge0, done_f))

        excess = jnp.sum(jnp.where(cge > kf, 1.0, 0.0)) > 0.0

        @pl.when(jnp.logical_not(excess))
        def _():
            def mk(j, carry):
                sc_ref[j] = jnp.where(sc_ref[j] >= thr, 0.0, NEG)
                return carry
            lax.fori_loop(0, nj, mk, 0)

        @pl.when(excess)
        def _():
            need = kf - count_gt(thr)
            lower = jnp.where(qcol <= krow, 1.0, 0.0).astype(bf16)

            def mk(j, seen):
                s = sc_ref[j]
                tie = s == thr
                rank = seen + jnp.dot(lower, jnp.where(tie, 1.0, 0.0).astype(bf16), preferred_element_type=f32)
                sel = jnp.logical_or(s > thr, jnp.logical_and(tie, rank <= need))
                sc_ref[j] = jnp.where(sel, 0.0, NEG)
                return rank[tq - 1:tq, :]
            lax.fori_loop(0, nj, mk, jnp.zeros((1, tq), f32))

    sc_ref[i] = jnp.where(causal, sc_ref[i], NEG)

    m_ref[...] = jnp.full(m_ref.shape, -jnp.inf, f32)
    l_ref[...] = jnp.zeros(l_ref.shape, f32)
    acc_ref[...] = jnp.zeros(acc_ref.shape, f32)
    ncb = tq // LANES

    def lane_blocks(x, comb):
        out = x[:, 0:LANES]
        for cb in range(1, x.shape[1] // LANES):
            out = comb(out, x[:, cb * LANES:(cb + 1) * LANES])
        return out

    def attend(j, width, bias_cols):
        off = pl.multiple_of(j * tq, tq)
        keys = pl.ds(off, width * tq)
        sel_mask = jnp.concatenate([sc_ref[j + c].T for c in range(width)], axis=1)
        for p in range(H_A // 2):
            blk = slice(LANES * p, LANES * (p + 1))
            s_pair = lax.dot_general(qap_ref[p], k_ref[keys, blk], NT_DIMS, preferred_element_type=f32)
            es = []
            for par in range(2):
                h = 2 * p + par
                s = s_pair[par * tq:(par + 1) * tq] + sel_mask
                if bias_cols is not None:
                    s = s + bias_ref[h, :, bias_cols]
                m_old = m_ref[h]
                row_max = jnp.max(lane_blocks(s, jnp.maximum), axis=1, keepdims=True)
                m_new = jnp.maximum(m_old, jnp.broadcast_to(row_max, (tq, LANES)))
                a = jnp.exp2(m_old - m_new)
                e = jnp.exp2(s - jnp.concatenate([m_new] * (width * ncb), axis=1))
                l_ref[h] = l_ref[h] * a + lane_blocks(e, jnp.add)
                acc_ref[h] = acc_ref[h] * a
                m_ref[h] = m_new
                es.append(e.astype(bf16))
            pv = jnp.dot(jnp.concatenate(es, axis=0), v_ref[keys, blk], preferred_element_type=f32)
            acc_ref[2 * p] += pv[0:tq]
            acc_ref[2 * p + 1] += pv[tq:2 * tq]

    n_far = jnp.maximum(i - 1, 0)
    if ATT_FAR_WIDTH == 2:
        def far_pair(jj, carry):
            attend(2 * jj, 2, None)
            return carry
        lax.fori_loop(0, n_far // 2, far_pair, 0)

        @pl.when(n_far % 2 == 1)
        def _():
            attend(n_far - 1, 1, None)
    else:
        def far_one(j, carry):
            attend(j, 1, None)
            return carry
        lax.fori_loop(0, n_far, far_one, 0)

    @pl.when(i >= 1)
    def _():
        attend(i - 1, 2, slice(0, 2 * tq))

    @pl.when(i == 0)
    def _():
        attend(0, 1, slice(tq, 2 * tq))

    for p in range(H_A // 2):
        l_even = jnp.broadcast_to(jnp.sum(l_ref[2 * p], axis=1, keepdims=True), (tq, LANES))
        l_odd = jnp.broadcast_to(jnp.sum(l_ref[2 * p + 1], axis=1, keepdims=True), (tq, LANES))
        o_pair = jnp.where(lo_half, acc_ref[2 * p] / l_even, acc_ref[2 * p + 1] / l_odd)
        oa_ref[:, LANES * p:LANES * (p + 1)] = o_pair.astype(oa_ref.dtype)


def _attn_km_call(qa, qi, wt, kab, vab, ki2, bias_tiles, topk, tq):
    b, s, _ = qa.shape
    nq = s // tq
    qspec = lambda width: pl.BlockSpec((None, tq, width), lambda bi, i: (bi, i, 0))
    whole = lambda shape: pl.BlockSpec((None,) + shape, lambda bi, i: (bi,) + (0,) * len(shape),
                                       pipeline_mode=pl.Buffered(1))
    scratch = [
        pltpu.VMEM((nq, tq, tq), f32),
        pltpu.VMEM((H_A // 2, 2 * tq, LANES), qa.dtype),
        pltpu.VMEM((H_IDX, tq, LANES), qi.dtype),
        pltpu.VMEM((H_A, tq, LANES), f32),
        pltpu.VMEM((H_A, tq, LANES), f32),
        pltpu.VMEM((H_A, tq, LANES), f32),
    ]
    return pl.pallas_call(
        functools.partial(_attn_km_kernel, topk),
        grid=(b, nq),
        in_specs=[qspec(W_A), qspec(W_A),
                  qspec(LANES),
                  whole((s, W_A)), whole((s, W_A)), whole((s, LANES)),
                  _const_spec((H_A, tq, 2 * tq))],
        out_specs=qspec(W_A),
        out_shape=jax.ShapeDtypeStruct((b, s, W_A), bf16),
        scratch_shapes=scratch,
        compiler_params=pltpu.CompilerParams(dimension_semantics=("arbitrary", "arbitrary"),
                                             vmem_limit_bytes=48 * MIB),
        name="attn_prompt",
    )(qa, qi, wt, kab, vab, ki2, bias_tiles)


def _ret_kernel(qr_ref, kr_ref, vr_ref, gr_ref, gng_ref, dmat_ref, cross_ref, kvd_ref, cdec_ref,
                yr_ref, st_ref):
    c = pl.program_id(1)

    @pl.when(c == 0)
    def _():
        st_ref[...] = jnp.zeros(st_ref.shape, f32)

    for h in range(H_R):
        sl = slice(DK_R * h, DK_R * (h + 1))
        q = qr_ref[:, sl]
        k = kr_ref[:, sl]
        v = vr_ref[:, sl]
        att = lax.dot_general(q, k, NT_DIMS, preferred_element_type=f32) * dmat_ref[h]
        st = st_ref[h]
        o = (jnp.dot(att.astype(bf16), v, preferred_element_type=f32)
             + jnp.dot(q, st.astype(bf16), preferred_element_type=f32) * cross_ref[h])
        kd = (k.astype(f32) * kvd_ref[h]).T.astype(bf16)
        st_ref[h] = cdec_ref[h, 0:1, :] * st + jnp.dot(kd, v, preferred_element_type=f32)
        mu = jnp.mean(o, axis=-1, keepdims=True)
        oc = o - mu
        var = jnp.mean(oc * oc, axis=-1, keepdims=True)
        yn = oc * lax.rsqrt(var + GN_EPS) * gng_ref[:, sl]
        yr_ref[:, sl] = (jax.nn.silu(gr_ref[:, sl]) * yn).astype(yr_ref.dtype)


def _ret_call(qr, kr, vr, gr, gng, dec, chunk):
    b, s, _ = qr.shape
    nc = s // chunk
    rspec = pl.BlockSpec((None, chunk, W_R), lambda bi, c: (bi, c, 0))
    return pl.pallas_call(
        _ret_kernel,
        grid=(b, nc),
        in_specs=[rspec, rspec, rspec, rspec, _const_spec((1, W_R)),
                  _const_spec((H_R, chunk, chunk)), _const_spec((H_R, chunk, LANES)),
                  _const_spec((H_R, chunk, LANES)), _const_spec((H_R, 8, LANES))],
        out_specs=[rspec, pl.BlockSpec((None, H_R, DK_R, DV_R), lambda bi, c: (bi, 0, 0, 0))],
        out_shape=[jax.ShapeDtypeStruct((b, s, W_R), bf16), jax.ShapeDtypeStruct((b, H_R, DK_R, DV_R), f32)],
        compiler_params=pltpu.CompilerParams(dimension_semantics=("arbitrary", "arbitrary"),
                                             vmem_limit_bytes=32 * MIB),
        name="ret_prompt",
    )(qr, kr, vr, gr, gng, dec["intra"], dec["cross"], dec["kv"], dec["chunk"])


def _sscore_kernel(layer, npages, pt_ref, ck_ref, q_ref, w_ref, kn_ref, sc_ref, snew_ref, kbuf, sem):
    b = pl.program_id(0)
    nb = pl.num_programs(0)
    slot = b % 2

    def page_copy(seq, p, sl):
        page = pt_ref[seq * npages + p]
        return pltpu.make_async_copy(ck_ref.at[layer, page], kbuf.at[sl, p], sem.at[sl])

    def start_all(seq, sl):
        def body(p, carry):
            page_copy(seq, p, sl).start()
            return carry
        lax.fori_loop(0, npages, body, 0)

    @pl.when(b == 0)
    def _():
        start_all(0, 0)

    @pl.when(b + 1 < nb)
    def _():
        start_all(b + 1, 1 - slot)

    def wait_body(p, carry):
        page_copy(b, p, slot).wait()
        return carry
    lax.fori_loop(0, npages, wait_body, 0)

    q = q_ref[...].astype(bf16)
    w = w_ref[...]

    pages_per_dot = min(8, npages)
    for p0 in range(0, npages, pages_per_dot):
        kw = jnp.concatenate([kbuf[slot, p] for p in range(p0, p0 + pages_per_dot)], axis=1).astype(bf16)
        d = jnp.dot(q, kw, preferred_element_type=f32)
        sc_ref[:, p0 * PAGE_SIZE:(p0 + pages_per_dot) * PAGE_SIZE] = jnp.sum(
            w * jnp.maximum(d, 0.0), axis=0, keepdims=True)

    kn = kn_ref[...].astype(bf16).astype(f32)
    dn = jnp.sum(q.astype(f32) * kn, axis=1, keepdims=True)
    s_new = jnp.sum(w * jnp.maximum(dn, 0.0), axis=0, keepdims=True)
    snew_ref[...] = jnp.broadcast_to(s_new, (1, LANES))


def _sscore_call(page_table_flat, cache_kidx_t, q16, w16, kn, layer, npages):
    db = q16.shape[0]
    assert npages % min(8, npages) == 0
    grid_spec = pltpu.PrefetchScalarGridSpec(
        num_scalar_prefetch=1,
        grid=(db,),
        in_specs=[
            pl.BlockSpec(memory_space=pl.ANY),
            pl.BlockSpec((None, 16, DH_IDX), lambda b, pt: (b, 0, 0)),
            pl.BlockSpec((None, 16, 1), lambda b, pt: (b, 0, 0)),
            pl.BlockSpec((None, 1, DH_IDX), lambda b, pt: (b, 0, 0)),
        ],
        out_specs=[pl.BlockSpec((None, 1, npages * PAGE_SIZE), lambda b, pt: (b, 0, 0)),
                   pl.BlockSpec((None, 1, LANES), lambda b, pt: (b, 0, 0))],
        scratch_shapes=[
            pltpu.VMEM((2, npages, DH_IDX, PAGE_SIZE), f32),
            pltpu.SemaphoreType.DMA((2,)),
        ],
    )
    return pl.pallas_call(
        functools.partial(_sscore_kernel, layer, npages),
        grid_spec=grid_spec,
        out_shape=[jax.ShapeDtypeStruct((db, 1, npages * PAGE_SIZE), f32),
                   jax.ShapeDtypeStruct((db, 1, LANES), f32)],
        compiler_params=pltpu.CompilerParams(dimension_semantics=("arbitrary",), vmem_limit_bytes=32 * MIB),
        name="sample_scores",
    )(page_table_flat, cache_kidx_t, q16, w16, kn)


PREFIX_CHUNK = 256


def _ssel_kernel(topk, sc_ref, sn_ref, mask_ref, selnew_ref):
    kf = float(topk)
    db, length = sc_ref.shape
    sc = sc_ref[...]
    s_new = sn_ref[:, 0:1]

    def rsum(x):
        return jnp.sum(x, axis=1, keepdims=True)

    def count(cmp, t):
        return rsum(jnp.where(cmp(sc, t), 1.0, 0.0)) + jnp.where(cmp(s_new, t), 1.0, 0.0)

    ge = lambda a, t: a >= t
    gt = lambda a, t: a > t
    mx = jnp.maximum(jnp.max(sc, axis=1, keepdims=True), s_new)
    mn = jnp.minimum(jnp.min(sc, axis=1, keepdims=True), s_new)
    c_max = count(ge, mx)
    done0 = jnp.where(c_max >= kf, 1.0, 0.0)

    def bis(_, st):
        lo, hi = st
        mid = 0.5 * (lo + hi)
        ok = count(ge, mid) >= kf
        return jnp.where(ok, mid, lo), jnp.where(ok, hi, mid)
    lo, hi = lax.fori_loop(0, BISECT_ITERS, bis, (mn, mx))

    def snap_cond(st):
        return st[0] > 0.0

    def snap_body(st):
        _, hi, thr, done = st
        below = jnp.maximum(jnp.max(jnp.where(sc < hi, sc, -jnp.inf), axis=1, keepdims=True),
                            jnp.where(s_new < hi, s_new, -jnp.inf))
        ok = count(ge, below) >= kf
        newly = jnp.logical_and(ok, done < 0.5)
        thr = jnp.where(newly, below, thr)
        hi = jnp.where(jnp.logical_or(done > 0.5, ok), hi, below)
        done = jnp.where(ok, 1.0, done)
        return jnp.sum(1.0 - done), hi, thr, done
    _, _, thr, _ = lax.while_loop(snap_cond, snap_body, (jnp.sum(1.0 - done0), hi, mx, done0))

    need = kf - count(gt, thr)
    pc = min(PREFIX_CHUNK, length)
    r_i = lax.broadcasted_iota(jnp.int32, (pc, pc), 0)
    c_i = lax.broadcasted_iota(jnp.int32, (pc, pc), 1)
    upper = jnp.where(r_i <= c_i, 1.0, 0.0).astype(bf16)
    seen = jnp.zeros((db, 1), f32)
    for c0 in range(0, length, pc):
        s_c = sc[:, c0:c0 + pc]
        tie = s_c == thr
        rank = seen + jnp.dot(jnp.where(tie, 1.0, 0.0).astype(bf16), upper, preferred_element_type=f32)
        sel = jnp.logical_or(s_c > thr, jnp.logical_and(tie, rank <= need))
        mask_ref[:, c0:c0 + pc] = jnp.where(sel, 0.0, NEG)
        seen = rank[:, pc - 1:pc]
    sel_new = jnp.logical_or(s_new > thr, jnp.logical_and(s_new == thr, seen + 1.0 <= need))
    selnew_ref[...] = jnp.broadcast_to(jnp.where(sel_new, 1.0, 0.0), (db, LANES))


def _ssel_call(scores, s_new, topk):
    db, length = scores.shape
    assert length % min(PREFIX_CHUNK, length) == 0
    return pl.pallas_call(
        functools.partial(_ssel_kernel, topk),
        out_shape=[jax.ShapeDtypeStruct((db, length), f32), jax.ShapeDtypeStruct((db, LANES), f32)],
        compiler_params=pltpu.CompilerParams(vmem_limit_bytes=32 * MIB),
        name="sample_select",
    )(scores, s_new)


def _satt_kernel(layer, npages, grp, pt_ref,
                 ckk_ref, ckv_ref, qt_ref, knt_ref, vnt_ref, bias_ref, bias0_ref, mask_ref, snew_ref,
                 qr_ref, kr_ref, vr_ref, gr_ref, gng_ref, gam_ref, s0_ref,
                 ot_ref, yr_ref, sn_ref,
                 kbuf, vbuf, sem, m_ref, l_ref, acc_ref):
    b = pl.program_id(0)
    g = pl.program_id(1)
    nb = pl.num_programs(0)
    ng = pl.num_programs(1)
    t = b * ng + g
    slot = t % 2

    def page_copies(seq, gi, p, sl):
        page = pt_ref[seq * npages + gi * grp + p]
        return (pltpu.make_async_copy(ckk_ref.at[layer, page], kbuf.at[sl, p], sem.at[sl, 0]),
                pltpu.make_async_copy(ckv_ref.at[layer, page], vbuf.at[sl, p], sem.at[sl, 1]))

    def start_all(seq, gi, sl):
        def body(p, carry):
            ck, cv = page_copies(seq, gi, p, sl)
            ck.start()
            cv.start()
            return carry
        lax.fori_loop(0, grp, body, 0)

    @pl.when(t == 0)
    def _():
        start_all(0, 0, 0)

    @pl.when(t + 1 < nb * ng)
    def _():
        wrap = g + 1 == ng
        start_all(jnp.where(wrap, b + 1, b), jnp.where(wrap, 0, g + 1), 1 - slot)

    @pl.when(g == 0)
    def _():
        m_ref[...] = jnp.full(m_ref.shape, -jnp.inf, f32)
        l_ref[...] = jnp.zeros(l_ref.shape, f32)
        acc_ref[...] = jnp.zeros(acc_ref.shape, f32)

    def wait_body(p, carry):
        ck, cv = page_copies(b, g, p, slot)
        ck.wait()
        cv.wait()
        return carry
    lax.fori_loop(0, grp, wait_body, 0)

    mask = mask_ref[...]
    for h in range(H_A):
        qc = qt_ref[:, h:h + 1]
        kh = kbuf[slot, :, h]
        lg = jnp.sum(kh * qc[None], axis=1, keepdims=True) + bias_ref[h] + mask
        m_old = m_ref[h]
        m_blk = jnp.max(jnp.max(lg, axis=0), axis=1, keepdims=True)
        m_new = jnp.maximum(m_old, m_blk)
        a = jnp.exp(m_old - m_new)
        e = jnp.exp(lg - m_new[None])
        l_ref[h] = l_ref[h] * a + jnp.sum(e, axis=0)
        acc_ref[h] = acc_ref[h] * a + jnp.sum(vbuf[slot, :, h] * e, axis=0)
        m_ref[h] = m_new

    @pl.when(g == ng - 1)
    def _():
        lg_new = jnp.sum(qt_ref[...] * knt_ref[...], axis=0, keepdims=True) + bias0_ref[...]
        lg_new = jnp.where(snew_ref[0:1, 0:1] > 0.5, lg_new, NEG)
        for h in range(H_A):
            m_c = m_ref[h][:, 0:1]
            lg_h = lg_new[:, h:h + 1]
            m_f = jnp.maximum(m_c, lg_h)
            a = jnp.exp(m_c - m_f)
            e_new = jnp.exp(lg_h - m_f)
            den = jnp.sum(l_ref[h], axis=1, keepdims=True) * a + e_new
            num = jnp.sum(acc_ref[h], axis=1, keepdims=True) * a + e_new * vnt_ref[:, h:h + 1]
            ot_ref[:, h:h + 1] = num / den

    @pl.when(g == 0)
    def _():
        r_i = lax.broadcasted_iota(jnp.int32, (DK_R, DK_R), 0)
        c_i = lax.broadcasted_iota(jnp.int32, (DK_R, DK_R), 1)
        eye = jnp.where(r_i == c_i, 1.0, 0.0)
        for h in range(H_R):
            qrow = qr_ref[h:h + 1, :]
            krow = kr_ref[h:h + 1, :]
            vrow = vr_ref[h:h + 1, :]
            gam = gam_ref[h:h + 1, :]
            qcol = jnp.sum(eye * qrow, axis=1, keepdims=True)
            kcol = jnp.sum(eye * krow, axis=1, keepdims=True)
            st = s0_ref[h]
            qk = jnp.sum(qrow * krow, axis=1, keepdims=True)
            o = qk * vrow + gam * jnp.sum(qcol * st, axis=0, keepdims=True)
            sn_ref[h] = gam * st + kcol * vrow
            mu = jnp.mean(o, axis=1, keepdims=True)
            oc = o - mu
            var = jnp.mean(oc * oc, axis=1, keepdims=True)
            yn = oc * lax.rsqrt(var + GN_EPS) * gng_ref[h:h + 1, :]
            yr_ref[h:h + 1, :] = jax.nn.silu(gr_ref[h:h + 1, :]) * yn


def _satt_call(pt_flat, cache_k_t, cache_v_t, qt, knt, vnt, bias_pos, bias0, mask, snew, qr, kr, vr, gr, gng, gam,
               state, layer, npages, grp):
    db = qt.shape[0]
    ng = npages // grp
    per_seq = lambda d0, d1: pl.BlockSpec((None, d0, d1), lambda b, g, pt: (b, 0, 0))
    grid_spec = pltpu.PrefetchScalarGridSpec(
        num_scalar_prefetch=1,
        grid=(db, ng),
        in_specs=[
            pl.BlockSpec(memory_space=pl.ANY), pl.BlockSpec(memory_space=pl.ANY),
            per_seq(DH_A, H_A), per_seq(DH_A, H_A), per_seq(DH_A, H_A),
            pl.BlockSpec((H_A, grp, 1, PAGE_SIZE), lambda b, g, pt: (0, g, 0, 0)),
            pl.BlockSpec((1, H_A), lambda b, g, pt: (0, 0)),
            pl.BlockSpec((None, grp, 1, PAGE_SIZE), lambda b, g, pt: (b, g, 0, 0)),
            per_seq(1, LANES),
            per_seq(H_R, DK_R), per_seq(H_R, DK_R), per_seq(H_R, DV_R), per_seq(H_R, DV_R),
            pl.BlockSpec((H_R, DV_R), lambda b, g, pt: (0, 0)),
            pl.BlockSpec((H_R, DV_R), lambda b, g, pt: (0, 0)),
            pl.BlockSpec((None, None, H_R, DK_R, DV_R), lambda b, g, pt: (layer, b, 0, 0, 0)),
        ],
        out_specs=[per_seq(DH_A, H_A), per_seq(H_R, DV_R),
                   pl.BlockSpec((None, H_R, DK_R, DV_R), lambda b, g, pt: (b, 0, 0, 0))],
        scratch_shapes=[
            pltpu.VMEM((2, grp, H_A, DH_A, PAGE_SIZE), f32),
            pltpu.VMEM((2, grp, H_A, DH_A, PAGE_SIZE), f32),
            pltpu.SemaphoreType.DMA((2, 2)),
            pltpu.VMEM((H_A, 1, PAGE_SIZE), f32),
            pltpu.VMEM((H_A, 1, PAGE_SIZE), f32),
            pltpu.VMEM((H_A, DH_A, PAGE_SIZE), f32),
        ],
    )
    return pl.pallas_call(
        functools.partial(_satt_kernel, layer, npages, grp),
        grid_spec=grid_spec,
        out_shape=[jax.ShapeDtypeStruct((db, DH_A, H_A), f32), jax.ShapeDtypeStruct((db, H_R, DV_R), f32),
                   jax.ShapeDtypeStruct((db, H_R, DK_R, DV_R), f32)],
        compiler_params=pltpu.CompilerParams(dimension_semantics=("arbitrary", "arbitrary"),
                                             vmem_limit_bytes=40 * MIB),
        name="sample_attend",
    )(pt_flat, cache_k_t, cache_v_t, qt, knt, vnt, bias_pos, bias0, mask, snew, qr, kr, vr, gr, gng, gam, state)


def _t5_bucket(rel):
    n = jnp.maximum(rel, 0)
    max_exact = NUM_BUCKETS // 2
    nf = jnp.maximum(n, 1).astype(f32)
    large = max_exact + (jnp.log(nf / max_exact) / math.log(MAX_DISTANCE / max_exact)
                         * (NUM_BUCKETS - max_exact)).astype(jnp.int32)
    large = jnp.minimum(large, NUM_BUCKETS - 1)
    return jnp.where(n < max_exact, n, large)


def _rotary_tables(pos):
    half = DK_R // 2
    freqs = ROPE_BASE ** (-jnp.arange(half, dtype=f32) / half)
    ang = pos.astype(f32)[:, None] * freqs[None, :]
    cos, sin = jnp.cos(ang), jnp.sin(ang)
    return jnp.concatenate([cos, cos], axis=1), jnp.concatenate([-sin, sin], axis=1)


def _decay_tables(chunk):
    lg = jnp.log1p(-jnp.exp2(-5.0 - jnp.arange(H_R, dtype=f32)))
    i = jnp.arange(chunk, dtype=f32)
    diff = i[:, None] - i[None, :]
    causal = diff >= 0
    intra = jnp.where(causal[None], jnp.exp(jnp.where(causal, diff, 0.0)[None] * lg[:, None, None]), 0.0)
    cross = jnp.exp((i[None, :] + 1.0) * lg[:, None])
    kv = jnp.exp((chunk - 1.0 - i)[None, :] * lg[:, None])
    cdec = jnp.exp(chunk * lg)
    bc = lambda a: jnp.broadcast_to(a[:, :, None], (H_R, chunk, LANES))
    return {"intra": intra, "cross": bc(cross), "kv": bc(kv),
            "chunk": jnp.broadcast_to(cdec[:, None, None], (H_R, 8, LANES))}


def _bias_tiles(rel_bias, tq):
    assert _last_bucket_from(tq + 1), "keys beyond the previous chunk must share the last bucket"
    rb = rel_bias.astype(f32) - rel_bias[NUM_BUCKETS - 1].astype(f32)[None, :]
    tab = rb[_t5_bucket(jnp.arange(2 * tq + 1, dtype=jnp.int32))].T
    m = 3 * tq
    g = jnp.concatenate([tab[:, tq::-1],
                         jnp.broadcast_to(tab[:, 0:1], (H_A, tq - 1)),
                         tab[:, 2 * tq:tq:-1]], axis=1)
    flat = jnp.tile(g, (1, tq))[:, :tq * (m - 1)]
    return flat.reshape(H_A, tq, m - 1)[:, :, :2 * tq]


def _last_bucket_from(n):
    large = NUM_BUCKETS // 2 + int(math.log(n / (NUM_BUCKETS // 2)) / math.log(MAX_DISTANCE / (NUM_BUCKETS // 2))
                                   * (NUM_BUCKETS - NUM_BUCKETS // 2) - 1e-3)
    return n >= NUM_BUCKETS // 2 and large >= NUM_BUCKETS - 1


def _layer_weights(l, ffn1_wg, ffn1_wu, ffn1_wd, ln1_g, ln1_b, w_in, ret_gn_g, w_out, ln2_g, ln2_b,
                   ffn2_wg, ffn2_wu, ffn2_wd, ln3_g, ln3_b):
    pts = np.cumsum((0,) + IN_SIZES)
    wi = w_in[l]
    col = lambda k: wi[:, pts[k]:pts[k + 1]]
    zeros = jnp.zeros((D_MODEL, LANES - H_IDX), wi.dtype)
    w_in2 = jnp.concatenate([col(0), col(1), col(2), col(3), col(4), col(4), col(5), zeros,
                             col(6), col(7), col(8), col(9)], axis=1).astype(bf16)
    r2 = lambda a: a[l].reshape(1, -1).astype(f32)
    return {
        "ffn1_wg": ffn1_wg[l].astype(bf16), "ffn1_wu": ffn1_wu[l].astype(bf16), "ffn1_wd": ffn1_wd[l].astype(bf16),
        "ln1_g": r2(ln1_g), "ln1_b": r2(ln1_b), "w_in": w_in2, "gng": r2(ret_gn_g),
        "w_out": w_out[l].astype(bf16), "ln2_g": r2(ln2_g), "ln2_b": r2(ln2_b),
        "ffn2_wg": ffn2_wg[l].astype(bf16), "ffn2_wu": ffn2_wu[l].astype(bf16), "ffn2_wd": ffn2_wd[l].astype(bf16),
        "ln3_g": r2(ln3_g), "ln3_b": r2(ln3_b),
    }


def kernel(x_prompt, x_sample, cache_k, cache_v, cache_kidx, state_ret, page_table, rel_bias,
           ffn1_wg, ffn1_wu, ffn1_wd, ln1_g, ln1_b, w_in, ret_gn_g, w_out,
           ln2_g, ln2_b, ffn2_wg, ffn2_wu, ffn2_wd, ln3_g, ln3_b):
    b, s, _ = x_prompt.shape
    db, ds, _ = x_sample.shape
    depth = w_in.shape[0]
    npages = page_table.shape[1]
    past = npages * PAGE_SIZE
    assert ds == 1, "the sample group decodes one token per sequence"
    alpha = (2 * depth) ** 0.25

    tq = min(ATT_TQ, s)
    chunk = min(RET_CHUNK, s)
    assert s % tq == 0 and s % chunk == 0 and tq % LANES == 0
    topk_p = min(TOPK_MAX, s // 4)
    topk_s = min(TOPK_MAX, (past + ds) // 4)

    rot_p = _rotary_tables(jnp.arange(s, dtype=jnp.int32))
    rot_s = _rotary_tables(jnp.full((db * ds,), past, jnp.int32))
    dec_p = _decay_tables(chunk)
    gam = jnp.broadcast_to(_decay_tables(1)["chunk"][:, 0, :], (H_R, LANES))
    bias_tiles = _bias_tiles(rel_bias, tq)
    near = min(past, 2 * PAGE_SIZE)
    assert _last_bucket_from(near + 1)
    bias_near = rel_bias.astype(f32)[_t5_bucket(jnp.arange(near, 0, -1, dtype=jnp.int32))].T
    bias_far = jnp.broadcast_to(rel_bias[NUM_BUCKETS - 1].astype(f32)[:, None], (H_A, past - near))
    bias_pos = jnp.concatenate([bias_far, bias_near], axis=1).reshape(H_A, npages, 1, PAGE_SIZE)
    bias0 = rel_bias[0:1].astype(f32)
    pt_flat = page_table.reshape(-1).astype(jnp.int32)
    grp = min(SAMPLE_PAGE_GROUP, npages)
    assert npages % grp == 0
    ckidx_t = jnp.transpose(cache_kidx, (0, 1, 3, 2))
    ck_t = jnp.transpose(cache_k, (0, 1, 3, 4, 2))
    cv_t = jnp.transpose(cache_v, (0, 1, 3, 4, 2))

    hp = x_prompt.reshape(b * s, D_MODEL)
    hs = x_sample.reshape(db * ds, D_MODEL)
    outs = {k: [] for k in ("kp", "vp", "kip", "sp", "ks", "vs", "kis", "ss")}
    for l in range(depth):
        w = _layer_weights(l, ffn1_wg, ffn1_wu, ffn1_wd, ln1_g, ln1_b, w_in, ret_gn_g, w_out, ln2_g, ln2_b,
                           ffn2_wg, ffn2_wu, ffn2_wd, ln3_g, ln3_b)
        pp = _pre_call(hp, w, rot_p[0], rot_p[1], alpha, DH_A ** -0.5 * LOG2E, tq, bf16, s, "pre_prompt")
        r3 = lambda a: a.reshape(b, s, a.shape[-1])
        oa = _attn_km_call(r3(pp["qa"]), r3(pp["qi"]), r3(pp["wi"]), r3(pp["kab"]), r3(pp["vab"]), r3(pp["ki2"]),
                           bias_tiles * LOG2E, topk_p, tq)
        yr, st_p = _ret_call(r3(pp["qr"]), r3(pp["kr"]), r3(pp["vr"]), r3(pp["gr"]), w["gng"], dec_p, chunk)
        hp = _post_call(pp["h"], oa.reshape(b * s, W_A), yr.reshape(b * s, W_R), w, alpha,
                        min(POST_TM, b * s), "post_prompt")
        outs["kp"].append(pp["ka"].reshape(b, H_A, DH_A, s).transpose(0, 3, 1, 2))
        outs["vp"].append(pp["va"].reshape(b, H_A, DH_A, s).transpose(0, 3, 1, 2))
        outs["kip"].append(pp["ki"].transpose(0, 2, 1))
        outs["sp"].append(st_p)

        ps = _pre_call(hs, w, rot_s[0], rot_s[1], alpha, DH_A ** -0.5, db * ds, f32, None, "pre_sample")
        q16 = jnp.pad(ps["qi"].reshape(db, H_IDX, DH_IDX), ((0, 0), (0, 16 - H_IDX), (0, 0)))
        w16 = jnp.pad((ps["wi"][:, :H_IDX] * H_IDX ** -0.5).reshape(db, H_IDX, 1), ((0, 0), (0, 16 - H_IDX), (0, 0)))
        sc_s, sn_s = _sscore_call(pt_flat, ckidx_t, q16, w16, ps["ki"].reshape(db, 1, DH_IDX), l, npages)
        mask, snew = _ssel_call(sc_s.reshape(db, past), sn_s.reshape(db, LANES), topk_s)
        t8 = lambda a: a.reshape(db, H_A, DH_A).transpose(0, 2, 1)
        r4 = lambda a: a.reshape(db, H_R, DK_R)
        ot_s, yr_s, st_s = _satt_call(
            pt_flat, ck_t, cv_t, t8(ps["qa"]), t8(ps["ka"]), t8(ps["va"]), bias_pos, bias0,
            mask.reshape(db, npages, 1, PAGE_SIZE), snew.reshape(db, 1, LANES),
            r4(ps["qr"]), r4(ps["kr"]), r4(ps["vr"]), r4(ps["gr"]), w["gng"].reshape(H_R, DV_R), gam,
            state_ret, l, npages, grp)
        oa_s = ot_s.transpose(0, 2, 1).reshape(db, W_A)
        hs = _post_call(ps["h"], oa_s, yr_s.reshape(db, W_R), w, alpha, db * ds, "post_sample")
        outs["ks"].append(ps["ka"].reshape(db, ds, H_A, DH_A))
        outs["vs"].append(ps["va"].reshape(db, ds, H_A, DH_A))
        outs["kis"].append(ps["ki"].reshape(db, ds, DH_IDX))
        outs["ss"].append(st_s)

    stack = lambda k: jnp.stack(outs[k])
    return (hp.reshape(b, s, D_MODEL), hs.reshape(db, ds, D_MODEL),
            stack("kp"), stack("vp"), stack("kip"), stack("sp"),
            stack("ks"), stack("vs"), stack("kis"), stack("ss"))
```

```python
import functools
import math

import numpy as np
import jax
import jax.numpy as jnp
from jax import lax
from jax.experimental import pallas as pl
from jax.experimental.pallas import tpu as pltpu

D_MODEL = 1024
D_FF = 2816
PAGE_SIZE = 128
H_A = 8
DH_A = 64
W_A = H_A * DH_A
H_IDX = 8
DH_IDX = 64
TOPK_MAX = 256
NUM_BUCKETS = 32
MAX_DISTANCE = 128
H_R = 4
DK_R = 128
DV_R = 128
W_R = H_R * DV_R
ROPE_BASE = 10000.0
LN_EPS = 1e-5
GN_EPS = 1e-5
IN_SIZES = (W_A, W_A, W_A, H_IDX * DH_IDX, DH_IDX, H_IDX, H_R * DK_R, H_R * DK_R, W_R, W_R)

LANES = 128
MIB = 1024 * 1024
NEG = -1e30
LOG2E = math.log2(math.e)

FF_CHUNK = 512
PRE_TM = 256
POST_TM = 512
ATT_TQ = 256
RET_CHUNK = 256
BISECT_ITERS = 16
SAMPLE_PAGE_GROUP = 16
STREAM_POST_TM = 256
STREAM_PAGE_GROUP = 8
SELECT_ROWS = 128
ATT_FAR_WIDTH = 2

C_QA, C_KA, C_VA, C_QI, C_KI2, C_WI, C_QR, C_KR, C_VR, C_GR, C_END = (
    0, 512, 1024, 1536, 2048, 2176, 2304, 2816, 3328, 3840, 4352)

f32 = jnp.float32
bf16 = jnp.bfloat16
NT_DIMS = (((1,), (1,)), ((), ()))


def _const_spec(shape):
    nd = len(shape)
    return pl.BlockSpec(shape, lambda *_: (0,) * nd, pipeline_mode=pl.Buffered(1))


def _ln(x, g, b):
    mu = jnp.mean(x, axis=-1, keepdims=True)
    xc = x - mu
    var = jnp.mean(xc * xc, axis=-1, keepdims=True)
    return xc * lax.rsqrt(var + LN_EPS) * g + b


def _ffn(xb, wg_ref, wu_ref, wd_ref, midway=None):
    acc = None
    starts = list(range(0, D_FF, FF_CHUNK))
    for ci, c0 in enumerate(starts):
        if midway is not None and ci == len(starts) // 2:
            midway()
        c1 = min(c0 + FF_CHUNK, D_FF)
        g = jnp.dot(xb, wg_ref[:, c0:c1], preferred_element_type=f32)
        u = jnp.dot(xb, wu_ref[:, c0:c1], preferred_element_type=f32)
        a = (jax.nn.silu(g) * u).astype(bf16)
        part = jnp.dot(a, wd_ref[c0:c1, :], preferred_element_type=f32)
        acc = part if acc is None else acc + part
    return acc


def _pre_kernel(alpha, q_scale, feature_major, x_ref, wg_ref, wu_ref, wd_ref, lng_ref, lnb_ref, win_ref,
                rc_ref, rs_ref,
                h_ref, qa_ref, ka_ref, va_ref, kab_ref, vab_ref, qi_ref, ki_ref, ki2_ref, wi_ref,
                qr_ref, kr_ref, vr_ref, gr_ref):
    act = qa_ref.dtype
    x = x_ref[...]
    f = _ffn(x.astype(bf16), wg_ref, wu_ref, wd_ref)
    h = _ln(alpha * x + 0.5 * f, lng_ref[...], lnb_ref[...])
    h_ref[...] = h
    hb = h.astype(bf16)

    def proj(c0, c1):
        return jnp.dot(hb, win_ref[:, c0:c1], preferred_element_type=f32)

    qa_ref[...] = (proj(C_QA, C_KA) * q_scale).astype(act)
    ka = proj(C_KA, C_VA)
    kab_ref[...] = ka.astype(act)
    va = proj(C_VA, C_QI)
    qi_ref[...] = (proj(C_QI, C_KI2) * DH_IDX ** -0.5).astype(act)
    kk = proj(C_KI2, C_WI)
    wi_ref[...] = proj(C_WI, C_QR)
    vab_ref[...] = va.astype(act)
    if feature_major:
        ka_ref[...] = ka.T
        va_ref[...] = va.T
        ki_ref[...] = kk.T[:DH_IDX, :]
    else:
        ka_ref[...] = ka
        va_ref[...] = va
        ki_ref[...] = kk[:, :DH_IDX]
    ki2_ref[...] = kk.astype(act)
    qr = proj(C_QR, C_KR)
    kr = proj(C_KR, C_VR)
    c = rc_ref[...]
    s = rs_ref[...]
    for hh in range(H_R):
        sl = slice(DK_R * hh, DK_R * (hh + 1))
        qh = qr[:, sl]
        kh = kr[:, sl]
        qr_ref[:, sl] = (qh * c + pltpu.roll(qh, DK_R // 2, 1) * s).astype(act)
        kr_ref[:, sl] = ((kh * c + pltpu.roll(kh, DK_R // 2, 1) * s) * DK_R ** -0.5).astype(act)
    vr_ref[...] = proj(C_VR, C_GR).astype(act)
    gr_ref[...] = proj(C_GR, C_END)


def _pre_call(x, w, rot_c, rot_s, alpha, q_scale, tm, act, seq_len, name):
    n = x.shape[0]
    grid = (pl.cdiv(n, tm),)
    row = lambda width: pl.BlockSpec((tm, width), lambda i: (i, 0))
    rot_blocks = rot_c.shape[0] // tm
    rot = pl.BlockSpec((tm, LANES), lambda i: (i % rot_blocks, 0))
    feature_major = seq_len is not None
    if feature_major:
        assert seq_len % tm == 0 and n % seq_len == 0
        seq_blocks = seq_len // tm
        kv_shape = lambda width: jax.ShapeDtypeStruct((n // seq_len, width, seq_len), f32)
        kv_spec = lambda width: pl.BlockSpec((None, width, tm), lambda i: (i // seq_blocks, 0, i % seq_blocks))
    else:
        kv_shape = lambda width: jax.ShapeDtypeStruct((n, width), f32)
        kv_spec = row
    in_specs = [
        row(D_MODEL),
        _const_spec((D_MODEL, D_FF)), _const_spec((D_MODEL, D_FF)), _const_spec((D_FF, D_MODEL)),
        _const_spec((1, D_MODEL)), _const_spec((1, D_MODEL)),
        _const_spec((D_MODEL, C_END)),
        rot, rot,
    ]
    outs = [
        ("h", D_MODEL, f32), ("qa", W_A, act), ("ka", W_A, f32), ("va", W_A, f32), ("kab", W_A, act),
        ("vab", W_A, act), ("qi", W_A, act), ("ki", DH_IDX, f32), ("ki2", LANES, act), ("wi", LANES, f32),
        ("qr", W_R, act), ("kr", W_R, act), ("vr", W_R, act), ("gr", W_R, f32),
    ]
    kv_names = ("ka", "va", "ki")
    out_shape = [kv_shape(wd) if k in kv_names else jax.ShapeDtypeStruct((n, wd), dt) for k, wd, dt in outs]
    out_specs = [kv_spec(wd) if k in kv_names else row(wd) for k, wd, _ in outs]
    res = pl.pallas_call(
        functools.partial(_pre_kernel, alpha, q_scale, feature_major),
        grid=grid, in_specs=in_specs, out_specs=out_specs, out_shape=out_shape,
        compiler_params=pltpu.CompilerParams(dimension_semantics=("arbitrary",), vmem_limit_bytes=52 * MIB),
        name=name,
    )(x, w["ffn1_wg"], w["ffn1_wu"], w["ffn1_wd"], w["ln1_g"], w["ln1_b"], w["w_in"], rot_c, rot_s)
    return {k: v for (k, _, _), v in zip(outs, res)}


def _post_kernel(alpha, h_ref, oa_ref, yr_ref, wo_ref, l2g_ref, l2b_ref, wg_ref, wu_ref, wd_ref,
                 l3g_ref, l3b_ref, out_ref):
    h = h_ref[...]
    mix = (jnp.dot(oa_ref[...].astype(bf16), wo_ref[0:W_A, :], preferred_element_type=f32)
           + jnp.dot(yr_ref[...].astype(bf16), wo_ref[W_A:W_A + W_R, :], preferred_element_type=f32))
    h2 = _ln(alpha * h + mix, l2g_ref[...], l2b_ref[...])
    f = _ffn(h2.astype(bf16), wg_ref, wu_ref, wd_ref)
    out_ref[...] = _ln(alpha * h2 + 0.5 * f, l3g_ref[...], l3b_ref[...])


def _post_stream_kernel(alpha, layer, npages, grp, units, steps_per_seq, pt_ref,
                        h_ref, oa_ref, yr_ref, wo_ref, l2g_ref, l2b_ref, wg_ref, wu_ref, wd_ref, l3g_ref, l3b_ref,
                        ckk_ref, ckv_ref, qt_ref, knt_ref, vnt_ref, bias_ref, bias0_ref, mask_ref, snew_ref,
                        out_ref, ot_ref, kbuf, vbuf, sem, m_ref, l_ref, acc_ref):
    st = pl.program_id(0)
    nsteps = pl.num_programs(0)
    part = st % steps_per_seq
    half = units // 2
    first, second = range(0, half), range(half, units)

    def page_copies(step, u, p):
        seq_s, part_s = step // steps_per_seq, step % steps_per_seq
        page = pt_ref[seq_s * npages + (part_s * units + u) * grp + p]
        return (pltpu.make_async_copy(ckk_ref.at[layer, page], kbuf.at[u, p], sem.at[u, 0]),
                pltpu.make_async_copy(ckv_ref.at[layer, page], vbuf.at[u, p], sem.at[u, 1]))

    def start(step, us):
        for u in us:
            for p in range(grp):
                ck, cv = page_copies(step, u, p)
                ck.start()
                cv.start()

    def wait(us):
        for u in us:
            for p in range(grp):
                ck, cv = page_copies(st, u, p)
                ck.wait()
                cv.wait()

    def fold(us):
        for u in us:
            _satt_fold(kbuf.at[u], vbuf.at[u], qt_ref, bias_ref, mask_ref, slice(u * grp, (u + 1) * grp),
                       m_ref, l_ref, acc_ref)

    @pl.when(st == 0)
    def _():
        start(0, first)

    @pl.when(part == 0)
    def _():
        _satt_reset(m_ref, l_ref, acc_ref)

    wait(first)
    start(st, second)
    h = h_ref[...]
    mix = (jnp.dot(oa_ref[...].astype(bf16), wo_ref[0:W_A, :], preferred_element_type=f32)
           + jnp.dot(yr_ref[...].astype(bf16), wo_ref[W_A:W_A + W_R, :], preferred_element_type=f32))
    h2 = _ln(alpha * h + mix, l2g_ref[...], l2b_ref[...])
    fold(first)

    def midway():
        wait(second)

        @pl.when(st + 1 < nsteps)
        def _():
            start(st + 1, first)
        fold(second)

    f = _ffn(h2.astype(bf16), wg_ref, wu_ref, wd_ref, midway)
    out_ref[...] = _ln(alpha * h2 + 0.5 * f, l3g_ref[...], l3b_ref[...])

    @pl.when(part == steps_per_seq - 1)
    def _():
        _satt_finish(qt_ref, knt_ref, vnt_ref, bias0_ref, snew_ref, m_ref, l_ref, acc_ref, ot_ref)


def _post_stream_call(h, oa, yr, w, alpha, tm, pt_flat, cache_k_t, cache_v_t, qt, knt, vnt, bias_pos, bias0, mask,
                      snew, layer, npages, grp, units, name):
    n = h.shape[0]
    db = qt.shape[0]
    nsteps = n // tm
    steps_per_seq = npages // (grp * units)
    assert n % tm == 0 and npages % (grp * units) == 0 and nsteps == db * steps_per_seq
    row = lambda width: pl.BlockSpec((tm, width), lambda i, pt: (i, 0))
    per_seq = lambda d0, d1: pl.BlockSpec((None, d0, d1), lambda i, pt: (i // steps_per_seq, 0, 0))
    pages = units * grp
    grid_spec = pltpu.PrefetchScalarGridSpec(
        num_scalar_prefetch=1,
        grid=(nsteps,),
        in_specs=[
            row(D_MODEL), row(W_A), row(W_R),
            _const_spec((W_A + W_R, D_MODEL)), _const_spec((1, D_MODEL)), _const_spec((1, D_MODEL)),
            _const_spec((D_MODEL, D_FF)), _const_spec((D_MODEL, D_FF)), _const_spec((D_FF, D_MODEL)),
            _const_spec((1, D_MODEL)), _const_spec((1, D_MODEL)),
            pl.BlockSpec(memory_space=pl.ANY), pl.BlockSpec(memory_space=pl.ANY),
            per_seq(DH_A, H_A), per_seq(DH_A, H_A), per_seq(DH_A, H_A),
            pl.BlockSpec((H_A, pages, 1, PAGE_SIZE), lambda i, pt: (0, i % steps_per_seq, 0, 0)),
            pl.BlockSpec((1, H_A), lambda i, pt: (0, 0)),
            pl.BlockSpec((None, pages, 1, PAGE_SIZE), lambda i, pt: (i // steps_per_seq, i % steps_per_seq, 0, 0)),
            per_seq(1, LANES),
        ],
        out_specs=[row(D_MODEL), per_seq(DH_A, H_A)],
        scratch_shapes=[
            pltpu.VMEM((units, grp, H_A, DH_A, PAGE_SIZE), f32),
            pltpu.VMEM((units, grp, H_A, DH_A, PAGE_SIZE), f32),
            pltpu.SemaphoreType.DMA((units, 2)),
            pltpu.VMEM((H_A, 1, PAGE_SIZE), f32),
            pltpu.VMEM((H_A, 1, PAGE_SIZE), f32),
            pltpu.VMEM((H_A, DH_A, PAGE_SIZE), f32),
        ],
    )
    return pl.pallas_call(
        functools.partial(_post_stream_kernel, alpha, layer, npages, grp, units, steps_per_seq),
        grid_spec=grid_spec,
        out_shape=[jax.ShapeDtypeStruct((n, D_MODEL), f32), jax.ShapeDtypeStruct((db, DH_A, H_A), f32)],
        compiler_params=pltpu.CompilerParams(dimension_semantics=("arbitrary",), vmem_limit_bytes=58 * MIB),
        name=name,
    )(pt_flat, h, oa, yr, w["w_out"], w["ln2_g"], w["ln2_b"], w["ffn2_wg"], w["ffn2_wu"], w["ffn2_wd"],
      w["ln3_g"], w["ln3_b"], cache_k_t, cache_v_t, qt, knt, vnt, bias_pos, bias0, mask, snew)


def _post_call(h, oa, yr, w, alpha, tm, name):
    n = h.shape[0]
    row = lambda width: pl.BlockSpec((tm, width), lambda i: (i, 0))
    in_specs = [
        row(D_MODEL), row(W_A), row(W_R),
        _const_spec((W_A + W_R, D_MODEL)), _const_spec((1, D_MODEL)), _const_spec((1, D_MODEL)),
        _const_spec((D_MODEL, D_FF)), _const_spec((D_MODEL, D_FF)), _const_spec((D_FF, D_MODEL)),
        _const_spec((1, D_MODEL)), _const_spec((1, D_MODEL)),
    ]
    return pl.pallas_call(
        functools.partial(_post_kernel, alpha),
        grid=(pl.cdiv(n, tm),), in_specs=in_specs, out_specs=row(D_MODEL),
        out_shape=jax.ShapeDtypeStruct((n, D_MODEL), f32),
        compiler_params=pltpu.CompilerParams(dimension_semantics=("arbitrary",), vmem_limit_bytes=48 * MIB),
        name=name,
    )(h, oa, yr, w["w_out"], w["ln2_g"], w["ln2_b"], w["ffn2_wg"], w["ffn2_wu"], w["ffn2_wd"],
      w["ln3_g"], w["ln3_b"])


def _attn_kernel(topk, qa_ref, qi_ref, wi_ref, k_ref, v_ref, ki2_ref, bias_ref, oa_ref,
                 sc_ref, qap_ref, qip_ref, wb_ref, m_ref, l_ref, acc_ref):
    tq = qa_ref.shape[0]
    ncb = tq // LANES
    i = pl.program_id(1)
    nj = i + 1
    kf = float(topk)

    lane = lax.broadcasted_iota(jnp.int32, (tq, LANES), 1)
    lo_half = lane < DH_A
    for h in range(H_A):
        p = h // 2
        keep = lo_half if h % 2 == 0 else jnp.logical_not(lo_half)
        blk = slice(LANES * p, LANES * (p + 1))
        qip_ref[h] = jnp.where(keep, qi_ref[:, blk], jnp.zeros((), qi_ref.dtype))
        qap_ref[h] = jnp.where(keep, qa_ref[:, blk], jnp.zeros((), qa_ref.dtype))
        wb_ref[h] = jnp.broadcast_to(wi_ref[:, h:h + 1] * H_IDX ** -0.5, (tq, LANES))

    def score_chunk(j, carry):
        off = pl.multiple_of(j * tq, tq)
        kj = ki2_ref[pl.ds(off, tq), :]
        accs = [jnp.zeros((tq, LANES), f32) for _ in range(ncb)]
        for h in range(H_IDX):
            d = lax.dot_general(qip_ref[h], kj, NT_DIMS, preferred_element_type=f32)
            wb = wb_ref[h]
            for cb in range(ncb):
                accs[cb] = accs[cb] + wb * jnp.maximum(d[:, cb * LANES:(cb + 1) * LANES], 0.0)
        for cb in range(ncb):
            sc_ref[j, :, cb * LANES:(cb + 1) * LANES] = accs[cb]
        return carry

    lax.fori_loop(0, nj, score_chunk, 0)

    row = lax.broadcasted_iota(jnp.int32, (tq, tq), 0)
    col = lax.broadcasted_iota(jnp.int32, (tq, tq), 1)
    causal = col <= row
    sd = sc_ref[i]
    mn_diag = jnp.min(jnp.where(causal, sd, jnp.inf), axis=1, keepdims=True)
    sc_ref[i] = jnp.where(causal, sd, -jnp.inf)

    rb = min(SELECT_ROWS, tq)

    def fold(fn, init, comb, n_chunks, t=None):
        outs = []
        for r0 in range(0, tq, rb):
            t_blk = None if t is None else t[r0:r0 + rb]

            def body(j, acc, r0=r0, t_blk=t_blk):
                for cb in range(ncb):
                    acc = comb(acc, fn(sc_ref[j, r0:r0 + rb, cb * LANES:(cb + 1) * LANES], t_blk))
                return acc
            outs.append(lax.fori_loop(0, n_chunks, body, jnp.full((rb, LANES), init, f32)))
        return jnp.concatenate(outs, axis=0) if len(outs) > 1 else outs[0]

    def tile_lanes(x):
        return jnp.concatenate([x] * ncb, axis=1) if ncb > 1 else x

    def lane_sum(x):
        return jnp.broadcast_to(jnp.sum(x, axis=1, keepdims=True), (tq, LANES))

    def lane_max(x):
        return jnp.broadcast_to(jnp.max(x, axis=1, keepdims=True), (tq, LANES))

    def count_ge(t):
        return lane_sum(fold(lambda s, tb: jnp.where(s >= tb, 1.0, 0.0), 0.0, jnp.add, nj, t))

    def count_gt(t):
        return lane_sum(fold(lambda s, tb: jnp.where(s > tb, 1.0, 0.0), 0.0, jnp.add, nj, t))

    def max_below(t):
        return lane_max(fold(lambda s, tb: jnp.where(s < tb, s, -jnp.inf), -jnp.inf, jnp.maximum, nj, t))

    n_keys = (lax.broadcasted_iota(jnp.int32, (tq, LANES), 0) + (i * tq + 1)).astype(f32)
    take_all = n_keys <= kf

    mx = lane_max(fold(lambda s, _: s, -jnp.inf, jnp.maximum, nj))
    mn_far = -lane_max(-fold(lambda s, _: s, jnp.inf, jnp.minimum, i))
    mn = jnp.minimum(mn_far, jnp.broadcast_to(mn_diag, (tq, LANES)))

    @pl.when((i + 1) * tq <= topk)
    def _():
        def mk(j, carry):
            sc_ref[j] = jnp.zeros((tq, tq), f32)
            return carry
        lax.fori_loop(0, nj, mk, 0)

    @pl.when((i + 1) * tq > topk)
    def _():
        c_max = count_ge(mx)
        done0 = jnp.logical_or(take_all, c_max >= kf)
        thr0 = jnp.where(take_all, -jnp.inf, mx)
        cge0 = jnp.where(take_all, n_keys, c_max)

        def bis(_, st):
            lo, hi = st
            mid = 0.5 * (lo + hi)
            ge = count_ge(mid) >= kf
            return jnp.where(ge, mid, lo), jnp.where(ge, hi, mid)

        lo, hi = lax.fori_loop(0, BISECT_ITERS, bis, (mn, mx))

        def snap_cond(st):
            return st[0] > 0.0

        def snap_body(st):
            _, hi, thr, cge, done = st
            m = max_below(hi)
            c = count_ge(m)
            ok = c >= kf
            newly = jnp.logical_and(ok, done < 0.5)
            thr = jnp.where(newly, m, thr)
            cge = jnp.where(newly, c, cge)
            hi = jnp.where(jnp.logical_or(done > 0.5, ok), hi, m)
            done = jnp.where(ok, 1.0, done)
            left = jnp.sum(1.0 - done[:, 0:1])
            return left, hi, thr, cge, done

        done_f = jnp.where(done0, 1.0, 0.0)
        left0 = jnp.sum(1.0 - done_f[:, 0:1])
        _, _, thr, cge, _ = lax.while_loop(snap_cond, snap_body, (left0, hi, thr0, cge0, done_f))

        excess = jnp.sum(jnp.where(cge[:, 0:1] > kf, 1.0, 0.0)) > 0.0
        thr_t = tile_lanes(thr)

        @pl.when(jnp.logical_not(excess))
        def _():
            def mk(j, carry):
                sc_ref[j] = jnp.where(sc_ref[j] >= thr_t, 0.0, NEG)
                return carry
            lax.fori_loop(0, nj, mk, 0)

        @pl.when(excess)
        def _():
            need = tile_lanes(kf - count_gt(thr))
            upper = jnp.where(row <= col, 1.0, 0.0).astype(bf16)

            def mk(j, seen):
                s = sc_ref[j]
                tie = s == thr_t
                tie_f = jnp.where(tie, 1.0, 0.0)
                rank = seen + jnp.dot(tie_f.astype(bf16), upper, preferred_element_type=f32)
                sel = jnp.logical_or(s > thr_t, jnp.logical_and(tie, rank <= need))
                sc_ref[j] = jnp.where(sel, 0.0, NEG)
                part = tie_f[:, 0:LANES]
                for cb in range(1, ncb):
                    part = part + tie_f[:, cb * LANES:(cb + 1) * LANES]
                return seen + tile_lanes(lane_sum(part))
            lax.fori_loop(0, nj, mk, jnp.zeros((tq, tq), f32))

    sc_ref[i] = jnp.where(causal, sc_ref[i], NEG)


    m_ref[...] = jnp.full(m_ref.shape, -jnp.inf, f32)
    l_ref[...] = jnp.zeros(l_ref.shape, f32)
    acc_ref[...] = jnp.zeros(acc_ref.shape, f32)

    def attend_chunk(j, bias_cols):
        off = pl.multiple_of(j * tq, tq)
        sel_mask = sc_ref[j]
        for h in range(H_A):
            blk = slice(LANES * (h // 2), LANES * (h // 2 + 1))
            s = lax.dot_general(qap_ref[h], k_ref[pl.ds(off, tq), blk], NT_DIMS, preferred_element_type=f32)
            s = s + sel_mask
            if bias_cols is not None:
                s = s + bias_ref[h, :, bias_cols]
            m_old = m_ref[h]
            cmax = s[:, 0:LANES]
            for cb in range(1, ncb):
                cmax = jnp.maximum(cmax, s[:, cb * LANES:(cb + 1) * LANES])
            m_new = jnp.maximum(m_old, lane_max(cmax))
            a = jnp.exp2(m_old - m_new)
            e = jnp.exp2(s - tile_lanes(m_new))
            esum = e[:, 0:LANES]
            for cb in range(1, ncb):
                esum = esum + e[:, cb * LANES:(cb + 1) * LANES]
            l_ref[h] = l_ref[h] * a + esum
            acc_ref[h] = acc_ref[h] * a + jnp.dot(e.astype(bf16), v_ref[pl.ds(off, tq), blk],
                                                  preferred_element_type=f32)
            m_ref[h] = m_new

    def far(j, carry):
        attend_chunk(j, None)
        return carry
    lax.fori_loop(0, i - 1, far, 0)

    @pl.when(i >= 1)
    def _():
        attend_chunk(i - 1, slice(0, tq))
    attend_chunk(i, slice(tq, 2 * tq))

    for p in range(H_A // 2):
        o_even = acc_ref[2 * p] / lane_sum(l_ref[2 * p])
        o_odd = acc_ref[2 * p + 1] / lane_sum(l_ref[2 * p + 1])
        oa_ref[:, LANES * p:LANES * (p + 1)] = jnp.where(lo_half, o_even, o_odd).astype(oa_ref.dtype)


def _attn_call(qa, qi, wi, kab, vab, ki2, bias_tiles, topk, tq):
    b, s, _ = qa.shape
    nq = s // tq
    qspec = lambda width: pl.BlockSpec((None, tq, width), lambda bi, i: (bi, i, 0))
    kvspec = lambda width: pl.BlockSpec((None, s, width), lambda bi, i: (bi, 0, 0), pipeline_mode=pl.Buffered(1))
    scratch = [
        pltpu.VMEM((nq, tq, tq), f32),
        pltpu.VMEM((H_A, tq, LANES), qa.dtype),
        pltpu.VMEM((H_IDX, tq, LANES), qi.dtype),
        pltpu.VMEM((H_IDX, tq, LANES), f32),
        pltpu.VMEM((H_A, tq, LANES), f32),
        pltpu.VMEM((H_A, tq, LANES), f32),
        pltpu.VMEM((H_A, tq, LANES), f32),
    ]
    return pl.pallas_call(
        functools.partial(_attn_kernel, topk),
        grid=(b, nq),
        in_specs=[qspec(W_A), qspec(W_A), qspec(LANES), kvspec(W_A), kvspec(W_A), kvspec(LANES),
                  _const_spec((H_A, tq, 2 * tq))],
        out_specs=qspec(W_A),
        out_shape=jax.ShapeDtypeStruct((b, s, W_A), bf16),
        scratch_shapes=scratch,
        compiler_params=pltpu.CompilerParams(dimension_semantics=("arbitrary", "arbitrary"),
                                             vmem_limit_bytes=48 * MIB),
        name="attn_prompt",
    )(qa, qi, wi, kab, vab, ki2, bias_tiles)


def _attn_km_kernel(topk, qa_ref, qi_ref, wi_ref, k_ref, v_ref, ki2_ref, bias_ref, oa_ref,
                    sc_ref, qap_ref, qip_ref, m_ref, l_ref, acc_ref):
    tq = qa_ref.shape[0]
    ngrp = tq // 8
    i = pl.program_id(1)
    nj = i + 1
    kf = float(topk)

    lane = lax.broadcasted_iota(jnp.int32, (tq, LANES), 1)
    lo_half = lane < DH_A
    for h in range(H_A):
        p, par = h // 2, h % 2
        keep = lo_half if par == 0 else jnp.logical_not(lo_half)
        blk = slice(LANES * p, LANES * (p + 1))
        qip_ref[h] = jnp.where(keep, qi_ref[:, blk], jnp.zeros((), qi_ref.dtype))
        qap_ref[p, par * tq:(par + 1) * tq, :] = jnp.where(keep, qa_ref[:, blk], jnp.zeros((), qa_ref.dtype))
    w_heads = wi_ref[...].T[0:H_IDX, :] * H_IDX ** -0.5

    def row_groups(x, comb):
        parts = [x[8 * r:8 * r + 8] for r in range(ngrp)]
        while len(parts) > 1:
            parts = [comb(parts[k], parts[k + 1]) for k in range(0, len(parts) - 1, 2)] + (
                [parts[-1]] if len(parts) % 2 else [])
        return parts[0]

    def score_chunk(j, carry):
        off = pl.multiple_of(j * tq, tq)
        kj = ki2_ref[pl.ds(off, tq), :]
        acc = jnp.zeros((tq, tq), f32)
        for h in range(H_IDX):
            d = lax.dot_general(kj, qip_ref[h], NT_DIMS, preferred_element_type=f32)
            acc = acc + w_heads[h:h + 1, :] * jnp.maximum(d, 0.0)
        sc_ref[j] = acc
        return carry

    lax.fori_loop(0, nj, score_chunk, 0)

    krow = lax.broadcasted_iota(jnp.int32, (tq, tq), 0)
    qcol = lax.broadcasted_iota(jnp.int32, (tq, tq), 1)
    causal = krow <= qcol
    sd = sc_ref[i]
    mn_diag = jnp.min(jnp.where(causal, sd, jnp.inf), axis=0, keepdims=True)
    sc_ref[i] = jnp.where(causal, sd, -jnp.inf)

    def fold(fn, init, comb, n_chunks, t=None):
        t8 = None if t is None else jnp.broadcast_to(t, (8, tq))

        nacc = 4

        def body(j, accs):
            accs = list(accs)
            for r in range(ngrp):
                accs[r % nacc] = comb(accs[r % nacc], fn(sc_ref[j, 8 * r:8 * r + 8, :], t8))
            return tuple(accs)
        accs = lax.fori_loop(0, n_chunks, body, tuple(jnp.full((8, tq), init, f32) for _ in range(nacc)))
        acc = comb(comb(accs[0], accs[1]), comb(accs[2], accs[3]))
        if comb is jnp.add:
            return jnp.sum(acc, axis=0, keepdims=True)
        if comb is jnp.maximum:
            return jnp.max(acc, axis=0, keepdims=True)
        return jnp.min(acc, axis=0, keepdims=True)

    def count_ge(t):
        return fold(lambda s, t8: jnp.where(s >= t8, 1.0, 0.0), 0.0, jnp.add, nj, t)

    def count_gt(t):
        return fold(lambda s, t8: jnp.where(s > t8, 1.0, 0.0), 0.0, jnp.add, nj, t)

    def max_below(t):
        return fold(lambda s, t8: jnp.where(s < t8, s, -jnp.inf), -jnp.inf, jnp.maximum, nj, t)

    n_keys = (lax.broadcasted_iota(jnp.int32, (1, tq), 1) + (i * tq + 1)).astype(f32)
    take_all = n_keys <= kf

    @pl.when((i + 1) * tq <= topk)
    def _():
        def mk(j, carry):
            sc_ref[j] = jnp.zeros((tq, tq), f32)
            return carry
        lax.fori_loop(0, nj, mk, 0)

    @pl.when((i + 1) * tq > topk)
    def _():
        mx = fold(lambda s, _: s, -jnp.inf, jnp.maximum, nj)
        mn = jnp.minimum(fold(lambda s, _: s, jnp.inf, jnp.minimum, i), mn_diag)
        c_max = count_ge(mx)
        done0 = jnp.logical_or(take_all, c_max >= kf)
        thr0 = jnp.where(take_all, -jnp.inf, mx)
        cge0 = jnp.where(take_all, n_keys, c_max)

        def bis(_, st):
            lo, hi = st
            mid = 0.5 * (lo + hi)
            ge = count_ge(mid) >= kf
            return jnp.where(ge, mid, lo), jnp.where(ge, hi, mid)

        lo, hi = lax.fori_loop(0, BISECT_ITERS, bis, (mn, mx))

        def snap_cond(st):
            return st[0] > 0.0

        def snap_body(st):
            _, hi, thr, cge, done = st
            m = max_below(hi)
            c = count_ge(m)
            ok = c >= kf
            newly = jnp.logical_and(ok, done < 0.5)
            thr = jnp.where(newly, m, thr)
            cge = jnp.where(newly, c, cge)
            hi = jnp.where(jnp.logical_or(done > 0.5, ok), hi, m)
            done = jnp.where(ok, 1.0, done)
            return jnp.sum(1.0 - done), hi, thr, cge, done

        done_f = jnp.where(done0, 1.0, 0.0)
        _, _, thr, cge, _ = lax.while_loop(snap_cond, snap_body, (jnp.sum(1.0 - done_f), hi, thr0, cge0, done_f))

        excess = jnp.sum(jnp.where(cge > kf, 1.0, 0.0)) > 0.0

        @pl.when(jnp.logical_not(excess))
        def _():
            def mk(j, carry):
                sc_ref[j] = jnp.where(sc_ref[j] >= thr, 0.0, NEG)
                return carry
            lax.fori_loop(0, nj, mk, 0)

        @pl.when(excess)
        def _():
            need = kf - count_gt(thr)
            lower = jnp.where(qcol <= krow, 1.0, 0.0).astype(bf16)

            def mk(j, seen):
                s = sc_ref[j]
                tie = s == thr
                rank = seen + jnp.dot(lower, jnp.where(tie, 1.0, 0.0).astype(bf16), preferred_element_type=f32)
                sel = jnp.logical_or(s > thr, jnp.logical_and(tie, rank <= need))
                sc_ref[j] = jnp.where(sel, 0.0, NEG)
                return rank[tq - 1:tq, :]
            lax.fori_loop(0, nj, mk, jnp.zeros((1, tq), f32))

    sc_ref[i] = jnp.where(causal, sc_ref[i], NEG)

    m_ref[...] = jnp.full(m_ref.shape, -jnp.inf, f32)
    l_ref[...] = jnp.zeros(l_ref.shape, f32)
    acc_ref[...] = jnp.zeros(acc_ref.shape, f32)
    ncb = tq // LANES

    def lane_blocks(x, comb):
        out = x[:, 0:LANES]
        for cb in range(1, x.shape[1] // LANES):
            out = comb(out, x[:, cb * LANES:(cb + 1) * LANES])
        return out

    def attend(j, width, bias_cols):
        off = pl.multiple_of(j * tq, tq)
        keys = pl.ds(off, width * tq)
        sel_mask = jnp.concatenate([sc_ref[j + c].T for c in range(width)], axis=1)
        for p in range(H_A // 2):
            blk = slice(LANES * p, LANES * (p + 1))
            s_pair = lax.dot_general(qap_ref[p], k_ref[keys, blk], NT_DIMS, preferred_element_type=f32)
            es = []
            for par in range(2):
                h = 2 * p + par
                s = s_pair[par * tq:(par + 1) * tq] + sel_mask
                if bias_cols is not None:
                    s = s + bias_ref[h, :, bias_cols]
                m_old = m_ref[h]
                row_max = jnp.max(lane_blocks(s, jnp.maximum), axis=1, keepdims=True)
                m_new = jnp.maximum(m_old, jnp.broadcast_to(row_max, (tq, LANES)))
                a = jnp.exp2(m_old - m_new)
                e = jnp.exp2(s - jnp.concatenate([m_new] * (width * ncb), axis=1))
                l_ref[h] = l_ref[h] * a + lane_blocks(e, jnp.add)
                acc_ref[h] = acc_ref[h] * a
                m_ref[h] = m_new
                es.append(e.astype(bf16))
            pv = jnp.dot(jnp.concatenate(es, axis=0), v_ref[keys, blk], preferred_element_type=f32)
            acc_ref[2 * p] += pv[0:tq]
            acc_ref[2 * p + 1] += pv[tq:2 * tq]

    n_far = jnp.maximum(i - 1, 0)
    if ATT_FAR_WIDTH == 2:
        def far_pair(jj, carry):
            attend(2 * jj, 2, None)
            return carry
        lax.fori_loop(0, n_far // 2, far_pair, 0)

        @pl.when(n_far % 2 == 1)
        def _():
            attend(n_far - 1, 1, None)
    else:
        def far_one(j, carry):
            attend(j, 1, None)
            return carry
        lax.fori_loop(0, n_far, far_one, 0)

    @pl.when(i >= 1)
    def _():
        attend(i - 1, 2, slice(0, 2 * tq))

    @pl.when(i == 0)
    def _():
        attend(0, 1, slice(tq, 2 * tq))

    for p in range(H_A // 2):
        l_even = jnp.broadcast_to(jnp.sum(l_ref[2 * p], axis=1, keepdims=True), (tq, LANES))
        l_odd = jnp.broadcast_to(jnp.sum(l_ref[2 * p + 1], axis=1, keepdims=True), (tq, LANES))
        o_pair = jnp.where(lo_half, acc_ref[2 * p] / l_even, acc_ref[2 * p + 1] / l_odd)
        oa_ref[:, LANES * p:LANES * (p + 1)] = o_pair.astype(oa_ref.dtype)


def _attn_km_call(qa, qi, wt, kab, vab, ki2, bias_tiles, topk, tq):
    b, s, _ = qa.shape
    nq = s // tq
    qspec = lambda width: pl.BlockSpec((None, tq, width), lambda bi, i: (bi, i, 0))
    whole = lambda shape: pl.BlockSpec((None,) + shape, lambda bi, i: (bi,) + (0,) * len(shape),
                                       pipeline_mode=pl.Buffered(1))
    scratch = [
        pltpu.VMEM((nq, tq, tq), f32),
        pltpu.VMEM((H_A // 2, 2 * tq, LANES), qa.dtype),
        pltpu.VMEM((H_IDX, tq, LANES), qi.dtype),
        pltpu.VMEM((H_A, tq, LANES), f32),
        pltpu.VMEM((H_A, tq, LANES), f32),
        pltpu.VMEM((H_A, tq, LANES), f32),
    ]
    return pl.pallas_call(
        functools.partial(_attn_km_kernel, topk),
        grid=(b, nq),
        in_specs=[qspec(W_A), qspec(W_A),
                  qspec(LANES),
                  whole((s, W_A)), whole((s, W_A)), whole((s, LANES)),
                  _const_spec((H_A, tq, 2 * tq))],
        out_specs=qspec(W_A),
        out_shape=jax.ShapeDtypeStruct((b, s, W_A), bf16),
        scratch_shapes=scratch,
        compiler_params=pltpu.CompilerParams(dimension_semantics=("arbitrary", "arbitrary"),
                                             vmem_limit_bytes=48 * MIB),
        name="attn_prompt",
    )(qa, qi, wt, kab, vab, ki2, bias_tiles)


def _ret_kernel(qr_ref, kr_ref, vr_ref, gr_ref, gng_ref, dmat_ref, cross_ref, kvd_ref, cdec_ref,
                yr_ref, st_ref):
    c = pl.program_id(1)

    @pl.when(c == 0)
    def _():
        st_ref[...] = jnp.zeros(st_ref.shape, f32)

    for h in range(H_R):
        sl = slice(DK_R * h, DK_R * (h + 1))
        q = qr_ref[:, sl]
        k = kr_ref[:, sl]
        v = vr_ref[:, sl]
        att = lax.dot_general(q, k, NT_DIMS, preferred_element_type=f32) * dmat_ref[h]
        st = st_ref[h]
        o = (jnp.dot(att.astype(bf16), v, preferred_element_type=f32)
             + jnp.dot(q, st.astype(bf16), preferred_element_type=f32) * cross_ref[h])
        kd = (k.astype(f32) * kvd_ref[h]).T.astype(bf16)
        st_ref[h] = cdec_ref[h, 0:1, :] * st + jnp.dot(kd, v, preferred_element_type=f32)
        mu = jnp.mean(o, axis=-1, keepdims=True)
        oc = o - mu
        var = jnp.mean(oc * oc, axis=-1, keepdims=True)
        yn = oc * lax.rsqrt(var + GN_EPS) * gng_ref[:, sl]
        yr_ref[:, sl] = (jax.nn.silu(gr_ref[:, sl]) * yn).astype(yr_ref.dtype)


def _ret_call(qr, kr, vr, gr, gng, dec, chunk):
    b, s, _ = qr.shape
    nc = s // chunk
    rspec = pl.BlockSpec((None, chunk, W_R), lambda bi, c: (bi, c, 0))
    return pl.pallas_call(
        _ret_kernel,
        grid=(b, nc),
        in_specs=[rspec, rspec, rspec, rspec, _const_spec((1, W_R)),
                  _const_spec((H_R, chunk, chunk)), _const_spec((H_R, chunk, LANES)),
                  _const_spec((H_R, chunk, LANES)), _const_spec((H_R, 8, LANES))],
        out_specs=[rspec, pl.BlockSpec((None, H_R, DK_R, DV_R), lambda bi, c: (bi, 0, 0, 0))],
        out_shape=[jax.ShapeDtypeStruct((b, s, W_R), bf16), jax.ShapeDtypeStruct((b, H_R, DK_R, DV_R), f32)],
        compiler_params=pltpu.CompilerParams(dimension_semantics=("arbitrary", "arbitrary"),
                                             vmem_limit_bytes=32 * MIB),
        name="ret_prompt",
    )(qr, kr, vr, gr, gng, dec["intra"], dec["cross"], dec["kv"], dec["chunk"])


def _sscore_kernel(layer, npages, pt_ref, ck_ref, q_ref, w_ref, kn_ref, sc_ref, snew_ref, kbuf, sem):
    b = pl.program_id(0)
    nb = pl.num_programs(0)
    slot = b % 2

    def page_copy(seq, p, sl):
        page = pt_ref[seq * npages + p]
        return pltpu.make_async_copy(ck_ref.at[layer, page], kbuf.at[sl, p], sem.at[sl])

    def start_all(seq, sl):
        def body(p, carry):
            page_copy(seq, p, sl).start()
            return carry
        lax.fori_loop(0, npages, body, 0)

    @pl.when(b == 0)
    def _():
        start_all(0, 0)

    @pl.when(b + 1 < nb)
    def _():
        start_all(b + 1, 1 - slot)

    def wait_body(p, carry):
        page_copy(b, p, slot).wait()
        return carry
    lax.fori_loop(0, npages, wait_body, 0)

    q = q_ref[...].astype(bf16)
    w = w_ref[...]

    pages_per_dot = min(8, npages)
    for p0 in range(0, npages, pages_per_dot):
        kw = jnp.concatenate([kbuf[slot, p] for p in range(p0, p0 + pages_per_dot)], axis=1).astype(bf16)
        d = jnp.dot(q, kw, preferred_element_type=f32)
        sc_ref[:, p0 * PAGE_SIZE:(p0 + pages_per_dot) * PAGE_SIZE] = jnp.sum(
            w * jnp.maximum(d, 0.0), axis=0, keepdims=True)

    kn = kn_ref[...].astype(bf16).astype(f32)
    dn = jnp.sum(q.astype(f32) * kn, axis=1, keepdims=True)
    s_new = jnp.sum(w * jnp.maximum(dn, 0.0), axis=0, keepdims=True)
    snew_ref[...] = jnp.broadcast_to(s_new, (1, LANES))


def _sscore_call(page_table_flat, cache_kidx_t, q16, w16, kn, layer, npages):
    db = q16.shape[0]
    assert npages % min(8, npages) == 0
    grid_spec = pltpu.PrefetchScalarGridSpec(
        num_scalar_prefetch=1,
        grid=(db,),
        in_specs=[
            pl.BlockSpec(memory_space=pl.ANY),
            pl.BlockSpec((None, 16, DH_IDX), lambda b, pt: (b, 0, 0)),
            pl.BlockSpec((None, 16, 1), lambda b, pt: (b, 0, 0)),
            pl.BlockSpec((None, 1, DH_IDX), lambda b, pt: (b, 0, 0)),
        ],
        out_specs=[pl.BlockSpec((None, 1, npages * PAGE_SIZE), lambda b, pt: (b, 0, 0)),
                   pl.BlockSpec((None, 1, LANES), lambda b, pt: (b, 0, 0))],
        scratch_shapes=[
            pltpu.VMEM((2, npages, DH_IDX, PAGE_SIZE), f32),
            pltpu.SemaphoreType.DMA((2,)),
        ],
    )
    return pl.pallas_call(
        functools.partial(_sscore_kernel, layer, npages),
        grid_spec=grid_spec,
        out_shape=[jax.ShapeDtypeStruct((db, 1, npages * PAGE_SIZE), f32),
                   jax.ShapeDtypeStruct((db, 1, LANES), f32)],
        compiler_params=pltpu.CompilerParams(dimension_semantics=("arbitrary",), vmem_limit_bytes=32 * MIB),
        name="sample_scores",
    )(page_table_flat, cache_kidx_t, q16, w16, kn)


PREFIX_CHUNK = 256


def _ssel_kernel(topk, sc_ref, sn_ref, mask_ref, selnew_ref):
    kf = float(topk)
    db, length = sc_ref.shape
    sc = sc_ref[...]
    s_new = sn_ref[:, 0:1]

    def rsum(x):
        return jnp.sum(x, axis=1, keepdims=True)

    def count(cmp, t):
        return rsum(jnp.where(cmp(sc, t), 1.0, 0.0)) + jnp.where(cmp(s_new, t), 1.0, 0.0)

    ge = lambda a, t: a >= t
    gt = lambda a, t: a > t
    mx = jnp.maximum(jnp.max(sc, axis=1, keepdims=True), s_new)
    mn = jnp.minimum(jnp.min(sc, axis=1, keepdims=True), s_new)
    c_max = count(ge, mx)
    done0 = jnp.where(c_max >= kf, 1.0, 0.0)

    def bis(_, st):
        lo, hi = st
        mid = 0.5 * (lo + hi)
        ok = count(ge, mid) >= kf
        return jnp.where(ok, mid, lo), jnp.where(ok, hi, mid)
    lo, hi = lax.fori_loop(0, BISECT_ITERS, bis, (mn, mx))

    def snap_cond(st):
        return st[0] > 0.0

    def snap_body(st):
        _, hi, thr, done = st
        below = jnp.maximum(jnp.max(jnp.where(sc < hi, sc, -jnp.inf), axis=1, keepdims=True),
                            jnp.where(s_new < hi, s_new, -jnp.inf))
        ok = count(ge, below) >= kf
        newly = jnp.logical_and(ok, done < 0.5)
        thr = jnp.where(newly, below, thr)
        hi = jnp.where(jnp.logical_or(done > 0.5, ok), hi, below)
        done = jnp.where(ok, 1.0, done)
        return jnp.sum(1.0 - done), hi, thr, done
    _, _, thr, _ = lax.while_loop(snap_cond, snap_body, (jnp.sum(1.0 - done0), hi, mx, done0))

    need = kf - count(gt, thr)
    pc = min(PREFIX_CHUNK, length)
    r_i = lax.broadcasted_iota(jnp.int32, (pc, pc), 0)
    c_i = lax.broadcasted_iota(jnp.int32, (pc, pc), 1)
    upper = jnp.where(r_i <= c_i, 1.0, 0.0).astype(bf16)
    seen = jnp.zeros((db, 1), f32)
    for c0 in range(0, length, pc):
        s_c = sc[:, c0:c0 + pc]
        tie = s_c == thr
        rank = seen + jnp.dot(jnp.where(tie, 1.0, 0.0).astype(bf16), upper, preferred_element_type=f32)
        sel = jnp.logical_or(s_c > thr, jnp.logical_and(tie, rank <= need))
        mask_ref[:, c0:c0 + pc] = jnp.where(sel, 0.0, NEG)
        seen = rank[:, pc - 1:pc]
    sel_new = jnp.logical_or(s_new > thr, jnp.logical_and(s_new == thr, seen + 1.0 <= need))
    selnew_ref[...] = jnp.broadcast_to(jnp.where(sel_new, 1.0, 0.0), (db, LANES))


def _ssel_call(scores, s_new, topk):
    db, length = scores.shape
    assert length % min(PREFIX_CHUNK, length) == 0
    return pl.pallas_call(
        functools.partial(_ssel_kernel, topk),
        out_shape=[jax.ShapeDtypeStruct((db, length), f32), jax.ShapeDtypeStruct((db, LANES), f32)],
        compiler_params=pltpu.CompilerParams(vmem_limit_bytes=32 * MIB),
        name="sample_select",
    )(scores, s_new)


def _satt_kernel(layer, npages, grp, pt_ref,
                 ckk_ref, ckv_ref, qt_ref, knt_ref, vnt_ref, bias_ref, bias0_ref, mask_ref, snew_ref,
                 qr_ref, kr_ref, vr_ref, gr_ref, gng_ref, gam_ref, s0_ref,
                 ot_ref, yr_ref, sn_ref,
                 kbuf, vbuf, sem, m_ref, l_ref, acc_ref):
    b = pl.program_id(0)
    g = pl.program_id(1)
    nb = pl.num_programs(0)
    ng = pl.num_programs(1)
    t = b * ng + g
    slot = t % 2

    def page_copies(seq, gi, p, sl):
        page = pt_ref[seq * npages + gi * grp + p]
        return (pltpu.make_async_copy(ckk_ref.at[layer, page], kbuf.at[sl, p], sem.at[sl, 0]),
                pltpu.make_async_copy(ckv_ref.at[layer, page], vbuf.at[sl, p], sem.at[sl, 1]))

    def start_all(seq, gi, sl):
        def body(p, carry):
            ck, cv = page_copies(seq, gi, p, sl)
            ck.start()
            cv.start()
            return carry
        lax.fori_loop(0, grp, body, 0)

    @pl.when(t == 0)
    def _():
        start_all(0, 0, 0)

    @pl.when(t + 1 < nb * ng)
    def _():
        wrap = g + 1 == ng
        start_all(jnp.where(wrap, b + 1, b), jnp.where(wrap, 0, g + 1), 1 - slot)

    @pl.when(g == 0)
    def _():
        m_ref[...] = jnp.full(m_ref.shape, -jnp.inf, f32)
        l_ref[...] = jnp.zeros(l_ref.shape, f32)
        acc_ref[...] = jnp.zeros(acc_ref.shape, f32)

    def wait_body(p, carry):
        ck, cv = page_copies(b, g, p, slot)
        ck.wait()
        cv.wait()
        return carry
    lax.fori_loop(0, grp, wait_body, 0)

    _satt_fold(kbuf.at[slot], vbuf.at[slot], qt_ref, bias_ref, mask_ref, slice(0, grp), m_ref, l_ref, acc_ref)

    @pl.when(g == ng - 1)
    def _():
        _satt_finish(qt_ref, knt_ref, vnt_ref, bias0_ref, snew_ref, m_ref, l_ref, acc_ref, ot_ref)

    @pl.when(g == 0)
    def _():
        _sample_retention_step(qr_ref, kr_ref, vr_ref, gr_ref, gng_ref, gam_ref, s0_ref, yr_ref, sn_ref)


def _satt_fold(kb_ref, vb_ref, qt_ref, bias_ref, mask_ref, rows, m_ref, l_ref, acc_ref):
    mask = mask_ref[rows]
    for h in range(H_A):
        qc = qt_ref[:, h:h + 1]
        kh = kb_ref[:, h]
        lg = jnp.sum(kh * qc[None], axis=1, keepdims=True) + bias_ref[h, rows] + mask
        m_old = m_ref[h]
        m_blk = jnp.max(jnp.max(lg, axis=0), axis=1, keepdims=True)
        m_new = jnp.maximum(m_old, m_blk)
        a = jnp.exp(m_old - m_new)
        e = jnp.exp(lg - m_new[None])
        l_ref[h] = l_ref[h] * a + jnp.sum(e, axis=0)
        acc_ref[h] = acc_ref[h] * a + jnp.sum(vb_ref[:, h] * e, axis=0)
        m_ref[h] = m_new


def _satt_finish(qt_ref, knt_ref, vnt_ref, bias0_ref, snew_ref, m_ref, l_ref, acc_ref, ot_ref):
    lg_new = jnp.sum(qt_ref[...] * knt_ref[...], axis=0, keepdims=True) + bias0_ref[...]
    lg_new = jnp.where(snew_ref[0:1, 0:1] > 0.5, lg_new, NEG)
    for h in range(H_A):
        m_c = m_ref[h][:, 0:1]
        lg_h = lg_new[:, h:h + 1]
        m_f = jnp.maximum(m_c, lg_h)
        a = jnp.exp(m_c - m_f)
        e_new = jnp.exp(lg_h - m_f)
        den = jnp.sum(l_ref[h], axis=1, keepdims=True) * a + e_new
        num = jnp.sum(acc_ref[h], axis=1, keepdims=True) * a + e_new * vnt_ref[:, h:h + 1]
        ot_ref[:, h:h + 1] = num / den


def _satt_reset(m_ref, l_ref, acc_ref):
    m_ref[...] = jnp.full(m_ref.shape, -jnp.inf, f32)
    l_ref[...] = jnp.zeros(l_ref.shape, f32)
    acc_ref[...] = jnp.zeros(acc_ref.shape, f32)


def _sample_retention_step(qr_ref, kr_ref, vr_ref, gr_ref, gng_ref, gam_ref, s0_ref, yr_ref, sn_ref):
    r_i = lax.broadcasted_iota(jnp.int32, (DK_R, DK_R), 0)
    c_i = lax.broadcasted_iota(jnp.int32, (DK_R, DK_R), 1)
    eye = jnp.where(r_i == c_i, 1.0, 0.0)
    for h in range(H_R):
        qrow = qr_ref[h:h + 1, :]
        krow = kr_ref[h:h + 1, :]
        vrow = vr_ref[h:h + 1, :]
        gam = gam_ref[h:h + 1, :]
        qcol = jnp.sum(eye * qrow, axis=1, keepdims=True)
        kcol = jnp.sum(eye * krow, axis=1, keepdims=True)
        st = s0_ref[h]
        qk = jnp.sum(qrow * krow, axis=1, keepdims=True)
        o = qk * vrow + gam * jnp.sum(qcol * st, axis=0, keepdims=True)
        sn_ref[h] = gam * st + kcol * vrow
        mu = jnp.mean(o, axis=1, keepdims=True)
        oc = o - mu
        var = jnp.mean(oc * oc, axis=1, keepdims=True)
        yn = oc * lax.rsqrt(var + GN_EPS) * gng_ref[h:h + 1, :]
        yr_ref[h:h + 1, :] = jax.nn.silu(gr_ref[h:h + 1, :]) * yn


def _sret_kernel(qr_ref, kr_ref, vr_ref, gr_ref, gng_ref, gam_ref, s0_ref, yr_ref, sn_ref):
    _sample_retention_step(qr_ref, kr_ref, vr_ref, gr_ref, gng_ref, gam_ref, s0_ref, yr_ref, sn_ref)


def _sret_call(qr, kr, vr, gr, gng, gam, state, layer):
    db = qr.shape[0]
    per_seq = lambda d0, d1: pl.BlockSpec((None, d0, d1), lambda b: (b, 0, 0))
    shared = pl.BlockSpec((H_R, DV_R), lambda b: (0, 0))
    return pl.pallas_call(
        _sret_kernel,
        grid=(db,),
        in_specs=[per_seq(H_R, DK_R), per_seq(H_R, DK_R), per_seq(H_R, DV_R), per_seq(H_R, DV_R), shared, shared,
                  pl.BlockSpec((None, None, H_R, DK_R, DV_R), lambda b: (layer, b, 0, 0, 0))],
        out_specs=[per_seq(H_R, DV_R), pl.BlockSpec((None, H_R, DK_R, DV_R), lambda b: (b, 0, 0, 0))],
        out_shape=[jax.ShapeDtypeStruct((db, H_R, DV_R), f32), jax.ShapeDtypeStruct((db, H_R, DK_R, DV_R), f32)],
        compiler_params=pltpu.CompilerParams(dimension_semantics=("arbitrary",)),
        name="sample_retention",
    )(qr, kr, vr, gr, gng, gam, state)


def _satt_call(pt_flat, cache_k_t, cache_v_t, qt, knt, vnt, bias_pos, bias0, mask, snew, qr, kr, vr, gr, gng, gam,
               state, layer, npages, grp):
    db = qt.shape[0]
    ng = npages // grp
    per_seq = lambda d0, d1: pl.BlockSpec((None, d0, d1), lambda b, g, pt: (b, 0, 0))
    grid_spec = pltpu.PrefetchScalarGridSpec(
        num_scalar_prefetch=1,
        grid=(db, ng),
        in_specs=[
            pl.BlockSpec(memory_space=pl.ANY), pl.BlockSpec(memory_space=pl.ANY),
            per_seq(DH_A, H_A), per_seq(DH_A, H_A), per_seq(DH_A, H_A),
            pl.BlockSpec((H_A, grp, 1, PAGE_SIZE), lambda b, g, pt: (0, g, 0, 0)),
            pl.BlockSpec((1, H_A), lambda b, g, pt: (0, 0)),
            pl.BlockSpec((None, grp, 1, PAGE_SIZE), lambda b, g, pt: (b, g, 0, 0)),
            per_seq(1, LANES),
            per_seq(H_R, DK_R), per_seq(H_R, DK_R), per_seq(H_R, DV_R), per_seq(H_R, DV_R),
            pl.BlockSpec((H_R, DV_R), lambda b, g, pt: (0, 0)),
            pl.BlockSpec((H_R, DV_R), lambda b, g, pt: (0, 0)),
            pl.BlockSpec((None, None, H_R, DK_R, DV_R), lambda b, g, pt: (layer, b, 0, 0, 0)),
        ],
        out_specs=[per_seq(DH_A, H_A), per_seq(H_R, DV_R),
                   pl.BlockSpec((None, H_R, DK_R, DV_R), lambda b, g, pt: (b, 0, 0, 0))],
        scratch_shapes=[
            pltpu.VMEM((2, grp, H_A, DH_A, PAGE_SIZE), f32),
            pltpu.VMEM((2, grp, H_A, DH_A, PAGE_SIZE), f32),
            pltpu.SemaphoreType.DMA((2, 2)),
            pltpu.VMEM((H_A, 1, PAGE_SIZE), f32),
            pltpu.VMEM((H_A, 1, PAGE_SIZE), f32),
            pltpu.VMEM((H_A, DH_A, PAGE_SIZE), f32),
        ],
    )
    return pl.pallas_call(
        functools.partial(_satt_kernel, layer, npages, grp),
        grid_spec=grid_spec,
        out_shape=[jax.ShapeDtypeStruct((db, DH_A, H_A), f32), jax.ShapeDtypeStruct((db, H_R, DV_R), f32),
                   jax.ShapeDtypeStruct((db, H_R, DK_R, DV_R), f32)],
        compiler_params=pltpu.CompilerParams(dimension_semantics=("arbitrary", "arbitrary"),
                                             vmem_limit_bytes=40 * MIB),
        name="sample_attend",
    )(pt_flat, cache_k_t, cache_v_t, qt, knt, vnt, bias_pos, bias0, mask, snew, qr, kr, vr, gr, gng, gam, state)


def _t5_bucket(rel):
    n = jnp.maximum(rel, 0)
    max_exact = NUM_BUCKETS // 2
    nf = jnp.maximum(n, 1).astype(f32)
    large = max_exact + (jnp.log(nf / max_exact) / math.log(MAX_DISTANCE / max_exact)
                         * (NUM_BUCKETS - max_exact)).astype(jnp.int32)
    large = jnp.minimum(large, NUM_BUCKETS - 1)
    return jnp.where(n < max_exact, n, large)


def _rotary_tables(pos):
    half = DK_R // 2
    freqs = ROPE_BASE ** (-jnp.arange(half, dtype=f32) / half)
    ang = pos.astype(f32)[:, None] * freqs[None, :]
    cos, sin = jnp.cos(ang), jnp.sin(ang)
    return jnp.concatenate([cos, cos], axis=1), jnp.concatenate([-sin, sin], axis=1)


def _decay_tables(chunk):
    lg = jnp.log1p(-jnp.exp2(-5.0 - jnp.arange(H_R, dtype=f32)))
    i = jnp.arange(chunk, dtype=f32)
    diff = i[:, None] - i[None, :]
    causal = diff >= 0
    intra = jnp.where(causal[None], jnp.exp(jnp.where(causal, diff, 0.0)[None] * lg[:, None, None]), 0.0)
    cross = jnp.exp((i[None, :] + 1.0) * lg[:, None])
    kv = jnp.exp((chunk - 1.0 - i)[None, :] * lg[:, None])
    cdec = jnp.exp(chunk * lg)
    bc = lambda a: jnp.broadcast_to(a[:, :, None], (H_R, chunk, LANES))
    return {"intra": intra, "cross": bc(cross), "kv": bc(kv),
            "chunk": jnp.broadcast_to(cdec[:, None, None], (H_R, 8, LANES))}


def _bias_tiles(rel_bias, tq):
    assert _last_bucket_from(tq + 1), "keys beyond the previous chunk must share the last bucket"
    rb = rel_bias.astype(f32) - rel_bias[NUM_BUCKETS - 1].astype(f32)[None, :]
    tab = rb[_t5_bucket(jnp.arange(2 * tq + 1, dtype=jnp.int32))].T
    m = 3 * tq
    g = jnp.concatenate([tab[:, tq::-1],
                         jnp.broadcast_to(tab[:, 0:1], (H_A, tq - 1)),
                         tab[:, 2 * tq:tq:-1]], axis=1)
    flat = jnp.tile(g, (1, tq))[:, :tq * (m - 1)]
    return flat.reshape(H_A, tq, m - 1)[:, :, :2 * tq]


def _last_bucket_from(n):
    large = NUM_BUCKETS // 2 + int(math.log(n / (NUM_BUCKETS // 2)) / math.log(MAX_DISTANCE / (NUM_BUCKETS // 2))
                                   * (NUM_BUCKETS - NUM_BUCKETS // 2) - 1e-3)
    return n >= NUM_BUCKETS // 2 and large >= NUM_BUCKETS - 1


def _layer_weights(l, ffn1_wg, ffn1_wu, ffn1_wd, ln1_g, ln1_b, w_in, ret_gn_g, w_out, ln2_g, ln2_b,
                   ffn2_wg, ffn2_wu, ffn2_wd, ln3_g, ln3_b):
    pts = np.cumsum((0,) + IN_SIZES)
    wi = w_in[l]
    col = lambda k: wi[:, pts[k]:pts[k + 1]]
    zeros = jnp.zeros((D_MODEL, LANES - H_IDX), wi.dtype)
    w_in2 = jnp.concatenate([col(0), col(1), col(2), col(3), col(4), col(4), col(5), zeros,
                             col(6), col(7), col(8), col(9)], axis=1).astype(bf16)
    r2 = lambda a: a[l].reshape(1, -1).astype(f32)
    return {
        "ffn1_wg": ffn1_wg[l].astype(bf16), "ffn1_wu": ffn1_wu[l].astype(bf16), "ffn1_wd": ffn1_wd[l].astype(bf16),
        "ln1_g": r2(ln1_g), "ln1_b": r2(ln1_b), "w_in": w_in2, "gng": r2(ret_gn_g),
        "w_out": w_out[l].astype(bf16), "ln2_g": r2(ln2_g), "ln2_b": r2(ln2_b),
        "ffn2_wg": ffn2_wg[l].astype(bf16), "ffn2_wu": ffn2_wu[l].astype(bf16), "ffn2_wd": ffn2_wd[l].astype(bf16),
        "ln3_g": r2(ln3_g), "ln3_b": r2(ln3_b),
    }


def kernel(x_prompt, x_sample, cache_k, cache_v, cache_kidx, state_ret, page_table, rel_bias,
           ffn1_wg, ffn1_wu, ffn1_wd, ln1_g, ln1_b, w_in, ret_gn_g, w_out,
           ln2_g, ln2_b, ffn2_wg, ffn2_wu, ffn2_wd, ln3_g, ln3_b):
    b, s, _ = x_prompt.shape
    db, ds, _ = x_sample.shape
    depth = w_in.shape[0]
    npages = page_table.shape[1]
    past = npages * PAGE_SIZE
    assert ds == 1, "the sample group decodes one token per sequence"
    alpha = (2 * depth) ** 0.25

    tq = min(ATT_TQ, s)
    chunk = min(RET_CHUNK, s)
    assert s % tq == 0 and s % chunk == 0 and tq % LANES == 0
    topk_p = min(TOPK_MAX, s // 4)
    topk_s = min(TOPK_MAX, (past + ds) // 4)

    rot_p = _rotary_tables(jnp.arange(s, dtype=jnp.int32))
    rot_s = _rotary_tables(jnp.full((db * ds,), past, jnp.int32))
    dec_p = _decay_tables(chunk)
    gam = jnp.broadcast_to(_decay_tables(1)["chunk"][:, 0, :], (H_R, LANES))
    bias_tiles = _bias_tiles(rel_bias, tq)
    near = min(past, 2 * PAGE_SIZE)
    assert _last_bucket_from(near + 1)
    bias_near = rel_bias.astype(f32)[_t5_bucket(jnp.arange(near, 0, -1, dtype=jnp.int32))].T
    bias_far = jnp.broadcast_to(rel_bias[NUM_BUCKETS - 1].astype(f32)[:, None], (H_A, past - near))
    bias_pos = jnp.concatenate([bias_far, bias_near], axis=1).reshape(H_A, npages, 1, PAGE_SIZE)
    bias0 = rel_bias[0:1].astype(f32)
    pt_flat = page_table.reshape(-1).astype(jnp.int32)
    grp = min(SAMPLE_PAGE_GROUP, npages)
    assert npages % grp == 0
    ckidx_t = jnp.transpose(cache_kidx, (0, 1, 3, 2))
    ck_t = jnp.transpose(cache_k, (0, 1, 3, 4, 2))
    cv_t = jnp.transpose(cache_v, (0, 1, 3, 4, 2))

    hp = x_prompt.reshape(b * s, D_MODEL)
    hs = x_sample.reshape(db * ds, D_MODEL)
    outs = {k: [] for k in ("kp", "vp", "kip", "sp", "ks", "vs", "kis", "ss")}
    for l in range(depth):
        w = _layer_weights(l, ffn1_wg, ffn1_wu, ffn1_wd, ln1_g, ln1_b, w_in, ret_gn_g, w_out, ln2_g, ln2_b,
                           ffn2_wg, ffn2_wu, ffn2_wd, ln3_g, ln3_b)
        pp = _pre_call(hp, w, rot_p[0], rot_p[1], alpha, DH_A ** -0.5 * LOG2E, tq, bf16, s, "pre_prompt")
        r3 = lambda a: a.reshape(b, s, a.shape[-1])
        oa = _attn_km_call(r3(pp["qa"]), r3(pp["qi"]), r3(pp["wi"]), r3(pp["kab"]), r3(pp["vab"]), r3(pp["ki2"]),
                           bias_tiles * LOG2E, topk_p, tq)
        yr, st_p = _ret_call(r3(pp["qr"]), r3(pp["kr"]), r3(pp["vr"]), r3(pp["gr"]), w["gng"], dec_p, chunk)
        ps = _pre_call(hs, w, rot_s[0], rot_s[1], alpha, DH_A ** -0.5, db * ds, f32, None, "pre_sample")
        q16 = jnp.pad(ps["qi"].reshape(db, H_IDX, DH_IDX), ((0, 0), (0, 16 - H_IDX), (0, 0)))
        w16 = jnp.pad((ps["wi"][:, :H_IDX] * H_IDX ** -0.5).reshape(db, H_IDX, 1), ((0, 0), (0, 16 - H_IDX), (0, 0)))
        sc_s, sn_s = _sscore_call(pt_flat, ckidx_t, q16, w16, ps["ki"].reshape(db, 1, DH_IDX), l, npages)
        mask, snew = _ssel_call(sc_s.reshape(db, past), sn_s.reshape(db, LANES), topk_s)
        mask = mask.reshape(db, npages, 1, PAGE_SIZE)
        snew = snew.reshape(db, 1, LANES)
        t8 = lambda a: a.reshape(db, H_A, DH_A).transpose(0, 2, 1)
        r4 = lambda a: a.reshape(db, H_R, DK_R)
        ret_args = (r4(ps["qr"]), r4(ps["kr"]), r4(ps["vr"]), r4(ps["gr"]), w["gng"].reshape(H_R, DV_R), gam)

        post_steps = (b * s) // STREAM_POST_TM if (b * s) % STREAM_POST_TM == 0 else 0
        sgrp = min(STREAM_PAGE_GROUP, npages)
        groups = db * (npages // sgrp) if npages % sgrp == 0 else 0
        units = groups // post_steps if post_steps and groups % post_steps == 0 else 0
        if units and (npages // sgrp) % units == 0:
            hp, ot_s = _post_stream_call(pp["h"], oa.reshape(b * s, W_A), yr.reshape(b * s, W_R), w, alpha,
                                         STREAM_POST_TM, pt_flat, ck_t, cv_t, t8(ps["qa"]), t8(ps["ka"]),
                                         t8(ps["va"]), bias_pos, bias0, mask, snew, l, npages, sgrp, units,
                                         "post_prompt")
            yr_s, st_s = _sret_call(*ret_args, state_ret, l)
        else:
            hp = _post_call(pp["h"], oa.reshape(b * s, W_A), yr.reshape(b * s, W_R), w, alpha,
                            min(POST_TM, b * s), "post_prompt")
            ot_s, yr_s, st_s = _satt_call(pt_flat, ck_t, cv_t, t8(ps["qa"]), t8(ps["ka"]), t8(ps["va"]), bias_pos,
                                          bias0, mask, snew, *ret_args, state_ret, l, npages, grp)
        outs["kp"].append(pp["ka"].reshape(b, H_A, DH_A, s).transpose(0, 3, 1, 2))
        outs["vp"].append(pp["va"].reshape(b, H_A, DH_A, s).transpose(0, 3, 1, 2))
        outs["kip"].append(pp["ki"].transpose(0, 2, 1))
        outs["sp"].append(st_p)
        oa_s = ot_s.transpose(0, 2, 1).reshape(db, W_A)
        hs = _post_call(ps["h"], oa_s, yr_s.reshape(db, W_R), w, alpha, db * ds, "post_sample")
        outs["ks"].append(ps["ka"].reshape(db, ds, H_A, DH_A))
        outs["vs"].append(ps["va"].reshape(db, ds, H_A, DH_A))
        outs["kis"].append(ps["ki"].reshape(db, ds, DH_IDX))
        outs["ss"].append(st_s)

    stack = lambda k: jnp.stack(outs[k])
    return (hp.reshape(b, s, D_MODEL), hs.reshape(db, ds, D_MODEL),
            stack("kp"), stack("vp"), stack("kip"), stack("sp"),
            stack("ks"), stack("vs"), stack("kis"), stack("ss"))
```

```python
import functools
import math

import numpy as np
import jax
import jax.numpy as jnp
from jax import lax
from jax.experimental import pallas as pl
from jax.experimental.pallas import tpu as pltpu

D_MODEL = 1024
D_FF = 2816
PAGE_SIZE = 128
H_A = 8
DH_A = 64
W_A = H_A * DH_A
H_IDX = 8
DH_IDX = 64
TOPK_MAX = 256
NUM_BUCKETS = 32
MAX_DISTANCE = 128
H_R = 4
DK_R = 128
DV_R = 128
W_R = H_R * DV_R
ROPE_BASE = 10000.0
LN_EPS = 1e-5
GN_EPS = 1e-5
IN_SIZES = (W_A, W_A, W_A, H_IDX * DH_IDX, DH_IDX, H_IDX, H_R * DK_R, H_R * DK_R, W_R, W_R)

LANES = 128
MIB = 1024 * 1024
NEG = -1e30
LOG2E = math.log2(math.e)

FF_CHUNK = 512
PRE_TM = 256
POST_TM = 512
ATT_TQ = 256
RET_CHUNK = 256
BISECT_ITERS = 16
SAMPLE_PAGE_GROUP = 16
STREAM_POST_TM = 256
STREAM_PAGE_GROUP = 8
SELECT_ROWS = 128
ATT_FAR_WIDTH = 2

C_QA, C_KA, C_VA, C_QI, C_KI2, C_WI, C_QR, C_KR, C_VR, C_GR, C_END = (
    0, 512, 1024, 1536, 2048, 2176, 2304, 2816, 3328, 3840, 4352)

f32 = jnp.float32
bf16 = jnp.bfloat16
NT_DIMS = (((1,), (1,)), ((), ()))


def _const_spec(shape):
    nd = len(shape)
    return pl.BlockSpec(shape, lambda *_: (0,) * nd, pipeline_mode=pl.Buffered(1))


def _ln(x, g, b):
    mu = jnp.mean(x, axis=-1, keepdims=True)
    xc = x - mu
    var = jnp.mean(xc * xc, axis=-1, keepdims=True)
    return xc * lax.rsqrt(var + LN_EPS) * g + b


def _ffn(xb, wg_ref, wu_ref, wd_ref, midway=None):
    acc = None
    starts = list(range(0, D_FF, FF_CHUNK))
    for ci, c0 in enumerate(starts):
        if midway is not None and ci == len(starts) // 2:
            midway()
        c1 = min(c0 + FF_CHUNK, D_FF)
        g = jnp.dot(xb, wg_ref[:, c0:c1], preferred_element_type=f32)
        u = jnp.dot(xb, wu_ref[:, c0:c1], preferred_element_type=f32)
        a = (jax.nn.silu(g) * u).astype(bf16)
        part = jnp.dot(a, wd_ref[c0:c1, :], preferred_element_type=f32)
        acc = part if acc is None else acc + part
    return acc


def _pre_stream_kernel(alpha, q_scale, feature_major, layer, npages, grp, units, steps_per_seq, pt_ref,
                       x_ref, wg_ref, wu_ref, wd_ref, lng_ref, lnb_ref, win_ref, rc_ref, rs_ref,
                       ckk_ref, ckv_ref, qt_ref, bias_ref, mask_ref, *rest):
    outs, (sm_ref, sl_ref, sacc_ref), (kbuf, vbuf, sem, m_ref, l_ref, acc_ref) = rest[:14], rest[14:17], rest[17:]
    stream = _page_stream((layer, npages, 0, grp, units, steps_per_seq), pt_ref, ckk_ref, ckv_ref,
                          qt_ref, bias_ref, mask_ref, kbuf, vbuf, sem, m_ref, l_ref, acc_ref)

    @pl.when(stream.part == 0)
    def _():
        _satt_reset(m_ref, l_ref, acc_ref)

    stream.begin()
    _pre_kernel(alpha, q_scale, feature_major, x_ref, wg_ref, wu_ref, wd_ref, lng_ref, lnb_ref, win_ref,
                rc_ref, rs_ref, *outs, before_ffn=stream.fold_first, midway=stream.midway)

    @pl.when(stream.part == steps_per_seq - 1)
    def _():
        sm_ref[...] = m_ref[...]
        sl_ref[...] = l_ref[...]
        sacc_ref[...] = acc_ref[...]


def _pre_kernel(alpha, q_scale, feature_major, x_ref, wg_ref, wu_ref, wd_ref, lng_ref, lnb_ref, win_ref,
                rc_ref, rs_ref,
                h_ref, qa_ref, ka_ref, va_ref, kab_ref, vab_ref, qi_ref, ki_ref, ki2_ref, wi_ref,
                qr_ref, kr_ref, vr_ref, gr_ref, before_ffn=None, midway=None):
    act = qa_ref.dtype
    x = x_ref[...]
    if before_ffn is not None:
        before_ffn()
    f = _ffn(x.astype(bf16), wg_ref, wu_ref, wd_ref, midway)
    h = _ln(alpha * x + 0.5 * f, lng_ref[...], lnb_ref[...])
    h_ref[...] = h
    hb = h.astype(bf16)

    def proj(c0, c1):
        return jnp.dot(hb, win_ref[:, c0:c1], preferred_element_type=f32)

    qa_ref[...] = (proj(C_QA, C_KA) * q_scale).astype(act)
    ka = proj(C_KA, C_VA)
    kab_ref[...] = ka.astype(act)
    va = proj(C_VA, C_QI)
    qi_ref[...] = (proj(C_QI, C_KI2) * DH_IDX ** -0.5).astype(act)
    kk = proj(C_KI2, C_WI)
    wi_ref[...] = proj(C_WI, C_QR)
    vab_ref[...] = va.astype(act)
    if feature_major:
        ka_ref[...] = ka.T
        va_ref[...] = va.T
        ki_ref[...] = kk.T[:DH_IDX, :]
    else:
        ka_ref[...] = ka
        va_ref[...] = va
        ki_ref[...] = kk[:, :DH_IDX]
    ki2_ref[...] = kk.astype(act)
    qr = proj(C_QR, C_KR)
    kr = proj(C_KR, C_VR)
    c = rc_ref[...]
    s = rs_ref[...]
    for hh in range(H_R):
        sl = slice(DK_R * hh, DK_R * (hh + 1))
        qh = qr[:, sl]
        kh = kr[:, sl]
        qr_ref[:, sl] = (qh * c + pltpu.roll(qh, DK_R // 2, 1) * s).astype(act)
        kr_ref[:, sl] = ((kh * c + pltpu.roll(kh, DK_R // 2, 1) * s) * DK_R ** -0.5).astype(act)
    vr_ref[...] = proj(C_VR, C_GR).astype(act)
    gr_ref[...] = proj(C_GR, C_END)


def _pre_call(x, w, rot_c, rot_s, alpha, q_scale, tm, act, seq_len, name, stream=None):
    n = x.shape[0]
    grid = (pl.cdiv(n, tm),)
    row = lambda width: pl.BlockSpec((tm, width), lambda i, *_: (i, 0))
    rot_blocks = rot_c.shape[0] // tm
    rot = pl.BlockSpec((tm, LANES), lambda i, *_: (i % rot_blocks, 0))
    feature_major = seq_len is not None
    if feature_major:
        assert seq_len % tm == 0 and n % seq_len == 0
        seq_blocks = seq_len // tm
        kv_shape = lambda width: jax.ShapeDtypeStruct((n // seq_len, width, seq_len), f32)
        kv_spec = lambda width: pl.BlockSpec((None, width, tm), lambda i, *_: (i // seq_blocks, 0, i % seq_blocks))
    else:
        kv_shape = lambda width: jax.ShapeDtypeStruct((n, width), f32)
        kv_spec = row
    in_specs = [
        row(D_MODEL),
        _const_spec((D_MODEL, D_FF)), _const_spec((D_MODEL, D_FF)), _const_spec((D_FF, D_MODEL)),
        _const_spec((1, D_MODEL)), _const_spec((1, D_MODEL)),
        _const_spec((D_MODEL, C_END)),
        rot, rot,
    ]
    outs = [
        ("h", D_MODEL, f32), ("qa", W_A, act), ("ka", W_A, f32), ("va", W_A, f32), ("kab", W_A, act),
        ("vab", W_A, act), ("qi", W_A, act), ("ki", DH_IDX, f32), ("ki2", LANES, act), ("wi", LANES, f32),
        ("qr", W_R, act), ("kr", W_R, act), ("vr", W_R, act), ("gr", W_R, f32),
    ]
    kv_names = ("ka", "va", "ki")
    out_shape = [kv_shape(wd) if k in kv_names else jax.ShapeDtypeStruct((n, wd), dt) for k, wd, dt in outs]
    out_specs = [kv_spec(wd) if k in kv_names else row(wd) for k, wd, _ in outs]
    operands = (x, w["ffn1_wg"], w["ffn1_wu"], w["ffn1_wd"], w["ln1_g"], w["ln1_b"], w["w_in"], rot_c, rot_s)
    if stream is None:
        res = pl.pallas_call(
            functools.partial(_pre_kernel, alpha, q_scale, feature_major),
            grid=grid, in_specs=in_specs, out_specs=out_specs, out_shape=out_shape,
            compiler_params=pltpu.CompilerParams(dimension_semantics=("arbitrary",), vmem_limit_bytes=52 * MIB),
            name=name,
        )(*operands)
        return {k: v for (k, _, _), v in zip(outs, res)}
    grp, units, sps = stream["grp"], stream["units"], stream["steps_per_seq"]
    db = stream["qt"].shape[0]
    assert grid[0] == db * sps
    per_seq, bias_spec, mask_spec, state_specs, scratch = _stream_specs(grp, units, sps, 0)
    grid_spec = pltpu.PrefetchScalarGridSpec(
        num_scalar_prefetch=1,
        grid=grid,
        in_specs=in_specs + [pl.BlockSpec(memory_space=pl.ANY), pl.BlockSpec(memory_space=pl.ANY),
                             per_seq(DH_A, H_A), bias_spec, mask_spec],
        out_specs=out_specs + state_specs,
        scratch_shapes=scratch,
    )
    res = pl.pallas_call(
        functools.partial(_pre_stream_kernel, alpha, q_scale, feature_major, stream["layer"], stream["npages"],
                          grp, units, sps),
        grid_spec=grid_spec, out_shape=out_shape + _stream_state_shapes(db),
        compiler_params=pltpu.CompilerParams(dimension_semantics=("arbitrary",), vmem_limit_bytes=58 * MIB),
        name=name,
    )(stream["pt_flat"], *operands, stream["ck_t"], stream["cv_t"], stream["qt"], stream["bias_pos"],
      stream["mask"])
    out = {k: v for (k, _, _), v in zip(outs, res[:len(outs)])}
    out["state"] = tuple(res[len(outs):])
    return out


def _post_kernel(alpha, h_ref, oa_ref, yr_ref, wo_ref, l2g_ref, l2b_ref, wg_ref, wu_ref, wd_ref,
                 l3g_ref, l3b_ref, out_ref):
    h = h_ref[...]
    mix = (jnp.dot(oa_ref[...].astype(bf16), wo_ref[0:W_A, :], preferred_element_type=f32)
           + jnp.dot(yr_ref[...].astype(bf16), wo_ref[W_A:W_A + W_R, :], preferred_element_type=f32))
    h2 = _ln(alpha * h + mix, l2g_ref[...], l2b_ref[...])
    f = _ffn(h2.astype(bf16), wg_ref, wu_ref, wd_ref)
    out_ref[...] = _ln(alpha * h2 + 0.5 * f, l3g_ref[...], l3b_ref[...])


def _post_stream_kernel(alpha, layer, npages, grp, units, steps_per_seq, pt_ref,
                        h_ref, oa_ref, yr_ref, wo_ref, l2g_ref, l2b_ref, wg_ref, wu_ref, wd_ref, l3g_ref, l3b_ref,
                        ckk_ref, ckv_ref, qt_ref, knt_ref, vnt_ref, bias_ref, bias0_ref, mask_ref, snew_ref,
                        sm_ref, sl_ref, sacc_ref,
                        out_ref, ot_ref, kbuf, vbuf, sem, m_ref, l_ref, acc_ref):
    stream = _page_stream((layer, npages, npages // 2, grp, units, steps_per_seq), pt_ref, ckk_ref, ckv_ref,
                          qt_ref, bias_ref, mask_ref, kbuf, vbuf, sem, m_ref, l_ref, acc_ref)

    @pl.when(stream.part == 0)
    def _():
        m_ref[...] = sm_ref[...]
        l_ref[...] = sl_ref[...]
        acc_ref[...] = sacc_ref[...]

    stream.begin()
    h = h_ref[...]
    mix = (jnp.dot(oa_ref[...].astype(bf16), wo_ref[0:W_A, :], preferred_element_type=f32)
           + jnp.dot(yr_ref[...].astype(bf16), wo_ref[W_A:W_A + W_R, :], preferred_element_type=f32))
    h2 = _ln(alpha * h + mix, l2g_ref[...], l2b_ref[...])
    stream.fold_first()
    f = _ffn(h2.astype(bf16), wg_ref, wu_ref, wd_ref, stream.midway)
    out_ref[...] = _ln(alpha * h2 + 0.5 * f, l3g_ref[...], l3b_ref[...])

    @pl.when(stream.part == steps_per_seq - 1)
    def _():
        _satt_finish(qt_ref, knt_ref, vnt_ref, bias0_ref, snew_ref, m_ref, l_ref, acc_ref, ot_ref)


class _page_stream:
    def __init__(self, cfg, pt_ref, ckk_ref, ckv_ref, qt_ref, bias_ref, mask_ref, kbuf, vbuf, sem,
                 m_ref, l_ref, acc_ref):
        self.layer, self.npages, self.page0, self.grp, self.units, self.steps_per_seq = cfg
        self.pt_ref, self.ckk_ref, self.ckv_ref = pt_ref, ckk_ref, ckv_ref
        self.qt_ref, self.bias_ref, self.mask_ref = qt_ref, bias_ref, mask_ref
        self.kbuf, self.vbuf, self.sem = kbuf, vbuf, sem
        self.state = (m_ref, l_ref, acc_ref)
        self.st = pl.program_id(0)
        self.nsteps = pl.num_programs(0)
        self.part = self.st % self.steps_per_seq
        self.first = range(0, self.units // 2)
        self.second = range(self.units // 2, self.units)

    def _copies(self, step, u, p):
        seq, part = step // self.steps_per_seq, step % self.steps_per_seq
        page = self.pt_ref[seq * self.npages + self.page0 + (part * self.units + u) * self.grp + p]
        return (pltpu.make_async_copy(self.ckk_ref.at[self.layer, page], self.kbuf.at[u, p], self.sem.at[u, 0]),
                pltpu.make_async_copy(self.ckv_ref.at[self.layer, page], self.vbuf.at[u, p], self.sem.at[u, 1]))

    def _start(self, step, us):
        for u in us:
            for p in range(self.grp):
                ck, cv = self._copies(step, u, p)
                ck.start()
                cv.start()

    def _wait(self, us):
        for u in us:
            for p in range(self.grp):
                ck, cv = self._copies(self.st, u, p)
                ck.wait()
                cv.wait()

    def _fold(self, us):
        for u in us:
            _satt_fold(self.kbuf.at[u], self.vbuf.at[u], self.qt_ref, self.bias_ref, self.mask_ref,
                       slice(u * self.grp, (u + 1) * self.grp), *self.state)

    def begin(self):
        @pl.when(self.st == 0)
        def _():
            self._start(0, self.first)
        self._wait(self.first)
        self._start(self.st, self.second)

    def fold_first(self):
        self._fold(self.first)

    def midway(self):
        self._wait(self.second)

        @pl.when(self.st + 1 < self.nsteps)
        def _():
            self._start(self.st + 1, self.first)
        self._fold(self.second)


def _stream_specs(grp, units, steps_per_seq, block0):
    pages = units * grp
    per_seq = lambda *dims: pl.BlockSpec((None,) + dims, lambda i, pt: (i // steps_per_seq,) + (0,) * len(dims))
    bias = pl.BlockSpec((H_A, pages, 1, PAGE_SIZE), lambda i, pt: (0, block0 + i % steps_per_seq, 0, 0))
    mask = pl.BlockSpec((None, pages, 1, PAGE_SIZE),
                        lambda i, pt: (i // steps_per_seq, block0 + i % steps_per_seq, 0, 0))
    state = [per_seq(H_A, 1, PAGE_SIZE), per_seq(H_A, 1, PAGE_SIZE), per_seq(H_A, DH_A, PAGE_SIZE)]
    scratch = [
        pltpu.VMEM((units, grp, H_A, DH_A, PAGE_SIZE), f32),
        pltpu.VMEM((units, grp, H_A, DH_A, PAGE_SIZE), f32),
        pltpu.SemaphoreType.DMA((units, 2)),
        pltpu.VMEM((H_A, 1, PAGE_SIZE), f32),
        pltpu.VMEM((H_A, 1, PAGE_SIZE), f32),
        pltpu.VMEM((H_A, DH_A, PAGE_SIZE), f32),
    ]
    return per_seq, bias, mask, state, scratch


def _stream_state_shapes(db):
    return [jax.ShapeDtypeStruct((db, H_A, 1, PAGE_SIZE), f32), jax.ShapeDtypeStruct((db, H_A, 1, PAGE_SIZE), f32),
            jax.ShapeDtypeStruct((db, H_A, DH_A, PAGE_SIZE), f32)]


def _post_stream_call(h, oa, yr, w, alpha, tm, pt_flat, cache_k_t, cache_v_t, qt, knt, vnt, bias_pos, bias0, mask,
                      snew, state, layer, npages, grp, units, name):
    n = h.shape[0]
    db = qt.shape[0]
    nsteps = n // tm
    steps_per_seq = (npages // 2) // (grp * units)
    assert n % tm == 0 and nsteps == db * steps_per_seq
    row = lambda width: pl.BlockSpec((tm, width), lambda i, pt: (i, 0))
    per_seq, bias_spec, mask_spec, state_specs, scratch = _stream_specs(grp, units, steps_per_seq, steps_per_seq)
    grid_spec = pltpu.PrefetchScalarGridSpec(
        num_scalar_prefetch=1,
        grid=(nsteps,),
        in_specs=[
            row(D_MODEL), row(W_A), row(W_R),
            _const_spec((W_A + W_R, D_MODEL)), _const_spec((1, D_MODEL)), _const_spec((1, D_MODEL)),
            _const_spec((D_MODEL, D_FF)), _const_spec((D_MODEL, D_FF)), _const_spec((D_FF, D_MODEL)),
            _const_spec((1, D_MODEL)), _const_spec((1, D_MODEL)),
            pl.BlockSpec(memory_space=pl.ANY), pl.BlockSpec(memory_space=pl.ANY),
            per_seq(DH_A, H_A), per_seq(DH_A, H_A), per_seq(DH_A, H_A),
            bias_spec, pl.BlockSpec((1, H_A), lambda i, pt: (0, 0)), mask_spec, per_seq(1, LANES),
        ] + state_specs,
        out_specs=[row(D_MODEL), per_seq(DH_A, H_A)],
        scratch_shapes=scratch,
    )
    return pl.pallas_call(
        functools.partial(_post_stream_kernel, alpha, layer, npages, grp, units, steps_per_seq),
        grid_spec=grid_spec,
        out_shape=[jax.ShapeDtypeStruct((n, D_MODEL), f32), jax.ShapeDtypeStruct((db, DH_A, H_A), f32)],
        compiler_params=pltpu.CompilerParams(dimension_semantics=("arbitrary",), vmem_limit_bytes=52 * MIB),
        name=name,
    )(pt_flat, h, oa, yr, w["w_out"], w["ln2_g"], w["ln2_b"], w["ffn2_wg"], w["ffn2_wu"], w["ffn2_wd"],
      w["ln3_g"], w["ln3_b"], cache_k_t, cache_v_t, qt, knt, vnt, bias_pos, bias0, mask, snew, *state)


def _post_call(h, oa, yr, w, alpha, tm, name):
    n = h.shape[0]
    row = lambda width: pl.BlockSpec((tm, width), lambda i: (i, 0))
    in_specs = [
        row(D_MODEL), row(W_A), row(W_R),
        _const_spec((W_A + W_R, D_MODEL)), _const_spec((1, D_MODEL)), _const_spec((1, D_MODEL)),
        _const_spec((D_MODEL, D_FF)), _const_spec((D_MODEL, D_FF)), _const_spec((D_FF, D_MODEL)),
        _const_spec((1, D_MODEL)), _const_spec((1, D_MODEL)),
    ]
    return pl.pallas_call(
        functools.partial(_post_kernel, alpha),
        grid=(pl.cdiv(n, tm),), in_specs=in_specs, out_specs=row(D_MODEL),
        out_shape=jax.ShapeDtypeStruct((n, D_MODEL), f32),
        compiler_params=pltpu.CompilerParams(dimension_semantics=("arbitrary",), vmem_limit_bytes=48 * MIB),
        name=name,
    )(h, oa, yr, w["w_out"], w["ln2_g"], w["ln2_b"], w["ffn2_wg"], w["ffn2_wu"], w["ffn2_wd"],
      w["ln3_g"], w["ln3_b"])


def _attn_kernel(topk, qa_ref, qi_ref, wi_ref, k_ref, v_ref, ki2_ref, bias_ref, oa_ref,
                 sc_ref, qap_ref, qip_ref, wb_ref, m_ref, l_ref, acc_ref):
    tq = qa_ref.shape[0]
    ncb = tq // LANES
    i = pl.program_id(1)
    nj = i + 1
    kf = float(topk)

    lane = lax.broadcasted_iota(jnp.int32, (tq, LANES), 1)
    lo_half = lane < DH_A
    for h in range(H_A):
        p = h // 2
        keep = lo_half if h % 2 == 0 else jnp.logical_not(lo_half)
        blk = slice(LANES * p, LANES * (p + 1))
        qip_ref[h] = jnp.where(keep, qi_ref[:, blk], jnp.zeros((), qi_ref.dtype))
        qap_ref[h] = jnp.where(keep, qa_ref[:, blk], jnp.zeros((), qa_ref.dtype))
        wb_ref[h] = jnp.broadcast_to(wi_ref[:, h:h + 1] * H_IDX ** -0.5, (tq, LANES))

    def score_chunk(j, carry):
        off = pl.multiple_of(j * tq, tq)
        kj = ki2_ref[pl.ds(off, tq), :]
        accs = [jnp.zeros((tq, LANES), f32) for _ in range(ncb)]
        for h in range(H_IDX):
            d = lax.dot_general(qip_ref[h], kj, NT_DIMS, preferred_element_type=f32)
            wb = wb_ref[h]
            for cb in range(ncb):
                accs[cb] = accs[cb] + wb * jnp.maximum(d[:, cb * LANES:(cb + 1) * LANES], 0.0)
        for cb in range(ncb):
            sc_ref[j, :, cb * LANES:(cb + 1) * LANES] = accs[cb]
        return carry

    lax.fori_loop(0, nj, score_chunk, 0)

    row = lax.broadcasted_iota(jnp.int32, (tq, tq), 0)
    col = lax.broadcasted_iota(jnp.int32, (tq, tq), 1)
    causal = col <= row
    sd = sc_ref[i]
    mn_diag = jnp.min(jnp.where(causal, sd, jnp.inf), axis=1, keepdims=True)
    sc_ref[i] = jnp.where(causal, sd, -jnp.inf)

    rb = min(SELECT_ROWS, tq)

    def fold(fn, init, comb, n_chunks, t=None):
        outs = []
        for r0 in range(0, tq, rb):
            t_blk = None if t is None else t[r0:r0 + rb]

            def body(j, acc, r0=r0, t_blk=t_blk):
                for cb in range(ncb):
                    acc = comb(acc, fn(sc_ref[j, r0:r0 + rb, cb * LANES:(cb + 1) * LANES], t_blk))
                return acc
            outs.append(lax.fori_loop(0, n_chunks, body, jnp.full((rb, LANES), init, f32)))
        return jnp.concatenate(outs, axis=0) if len(outs) > 1 else outs[0]

    def tile_lanes(x):
        return jnp.concatenate([x] * ncb, axis=1) if ncb > 1 else x

    def lane_sum(x):
        return jnp.broadcast_to(jnp.sum(x, axis=1, keepdims=True), (tq, LANES))

    def lane_max(x):
        return jnp.broadcast_to(jnp.max(x, axis=1, keepdims=True), (tq, LANES))

    def count_ge(t):
        return lane_sum(fold(lambda s, tb: jnp.where(s >= tb, 1.0, 0.0), 0.0, jnp.add, nj, t))

    def count_gt(t):
        return lane_sum(fold(lambda s, tb: jnp.where(s > tb, 1.0, 0.0), 0.0, jnp.add, nj, t))

    def max_below(t):
        return lane_max(fold(lambda s, tb: jnp.where(s < tb, s, -jnp.inf), -jnp.inf, jnp.maximum, nj, t))

    n_keys = (lax.broadcasted_iota(jnp.int32, (tq, LANES), 0) + (i * tq + 1)).astype(f32)
    take_all = n_keys <= kf

    mx = lane_max(fold(lambda s, _: s, -jnp.inf, jnp.maximum, nj))
    mn_far = -lane_max(-fold(lambda s, _: s, jnp.inf, jnp.minimum, i))
    mn = jnp.minimum(mn_far, jnp.broadcast_to(mn_diag, (tq, LANES)))

    @pl.when((i + 1) * tq <= topk)
    def _():
        def mk(j, carry):
            sc_ref[j] = jnp.zeros((tq, tq), f32)
            return carry
        lax.fori_loop(0, nj, mk, 0)

    @pl.when((i + 1) * tq > topk)
    def _():
        c_max = count_ge(mx)
        done0 = jnp.logical_or(take_all, c_max >= kf)
        thr0 = jnp.where(take_all, -jnp.inf, mx)
        cge0 = jnp.where(take_all, n_keys, c_max)

        def bis(_, st):
            lo, hi = st
            mid = 0.5 * (lo + hi)
            ge = count_ge(mid) >= kf
            return jnp.where(ge, mid, lo), jnp.where(ge, hi, mid)

        lo, hi = lax.fori_loop(0, BISECT_ITERS, bis, (mn, mx))

        def snap_cond(st):
            return st[0] > 0.0

        def snap_body(st):
            _, hi, thr, cge, done = st
            m = max_below(hi)
            c = count_ge(m)
            ok = c >= kf
            newly = jnp.logical_and(ok, done < 0.5)
            thr = jnp.where(newly, m, thr)
            cge = jnp.where(newly, c, cge)
            hi = jnp.where(jnp.logical_or(done > 0.5, ok), hi, m)
            done = jnp.where(ok, 1.0, done)
            left = jnp.sum(1.0 - done[:, 0:1])
            return left, hi, thr, cge, done

        done_f = jnp.where(done0, 1.0, 0.0)
        left0 = jnp.sum(1.0 - done_f[:, 0:1])
        _, _, thr, cge, _ = lax.while_loop(snap_cond, snap_body, (left0, hi, thr0, cge0, done_f))

        excess = jnp.sum(jnp.where(cge[:, 0:1] > kf, 1.0, 0.0)) > 0.0
        thr_t = tile_lanes(thr)

        @pl.when(jnp.logical_not(excess))
        def _():
            def mk(j, carry):
                sc_ref[j] = jnp.where(sc_ref[j] >= thr_t, 0.0, NEG)
                return carry
            lax.fori_loop(0, nj, mk, 0)

        @pl.when(excess)
        def _():
            need = tile_lanes(kf - count_gt(thr))
            upper = jnp.where(row <= col, 1.0, 0.0).astype(bf16)

            def mk(j, seen):
                s = sc_ref[j]
                tie = s == thr_t
                tie_f = jnp.where(tie, 1.0, 0.0)
                rank = seen + jnp.dot(tie_f.astype(bf16), upper, preferred_element_type=f32)
                sel = jnp.logical_or(s > thr_t, jnp.logical_and(tie, rank <= need))
                sc_ref[j] = jnp.where(sel, 0.0, NEG)
                part = tie_f[:, 0:LANES]
                for cb in range(1, ncb):
                    part = part + tie_f[:, cb * LANES:(cb + 1) * LANES]
                return seen + tile_lanes(lane_sum(part))
            lax.fori_loop(0, nj, mk, jnp.zeros((tq, tq), f32))

    sc_ref[i] = jnp.where(causal, sc_ref[i], NEG)


    m_ref[...] = jnp.full(m_ref.shape, -jnp.inf, f32)
    l_ref[...] = jnp.zeros(l_ref.shape, f32)
    acc_ref[...] = jnp.zeros(acc_ref.shape, f32)

    def attend_chunk(j, bias_cols):
        off = pl.multiple_of(j * tq, tq)
        sel_mask = sc_ref[j]
        for h in range(H_A):
            blk = slice(LANES * (h // 2), LANES * (h // 2 + 1))
            s = lax.dot_general(qap_ref[h], k_ref[pl.ds(off, tq), blk], NT_DIMS, preferred_element_type=f32)
            s = s + sel_mask
            if bias_cols is not None:
                s = s + bias_ref[h, :, bias_cols]
            m_old = m_ref[h]
            cmax = s[:, 0:LANES]
            for cb in range(1, ncb):
                cmax = jnp.maximum(cmax, s[:, cb * LANES:(cb + 1) * LANES])
            m_new = jnp.maximum(m_old, lane_max(cmax))
            a = jnp.exp2(m_old - m_new)
            e = jnp.exp2(s - tile_lanes(m_new))
            esum = e[:, 0:LANES]
            for cb in range(1, ncb):
                esum = esum + e[:, cb * LANES:(cb + 1) * LANES]
            l_ref[h] = l_ref[h] * a + esum
            acc_ref[h] = acc_ref[h] * a + jnp.dot(e.astype(bf16), v_ref[pl.ds(off, tq), blk],
                                                  preferred_element_type=f32)
            m_ref[h] = m_new

    def far(j, carry):
        attend_chunk(j, None)
        return carry
    lax.fori_loop(0, i - 1, far, 0)

    @pl.when(i >= 1)
    def _():
        attend_chunk(i - 1, slice(0, tq))
    attend_chunk(i, slice(tq, 2 * tq))

    for p in range(H_A // 2):
        o_even = acc_ref[2 * p] / lane_sum(l_ref[2 * p])
        o_odd = acc_ref[2 * p + 1] / lane_sum(l_ref[2 * p + 1])
        oa_ref[:, LANES * p:LANES * (p + 1)] = jnp.where(lo_half, o_even, o_odd).astype(oa_ref.dtype)


def _attn_call(qa, qi, wi, kab, vab, ki2, bias_tiles, topk, tq):
    b, s, _ = qa.shape
    nq = s // tq
    qspec = lambda width: pl.BlockSpec((None, tq, width), lambda bi, i: (bi, i, 0))
    kvspec = lambda width: pl.BlockSpec((None, s, width), lambda bi, i: (bi, 0, 0), pipeline_mode=pl.Buffered(1))
    scratch = [
        pltpu.VMEM((nq, tq, tq), f32),
        pltpu.VMEM((H_A, tq, LANES), qa.dtype),
        pltpu.VMEM((H_IDX, tq, LANES), qi.dtype),
        pltpu.VMEM((H_IDX, tq, LANES), f32),
        pltpu.VMEM((H_A, tq, LANES), f32),
        pltpu.VMEM((H_A, tq, LANES), f32),
        pltpu.VMEM((H_A, tq, LANES), f32),
    ]
    return pl.pallas_call(
        functools.partial(_attn_kernel, topk),
        grid=(b, nq),
        in_specs=[qspec(W_A), qspec(W_A), qspec(LANES), kvspec(W_A), kvspec(W_A), kvspec(LANES),
                  _const_spec((H_A, tq, 2 * tq))],
        out_specs=qspec(W_A),
        out_shape=jax.ShapeDtypeStruct((b, s, W_A), bf16),
        scratch_shapes=scratch,
        compiler_params=pltpu.CompilerParams(dimension_semantics=("arbitrary", "arbitrary"),
                                             vmem_limit_bytes=48 * MIB),
        name="attn_prompt",
    )(qa, qi, wi, kab, vab, ki2, bias_tiles)


def _attn_km_kernel(topk, qa_ref, qi_ref, wi_ref, k_ref, v_ref, ki2_ref, bias_ref, oa_ref,
                    sc_ref, qap_ref, qip_ref, m_ref, l_ref, acc_ref):
    tq = qa_ref.shape[0]
    ngrp = tq // 8
    i = pl.program_id(1)
    nj = i + 1
    kf = float(topk)

    lane = lax.broadcasted_iota(jnp.int32, (tq, LANES), 1)
    lo_half = lane < DH_A
    for h in range(H_A):
        p, par = h // 2, h % 2
        keep = lo_half if par == 0 else jnp.logical_not(lo_half)
        blk = slice(LANES * p, LANES * (p + 1))
        qip_ref[h] = jnp.where(keep, qi_ref[:, blk], jnp.zeros((), qi_ref.dtype))
        qap_ref[p, par * tq:(par + 1) * tq, :] = jnp.where(keep, qa_ref[:, blk], jnp.zeros((), qa_ref.dtype))
    w_heads = wi_ref[...].T[0:H_IDX, :] * H_IDX ** -0.5

    def row_groups(x, comb):
        parts = [x[8 * r:8 * r + 8] for r in range(ngrp)]
        while len(parts) > 1:
            parts = [comb(parts[k], parts[k + 1]) for k in range(0, len(parts) - 1, 2)] + (
                [parts[-1]] if len(parts) % 2 else [])
        return parts[0]

    def score_chunk(j, carry):
        off = pl.multiple_of(j * tq, tq)
        kj = ki2_ref[pl.ds(off, tq), :]
        acc = jnp.zeros((tq, tq), f32)
        for h in range(H_IDX):
            d = lax.dot_general(kj, qip_ref[h], NT_DIMS, preferred_element_type=f32)
            acc = acc + w_heads[h:h + 1, :] * jnp.maximum(d, 0.0)
        sc_ref[j] = acc
        return carry

    lax.fori_loop(0, nj, score_chunk, 0)

    krow = lax.broadcasted_iota(jnp.int32, (tq, tq), 0)
    qcol = lax.broadcasted_iota(jnp.int32, (tq, tq), 1)
    causal = krow <= qcol
    sd = sc_ref[i]
    mn_diag = jnp.min(jnp.where(causal, sd, jnp.inf), axis=0, keepdims=True)
    sc_ref[i] = jnp.where(causal, sd, -jnp.inf)

    def fold(fn, init, comb, n_chunks, t=None):
        t8 = None if t is None else jnp.broadcast_to(t, (8, tq))

        nacc = 4

        def body(j, accs):
            accs = list(accs)
            for r in range(ngrp):
                accs[r % nacc] = comb(accs[r % nacc], fn(sc_ref[j, 8 * r:8 * r + 8, :], t8))
            return tuple(accs)
        accs = lax.fori_loop(0, n_chunks, body, tuple(jnp.full((8, tq), init, f32) for _ in range(nacc)))
        acc = comb(comb(accs[0], accs[1]), comb(accs[2], accs[3]))
        if comb is jnp.add:
            return jnp.sum(acc, axis=0, keepdims=True)
        if comb is jnp.maximum:
            return jnp.max(acc, axis=0, keepdims=True)
        return jnp.min(acc, axis=0, keepdims=True)

    def count_ge(t):
        return fold(lambda s, t8: jnp.where(s >= t8, 1.0, 0.0), 0.0, jnp.add, nj, t)

    def count_gt(t):
        return fold(lambda s, t8: jnp.where(s > t8, 1.0, 0.0), 0.0, jnp.add, nj, t)

    def max_below(t):
        return fold(lambda s, t8: jnp.where(s < t8, s, -jnp.inf), -jnp.inf, jnp.maximum, nj, t)

    n_keys = (lax.broadcasted_iota(jnp.int32, (1, tq), 1) + (i * tq + 1)).astype(f32)
    take_all = n_keys <= kf

    @pl.when((i + 1) * tq <= topk)
    def _():
        def mk(j, carry):
            sc_ref[j] = jnp.zeros((tq, tq), f32)
            return carry
        lax.fori_loop(0, nj, mk, 0)

    @pl.when((i + 1) * tq > topk)
    def _():
        def minmax_body(j, c):
            mxs, mns = list(c[0]), list(c[1])
            for r in range(ngrp):
                v = sc_ref[j, 8 * r:8 * r + 8, :]
                mxs[r % 2] = jnp.maximum(mxs[r % 2], v)
                mns[r % 2] = jnp.minimum(mns[r % 2], v)
            return tuple(mxs), tuple(mns)
        full8 = lambda val: jnp.full((8, tq), val, f32)
        mxs, mns = lax.fori_loop(0, i, minmax_body, ((full8(-jnp.inf),) * 2, (full8(jnp.inf),) * 2))
        mx = jnp.maximum(jnp.max(jnp.maximum(mxs[0], mxs[1]), axis=0, keepdims=True),
                         jnp.max(sc_ref[i], axis=0, keepdims=True))
        mn = jnp.minimum(jnp.min(jnp.minimum(mns[0], mns[1]), axis=0, keepdims=True), mn_diag)
        c_max = count_ge(mx)
        done0 = jnp.logical_or(take_all, c_max >= kf)
        thr0 = jnp.where(take_all, -jnp.inf, mx)
        cge0 = jnp.where(take_all, n_keys, c_max)

        def bis(_, st):
            lo, hi = st
            mid = 0.5 * (lo + hi)
            ge = count_ge(mid) >= kf
            return jnp.where(ge, mid, lo), jnp.where(ge, hi, mid)

        lo, hi = lax.fori_loop(0, BISECT_ITERS, bis, (mn, mx))

        def snap_cond(st):
            return st[0] > 0.0

        def snap_body(st):
            _, hi, thr, cge, done = st
            m = max_below(hi)
            c = count_ge(m)
            ok = c >= kf
            newly = jnp.logical_and(ok, done < 0.5)
            thr = jnp.where(newly, m, thr)
            cge = jnp.where(newly, c, cge)
            hi = jnp.where(jnp.logical_or(done > 0.5, ok), hi, m)
            done = jnp.where(ok, 1.0, done)
            return jnp.sum(1.0 - done), hi, thr, cge, done

        done_f = jnp.where(done0, 1.0, 0.0)
        _, _, thr, cge, _ = lax.while_loop(snap_cond, snap_body, (jnp.sum(1.0 - done_f), hi, thr0, cge0, done_f))

        excess = jnp.sum(jnp.where(cge > kf, 1.0, 0.0)) > 0.0

        @pl.when(jnp.logical_not(excess))
        def _():
            def mk(j, carry):
                sc_ref[j] = jnp.where(sc_ref[j] >= thr, 0.0, NEG)
                return carry
            lax.fori_loop(0, nj, mk, 0)

        @pl.when(excess)
        def _():
            need = kf - count_gt(thr)
            lower = jnp.where(qcol <= krow, 1.0, 0.0).astype(bf16)

            def mk(j, seen):
                s = sc_ref[j]
                tie = s == thr
                rank = seen + jnp.dot(lower, jnp.where(tie, 1.0, 0.0).astype(bf16), preferred_element_type=f32)
                sel = jnp.logical_or(s > thr, jnp.logical_and(tie, rank <= need))
                sc_ref[j] = jnp.where(sel, 0.0, NEG)
                return rank[tq - 1:tq, :]
            lax.fori_loop(0, nj, mk, jnp.zeros((1, tq), f32))

    sc_ref[i] = jnp.where(causal, sc_ref[i], NEG)

    m_ref[...] = jnp.full(m_ref.shape, -jnp.inf, f32)
    l_ref[...] = jnp.zeros(l_ref.shape, f32)
    acc_ref[...] = jnp.zeros(acc_ref.shape, f32)
    ncb = tq // LANES

    def lane_blocks(x, comb):
        out = x[:, 0:LANES]
        for cb in range(1, x.shape[1] // LANES):
            out = comb(out, x[:, cb * LANES:(cb + 1) * LANES])
        return out

    def attend(j, width, bias_cols):
        off = pl.multiple_of(j * tq, tq)
        keys = pl.ds(off, width * tq)
        sel_mask = jnp.concatenate([sc_ref[j + c].T for c in range(width)], axis=1)
        for p in range(H_A // 2):
            blk = slice(LANES * p, LANES * (p + 1))
            s_pair = lax.dot_general(qap_ref[p], k_ref[keys, blk], NT_DIMS, preferred_element_type=f32)
            es = []
            for par in range(2):
                h = 2 * p + par
                s = s_pair[par * tq:(par + 1) * tq] + sel_mask
                if bias_cols is not None:
                    s = s + bias_ref[h, :, bias_cols]
                m_old = m_ref[h]
                row_max = jnp.max(lane_blocks(s, jnp.maximum), axis=1, keepdims=True)
                m_new = jnp.maximum(m_old, jnp.broadcast_to(row_max, (tq, LANES)))
                a = jnp.exp2(m_old - m_new)
                e = jnp.exp2(s - jnp.concatenate([m_new] * (width * ncb), axis=1))
                l_ref[h] = l_ref[h] * a + lane_blocks(e, jnp.add)
                acc_ref[h] = acc_ref[h] * a
                m_ref[h] = m_new
                es.append(e.astype(bf16))
            pv = jnp.dot(jnp.concatenate(es, axis=0), v_ref[keys, blk], preferred_element_type=f32)
            acc_ref[2 * p] += pv[0:tq]
            acc_ref[2 * p + 1] += pv[tq:2 * tq]

    n_far = jnp.maximum(i - 1, 0)
    if ATT_FAR_WIDTH == 2:
        def far_pair(jj, carry):
            attend(2 * jj, 2, None)
            return carry
        lax.fori_loop(0, n_far // 2, far_pair, 0)

        @pl.when(n_far % 2 == 1)
        def _():
            attend(n_far - 1, 1, None)
    else:
        def far_one(j, carry):
            attend(j, 1, None)
            return carry
        lax.fori_loop(0, n_far, far_one, 0)

    @pl.when(i >= 1)
    def _():
        attend(i - 1, 2, slice(0, 2 * tq))

    @pl.when(i == 0)
    def _():
        attend(0, 1, slice(tq, 2 * tq))

    for p in range(H_A // 2):
        l_even = jnp.broadcast_to(jnp.sum(l_ref[2 * p], axis=1, keepdims=True), (tq, LANES))
        l_odd = jnp.broadcast_to(jnp.sum(l_ref[2 * p + 1], axis=1, keepdims=True), (tq, LANES))
        o_pair = jnp.where(lo_half, acc_ref[2 * p] / l_even, acc_ref[2 * p + 1] / l_odd)
        oa_ref[:, LANES * p:LANES * (p + 1)] = o_pair.astype(oa_ref.dtype)


def _attn_km_call(qa, qi, wt, kab, vab, ki2, bias_tiles, topk, tq):
    b, s, _ = qa.shape
    nq = s // tq
    qspec = lambda width: pl.BlockSpec((None, tq, width), lambda bi, i: (bi, i, 0))
    whole = lambda shape: pl.BlockSpec((None,) + shape, lambda bi, i: (bi,) + (0,) * len(shape),
                                       pipeline_mode=pl.Buffered(1))
    scratch = [
        pltpu.VMEM((nq, tq, tq), f32),
        pltpu.VMEM((H_A // 2, 2 * tq, LANES), qa.dtype),
        pltpu.VMEM((H_IDX, tq, LANES), qi.dtype),
        pltpu.VMEM((H_A, tq, LANES), f32),
        pltpu.VMEM((H_A, tq, LANES), f32),
        pltpu.VMEM((H_A, tq, LANES), f32),
    ]
    return pl.pallas_call(
        functools.partial(_attn_km_kernel, topk),
        grid=(b, nq),
        in_specs=[qspec(W_A), qspec(W_A),
                  qspec(LANES),
                  whole((s, W_A)), whole((s, W_A)), whole((s, LANES)),
                  _const_spec((H_A, tq, 2 * tq))],
        out_specs=qspec(W_A),
        out_shape=jax.ShapeDtypeStruct((b, s, W_A), bf16),
        scratch_shapes=scratch,
        compiler_params=pltpu.CompilerParams(dimension_semantics=("arbitrary", "arbitrary"),
                                             vmem_limit_bytes=48 * MIB),
        name="attn_prompt",
    )(qa, qi, wt, kab, vab, ki2, bias_tiles)


def _ret_kernel(qr_ref, kr_ref, vr_ref, gr_ref, gng_ref, dmat_ref, cross_ref, kvd_ref, cdec_ref,
                yr_ref, st_ref):
    c = pl.program_id(1)

    @pl.when(c == 0)
    def _():
        st_ref[...] = jnp.zeros(st_ref.shape, f32)

    for h in range(H_R):
        sl = slice(DK_R * h, DK_R * (h + 1))
        q = qr_ref[:, sl]
        k = kr_ref[:, sl]
        v = vr_ref[:, sl]
        att = lax.dot_general(q, k, NT_DIMS, preferred_element_type=f32) * dmat_ref[h]
        st = st_ref[h]
        o = (jnp.dot(att.astype(bf16), v, preferred_element_type=f32)
             + jnp.dot(q, st.astype(bf16), preferred_element_type=f32) * cross_ref[h])
        kd = (k.astype(f32) * kvd_ref[h]).T.astype(bf16)
        st_ref[h] = cdec_ref[h, 0:1, :] * st + jnp.dot(kd, v, preferred_element_type=f32)
        mu = jnp.mean(o, axis=-1, keepdims=True)
        oc = o - mu
        var = jnp.mean(oc * oc, axis=-1, keepdims=True)
        yn = oc * lax.rsqrt(var + GN_EPS) * gng_ref[:, sl]
        yr_ref[:, sl] = (jax.nn.silu(gr_ref[:, sl]) * yn).astype(yr_ref.dtype)


def _ret_call(qr, kr, vr, gr, gng, dec, chunk):
    b, s, _ = qr.shape
    nc = s // chunk
    rspec = pl.BlockSpec((None, chunk, W_R), lambda bi, c: (bi, c, 0))
    return pl.pallas_call(
        _ret_kernel,
        grid=(b, nc),
        in_specs=[rspec, rspec, rspec, rspec, _const_spec((1, W_R)),
                  _const_spec((H_R, chunk, chunk)), _const_spec((H_R, chunk, LANES)),
                  _const_spec((H_R, chunk, LANES)), _const_spec((H_R, 8, LANES))],
        out_specs=[rspec, pl.BlockSpec((None, H_R, DK_R, DV_R), lambda bi, c: (bi, 0, 0, 0))],
        out_shape=[jax.ShapeDtypeStruct((b, s, W_R), bf16), jax.ShapeDtypeStruct((b, H_R, DK_R, DV_R), f32)],
        compiler_params=pltpu.CompilerParams(dimension_semantics=("arbitrary", "arbitrary"),
                                             vmem_limit_bytes=32 * MIB),
        name="ret_prompt",
    )(qr, kr, vr, gr, gng, dec["intra"], dec["cross"], dec["kv"], dec["chunk"])


def _sscore_kernel(layer, npages, pt_ref, ck_ref, q_ref, w_ref, kn_ref,
                   qr_ref, kr_ref, vr_ref, gr_ref, gng_ref, gam_ref, s0_ref,
                   sc_ref, snew_ref, yr_ref, sn_ref, kbuf, sem):
    b = pl.program_id(0)
    nb = pl.num_programs(0)
    slot = b % 2

    def page_copy(seq, p, sl):
        page = pt_ref[seq * npages + p]
        return pltpu.make_async_copy(ck_ref.at[layer, page], kbuf.at[sl, p], sem.at[sl])

    def start_all(seq, sl):
        def body(p, carry):
            page_copy(seq, p, sl).start()
            return carry
        lax.fori_loop(0, npages, body, 0)

    @pl.when(b == 0)
    def _():
        start_all(0, 0)

    @pl.when(b + 1 < nb)
    def _():
        start_all(b + 1, 1 - slot)

    _sample_retention_step(qr_ref, kr_ref, vr_ref, gr_ref, gng_ref, gam_ref, s0_ref, yr_ref, sn_ref)

    def wait_body(p, carry):
        page_copy(b, p, slot).wait()
        return carry
    lax.fori_loop(0, npages, wait_body, 0)

    q = q_ref[...].astype(bf16)
    w = w_ref[...]

    pages_per_dot = min(8, npages)
    for p0 in range(0, npages, pages_per_dot):
        kw = jnp.concatenate([kbuf[slot, p] for p in range(p0, p0 + pages_per_dot)], axis=1).astype(bf16)
        d = jnp.dot(q, kw, preferred_element_type=f32)
        sc_ref[:, p0 * PAGE_SIZE:(p0 + pages_per_dot) * PAGE_SIZE] = jnp.sum(
            w * jnp.maximum(d, 0.0), axis=0, keepdims=True)

    kn = kn_ref[...].astype(bf16).astype(f32)
    dn = jnp.sum(q.astype(f32) * kn, axis=1, keepdims=True)
    s_new = jnp.sum(w * jnp.maximum(dn, 0.0), axis=0, keepdims=True)
    snew_ref[...] = jnp.broadcast_to(s_new, (1, LANES))


def _sscore_call(page_table_flat, cache_kidx_t, q16, w16, kn, qr, kr, vr, gr, gng, gam, state, layer, npages):
    db = q16.shape[0]
    assert npages % min(8, npages) == 0
    per_seq = lambda d0, d1: pl.BlockSpec((None, d0, d1), lambda b, pt: (b, 0, 0))
    shared = pl.BlockSpec((H_R, DV_R), lambda b, pt: (0, 0))
    grid_spec = pltpu.PrefetchScalarGridSpec(
        num_scalar_prefetch=1,
        grid=(db,),
        in_specs=[
            pl.BlockSpec(memory_space=pl.ANY),
            per_seq(16, DH_IDX), per_seq(16, 1), per_seq(1, DH_IDX),
            per_seq(H_R, DK_R), per_seq(H_R, DK_R), per_seq(H_R, DV_R), per_seq(H_R, DV_R), shared, shared,
            pl.BlockSpec((None, None, H_R, DK_R, DV_R), lambda b, pt: (layer, b, 0, 0, 0)),
        ],
        out_specs=[per_seq(1, npages * PAGE_SIZE), per_seq(1, LANES), per_seq(H_R, DV_R),
                   pl.BlockSpec((None, H_R, DK_R, DV_R), lambda b, pt: (b, 0, 0, 0))],
        scratch_shapes=[
            pltpu.VMEM((2, npages, DH_IDX, PAGE_SIZE), f32),
            pltpu.SemaphoreType.DMA((2,)),
        ],
    )
    return pl.pallas_call(
        functools.partial(_sscore_kernel, layer, npages),
        grid_spec=grid_spec,
        out_shape=[jax.ShapeDtypeStruct((db, 1, npages * PAGE_SIZE), f32),
                   jax.ShapeDtypeStruct((db, 1, LANES), f32),
                   jax.ShapeDtypeStruct((db, H_R, DV_R), f32), jax.ShapeDtypeStruct((db, H_R, DK_R, DV_R), f32)],
        compiler_params=pltpu.CompilerParams(dimension_semantics=("arbitrary",), vmem_limit_bytes=32 * MIB),
        name="sample_scores",
    )(page_table_flat, cache_kidx_t, q16, w16, kn, qr, kr, vr, gr, gng, gam, state)


PREFIX_CHUNK = 256


def _ssel_kernel(topk, sc_ref, sn_ref, mask_ref, selnew_ref):
    kf = float(topk)
    db, length = sc_ref.shape
    sc = sc_ref[...]
    s_new = sn_ref[:, 0:1]

    def rsum(x):
        return jnp.sum(x, axis=1, keepdims=True)

    def count(cmp, t):
        return rsum(jnp.where(cmp(sc, t), 1.0, 0.0)) + jnp.where(cmp(s_new, t), 1.0, 0.0)

    ge = lambda a, t: a >= t
    gt = lambda a, t: a > t
    mx = jnp.maximum(jnp.max(sc, axis=1, keepdims=True), s_new)
    mn = jnp.minimum(jnp.min(sc, axis=1, keepdims=True), s_new)
    c_max = count(ge, mx)
    done0 = jnp.where(c_max >= kf, 1.0, 0.0)

    def bis(_, st):
        lo, hi = st
        mid = 0.5 * (lo + hi)
        ok = count(ge, mid) >= kf
        return jnp.where(ok, mid, lo), jnp.where(ok, hi, mid)
    lo, hi = lax.fori_loop(0, BISECT_ITERS, bis, (mn, mx))

    def snap_cond(st):
        return st[0] > 0.0

    def snap_body(st):
        _, hi, thr, done = st
        below = jnp.maximum(jnp.max(jnp.where(sc < hi, sc, -jnp.inf), axis=1, keepdims=True),
                            jnp.where(s_new < hi, s_new, -jnp.inf))
        ok = count(ge, below) >= kf
        newly = jnp.logical_and(ok, done < 0.5)
        thr = jnp.where(newly, below, thr)
        hi = jnp.where(jnp.logical_or(done > 0.5, ok), hi, below)
        done = jnp.where(ok, 1.0, done)
        return jnp.sum(1.0 - done), hi, thr, done
    _, _, thr, _ = lax.while_loop(snap_cond, snap_body, (jnp.sum(1.0 - done0), hi, mx, done0))

    need = kf - count(gt, thr)
    pc = min(PREFIX_CHUNK, length)
    r_i = lax.broadcasted_iota(jnp.int32, (pc, pc), 0)
    c_i = lax.broadcasted_iota(jnp.int32, (pc, pc), 1)
    upper = jnp.where(r_i <= c_i, 1.0, 0.0).astype(bf16)
    seen = jnp.zeros((db, 1), f32)
    for c0 in range(0, length, pc):
        s_c = sc[:, c0:c0 + pc]
        tie = s_c == thr
        rank = seen + jnp.dot(jnp.where(tie, 1.0, 0.0).astype(bf16), upper, preferred_element_type=f32)
        sel = jnp.logical_or(s_c > thr, jnp.logical_and(tie, rank <= need))
        mask_ref[:, c0:c0 + pc] = jnp.where(sel, 0.0, NEG)
        seen = rank[:, pc - 1:pc]
    sel_new = jnp.logical_or(s_new > thr, jnp.logical_and(s_new == thr, seen + 1.0 <= need))
    selnew_ref[...] = jnp.broadcast_to(jnp.where(sel_new, 1.0, 0.0), (db, LANES))


def _ssel_call(scores, s_new, topk):
    db, length = scores.shape
    assert length % min(PREFIX_CHUNK, length) == 0
    return pl.pallas_call(
        functools.partial(_ssel_kernel, topk),
        out_shape=[jax.ShapeDtypeStruct((db, length), f32), jax.ShapeDtypeStruct((db, LANES), f32)],
        compiler_params=pltpu.CompilerParams(vmem_limit_bytes=32 * MIB),
        name="sample_select",
    )(scores, s_new)


def _satt_kernel(layer, npages, grp, pt_ref,
                 ckk_ref, ckv_ref, qt_ref, knt_ref, vnt_ref, bias_ref, bias0_ref, mask_ref, snew_ref,
                 qr_ref, kr_ref, vr_ref, gr_ref, gng_ref, gam_ref, s0_ref,
                 ot_ref, yr_ref, sn_ref,
                 kbuf, vbuf, sem, m_ref, l_ref, acc_ref):
    b = pl.program_id(0)
    g = pl.program_id(1)
    nb = pl.num_programs(0)
    ng = pl.num_programs(1)
    t = b * ng + g
    slot = t % 2

    def page_copies(seq, gi, p, sl):
        page = pt_ref[seq * npages + gi * grp + p]
        return (pltpu.make_async_copy(ckk_ref.at[layer, page], kbuf.at[sl, p], sem.at[sl, 0]),
                pltpu.make_async_copy(ckv_ref.at[layer, page], vbuf.at[sl, p], sem.at[sl, 1]))

    def start_all(seq, gi, sl):
        def body(p, carry):
            ck, cv = page_copies(seq, gi, p, sl)
            ck.start()
            cv.start()
            return carry
        lax.fori_loop(0, grp, body, 0)

    @pl.when(t == 0)
    def _():
        start_all(0, 0, 0)

    @pl.when(t + 1 < nb * ng)
    def _():
        wrap = g + 1 == ng
        start_all(jnp.where(wrap, b + 1, b), jnp.where(wrap, 0, g + 1), 1 - slot)

    @pl.when(g == 0)
    def _():
        m_ref[...] = jnp.full(m_ref.shape, -jnp.inf, f32)
        l_ref[...] = jnp.zeros(l_ref.shape, f32)
        acc_ref[...] = jnp.zeros(acc_ref.shape, f32)

    def wait_body(p, carry):
        ck, cv = page_copies(b, g, p, slot)
        ck.wait()
        cv.wait()
        return carry
    lax.fori_loop(0, grp, wait_body, 0)

    _satt_fold(kbuf.at[slot], vbuf.at[slot], qt_ref, bias_ref, mask_ref, slice(0, grp), m_ref, l_ref, acc_ref)

    @pl.when(g == ng - 1)
    def _():
        _satt_finish(qt_ref, knt_ref, vnt_ref, bias0_ref, snew_ref, m_ref, l_ref, acc_ref, ot_ref)

    @pl.when(g == 0)
    def _():
        _sample_retention_step(qr_ref, kr_ref, vr_ref, gr_ref, gng_ref, gam_ref, s0_ref, yr_ref, sn_ref)


def _satt_fold(kb_ref, vb_ref, qt_ref, bias_ref, mask_ref, rows, m_ref, l_ref, acc_ref):
    mask = mask_ref[rows]
    for h in range(H_A):
        qc = qt_ref[:, h:h + 1]
        kh = kb_ref[:, h]
        lg = jnp.sum(kh * qc[None], axis=1, keepdims=True) + bias_ref[h, rows] + mask
        m_old = m_ref[h]
        m_blk = jnp.max(jnp.max(lg, axis=0), axis=1, keepdims=True)
        m_new = jnp.maximum(m_old, m_blk)
        a = jnp.exp(m_old - m_new)
        e = jnp.exp(lg - m_new[None])
        l_ref[h] = l_ref[h] * a + jnp.sum(e, axis=0)
        acc_ref[h] = acc_ref[h] * a + jnp.sum(vb_ref[:, h] * e, axis=0)
        m_ref[h] = m_new


def _satt_finish(qt_ref, knt_ref, vnt_ref, bias0_ref, snew_ref, m_ref, l_ref, acc_ref, ot_ref):
    lg_new = jnp.sum(qt_ref[...] * knt_ref[...], axis=0, keepdims=True) + bias0_ref[...]
    lg_new = jnp.where(snew_ref[0:1, 0:1] > 0.5, lg_new, NEG)
    for h in range(H_A):
        m_c = m_ref[h][:, 0:1]
        lg_h = lg_new[:, h:h + 1]
        m_f = jnp.maximum(m_c, lg_h)
        a = jnp.exp(m_c - m_f)
        e_new = jnp.exp(lg_h - m_f)
        den = jnp.sum(l_ref[h], axis=1, keepdims=True) * a + e_new
        num = jnp.sum(acc_ref[h], axis=1, keepdims=True) * a + e_new * vnt_ref[:, h:h + 1]
        ot_ref[:, h:h + 1] = num / den


def _satt_reset(m_ref, l_ref, acc_ref):
    m_ref[...] = jnp.full(m_ref.shape, -jnp.inf, f32)
    l_ref[...] = jnp.zeros(l_ref.shape, f32)
    acc_ref[...] = jnp.zeros(acc_ref.shape, f32)


def _sample_retention_step(qr_ref, kr_ref, vr_ref, gr_ref, gng_ref, gam_ref, s0_ref, yr_ref, sn_ref):
    r_i = lax.broadcasted_iota(jnp.int32, (DK_R, DK_R), 0)
    c_i = lax.broadcasted_iota(jnp.int32, (DK_R, DK_R), 1)
    eye = jnp.where(r_i == c_i, 1.0, 0.0)
    for h in range(H_R):
        qrow = qr_ref[h:h + 1, :]
        krow = kr_ref[h:h + 1, :]
        vrow = vr_ref[h:h + 1, :]
        gam = gam_ref[h:h + 1, :]
        qcol = jnp.sum(eye * qrow, axis=1, keepdims=True)
        kcol = jnp.sum(eye * krow, axis=1, keepdims=True)
        st = s0_ref[h]
        qk = jnp.sum(qrow * krow, axis=1, keepdims=True)
        o = qk * vrow + gam * jnp.sum(qcol * st, axis=0, keepdims=True)
        sn_ref[h] = gam * st + kcol * vrow
        mu = jnp.mean(o, axis=1, keepdims=True)
        oc = o - mu
        var = jnp.mean(oc * oc, axis=1, keepdims=True)
        yn = oc * lax.rsqrt(var + GN_EPS) * gng_ref[h:h + 1, :]
        yr_ref[h:h + 1, :] = jax.nn.silu(gr_ref[h:h + 1, :]) * yn


def _sret_kernel(qr_ref, kr_ref, vr_ref, gr_ref, gng_ref, gam_ref, s0_ref, yr_ref, sn_ref):
    _sample_retention_step(qr_ref, kr_ref, vr_ref, gr_ref, gng_ref, gam_ref, s0_ref, yr_ref, sn_ref)


def _sret_call(qr, kr, vr, gr, gng, gam, state, layer):
    db = qr.shape[0]
    per_seq = lambda d0, d1: pl.BlockSpec((None, d0, d1), lambda b: (b, 0, 0))
    shared = pl.BlockSpec((H_R, DV_R), lambda b: (0, 0))
    return pl.pallas_call(
        _sret_kernel,
        grid=(db,),
        in_specs=[per_seq(H_R, DK_R), per_seq(H_R, DK_R), per_seq(H_R, DV_R), per_seq(H_R, DV_R), shared, shared,
                  pl.BlockSpec((None, None, H_R, DK_R, DV_R), lambda b: (layer, b, 0, 0, 0))],
        out_specs=[per_seq(H_R, DV_R), pl.BlockSpec((None, H_R, DK_R, DV_R), lambda b: (b, 0, 0, 0))],
        out_shape=[jax.ShapeDtypeStruct((db, H_R, DV_R), f32), jax.ShapeDtypeStruct((db, H_R, DK_R, DV_R), f32)],
        compiler_params=pltpu.CompilerParams(dimension_semantics=("arbitrary",)),
        name="sample_retention",
    )(qr, kr, vr, gr, gng, gam, state)


def _satt_call(pt_flat, cache_k_t, cache_v_t, qt, knt, vnt, bias_pos, bias0, mask, snew, qr, kr, vr, gr, gng, gam,
               state, layer, npages, grp):
    db = qt.shape[0]
    ng = npages // grp
    per_seq = lambda d0, d1: pl.BlockSpec((None, d0, d1), lambda b, g, pt: (b, 0, 0))
    grid_spec = pltpu.PrefetchScalarGridSpec(
        num_scalar_prefetch=1,
        grid=(db, ng),
        in_specs=[
            pl.BlockSpec(memory_space=pl.ANY), pl.BlockSpec(memory_space=pl.ANY),
            per_seq(DH_A, H_A), per_seq(DH_A, H_A), per_seq(DH_A, H_A),
            pl.BlockSpec((H_A, grp, 1, PAGE_SIZE), lambda b, g, pt: (0, g, 0, 0)),
            pl.BlockSpec((1, H_A), lambda b, g, pt: (0, 0)),
            pl.BlockSpec((None, grp, 1, PAGE_SIZE), lambda b, g, pt: (b, g, 0, 0)),
            per_seq(1, LANES),
            per_seq(H_R, DK_R), per_seq(H_R, DK_R), per_seq(H_R, DV_R), per_seq(H_R, DV_R),
            pl.BlockSpec((H_R, DV_R), lambda b, g, pt: (0, 0)),
            pl.BlockSpec((H_R, DV_R), lambda b, g, pt: (0, 0)),
            pl.BlockSpec((None, None, H_R, DK_R, DV_R), lambda b, g, pt: (layer, b, 0, 0, 0)),
        ],
        out_specs=[per_seq(DH_A, H_A), per_seq(H_R, DV_R),
                   pl.BlockSpec((None, H_R, DK_R, DV_R), lambda b, g, pt: (b, 0, 0, 0))],
        scratch_shapes=[
            pltpu.VMEM((2, grp, H_A, DH_A, PAGE_SIZE), f32),
            pltpu.VMEM((2, grp, H_A, DH_A, PAGE_SIZE), f32),
            pltpu.SemaphoreType.DMA((2, 2)),
            pltpu.VMEM((H_A, 1, PAGE_SIZE), f32),
            pltpu.VMEM((H_A, 1, PAGE_SIZE), f32),
            pltpu.VMEM((H_A, DH_A, PAGE_SIZE), f32),
        ],
    )
    return pl.pallas_call(
        functools.partial(_satt_kernel, layer, npages, grp),
        grid_spec=grid_spec,
        out_shape=[jax.ShapeDtypeStruct((db, DH_A, H_A), f32), jax.ShapeDtypeStruct((db, H_R, DV_R), f32),
                   jax.ShapeDtypeStruct((db, H_R, DK_R, DV_R), f32)],
        compiler_params=pltpu.CompilerParams(dimension_semantics=("arbitrary", "arbitrary"),
                                             vmem_limit_bytes=40 * MIB),
        name="sample_attend",
    )(pt_flat, cache_k_t, cache_v_t, qt, knt, vnt, bias_pos, bias0, mask, snew, qr, kr, vr, gr, gng, gam, state)


def _t5_bucket(rel):
    n = jnp.maximum(rel, 0)
    max_exact = NUM_BUCKETS // 2
    nf = jnp.maximum(n, 1).astype(f32)
    large = max_exact + (jnp.log(nf / max_exact) / math.log(MAX_DISTANCE / max_exact)
                         * (NUM_BUCKETS - max_exact)).astype(jnp.int32)
    large = jnp.minimum(large, NUM_BUCKETS - 1)
    return jnp.where(n < max_exact, n, large)


def _rotary_tables(pos):
    half = DK_R // 2
    freqs = ROPE_BASE ** (-jnp.arange(half, dtype=f32) / half)
    ang = pos.astype(f32)[:, None] * freqs[None, :]
    cos, sin = jnp.cos(ang), jnp.sin(ang)
    return jnp.concatenate([cos, cos], axis=1), jnp.concatenate([-sin, sin], axis=1)


def _decay_tables(chunk):
    lg = jnp.log1p(-jnp.exp2(-5.0 - jnp.arange(H_R, dtype=f32)))
    i = jnp.arange(chunk, dtype=f32)
    diff = i[:, None] - i[None, :]
    causal = diff >= 0
    intra = jnp.where(causal[None], jnp.exp(jnp.where(causal, diff, 0.0)[None] * lg[:, None, None]), 0.0)
    cross = jnp.exp((i[None, :] + 1.0) * lg[:, None])
    kv = jnp.exp((chunk - 1.0 - i)[None, :] * lg[:, None])
    cdec = jnp.exp(chunk * lg)
    bc = lambda a: jnp.broadcast_to(a[:, :, None], (H_R, chunk, LANES))
    return {"intra": intra, "cross": bc(cross), "kv": bc(kv),
            "chunk": jnp.broadcast_to(cdec[:, None, None], (H_R, 8, LANES))}


def _bias_tiles(rel_bias, tq):
    assert _last_bucket_from(tq + 1), "keys beyond the previous chunk must share the last bucket"
    rb = rel_bias.astype(f32) - rel_bias[NUM_BUCKETS - 1].astype(f32)[None, :]
    tab = rb[_t5_bucket(jnp.arange(2 * tq + 1, dtype=jnp.int32))].T
    m = 3 * tq
    g = jnp.concatenate([tab[:, tq::-1],
                         jnp.broadcast_to(tab[:, 0:1], (H_A, tq - 1)),
                         tab[:, 2 * tq:tq:-1]], axis=1)
    flat = jnp.tile(g, (1, tq))[:, :tq * (m - 1)]
    return flat.reshape(H_A, tq, m - 1)[:, :, :2 * tq]


def _last_bucket_from(n):
    large = NUM_BUCKETS // 2 + int(math.log(n / (NUM_BUCKETS // 2)) / math.log(MAX_DISTANCE / (NUM_BUCKETS // 2))
                                   * (NUM_BUCKETS - NUM_BUCKETS // 2) - 1e-3)
    return n >= NUM_BUCKETS // 2 and large >= NUM_BUCKETS - 1


def _layer_weights(l, ffn1_wg, ffn1_wu, ffn1_wd, ln1_g, ln1_b, w_in, ret_gn_g, w_out, ln2_g, ln2_b,
                   ffn2_wg, ffn2_wu, ffn2_wd, ln3_g, ln3_b):
    pts = np.cumsum((0,) + IN_SIZES)
    wi = w_in[l]
    col = lambda k: wi[:, pts[k]:pts[k + 1]]
    zeros = jnp.zeros((D_MODEL, LANES - H_IDX), wi.dtype)
    w_in2 = jnp.concatenate([col(0), col(1), col(2), col(3), col(4), col(4), col(5), zeros,
                             col(6), col(7), col(8), col(9)], axis=1).astype(bf16)
    r2 = lambda a: a[l].reshape(1, -1).astype(f32)
    return {
        "ffn1_wg": ffn1_wg[l].astype(bf16), "ffn1_wu": ffn1_wu[l].astype(bf16), "ffn1_wd": ffn1_wd[l].astype(bf16),
        "ln1_g": r2(ln1_g), "ln1_b": r2(ln1_b), "w_in": w_in2, "gng": r2(ret_gn_g),
        "w_out": w_out[l].astype(bf16), "ln2_g": r2(ln2_g), "ln2_b": r2(ln2_b),
        "ffn2_wg": ffn2_wg[l].astype(bf16), "ffn2_wu": ffn2_wu[l].astype(bf16), "ffn2_wd": ffn2_wd[l].astype(bf16),
        "ln3_g": r2(ln3_g), "ln3_b": r2(ln3_b),
    }


def kernel(x_prompt, x_sample, cache_k, cache_v, cache_kidx, state_ret, page_table, rel_bias,
           ffn1_wg, ffn1_wu, ffn1_wd, ln1_g, ln1_b, w_in, ret_gn_g, w_out,
           ln2_g, ln2_b, ffn2_wg, ffn2_wu, ffn2_wd, ln3_g, ln3_b):
    b, s, _ = x_prompt.shape
    db, ds, _ = x_sample.shape
    depth = w_in.shape[0]
    npages = page_table.shape[1]
    past = npages * PAGE_SIZE
    assert ds == 1, "the sample group decodes one token per sequence"
    alpha = (2 * depth) ** 0.25

    tq = min(ATT_TQ, s)
    chunk = min(RET_CHUNK, s)
    assert s % tq == 0 and s % chunk == 0 and tq % LANES == 0
    topk_p = min(TOPK_MAX, s // 4)
    topk_s = min(TOPK_MAX, (past + ds) // 4)

    rot_p = _rotary_tables(jnp.arange(s, dtype=jnp.int32))
    rot_s = _rotary_tables(jnp.full((db * ds,), past, jnp.int32))
    dec_p = _decay_tables(chunk)
    gam = jnp.broadcast_to(_decay_tables(1)["chunk"][:, 0, :], (H_R, LANES))
    bias_tiles = _bias_tiles(rel_bias, tq)
    near = min(past, 2 * PAGE_SIZE)
    assert _last_bucket_from(near + 1)
    bias_near = rel_bias.astype(f32)[_t5_bucket(jnp.arange(near, 0, -1, dtype=jnp.int32))].T
    bias_far = jnp.broadcast_to(rel_bias[NUM_BUCKETS - 1].astype(f32)[:, None], (H_A, past - near))
    bias_pos = jnp.concatenate([bias_far, bias_near], axis=1).reshape(H_A, npages, 1, PAGE_SIZE)
    bias0 = rel_bias[0:1].astype(f32)
    pt_flat = page_table.reshape(-1).astype(jnp.int32)
    grp = min(SAMPLE_PAGE_GROUP, npages)
    assert npages % grp == 0
    ckidx_t = jnp.transpose(cache_kidx, (0, 1, 3, 2))
    ck_t = jnp.transpose(cache_k, (0, 1, 3, 4, 2))
    cv_t = jnp.transpose(cache_v, (0, 1, 3, 4, 2))

    hp = x_prompt.reshape(b * s, D_MODEL)
    hs = x_sample.reshape(db * ds, D_MODEL)
    outs = {k: [] for k in ("kp", "vp", "kip", "sp", "ks", "vs", "kis", "ss")}
    for l in range(depth):
        w = _layer_weights(l, ffn1_wg, ffn1_wu, ffn1_wd, ln1_g, ln1_b, w_in, ret_gn_g, w_out, ln2_g, ln2_b,
                           ffn2_wg, ffn2_wu, ffn2_wd, ln3_g, ln3_b)
        ps = _pre_call(hs, w, rot_s[0], rot_s[1], alpha, DH_A ** -0.5, db * ds, f32, None, "pre_sample")
        q16 = jnp.pad(ps["qi"].reshape(db, H_IDX, DH_IDX), ((0, 0), (0, 16 - H_IDX), (0, 0)))
        w16 = jnp.pad((ps["wi"][:, :H_IDX] * H_IDX ** -0.5).reshape(db, H_IDX, 1), ((0, 0), (0, 16 - H_IDX), (0, 0)))
        t8 = lambda a: a.reshape(db, H_A, DH_A).transpose(0, 2, 1)
        r4 = lambda a: a.reshape(db, H_R, DK_R)
        ret_args = (r4(ps["qr"]), r4(ps["kr"]), r4(ps["vr"]), r4(ps["gr"]), w["gng"].reshape(H_R, DV_R), gam)
        sc_s, sn_s, yr_s, st_s = _sscore_call(pt_flat, ckidx_t, q16, w16, ps["ki"].reshape(db, 1, DH_IDX),
                                              *ret_args, state_ret, l, npages)
        mask, snew = _ssel_call(sc_s.reshape(db, past), sn_s.reshape(db, LANES), topk_s)
        mask = mask.reshape(db, npages, 1, PAGE_SIZE)
        snew = snew.reshape(db, 1, LANES)

        steps = (b * s) // tq if tq == STREAM_POST_TM else 0
        sps = steps // db if steps and steps % db == 0 else 0
        step_pages = (npages // 2) // sps if sps and npages % 2 == 0 and (npages // 2) % sps == 0 else 0
        sgrp = min(STREAM_PAGE_GROUP, step_pages) if step_pages else 0
        stream = None
        if sgrp and step_pages % sgrp == 0:
            stream = dict(pt_flat=pt_flat, ck_t=ck_t, cv_t=cv_t, qt=t8(ps["qa"]), bias_pos=bias_pos, mask=mask,
                          layer=l, npages=npages, grp=sgrp, units=step_pages // sgrp, steps_per_seq=sps)
        pp = _pre_call(hp, w, rot_p[0], rot_p[1], alpha, DH_A ** -0.5 * LOG2E, tq, bf16, s, "pre_prompt", stream)
        r3 = lambda a: a.reshape(b, s, a.shape[-1])
        oa = _attn_km_call(r3(pp["qa"]), r3(pp["qi"]), r3(pp["wi"]), r3(pp["kab"]), r3(pp["vab"]), r3(pp["ki2"]),
                           bias_tiles * LOG2E, topk_p, tq)
        yr, st_p = _ret_call(r3(pp["qr"]), r3(pp["kr"]), r3(pp["vr"]), r3(pp["gr"]), w["gng"], dec_p, chunk)
        if stream is not None:
            hp, ot_s = _post_stream_call(pp["h"], oa.reshape(b * s, W_A), yr.reshape(b * s, W_R), w, alpha,
                                         STREAM_POST_TM, pt_flat, ck_t, cv_t, stream["qt"], t8(ps["ka"]),
                                         t8(ps["va"]), bias_pos, bias0, mask, snew, pp["state"], l, npages,
                                         sgrp, stream["units"], "post_prompt")
        else:
            hp = _post_call(pp["h"], oa.reshape(b * s, W_A), yr.reshape(b * s, W_R), w, alpha,
                            min(POST_TM, b * s), "post_prompt")
            ot_s, yr_s, st_s = _satt_call(pt_flat, ck_t, cv_t, t8(ps["qa"]), t8(ps["ka"]), t8(ps["va"]), bias_pos,
                                          bias0, mask, snew, *ret_args, state_ret, l, npages, grp)
        outs["kp"].append(pp["ka"].reshape(b, H_A, DH_A, s).transpose(0, 3, 1, 2))
        outs["vp"].append(pp["va"].reshape(b, H_A, DH_A, s).transpose(0, 3, 1, 2))
        outs["kip"].append(pp["ki"].transpose(0, 2, 1))
        outs["sp"].append(st_p)
        oa_s = ot_s.transpose(0, 2, 1).reshape(db, W_A)
        hs = _post_call(ps["h"], oa_s, yr_s.reshape(db, W_R), w, alpha, db * ds, "post_sample")
        outs["ks"].append(ps["ka"].reshape(db, ds, H_A, DH_A))
        outs["vs"].append(ps["va"].reshape(db, ds, H_A, DH_A))
        outs["kis"].append(ps["ki"].reshape(db, ds, DH_IDX))
        outs["ss"].append(st_s)

    stack = lambda k: jnp.stack(outs[k])
    return (hp.reshape(b, s, D_MODEL), hs.reshape(db, ds, D_MODEL),
            stack("kp"), stack("vp"), stack("kip"), stack("sp"),
            stack("ks"), stack("vs"), stack("kis"), stack("ss"))
```

```python
import functools
import math

import numpy as np
import jax
import jax.numpy as jnp
from jax import lax
from jax.experimental import pallas as pl
from jax.experimental.pallas import tpu as pltpu

D_MODEL = 1024
D_FF = 2816
PAGE_SIZE = 128
H_A = 8
DH_A = 64
W_A = H_A * DH_A
H_IDX = 8
DH_IDX = 64
TOPK_MAX = 256
NUM_BUCKETS = 32
MAX_DISTANCE = 128
H_R = 4
DK_R = 128
DV_R = 128
W_R = H_R * DV_R
ROPE_BASE = 10000.0
LN_EPS = 1e-5
GN_EPS = 1e-5
IN_SIZES = (W_A, W_A, W_A, H_IDX * DH_IDX, DH_IDX, H_IDX, H_R * DK_R, H_R * DK_R, W_R, W_R)

LANES = 128
MIB = 1024 * 1024
NEG = -1e30
LOG2E = math.log2(math.e)

FF_CHUNK = 512
POST_TM = 512
ATT_TQ = 256
RET_CHUNK = 256
BISECT_ITERS = 16
SAMPLE_PAGE_GROUP = 16
STREAM_POST_TM = 256
STREAM_PAGE_GROUP = 8
ATT_FAR_WIDTH = 2

C_QA, C_KA, C_VA, C_QI, C_KI2, C_WI, C_QR, C_KR, C_VR, C_GR, C_END = (
    0, 512, 1024, 1536, 2048, 2176, 2304, 2816, 3328, 3840, 4352)

f32 = jnp.float32
bf16 = jnp.bfloat16
NT_DIMS = (((1,), (1,)), ((), ()))


def _const_spec(shape):
    nd = len(shape)
    return pl.BlockSpec(shape, lambda *_: (0,) * nd, pipeline_mode=pl.Buffered(1))


def _ln(x, g, b):
    mu = jnp.mean(x, axis=-1, keepdims=True)
    xc = x - mu
    var = jnp.mean(xc * xc, axis=-1, keepdims=True)
    return xc * lax.rsqrt(var + LN_EPS) * g + b


def _ffn(xb, wg_ref, wu_ref, wd_ref, midway=None):
    acc = None
    starts = list(range(0, D_FF, FF_CHUNK))
    for ci, c0 in enumerate(starts):
        if midway is not None and ci == len(starts) // 2:
            midway()
        c1 = min(c0 + FF_CHUNK, D_FF)
        g = jnp.dot(xb, wg_ref[:, c0:c1], preferred_element_type=f32)
        u = jnp.dot(xb, wu_ref[:, c0:c1], preferred_element_type=f32)
        a = (jax.nn.silu(g) * u).astype(bf16)
        part = jnp.dot(a, wd_ref[c0:c1, :], preferred_element_type=f32)
        acc = part if acc is None else acc + part
    return acc


def _pre_kernel(alpha, q_scale, feature_major, x_ref, wg_ref, wu_ref, wd_ref, lng_ref, lnb_ref, win_ref,
                rc_ref, rs_ref,
                h_ref, qa_ref, ka_ref, va_ref, kab_ref, vab_ref, qi_ref, ki_ref, ki2_ref, wi_ref,
                qr_ref, kr_ref, vr_ref, gr_ref):
    act = qa_ref.dtype
    x = x_ref[...]
    f = _ffn(x.astype(bf16), wg_ref, wu_ref, wd_ref)
    h = _ln(alpha * x + 0.5 * f, lng_ref[...], lnb_ref[...])
    h_ref[...] = h
    hb = h.astype(bf16)

    def proj(c0, c1):
        return jnp.dot(hb, win_ref[:, c0:c1], preferred_element_type=f32)

    qa_ref[...] = (proj(C_QA, C_KA) * q_scale).astype(act)
    ka = proj(C_KA, C_VA)
    kab_ref[...] = ka.astype(act)
    va = proj(C_VA, C_QI)
    qi_ref[...] = (proj(C_QI, C_KI2) * DH_IDX ** -0.5).astype(act)
    kk = proj(C_KI2, C_WI)
    wi_ref[...] = proj(C_WI, C_QR)
    vab_ref[...] = va.astype(act)
    if feature_major:
        ka_ref[...] = ka.T
        va_ref[...] = va.T
        ki_ref[...] = kk.T[:DH_IDX, :]
    else:
        ka_ref[...] = ka
        va_ref[...] = va
        ki_ref[...] = kk[:, :DH_IDX]
    ki2_ref[...] = kk.astype(act)
    qr = proj(C_QR, C_KR)
    kr = proj(C_KR, C_VR)
    c = rc_ref[...]
    s = rs_ref[...]
    for hh in range(H_R):
        sl = slice(DK_R * hh, DK_R * (hh + 1))
        qh = qr[:, sl]
        kh = kr[:, sl]
        qr_ref[:, sl] = (qh * c + pltpu.roll(qh, DK_R // 2, 1) * s).astype(act)
        kr_ref[:, sl] = ((kh * c + pltpu.roll(kh, DK_R // 2, 1) * s) * DK_R ** -0.5).astype(act)
    vr_ref[...] = proj(C_VR, C_GR).astype(act)
    gr_ref[...] = proj(C_GR, C_END)


def _pre_call(x, w, rot_c, rot_s, alpha, q_scale, tm, act, seq_len, name):
    n = x.shape[0]
    grid = (pl.cdiv(n, tm),)
    row = lambda width: pl.BlockSpec((tm, width), lambda i: (i, 0))
    rot_blocks = rot_c.shape[0] // tm
    rot = pl.BlockSpec((tm, LANES), lambda i: (i % rot_blocks, 0))
    feature_major = seq_len is not None
    if feature_major:
        assert seq_len % tm == 0 and n % seq_len == 0
        seq_blocks = seq_len // tm
        kv_shape = lambda width: jax.ShapeDtypeStruct((n // seq_len, width, seq_len), f32)
        kv_spec = lambda width: pl.BlockSpec((None, width, tm), lambda i: (i // seq_blocks, 0, i % seq_blocks))
    else:
        kv_shape = lambda width: jax.ShapeDtypeStruct((n, width), f32)
        kv_spec = row
    in_specs = [
        row(D_MODEL),
        _const_spec((D_MODEL, D_FF)), _const_spec((D_MODEL, D_FF)), _const_spec((D_FF, D_MODEL)),
        _const_spec((1, D_MODEL)), _const_spec((1, D_MODEL)),
        _const_spec((D_MODEL, C_END)),
        rot, rot,
    ]
    outs = [
        ("h", D_MODEL, f32), ("qa", W_A, act), ("ka", W_A, f32), ("va", W_A, f32), ("kab", W_A, act),
        ("vab", W_A, act), ("qi", W_A, act), ("ki", DH_IDX, f32), ("ki2", LANES, act), ("wi", LANES, f32),
        ("qr", W_R, act), ("kr", W_R, act), ("vr", W_R, act), ("gr", W_R, f32),
    ]
    kv_names = ("ka", "va", "ki")
    out_shape = [kv_shape(wd) if k in kv_names else jax.ShapeDtypeStruct((n, wd), dt) for k, wd, dt in outs]
    out_specs = [kv_spec(wd) if k in kv_names else row(wd) for k, wd, _ in outs]
    res = pl.pallas_call(
        functools.partial(_pre_kernel, alpha, q_scale, feature_major),
        grid=grid, in_specs=in_specs, out_specs=out_specs, out_shape=out_shape,
        compiler_params=pltpu.CompilerParams(dimension_semantics=("arbitrary",), vmem_limit_bytes=52 * MIB),
        name=name,
    )(x, w["ffn1_wg"], w["ffn1_wu"], w["ffn1_wd"], w["ln1_g"], w["ln1_b"], w["w_in"], rot_c, rot_s)
    return {k: v for (k, _, _), v in zip(outs, res)}


def _post_kernel(alpha, h_ref, oa_ref, yr_ref, wo_ref, l2g_ref, l2b_ref, wg_ref, wu_ref, wd_ref,
                 l3g_ref, l3b_ref, out_ref):
    h = h_ref[...]
    mix = (jnp.dot(oa_ref[...].astype(bf16), wo_ref[0:W_A, :], preferred_element_type=f32)
           + jnp.dot(yr_ref[...].astype(bf16), wo_ref[W_A:W_A + W_R, :], preferred_element_type=f32))
    h2 = _ln(alpha * h + mix, l2g_ref[...], l2b_ref[...])
    f = _ffn(h2.astype(bf16), wg_ref, wu_ref, wd_ref)
    out_ref[...] = _ln(alpha * h2 + 0.5 * f, l3g_ref[...], l3b_ref[...])


def _post_stream_kernel(alpha, layer, npages, grp, units, steps_per_seq, pt_ref,
                        h_ref, oa_ref, yr_ref, wo_ref, l2g_ref, l2b_ref, wg_ref, wu_ref, wd_ref, l3g_ref, l3b_ref,
                        ckk_ref, ckv_ref, qt_ref, knt_ref, vnt_ref, bias_ref, bias0_ref, mask_ref, snew_ref,
                        out_ref, ot_ref, kbuf, vbuf, sem, m_ref, l_ref, acc_ref):
    stream = _page_stream((layer, npages, grp, units, steps_per_seq), pt_ref, ckk_ref, ckv_ref,
                          qt_ref, bias_ref, mask_ref, kbuf, vbuf, sem, m_ref, l_ref, acc_ref)

    @pl.when(stream.part == 0)
    def _():
        _satt_reset(m_ref, l_ref, acc_ref)

    stream.begin()
    h = h_ref[...]
    mix = (jnp.dot(oa_ref[...].astype(bf16), wo_ref[0:W_A, :], preferred_element_type=f32)
           + jnp.dot(yr_ref[...].astype(bf16), wo_ref[W_A:W_A + W_R, :], preferred_element_type=f32))
    h2 = _ln(alpha * h + mix, l2g_ref[...], l2b_ref[...])
    stream.fold_first()
    f = _ffn(h2.astype(bf16), wg_ref, wu_ref, wd_ref, stream.midway)
    out_ref[...] = _ln(alpha * h2 + 0.5 * f, l3g_ref[...], l3b_ref[...])

    @pl.when(stream.part == steps_per_seq - 1)
    def _():
        _satt_finish(qt_ref, knt_ref, vnt_ref, bias0_ref, snew_ref, m_ref, l_ref, acc_ref, ot_ref)


class _page_stream:
    def __init__(self, cfg, pt_ref, ckk_ref, ckv_ref, qt_ref, bias_ref, mask_ref, kbuf, vbuf, sem,
                 m_ref, l_ref, acc_ref):
        self.layer, self.npages, self.grp, self.units, self.steps_per_seq = cfg
        self.pt_ref, self.ckk_ref, self.ckv_ref = pt_ref, ckk_ref, ckv_ref
        self.qt_ref, self.bias_ref, self.mask_ref = qt_ref, bias_ref, mask_ref
        self.kbuf, self.vbuf, self.sem = kbuf, vbuf, sem
        self.state = (m_ref, l_ref, acc_ref)
        self.st = pl.program_id(0)
        self.nsteps = pl.num_programs(0)
        self.part = self.st % self.steps_per_seq
        self.first = range(0, self.units // 2)
        self.second = range(self.units // 2, self.units)

    def _copies(self, step, u, p):
        seq, part = step // self.steps_per_seq, step % self.steps_per_seq
        page = self.pt_ref[seq * self.npages + (part * self.units + u) * self.grp + p]
        return (pltpu.make_async_copy(self.ckk_ref.at[self.layer, page], self.kbuf.at[u, p], self.sem.at[u, 0]),
                pltpu.make_async_copy(self.ckv_ref.at[self.layer, page], self.vbuf.at[u, p], self.sem.at[u, 1]))

    def _start(self, step, us):
        for u in us:
            for p in range(self.grp):
                ck, cv = self._copies(step, u, p)
                ck.start()
                cv.start()

    def _wait(self, us):
        for u in us:
            for p in range(self.grp):
                ck, cv = self._copies(self.st, u, p)
                ck.wait()
                cv.wait()

    def _fold(self, us):
        for u in us:
            _satt_fold(self.kbuf.at[u], self.vbuf.at[u], self.qt_ref, self.bias_ref, self.mask_ref,
                       slice(u * self.grp, (u + 1) * self.grp), *self.state)

    def begin(self):
        @pl.when(self.st == 0)
        def _():
            self._start(0, self.first)
        self._wait(self.first)
        self._start(self.st, self.second)

    def fold_first(self):
        self._fold(self.first)

    def midway(self):
        self._wait(self.second)

        @pl.when(self.st + 1 < self.nsteps)
        def _():
            self._start(self.st + 1, self.first)
        self._fold(self.second)


def _post_stream_call(h, oa, yr, w, alpha, tm, pt_flat, cache_k_t, cache_v_t, qt, knt, vnt, bias_pos, bias0, mask,
                      snew, layer, npages, grp, units, name):
    n = h.shape[0]
    db = qt.shape[0]
    nsteps = n // tm
    pages = units * grp
    steps_per_seq = npages // pages
    assert n % tm == 0 and npages % pages == 0 and nsteps == db * steps_per_seq
    row = lambda width: pl.BlockSpec((tm, width), lambda i, pt: (i, 0))
    per_seq = lambda d0, d1: pl.BlockSpec((None, d0, d1), lambda i, pt: (i // steps_per_seq, 0, 0))
    bias_spec = pl.BlockSpec((H_A, pages, 1, PAGE_SIZE), lambda i, pt: (0, i % steps_per_seq, 0, 0))
    mask_spec = pl.BlockSpec((None, pages, 1, PAGE_SIZE),
                             lambda i, pt: (i // steps_per_seq, i % steps_per_seq, 0, 0))
    grid_spec = pltpu.PrefetchScalarGridSpec(
        num_scalar_prefetch=1,
        grid=(nsteps,),
        in_specs=[
            row(D_MODEL), row(W_A), row(W_R),
            _const_spec((W_A + W_R, D_MODEL)), _const_spec((1, D_MODEL)), _const_spec((1, D_MODEL)),
            _const_spec((D_MODEL, D_FF)), _const_spec((D_MODEL, D_FF)), _const_spec((D_FF, D_MODEL)),
            _const_spec((1, D_MODEL)), _const_spec((1, D_MODEL)),
            pl.BlockSpec(memory_space=pl.ANY), pl.BlockSpec(memory_space=pl.ANY),
            per_seq(DH_A, H_A), per_seq(DH_A, H_A), per_seq(DH_A, H_A),
            bias_spec, pl.BlockSpec((1, H_A), lambda i, pt: (0, 0)), mask_spec, per_seq(1, LANES),
        ],
        out_specs=[row(D_MODEL), per_seq(DH_A, H_A)],
        scratch_shapes=[
            pltpu.VMEM((units, grp, H_A, DH_A, PAGE_SIZE), f32),
            pltpu.VMEM((units, grp, H_A, DH_A, PAGE_SIZE), f32),
            pltpu.SemaphoreType.DMA((units, 2)),
            pltpu.VMEM((H_A, 1, PAGE_SIZE), f32),
            pltpu.VMEM((H_A, 1, PAGE_SIZE), f32),
            pltpu.VMEM((H_A, DH_A, PAGE_SIZE), f32),
        ],
    )
    return pl.pallas_call(
        functools.partial(_post_stream_kernel, alpha, layer, npages, grp, units, steps_per_seq),
        grid_spec=grid_spec,
        out_shape=[jax.ShapeDtypeStruct((n, D_MODEL), f32), jax.ShapeDtypeStruct((db, DH_A, H_A), f32)],
        compiler_params=pltpu.CompilerParams(dimension_semantics=("arbitrary",), vmem_limit_bytes=58 * MIB),
        name=name,
    )(pt_flat, h, oa, yr, w["w_out"], w["ln2_g"], w["ln2_b"], w["ffn2_wg"], w["ffn2_wu"], w["ffn2_wd"],
      w["ln3_g"], w["ln3_b"], cache_k_t, cache_v_t, qt, knt, vnt, bias_pos, bias0, mask, snew)


def _post_call(h, oa, yr, w, alpha, tm, name):
    n = h.shape[0]
    row = lambda width: pl.BlockSpec((tm, width), lambda i: (i, 0))
    in_specs = [
        row(D_MODEL), row(W_A), row(W_R),
        _const_spec((W_A + W_R, D_MODEL)), _const_spec((1, D_MODEL)), _const_spec((1, D_MODEL)),
        _const_spec((D_MODEL, D_FF)), _const_spec((D_MODEL, D_FF)), _const_spec((D_FF, D_MODEL)),
        _const_spec((1, D_MODEL)), _const_spec((1, D_MODEL)),
    ]
    return pl.pallas_call(
        functools.partial(_post_kernel, alpha),
        grid=(pl.cdiv(n, tm),), in_specs=in_specs, out_specs=row(D_MODEL),
        out_shape=jax.ShapeDtypeStruct((n, D_MODEL), f32),
        compiler_params=pltpu.CompilerParams(dimension_semantics=("arbitrary",), vmem_limit_bytes=48 * MIB),
        name=name,
    )(h, oa, yr, w["w_out"], w["ln2_g"], w["ln2_b"], w["ffn2_wg"], w["ffn2_wu"], w["ffn2_wd"],
      w["ln3_g"], w["ln3_b"])


def _attn_kernel(topk, qa_ref, qi_ref, wi_ref, k_ref, v_ref, ki2_ref, bias_ref, oa_ref,
                    sc_ref, qap_ref, qip_ref, m_ref, l_ref, acc_ref):
    tq = qa_ref.shape[0]
    ngrp = tq // 8
    i = pl.program_id(1)
    nj = i + 1
    kf = float(topk)

    lane = lax.broadcasted_iota(jnp.int32, (tq, LANES), 1)
    lo_half = lane < DH_A
    for h in range(H_A):
        p, par = h // 2, h % 2
        keep = lo_half if par == 0 else jnp.logical_not(lo_half)
        blk = slice(LANES * p, LANES * (p + 1))
        qip_ref[h] = jnp.where(keep, qi_ref[:, blk], jnp.zeros((), qi_ref.dtype))
        qap_ref[p, par * tq:(par + 1) * tq, :] = jnp.where(keep, qa_ref[:, blk], jnp.zeros((), qa_ref.dtype))
    w_heads = wi_ref[...].T[0:H_IDX, :] * H_IDX ** -0.5

    def score_chunk(j, carry):
        off = pl.multiple_of(j * tq, tq)
        kj = ki2_ref[pl.ds(off, tq), :]
        acc = jnp.zeros((tq, tq), f32)
        for h in range(H_IDX):
            d = lax.dot_general(kj, qip_ref[h], NT_DIMS, preferred_element_type=f32)
            acc = acc + w_heads[h:h + 1, :] * jnp.maximum(d, 0.0)
        sc_ref[j] = acc
        return carry

    lax.fori_loop(0, nj, score_chunk, 0)

    krow = lax.broadcasted_iota(jnp.int32, (tq, tq), 0)
    qcol = lax.broadcasted_iota(jnp.int32, (tq, tq), 1)
    causal = krow <= qcol
    sd = sc_ref[i]
    mn_diag = jnp.min(jnp.where(causal, sd, jnp.inf), axis=0, keepdims=True)
    sc_ref[i] = jnp.where(causal, sd, -jnp.inf)

    def fold(fn, init, comb, n_chunks, t=None):
        t8 = None if t is None else jnp.broadcast_to(t, (8, tq))

        nacc = 4

        def body(j, accs):
            accs = list(accs)
            for r in range(ngrp):
                accs[r % nacc] = comb(accs[r % nacc], fn(sc_ref[j, 8 * r:8 * r + 8, :], t8))
            return tuple(accs)
        accs = lax.fori_loop(0, n_chunks, body, tuple(jnp.full((8, tq), init, f32) for _ in range(nacc)))
        acc = comb(comb(accs[0], accs[1]), comb(accs[2], accs[3]))
        if comb is jnp.add:
            return jnp.sum(acc, axis=0, keepdims=True)
        if comb is jnp.maximum:
            return jnp.max(acc, axis=0, keepdims=True)
        return jnp.min(acc, axis=0, keepdims=True)

    def count_ge(t):
        return fold(lambda s, t8: jnp.where(s >= t8, 1.0, 0.0), 0.0, jnp.add, nj, t)

    def count_gt(t):
        return fold(lambda s, t8: jnp.where(s > t8, 1.0, 0.0), 0.0, jnp.add, nj, t)

    def max_below(t):
        return fold(lambda s, t8: jnp.where(s < t8, s, -jnp.inf), -jnp.inf, jnp.maximum, nj, t)

    n_keys = (lax.broadcasted_iota(jnp.int32, (1, tq), 1) + (i * tq + 1)).astype(f32)
    take_all = n_keys <= kf

    @pl.when((i + 1) * tq <= topk)
    def _():
        def mk(j, carry):
            sc_ref[j] = jnp.zeros((tq, tq), f32)
            return carry
        lax.fori_loop(0, nj, mk, 0)

    @pl.when((i + 1) * tq > topk)
    def _():
        def minmax_body(j, c):
            mxs, mns = list(c[0]), list(c[1])
            for r in range(ngrp):
                v = sc_ref[j, 8 * r:8 * r + 8, :]
                mxs[r % 2] = jnp.maximum(mxs[r % 2], v)
                mns[r % 2] = jnp.minimum(mns[r % 2], v)
            return tuple(mxs), tuple(mns)
        full8 = lambda val: jnp.full((8, tq), val, f32)
        mxs, mns = lax.fori_loop(0, i, minmax_body, ((full8(-jnp.inf),) * 2, (full8(jnp.inf),) * 2))
        mx = jnp.maximum(jnp.max(jnp.maximum(mxs[0], mxs[1]), axis=0, keepdims=True),
                         jnp.max(sc_ref[i], axis=0, keepdims=True))
        mn = jnp.minimum(jnp.min(jnp.minimum(mns[0], mns[1]), axis=0, keepdims=True), mn_diag)
        c_max = count_ge(mx)
        done0 = jnp.logical_or(take_all, c_max >= kf)
        thr0 = jnp.where(take_all, -jnp.inf, mx)
        cge0 = jnp.where(take_all, n_keys, c_max)

        def bis(_, st):
            lo, hi = st
            mid = 0.5 * (lo + hi)
            ge = count_ge(mid) >= kf
            return jnp.where(ge, mid, lo), jnp.where(ge, hi, mid)

        lo, hi = lax.fori_loop(0, BISECT_ITERS, bis, (mn, mx))

        def snap_cond(st):
            return st[0] > 0.0

        def snap_body(st):
            _, hi, thr, cge, done = st
            m = max_below(hi)
            c = count_ge(m)
            ok = c >= kf
            newly = jnp.logical_and(ok, done < 0.5)
            thr = jnp.where(newly, m, thr)
            cge = jnp.where(newly, c, cge)
            hi = jnp.where(jnp.logical_or(done > 0.5, ok), hi, m)
            done = jnp.where(ok, 1.0, done)
            return jnp.sum(1.0 - done), hi, thr, cge, done

        done_f = jnp.where(done0, 1.0, 0.0)
        _, _, thr, cge, _ = lax.while_loop(snap_cond, snap_body, (jnp.sum(1.0 - done_f), hi, thr0, cge0, done_f))

        excess = jnp.sum(jnp.where(cge > kf, 1.0, 0.0)) > 0.0

        @pl.when(jnp.logical_not(excess))
        def _():
            def mk(j, carry):
                sc_ref[j] = jnp.where(sc_ref[j] >= thr, 0.0, NEG)
                return carry
            lax.fori_loop(0, nj, mk, 0)

        @pl.when(excess)
        def _():
            need = kf - count_gt(thr)
            lower = jnp.where(qcol <= krow, 1.0, 0.0).astype(bf16)

            def mk(j, seen):
                s = sc_ref[j]
                tie = s == thr
                rank = seen + jnp.dot(lower, jnp.where(tie, 1.0, 0.0).astype(bf16), preferred_element_type=f32)
                sel = jnp.logical_or(s > thr, jnp.logical_and(tie, rank <= need))
                sc_ref[j] = jnp.where(sel, 0.0, NEG)
                return rank[tq - 1:tq, :]
            lax.fori_loop(0, nj, mk, jnp.zeros((1, tq), f32))

    sc_ref[i] = jnp.where(causal, sc_ref[i], NEG)

    m_ref[...] = jnp.full(m_ref.shape, -jnp.inf, f32)
    l_ref[...] = jnp.zeros(l_ref.shape, f32)
    acc_ref[...] = jnp.zeros(acc_ref.shape, f32)
    ncb = tq // LANES

    def lane_blocks(x, comb):
        out = x[:, 0:LANES]
        for cb in range(1, x.shape[1] // LANES):
            out = comb(out, x[:, cb * LANES:(cb + 1) * LANES])
        return out

    def attend(j, width, bias_cols):
        off = pl.multiple_of(j * tq, tq)
        keys = pl.ds(off, width * tq)
        sel_mask = jnp.concatenate([sc_ref[j + c].T for c in range(width)], axis=1)
        for p in range(H_A // 2):
            blk = slice(LANES * p, LANES * (p + 1))
            s_pair = lax.dot_general(qap_ref[p], k_ref[keys, blk], NT_DIMS, preferred_element_type=f32)
            es = []
            for par in range(2):
                h = 2 * p + par
                s = s_pair[par * tq:(par + 1) * tq] + sel_mask
                if bias_cols is not None:
                    s = s + bias_ref[h, :, bias_cols]
                m_old = m_ref[h]
                row_max = jnp.max(lane_blocks(s, jnp.maximum), axis=1, keepdims=True)
                m_new = jnp.maximum(m_old, jnp.broadcast_to(row_max, (tq, LANES)))
                a = jnp.exp2(m_old - m_new)
                e = jnp.exp2(s - jnp.concatenate([m_new] * (width * ncb), axis=1))
                l_ref[h] = l_ref[h] * a + lane_blocks(e, jnp.add)
                acc_ref[h] = acc_ref[h] * a
                m_ref[h] = m_new
                es.append(e.astype(bf16))
            pv = jnp.dot(jnp.concatenate(es, axis=0), v_ref[keys, blk], preferred_element_type=f32)
            acc_ref[2 * p] += pv[0:tq]
            acc_ref[2 * p + 1] += pv[tq:2 * tq]

    n_far = jnp.maximum(i - 1, 0)
    if ATT_FAR_WIDTH == 2:
        def far_pair(jj, carry):
            attend(2 * jj, 2, None)
            return carry
        lax.fori_loop(0, n_far // 2, far_pair, 0)

        @pl.when(n_far % 2 == 1)
        def _():
            attend(n_far - 1, 1, None)
    else:
        def far_one(j, carry):
            attend(j, 1, None)
            return carry
        lax.fori_loop(0, n_far, far_one, 0)

    @pl.when(i >= 1)
    def _():
        attend(i - 1, 2, slice(0, 2 * tq))

    @pl.when(i == 0)
    def _():
        attend(0, 1, slice(tq, 2 * tq))

    for p in range(H_A // 2):
        l_even = jnp.broadcast_to(jnp.sum(l_ref[2 * p], axis=1, keepdims=True), (tq, LANES))
        l_odd = jnp.broadcast_to(jnp.sum(l_ref[2 * p + 1], axis=1, keepdims=True), (tq, LANES))
        o_pair = jnp.where(lo_half, acc_ref[2 * p] / l_even, acc_ref[2 * p + 1] / l_odd)
        oa_ref[:, LANES * p:LANES * (p + 1)] = o_pair.astype(oa_ref.dtype)


def _attn_call(qa, qi, wi, kab, vab, ki2, bias_tiles, topk, tq):
    b, s, _ = qa.shape
    nq = s // tq
    qspec = lambda width: pl.BlockSpec((None, tq, width), lambda bi, i: (bi, i, 0))
    whole = lambda shape: pl.BlockSpec((None,) + shape, lambda bi, i: (bi,) + (0,) * len(shape),
                                       pipeline_mode=pl.Buffered(1))
    scratch = [
        pltpu.VMEM((nq, tq, tq), f32),
        pltpu.VMEM((H_A // 2, 2 * tq, LANES), qa.dtype),
        pltpu.VMEM((H_IDX, tq, LANES), qi.dtype),
        pltpu.VMEM((H_A, tq, LANES), f32),
        pltpu.VMEM((H_A, tq, LANES), f32),
        pltpu.VMEM((H_A, tq, LANES), f32),
    ]
    return pl.pallas_call(
        functools.partial(_attn_kernel, topk),
        grid=(b, nq),
        in_specs=[qspec(W_A), qspec(W_A),
                  qspec(LANES),
                  whole((s, W_A)), whole((s, W_A)), whole((s, LANES)),
                  _const_spec((H_A, tq, 2 * tq))],
        out_specs=qspec(W_A),
        out_shape=jax.ShapeDtypeStruct((b, s, W_A), bf16),
        scratch_shapes=scratch,
        compiler_params=pltpu.CompilerParams(dimension_semantics=("arbitrary", "arbitrary"),
                                             vmem_limit_bytes=48 * MIB),
        name="attn_prompt",
    )(qa, qi, wi, kab, vab, ki2, bias_tiles)


def _ret_kernel(qr_ref, kr_ref, vr_ref, gr_ref, gng_ref, dmat_ref, cross_ref, kvd_ref, cdec_ref,
                yr_ref, st_ref):
    c = pl.program_id(1)

    @pl.when(c == 0)
    def _():
        st_ref[...] = jnp.zeros(st_ref.shape, f32)

    for h in range(H_R):
        sl = slice(DK_R * h, DK_R * (h + 1))
        q = qr_ref[:, sl]
        k = kr_ref[:, sl]
        v = vr_ref[:, sl]
        att = lax.dot_general(q, k, NT_DIMS, preferred_element_type=f32) * dmat_ref[h]
        st = st_ref[h]
        o = (jnp.dot(att.astype(bf16), v, preferred_element_type=f32)
             + jnp.dot(q, st.astype(bf16), preferred_element_type=f32) * cross_ref[h])
        kd = (k.astype(f32) * kvd_ref[h]).T.astype(bf16)
        st_ref[h] = cdec_ref[h, 0:1, :] * st + jnp.dot(kd, v, preferred_element_type=f32)
        mu = jnp.mean(o, axis=-1, keepdims=True)
        oc = o - mu
        var = jnp.mean(oc * oc, axis=-1, keepdims=True)
        yn = oc * lax.rsqrt(var + GN_EPS) * gng_ref[:, sl]
        yr_ref[:, sl] = (jax.nn.silu(gr_ref[:, sl]) * yn).astype(yr_ref.dtype)


def _ret_call(qr, kr, vr, gr, gng, dec, chunk):
    b, s, _ = qr.shape
    nc = s // chunk
    rspec = pl.BlockSpec((None, chunk, W_R), lambda bi, c: (bi, c, 0))
    return pl.pallas_call(
        _ret_kernel,
        grid=(b, nc),
        in_specs=[rspec, rspec, rspec, rspec, _const_spec((1, W_R)),
                  _const_spec((H_R, chunk, chunk)), _const_spec((H_R, chunk, LANES)),
                  _const_spec((H_R, chunk, LANES)), _const_spec((H_R, 8, LANES))],
        out_specs=[rspec, pl.BlockSpec((None, H_R, DK_R, DV_R), lambda bi, c: (bi, 0, 0, 0))],
        out_shape=[jax.ShapeDtypeStruct((b, s, W_R), bf16), jax.ShapeDtypeStruct((b, H_R, DK_R, DV_R), f32)],
        compiler_params=pltpu.CompilerParams(dimension_semantics=("arbitrary", "arbitrary"),
                                             vmem_limit_bytes=32 * MIB),
        name="ret_prompt",
    )(qr, kr, vr, gr, gng, dec["intra"], dec["cross"], dec["kv"], dec["chunk"])


def _sscore_kernel(layer, npages, pt_ref, ck_ref, q_ref, w_ref, kn_ref,
                   qr_ref, kr_ref, vr_ref, gr_ref, gng_ref, gam_ref, s0_ref,
                   sc_ref, snew_ref, yr_ref, sn_ref, kbuf, sem):
    b = pl.program_id(0)
    nb = pl.num_programs(0)
    slot = b % 2

    def page_copy(seq, p, sl):
        page = pt_ref[seq * npages + p]
        return pltpu.make_async_copy(ck_ref.at[layer, page], kbuf.at[sl, p], sem.at[sl])

    def start_all(seq, sl):
        def body(p, carry):
            page_copy(seq, p, sl).start()
            return carry
        lax.fori_loop(0, npages, body, 0)

    @pl.when(b == 0)
    def _():
        start_all(0, 0)

    @pl.when(b + 1 < nb)
    def _():
        start_all(b + 1, 1 - slot)

    _sample_retention_step(qr_ref, kr_ref, vr_ref, gr_ref, gng_ref, gam_ref, s0_ref, yr_ref, sn_ref)

    def wait_body(p, carry):
        page_copy(b, p, slot).wait()
        return carry
    lax.fori_loop(0, npages, wait_body, 0)

    q = q_ref[...].astype(bf16)
    w = w_ref[...]

    pages_per_dot = min(8, npages)
    for p0 in range(0, npages, pages_per_dot):
        kw = jnp.concatenate([kbuf[slot, p] for p in range(p0, p0 + pages_per_dot)], axis=1).astype(bf16)
        d = jnp.dot(q, kw, preferred_element_type=f32)
        sc_ref[:, p0 * PAGE_SIZE:(p0 + pages_per_dot) * PAGE_SIZE] = jnp.sum(
            w * jnp.maximum(d, 0.0), axis=0, keepdims=True)

    kn = kn_ref[...].astype(bf16).astype(f32)
    dn = jnp.sum(q.astype(f32) * kn, axis=1, keepdims=True)
    s_new = jnp.sum(w * jnp.maximum(dn, 0.0), axis=0, keepdims=True)
    snew_ref[...] = jnp.broadcast_to(s_new, (1, LANES))


def _sscore_call(page_table_flat, cache_kidx_t, q16, w16, kn, qr, kr, vr, gr, gng, gam, state, layer, npages):
    db = q16.shape[0]
    assert npages % min(8, npages) == 0
    per_seq = lambda d0, d1: pl.BlockSpec((None, d0, d1), lambda b, pt: (b, 0, 0))
    shared = pl.BlockSpec((H_R, DV_R), lambda b, pt: (0, 0))
    grid_spec = pltpu.PrefetchScalarGridSpec(
        num_scalar_prefetch=1,
        grid=(db,),
        in_specs=[
            pl.BlockSpec(memory_space=pl.ANY),
            per_seq(16, DH_IDX), per_seq(16, 1), per_seq(1, DH_IDX),
            per_seq(H_R, DK_R), per_seq(H_R, DK_R), per_seq(H_R, DV_R), per_seq(H_R, DV_R), shared, shared,
            pl.BlockSpec((None, None, H_R, DK_R, DV_R), lambda b, pt: (layer, b, 0, 0, 0)),
        ],
        out_specs=[per_seq(1, npages * PAGE_SIZE), per_seq(1, LANES), per_seq(H_R, DV_R),
                   pl.BlockSpec((None, H_R, DK_R, DV_R), lambda b, pt: (b, 0, 0, 0))],
        scratch_shapes=[
            pltpu.VMEM((2, npages, DH_IDX, PAGE_SIZE), f32),
            pltpu.SemaphoreType.DMA((2,)),
        ],
    )
    return pl.pallas_call(
        functools.partial(_sscore_kernel, layer, npages),
        grid_spec=grid_spec,
        out_shape=[jax.ShapeDtypeStruct((db, 1, npages * PAGE_SIZE), f32),
                   jax.ShapeDtypeStruct((db, 1, LANES), f32),
                   jax.ShapeDtypeStruct((db, H_R, DV_R), f32), jax.ShapeDtypeStruct((db, H_R, DK_R, DV_R), f32)],
        compiler_params=pltpu.CompilerParams(dimension_semantics=("arbitrary",), vmem_limit_bytes=32 * MIB),
        name="sample_scores",
    )(page_table_flat, cache_kidx_t, q16, w16, kn, qr, kr, vr, gr, gng, gam, state)


PREFIX_CHUNK = 256


def _ssel_kernel(topk, sc_ref, sn_ref, mask_ref, selnew_ref):
    kf = float(topk)
    db, length = sc_ref.shape
    sc = sc_ref[...]
    s_new = sn_ref[:, 0:1]

    def rsum(x):
        return jnp.sum(x, axis=1, keepdims=True)

    def count(cmp, t):
        return rsum(jnp.where(cmp(sc, t), 1.0, 0.0)) + jnp.where(cmp(s_new, t), 1.0, 0.0)

    ge = lambda a, t: a >= t
    gt = lambda a, t: a > t
    mx = jnp.maximum(jnp.max(sc, axis=1, keepdims=True), s_new)
    mn = jnp.minimum(jnp.min(sc, axis=1, keepdims=True), s_new)
    c_max = count(ge, mx)
    done0 = jnp.where(c_max >= kf, 1.0, 0.0)

    def bis(_, st):
        lo, hi = st
        mid = 0.5 * (lo + hi)
        ok = count(ge, mid) >= kf
        return jnp.where(ok, mid, lo), jnp.where(ok, hi, mid)
    lo, hi = lax.fori_loop(0, BISECT_ITERS, bis, (mn, mx))

    def snap_cond(st):
        return st[0] > 0.0

    def snap_body(st):
        _, hi, thr, done = st
        below = jnp.maximum(jnp.max(jnp.where(sc < hi, sc, -jnp.inf), axis=1, keepdims=True),
                            jnp.where(s_new < hi, s_new, -jnp.inf))
        ok = count(ge, below) >= kf
        newly = jnp.logical_and(ok, done < 0.5)
        thr = jnp.where(newly, below, thr)
        hi = jnp.where(jnp.logical_or(done > 0.5, ok), hi, below)
        done = jnp.where(ok, 1.0, done)
        return jnp.sum(1.0 - done), hi, thr, done
    _, _, thr, _ = lax.while_loop(snap_cond, snap_body, (jnp.sum(1.0 - done0), hi, mx, done0))

    need = kf - count(gt, thr)
    pc = min(PREFIX_CHUNK, length)
    r_i = lax.broadcasted_iota(jnp.int32, (pc, pc), 0)
    c_i = lax.broadcasted_iota(jnp.int32, (pc, pc), 1)
    upper = jnp.where(r_i <= c_i, 1.0, 0.0).astype(bf16)
    seen = jnp.zeros((db, 1), f32)
    for c0 in range(0, length, pc):
        s_c = sc[:, c0:c0 + pc]
        tie = s_c == thr
        rank = seen + jnp.dot(jnp.where(tie, 1.0, 0.0).astype(bf16), upper, preferred_element_type=f32)
        sel = jnp.logical_or(s_c > thr, jnp.logical_and(tie, rank <= need))
        mask_ref[:, c0:c0 + pc] = jnp.where(sel, 0.0, NEG)
        seen = rank[:, pc - 1:pc]
    sel_new = jnp.logical_or(s_new > thr, jnp.logical_and(s_new == thr, seen + 1.0 <= need))
    selnew_ref[...] = jnp.broadcast_to(jnp.where(sel_new, 1.0, 0.0), (db, LANES))


def _ssel_call(scores, s_new, topk):
    db, length = scores.shape
    assert length % min(PREFIX_CHUNK, length) == 0
    return pl.pallas_call(
        functools.partial(_ssel_kernel, topk),
        out_shape=[jax.ShapeDtypeStruct((db, length), f32), jax.ShapeDtypeStruct((db, LANES), f32)],
        compiler_params=pltpu.CompilerParams(vmem_limit_bytes=32 * MIB),
        name="sample_select",
    )(scores, s_new)


def _satt_kernel(layer, npages, grp, pt_ref,
                 ckk_ref, ckv_ref, qt_ref, knt_ref, vnt_ref, bias_ref, bias0_ref, mask_ref, snew_ref,
                 ot_ref, kbuf, vbuf, sem, m_ref, l_ref, acc_ref):
    b = pl.program_id(0)
    g = pl.program_id(1)
    nb = pl.num_programs(0)
    ng = pl.num_programs(1)
    t = b * ng + g
    slot = t % 2

    def page_copies(seq, gi, p, sl):
        page = pt_ref[seq * npages + gi * grp + p]
        return (pltpu.make_async_copy(ckk_ref.at[layer, page], kbuf.at[sl, p], sem.at[sl, 0]),
                pltpu.make_async_copy(ckv_ref.at[layer, page], vbuf.at[sl, p], sem.at[sl, 1]))

    def start_all(seq, gi, sl):
        def body(p, carry):
            ck, cv = page_copies(seq, gi, p, sl)
            ck.start()
            cv.start()
            return carry
        lax.fori_loop(0, grp, body, 0)

    @pl.when(t == 0)
    def _():
        start_all(0, 0, 0)

    @pl.when(t + 1 < nb * ng)
    def _():
        wrap = g + 1 == ng
        start_all(jnp.where(wrap, b + 1, b), jnp.where(wrap, 0, g + 1), 1 - slot)

    @pl.when(g == 0)
    def _():
        m_ref[...] = jnp.full(m_ref.shape, -jnp.inf, f32)
        l_ref[...] = jnp.zeros(l_ref.shape, f32)
        acc_ref[...] = jnp.zeros(acc_ref.shape, f32)

    def wait_body(p, carry):
        ck, cv = page_copies(b, g, p, slot)
        ck.wait()
        cv.wait()
        return carry
    lax.fori_loop(0, grp, wait_body, 0)

    _satt_fold(kbuf.at[slot], vbuf.at[slot], qt_ref, bias_ref, mask_ref, slice(0, grp), m_ref, l_ref, acc_ref)

    @pl.when(g == ng - 1)
    def _():
        _satt_finish(qt_ref, knt_ref, vnt_ref, bias0_ref, snew_ref, m_ref, l_ref, acc_ref, ot_ref)


def _satt_fold(kb_ref, vb_ref, qt_ref, bias_ref, mask_ref, rows, m_ref, l_ref, acc_ref):
    mask = mask_ref[rows]
    for h in range(H_A):
        qc = qt_ref[:, h:h + 1]
        kh = kb_ref[:, h]
        lg = jnp.sum(kh * qc[None], axis=1, keepdims=True) + bias_ref[h, rows] + mask
        m_old = m_ref[h]
        m_blk = jnp.max(jnp.max(lg, axis=0), axis=1, keepdims=True)
        m_new = jnp.maximum(m_old, m_blk)
        a = jnp.exp(m_old - m_new)
        e = jnp.exp(lg - m_new[None])
        l_ref[h] = l_ref[h] * a + jnp.sum(e, axis=0)
        acc_ref[h] = acc_ref[h] * a + jnp.sum(vb_ref[:, h] * e, axis=0)
        m_ref[h] = m_new


def _satt_finish(qt_ref, knt_ref, vnt_ref, bias0_ref, snew_ref, m_ref, l_ref, acc_ref, ot_ref):
    lg_new = jnp.sum(qt_ref[...] * knt_ref[...], axis=0, keepdims=True) + bias0_ref[...]
    lg_new = jnp.where(snew_ref[0:1, 0:1] > 0.5, lg_new, NEG)
    for h in range(H_A):
        m_c = m_ref[h][:, 0:1]
        lg_h = lg_new[:, h:h + 1]
        m_f = jnp.maximum(m_c, lg_h)
        a = jnp.exp(m_c - m_f)
        e_new = jnp.exp(lg_h - m_f)
        den = jnp.sum(l_ref[h], axis=1, keepdims=True) * a + e_new
        num = jnp.sum(acc_ref[h], axis=1, keepdims=True) * a + e_new * vnt_ref[:, h:h + 1]
        ot_ref[:, h:h + 1] = num / den


def _satt_reset(m_ref, l_ref, acc_ref):
    m_ref[...] = jnp.full(m_ref.shape, -jnp.inf, f32)
    l_ref[...] = jnp.zeros(l_ref.shape, f32)
    acc_ref[...] = jnp.zeros(acc_ref.shape, f32)


def _sample_retention_step(qr_ref, kr_ref, vr_ref, gr_ref, gng_ref, gam_ref, s0_ref, yr_ref, sn_ref):
    r_i = lax.broadcasted_iota(jnp.int32, (DK_R, DK_R), 0)
    c_i = lax.broadcasted_iota(jnp.int32, (DK_R, DK_R), 1)
    eye = jnp.where(r_i == c_i, 1.0, 0.0)
    for h in range(H_R):
        qrow = qr_ref[h:h + 1, :]
        krow = kr_ref[h:h + 1, :]
        vrow = vr_ref[h:h + 1, :]
        gam = gam_ref[h:h + 1, :]
        qcol = jnp.sum(eye * qrow, axis=1, keepdims=True)
        kcol = jnp.sum(eye * krow, axis=1, keepdims=True)
        st = s0_ref[h]
        qk = jnp.sum(qrow * krow, axis=1, keepdims=True)
        o = qk * vrow + gam * jnp.sum(qcol * st, axis=0, keepdims=True)
        sn_ref[h] = gam * st + kcol * vrow
        mu = jnp.mean(o, axis=1, keepdims=True)
        oc = o - mu
        var = jnp.mean(oc * oc, axis=1, keepdims=True)
        yn = oc * lax.rsqrt(var + GN_EPS) * gng_ref[h:h + 1, :]
        yr_ref[h:h + 1, :] = jax.nn.silu(gr_ref[h:h + 1, :]) * yn


def _satt_call(pt_flat, cache_k_t, cache_v_t, qt, knt, vnt, bias_pos, bias0, mask, snew, layer, npages, grp):
    db = qt.shape[0]
    ng = npages // grp
    per_seq = lambda d0, d1: pl.BlockSpec((None, d0, d1), lambda b, g, pt: (b, 0, 0))
    grid_spec = pltpu.PrefetchScalarGridSpec(
        num_scalar_prefetch=1,
        grid=(db, ng),
        in_specs=[
            pl.BlockSpec(memory_space=pl.ANY), pl.BlockSpec(memory_space=pl.ANY),
            per_seq(DH_A, H_A), per_seq(DH_A, H_A), per_seq(DH_A, H_A),
            pl.BlockSpec((H_A, grp, 1, PAGE_SIZE), lambda b, g, pt: (0, g, 0, 0)),
            pl.BlockSpec((1, H_A), lambda b, g, pt: (0, 0)),
            pl.BlockSpec((None, grp, 1, PAGE_SIZE), lambda b, g, pt: (b, g, 0, 0)),
            per_seq(1, LANES),
        ],
        out_specs=per_seq(DH_A, H_A),
        scratch_shapes=[
            pltpu.VMEM((2, grp, H_A, DH_A, PAGE_SIZE), f32),
            pltpu.VMEM((2, grp, H_A, DH_A, PAGE_SIZE), f32),
            pltpu.SemaphoreType.DMA((2, 2)),
            pltpu.VMEM((H_A, 1, PAGE_SIZE), f32),
            pltpu.VMEM((H_A, 1, PAGE_SIZE), f32),
            pltpu.VMEM((H_A, DH_A, PAGE_SIZE), f32),
        ],
    )
    return pl.pallas_call(
        functools.partial(_satt_kernel, layer, npages, grp),
        grid_spec=grid_spec,
        out_shape=jax.ShapeDtypeStruct((db, DH_A, H_A), f32),
        compiler_params=pltpu.CompilerParams(dimension_semantics=("arbitrary", "arbitrary"),
                                             vmem_limit_bytes=40 * MIB),
        name="sample_attend",
    )(pt_flat, cache_k_t, cache_v_t, qt, knt, vnt, bias_pos, bias0, mask, snew)


def _t5_bucket(rel):
    n = jnp.maximum(rel, 0)
    max_exact = NUM_BUCKETS // 2
    nf = jnp.maximum(n, 1).astype(f32)
    large = max_exact + (jnp.log(nf / max_exact) / math.log(MAX_DISTANCE / max_exact)
                         * (NUM_BUCKETS - max_exact)).astype(jnp.int32)
    large = jnp.minimum(large, NUM_BUCKETS - 1)
    return jnp.where(n < max_exact, n, large)


def _rotary_tables(pos):
    half = DK_R // 2
    freqs = ROPE_BASE ** (-jnp.arange(half, dtype=f32) / half)
    ang = pos.astype(f32)[:, None] * freqs[None, :]
    cos, sin = jnp.cos(ang), jnp.sin(ang)
    return jnp.concatenate([cos, cos], axis=1), jnp.concatenate([-sin, sin], axis=1)


def _decay_tables(chunk):
    lg = jnp.log1p(-jnp.exp2(-5.0 - jnp.arange(H_R, dtype=f32)))
    i = jnp.arange(chunk, dtype=f32)
    diff = i[:, None] - i[None, :]
    causal = diff >= 0
    intra = jnp.where(causal[None], jnp.exp(jnp.where(causal, diff, 0.0)[None] * lg[:, None, None]), 0.0)
    cross = jnp.exp((i[None, :] + 1.0) * lg[:, None])
    kv = jnp.exp((chunk - 1.0 - i)[None, :] * lg[:, None])
    cdec = jnp.exp(chunk * lg)
    bc = lambda a: jnp.broadcast_to(a[:, :, None], (H_R, chunk, LANES))
    return {"intra": intra, "cross": bc(cross), "kv": bc(kv),
            "chunk": jnp.broadcast_to(cdec[:, None, None], (H_R, 8, LANES))}


def _bias_tiles(rel_bias, tq):
    assert _last_bucket_from(tq + 1), "keys beyond the previous chunk must share the last bucket"
    rb = rel_bias.astype(f32) - rel_bias[NUM_BUCKETS - 1].astype(f32)[None, :]
    tab = rb[_t5_bucket(jnp.arange(2 * tq + 1, dtype=jnp.int32))].T
    m = 3 * tq
    g = jnp.concatenate([tab[:, tq::-1],
                         jnp.broadcast_to(tab[:, 0:1], (H_A, tq - 1)),
                         tab[:, 2 * tq:tq:-1]], axis=1)
    flat = jnp.tile(g, (1, tq))[:, :tq * (m - 1)]
    return flat.reshape(H_A, tq, m - 1)[:, :, :2 * tq]


def _last_bucket_from(n):
    large = NUM_BUCKETS // 2 + int(math.log(n / (NUM_BUCKETS // 2)) / math.log(MAX_DISTANCE / (NUM_BUCKETS // 2))
                                   * (NUM_BUCKETS - NUM_BUCKETS // 2) - 1e-3)
    return n >= NUM_BUCKETS // 2 and large >= NUM_BUCKETS - 1


def _layer_weights(l, ffn1_wg, ffn1_wu, ffn1_wd, ln1_g, ln1_b, w_in, ret_gn_g, w_out, ln2_g, ln2_b,
                   ffn2_wg, ffn2_wu, ffn2_wd, ln3_g, ln3_b):
    pts = np.cumsum((0,) + IN_SIZES)
    wi = w_in[l]
    col = lambda k: wi[:, pts[k]:pts[k + 1]]
    zeros = jnp.zeros((D_MODEL, LANES - H_IDX), wi.dtype)
    w_in2 = jnp.concatenate([col(0), col(1), col(2), col(3), col(4), col(4), col(5), zeros,
                             col(6), col(7), col(8), col(9)], axis=1).astype(bf16)
    r2 = lambda a: a[l].reshape(1, -1).astype(f32)
    return {
        "ffn1_wg": ffn1_wg[l].astype(bf16), "ffn1_wu": ffn1_wu[l].astype(bf16), "ffn1_wd": ffn1_wd[l].astype(bf16),
        "ln1_g": r2(ln1_g), "ln1_b": r2(ln1_b), "w_in": w_in2, "gng": r2(ret_gn_g),
        "w_out": w_out[l].astype(bf16), "ln2_g": r2(ln2_g), "ln2_b": r2(ln2_b),
        "ffn2_wg": ffn2_wg[l].astype(bf16), "ffn2_wu": ffn2_wu[l].astype(bf16), "ffn2_wd": ffn2_wd[l].astype(bf16),
        "ln3_g": r2(ln3_g), "ln3_b": r2(ln3_b),
    }


def kernel(x_prompt, x_sample, cache_k, cache_v, cache_kidx, state_ret, page_table, rel_bias,
           ffn1_wg, ffn1_wu, ffn1_wd, ln1_g, ln1_b, w_in, ret_gn_g, w_out,
           ln2_g, ln2_b, ffn2_wg, ffn2_wu, ffn2_wd, ln3_g, ln3_b):
    b, s, _ = x_prompt.shape
    db, ds, _ = x_sample.shape
    depth = w_in.shape[0]
    npages = page_table.shape[1]
    past = npages * PAGE_SIZE
    assert ds == 1, "the sample group decodes one token per sequence"
    alpha = (2 * depth) ** 0.25

    tq = min(ATT_TQ, s)
    chunk = min(RET_CHUNK, s)
    assert s % tq == 0 and s % chunk == 0 and tq % LANES == 0
    topk_p = min(TOPK_MAX, s // 4)
    topk_s = min(TOPK_MAX, (past + ds) // 4)

    rot_p = _rotary_tables(jnp.arange(s, dtype=jnp.int32))
    rot_s = _rotary_tables(jnp.full((db * ds,), past, jnp.int32))
    dec_p = _decay_tables(chunk)
    gam = jnp.broadcast_to(_decay_tables(1)["chunk"][:, 0, :], (H_R, LANES))
    bias_tiles = _bias_tiles(rel_bias, tq)
    near = min(past, 2 * PAGE_SIZE)
    assert _last_bucket_from(near + 1)
    bias_near = rel_bias.astype(f32)[_t5_bucket(jnp.arange(near, 0, -1, dtype=jnp.int32))].T
    bias_far = jnp.broadcast_to(rel_bias[NUM_BUCKETS - 1].astype(f32)[:, None], (H_A, past - near))
    bias_pos = jnp.concatenate([bias_far, bias_near], axis=1).reshape(H_A, npages, 1, PAGE_SIZE)
    bias0 = rel_bias[0:1].astype(f32)
    pt_flat = page_table.reshape(-1).astype(jnp.int32)
    grp = min(SAMPLE_PAGE_GROUP, npages)
    assert npages % grp == 0
    ckidx_t = jnp.transpose(cache_kidx, (0, 1, 3, 2))
    ck_t = jnp.transpose(cache_k, (0, 1, 3, 4, 2))
    cv_t = jnp.transpose(cache_v, (0, 1, 3, 4, 2))

    hp = x_prompt.reshape(b * s, D_MODEL)
    hs = x_sample.reshape(db * ds, D_MODEL)
    outs = {k: [] for k in ("kp", "vp", "kip", "sp", "ks", "vs", "kis", "ss")}
    for l in range(depth):
        w = _layer_weights(l, ffn1_wg, ffn1_wu, ffn1_wd, ln1_g, ln1_b, w_in, ret_gn_g, w_out, ln2_g, ln2_b,
                           ffn2_wg, ffn2_wu, ffn2_wd, ln3_g, ln3_b)
        ps = _pre_call(hs, w, rot_s[0], rot_s[1], alpha, DH_A ** -0.5, db * ds, f32, None, "pre_sample")
        q16 = jnp.pad(ps["qi"].reshape(db, H_IDX, DH_IDX), ((0, 0), (0, 16 - H_IDX), (0, 0)))
        w16 = jnp.pad((ps["wi"][:, :H_IDX] * H_IDX ** -0.5).reshape(db, H_IDX, 1), ((0, 0), (0, 16 - H_IDX), (0, 0)))
        t8 = lambda a: a.reshape(db, H_A, DH_A).transpose(0, 2, 1)
        r4 = lambda a: a.reshape(db, H_R, DK_R)
        ret_args = (r4(ps["qr"]), r4(ps["kr"]), r4(ps["vr"]), r4(ps["gr"]), w["gng"].reshape(H_R, DV_R), gam)
        sc_s, sn_s, yr_s, st_s = _sscore_call(pt_flat, ckidx_t, q16, w16, ps["ki"].reshape(db, 1, DH_IDX),
                                              *ret_args, state_ret, l, npages)
        mask, snew = _ssel_call(sc_s.reshape(db, past), sn_s.reshape(db, LANES), topk_s)
        mask = mask.reshape(db, npages, 1, PAGE_SIZE)
        snew = snew.reshape(db, 1, LANES)

        pp = _pre_call(hp, w, rot_p[0], rot_p[1], alpha, DH_A ** -0.5 * LOG2E, tq, bf16, s, "pre_prompt")
        r3 = lambda a: a.reshape(b, s, a.shape[-1])
        oa = _attn_call(r3(pp["qa"]), r3(pp["qi"]), r3(pp["wi"]), r3(pp["kab"]), r3(pp["vab"]), r3(pp["ki2"]),
                        bias_tiles * LOG2E, topk_p, tq)
        yr, st_p = _ret_call(r3(pp["qr"]), r3(pp["kr"]), r3(pp["vr"]), r3(pp["gr"]), w["gng"], dec_p, chunk)
        steps = (b * s) // STREAM_POST_TM if (b * s) % STREAM_POST_TM == 0 else 0
        sps = steps // db if steps and steps % db == 0 else 0
        step_pages = npages // sps if sps and npages % sps == 0 else 0
        sgrp = min(STREAM_PAGE_GROUP, step_pages) if step_pages else 0
        if sgrp and step_pages % sgrp == 0:
            hp, ot_s = _post_stream_call(pp["h"], oa.reshape(b * s, W_A), yr.reshape(b * s, W_R), w, alpha,
                                         STREAM_POST_TM, pt_flat, ck_t, cv_t, t8(ps["qa"]), t8(ps["ka"]),
                                         t8(ps["va"]), bias_pos, bias0, mask, snew, l, npages, sgrp,
                                         step_pages // sgrp, "post_prompt")
        else:
            hp = _post_call(pp["h"], oa.reshape(b * s, W_A), yr.reshape(b * s, W_R), w, alpha,
                            min(POST_TM, b * s), "post_prompt")
            ot_s = _satt_call(pt_flat, ck_t, cv_t, t8(ps["qa"]), t8(ps["ka"]), t8(ps["va"]), bias_pos,
                              bias0, mask, snew, l, npages, grp)
        outs["kp"].append(pp["ka"].reshape(b, H_A, DH_A, s).transpose(0, 3, 1, 2))
        outs["vp"].append(pp["va"].reshape(b, H_A, DH_A, s).transpose(0, 3, 1, 2))
        outs["kip"].append(pp["ki"].transpose(0, 2, 1))
        outs["sp"].append(st_p)
        oa_s = ot_s.transpose(0, 2, 1).reshape(db, W_A)
        hs = _post_call(ps["h"], oa_s, yr_s.reshape(db, W_R), w, alpha, db * ds, "post_sample")
        outs["ks"].append(ps["ka"].reshape(db, ds, H_A, DH_A))
        outs["vs"].append(ps["va"].reshape(db, ds, H_A, DH_A))
        outs["kis"].append(ps["ki"].reshape(db, ds, DH_IDX))
        outs["ss"].append(st_s)

    stack = lambda k: jnp.stack(outs[k])
    return (hp.reshape(b, s, D_MODEL), hs.reshape(db, ds, D_MODEL),
            stack("kp"), stack("vp"), stack("kip"), stack("sp"),
            stack("ks"), stack("vs"), stack("kis"), stack("ss"))
```

```python
import functools
import math

import numpy as np
import jax
import jax.numpy as jnp
from jax import lax
from jax.experimental import pallas as pl
from jax.experimental.pallas import tpu as pltpu

D_MODEL = 1024
D_FF = 2816
PAGE_SIZE = 128
H_A = 8
DH_A = 64
W_A = H_A * DH_A
H_IDX = 8
DH_IDX = 64
TOPK_MAX = 256
NUM_BUCKETS = 32
MAX_DISTANCE = 128
H_R = 4
DK_R = 128
DV_R = 128
W_R = H_R * DV_R
ROPE_BASE = 10000.0
LN_EPS = 1e-5
GN_EPS = 1e-5
IN_SIZES = (W_A, W_A, W_A, H_IDX * DH_IDX, DH_IDX, H_IDX, H_R * DK_R, H_R * DK_R, W_R, W_R)

LANES = 128
MIB = 1024 * 1024
NEG = -1e30
LOG2E = math.log2(math.e)

FF_CHUNK = 512
PRE_TM = 512
POST_TM = 512
ATT_TQ = 256
RET_CHUNK = 512
BISECT_ITERS = 16
SAMPLE_PAGE_GROUP = 16
STREAM_POST_TM = 256
STREAM_PAGE_GROUP = 8
ATT_FAR_WIDTH = 2

C_QA, C_KA, C_VA, C_QI, C_KI2, C_WI, C_QR, C_KR, C_VR, C_GR, C_END = (
    0, 512, 1024, 1536, 2048, 2176, 2304, 2816, 3328, 3840, 4352)

f32 = jnp.float32
bf16 = jnp.bfloat16
NT_DIMS = (((1,), (1,)), ((), ()))


def _const_spec(shape):
    nd = len(shape)
    return pl.BlockSpec(shape, lambda *_: (0,) * nd, pipeline_mode=pl.Buffered(1))


def _ln(x, g, b):
    mu = jnp.mean(x, axis=-1, keepdims=True)
    xc = x - mu
    var = jnp.mean(xc * xc, axis=-1, keepdims=True)
    return xc * lax.rsqrt(var + LN_EPS) * g + b


def _ffn(xb, wg_ref, wu_ref, wd_ref, midway=None):
    acc = None
    starts = list(range(0, D_FF, FF_CHUNK))
    for ci, c0 in enumerate(starts):
        if midway is not None and ci == len(starts) // 2:
            midway()
        c1 = min(c0 + FF_CHUNK, D_FF)
        g = jnp.dot(xb, wg_ref[:, c0:c1], preferred_element_type=f32)
        u = jnp.dot(xb, wu_ref[:, c0:c1], preferred_element_type=f32)
        a = (jax.nn.silu(g) * u).astype(bf16)
        part = jnp.dot(a, wd_ref[c0:c1, :], preferred_element_type=f32)
        acc = part if acc is None else acc + part
    return acc


def _pre_kernel(alpha, q_scale, feature_major, x_ref, wg_ref, wu_ref, wd_ref, lng_ref, lnb_ref, win_ref,
                rc_ref, rs_ref,
                h_ref, qa_ref, ka_ref, va_ref, kab_ref, vab_ref, qi_ref, ki_ref, ki2_ref, wi_ref,
                qr_ref, kr_ref, vr_ref, gr_ref):
    act = qa_ref.dtype
    x = x_ref[...]
    f = _ffn(x.astype(bf16), wg_ref, wu_ref, wd_ref)
    h = _ln(alpha * x + 0.5 * f, lng_ref[...], lnb_ref[...])
    h_ref[...] = h
    hb = h.astype(bf16)

    def proj(c0, c1):
        return jnp.dot(hb, win_ref[:, c0:c1], preferred_element_type=f32)

    qa_ref[...] = (proj(C_QA, C_KA) * q_scale).astype(act)
    ka = proj(C_KA, C_VA)
    kab_ref[...] = ka.astype(act)
    va = proj(C_VA, C_QI)
    qi_ref[...] = (proj(C_QI, C_KI2) * DH_IDX ** -0.5).astype(act)
    kk = proj(C_KI2, C_WI)
    wi_ref[...] = proj(C_WI, C_QR)
    vab_ref[...] = va.astype(act)
    if feature_major:
        ka_ref[...] = ka.T
        va_ref[...] = va.T
        ki_ref[...] = kk.T[:DH_IDX, :]
    else:
        ka_ref[...] = ka
        va_ref[...] = va
        ki_ref[...] = kk[:, :DH_IDX]
    ki2_ref[...] = kk.astype(act)
    qr = proj(C_QR, C_KR)
    kr = proj(C_KR, C_VR)
    c = rc_ref[...]
    s = rs_ref[...]
    for hh in range(H_R):
        sl = slice(DK_R * hh, DK_R * (hh + 1))
        qh = qr[:, sl]
        kh = kr[:, sl]
        qr_ref[:, sl] = (qh * c + pltpu.roll(qh, DK_R // 2, 1) * s).astype(act)
        kr_ref[:, sl] = ((kh * c + pltpu.roll(kh, DK_R // 2, 1) * s) * DK_R ** -0.5).astype(act)
    vr_ref[...] = proj(C_VR, C_GR).astype(act)
    gr_ref[...] = proj(C_GR, C_END)


def _pre_call(x, w, rot_c, rot_s, alpha, q_scale, tm, act, seq_len, name):
    n = x.shape[0]
    grid = (pl.cdiv(n, tm),)
    row = lambda width: pl.BlockSpec((tm, width), lambda i: (i, 0))
    rot_blocks = rot_c.shape[0] // tm
    rot = pl.BlockSpec((tm, LANES), lambda i: (i % rot_blocks, 0))
    feature_major = seq_len is not None
    if feature_major:
        assert seq_len % tm == 0 and n % seq_len == 0
        seq_blocks = seq_len // tm
        kv_shape = lambda width: jax.ShapeDtypeStruct((n // seq_len, width, seq_len), f32)
        kv_spec = lambda width: pl.BlockSpec((None, width, tm), lambda i: (i // seq_blocks, 0, i % seq_blocks))
    else:
        kv_shape = lambda width: jax.ShapeDtypeStruct((n, width), f32)
        kv_spec = row
    in_specs = [
        row(D_MODEL),
        _const_spec((D_MODEL, D_FF)), _const_spec((D_MODEL, D_FF)), _const_spec((D_FF, D_MODEL)),
        _const_spec((1, D_MODEL)), _const_spec((1, D_MODEL)),
        _const_spec((D_MODEL, C_END)),
        rot, rot,
    ]
    outs = [
        ("h", D_MODEL, f32), ("qa", W_A, act), ("ka", W_A, f32), ("va", W_A, f32), ("kab", W_A, act),
        ("vab", W_A, act), ("qi", W_A, act), ("ki", DH_IDX, f32), ("ki2", LANES, act), ("wi", LANES, f32),
        ("qr", W_R, act), ("kr", W_R, act), ("vr", W_R, act), ("gr", W_R, f32),
    ]
    kv_names = ("ka", "va", "ki")
    out_shape = [kv_shape(wd) if k in kv_names else jax.ShapeDtypeStruct((n, wd), dt) for k, wd, dt in outs]
    out_specs = [kv_spec(wd) if k in kv_names else row(wd) for k, wd, _ in outs]
    res = pl.pallas_call(
        functools.partial(_pre_kernel, alpha, q_scale, feature_major),
        grid=grid, in_specs=in_specs, out_specs=out_specs, out_shape=out_shape,
        compiler_params=pltpu.CompilerParams(dimension_semantics=("arbitrary",), vmem_limit_bytes=58 * MIB),
        name=name,
    )(x, w["ffn1_wg"], w["ffn1_wu"], w["ffn1_wd"], w["ln1_g"], w["ln1_b"], w["w_in"], rot_c, rot_s)
    return {k: v for (k, _, _), v in zip(outs, res)}


def _post_kernel(alpha, h_ref, oa_ref, yr_ref, wo_ref, l2g_ref, l2b_ref, wg_ref, wu_ref, wd_ref,
                 l3g_ref, l3b_ref, out_ref):
    h = h_ref[...]
    mix = (jnp.dot(oa_ref[...].astype(bf16), wo_ref[0:W_A, :], preferred_element_type=f32)
           + jnp.dot(yr_ref[...].astype(bf16), wo_ref[W_A:W_A + W_R, :], preferred_element_type=f32))
    h2 = _ln(alpha * h + mix, l2g_ref[...], l2b_ref[...])
    f = _ffn(h2.astype(bf16), wg_ref, wu_ref, wd_ref)
    out_ref[...] = _ln(alpha * h2 + 0.5 * f, l3g_ref[...], l3b_ref[...])


def _post_stream_kernel(alpha, layer, npages, grp, units, steps_per_seq, pt_ref,
                        h_ref, oa_ref, yr_ref, wo_ref, l2g_ref, l2b_ref, wg_ref, wu_ref, wd_ref, l3g_ref, l3b_ref,
                        ckk_ref, ckv_ref, qt_ref, knt_ref, vnt_ref, bias_ref, bias0_ref, mask_ref, snew_ref,
                        out_ref, ot_ref, kbuf, vbuf, sem, m_ref, l_ref, acc_ref):
    stream = _page_stream((layer, npages, grp, units, steps_per_seq), pt_ref, ckk_ref, ckv_ref,
                          qt_ref, bias_ref, mask_ref, kbuf, vbuf, sem, m_ref, l_ref, acc_ref)

    @pl.when(stream.part == 0)
    def _():
        _satt_reset(m_ref, l_ref, acc_ref)

    stream.begin()
    h = h_ref[...]
    mix = (jnp.dot(oa_ref[...].astype(bf16), wo_ref[0:W_A, :], preferred_element_type=f32)
           + jnp.dot(yr_ref[...].astype(bf16), wo_ref[W_A:W_A + W_R, :], preferred_element_type=f32))
    h2 = _ln(alpha * h + mix, l2g_ref[...], l2b_ref[...])
    stream.fold_first()
    f = _ffn(h2.astype(bf16), wg_ref, wu_ref, wd_ref, stream.midway)
    out_ref[...] = _ln(alpha * h2 + 0.5 * f, l3g_ref[...], l3b_ref[...])

    @pl.when(stream.part == steps_per_seq - 1)
    def _():
        _satt_finish(qt_ref, knt_ref, vnt_ref, bias0_ref, snew_ref, m_ref, l_ref, acc_ref, ot_ref)


class _page_stream:
    def __init__(self, cfg, pt_ref, ckk_ref, ckv_ref, qt_ref, bias_ref, mask_ref, kbuf, vbuf, sem,
                 m_ref, l_ref, acc_ref):
        self.layer, self.npages, self.grp, self.units, self.steps_per_seq = cfg
        self.pt_ref, self.ckk_ref, self.ckv_ref = pt_ref, ckk_ref, ckv_ref
        self.qt_ref, self.bias_ref, self.mask_ref = qt_ref, bias_ref, mask_ref
        self.kbuf, self.vbuf, self.sem = kbuf, vbuf, sem
        self.state = (m_ref, l_ref, acc_ref)
        self.st = pl.program_id(0)
        self.nsteps = pl.num_programs(0)
        self.part = self.st % self.steps_per_seq
        self.first = range(0, self.units // 2)
        self.second = range(self.units // 2, self.units)

    def _copies(self, step, u, p):
        seq, part = step // self.steps_per_seq, step % self.steps_per_seq
        page = self.pt_ref[seq * self.npages + (part * self.units + u) * self.grp + p]
        return (pltpu.make_async_copy(self.ckk_ref.at[self.layer, page], self.kbuf.at[u, p], self.sem.at[u, 0]),
                pltpu.make_async_copy(self.ckv_ref.at[self.layer, page], self.vbuf.at[u, p], self.sem.at[u, 1]))

    def _start(self, step, us):
        for u in us:
            for p in range(self.grp):
                ck, cv = self._copies(step, u, p)
                ck.start()
                cv.start()

    def _wait(self, us):
        for u in us:
            for p in range(self.grp):
                ck, cv = self._copies(self.st, u, p)
                ck.wait()
                cv.wait()

    def _fold(self, us):
        for u in us:
            _satt_fold(self.kbuf.at[u], self.vbuf.at[u], self.qt_ref, self.bias_ref, self.mask_ref,
                       slice(u * self.grp, (u + 1) * self.grp), *self.state)

    def begin(self):
        @pl.when(self.st == 0)
        def _():
            self._start(0, self.first)
        self._wait(self.first)
        self._start(self.st, self.second)

    def fold_first(self):
        self._fold(self.first)

    def midway(self):
        self._wait(self.second)

        @pl.when(self.st + 1 < self.nsteps)
        def _():
            self._start(self.st + 1, self.first)
        self._fold(self.second)


def _post_stream_call(h, oa, yr, w, alpha, tm, pt_flat, cache_k_t, cache_v_t, qt, knt, vnt, bias_pos, bias0, mask,
                      snew, layer, npages, grp, units, name):
    n = h.shape[0]
    db = qt.shape[0]
    nsteps = n // tm
    pages = units * grp
    steps_per_seq = npages // pages
    assert n % tm == 0 and npages % pages == 0 and nsteps == db * steps_per_seq
    row = lambda width: pl.BlockSpec((tm, width), lambda i, pt: (i, 0))
    per_seq = lambda d0, d1: pl.BlockSpec((None, d0, d1), lambda i, pt: (i // steps_per_seq, 0, 0))
    bias_spec = pl.BlockSpec((H_A, pages, 1, PAGE_SIZE), lambda i, pt: (0, i % steps_per_seq, 0, 0))
    mask_spec = pl.BlockSpec((None, pages, 1, PAGE_SIZE),
                             lambda i, pt: (i // steps_per_seq, i % steps_per_seq, 0, 0))
    grid_spec = pltpu.PrefetchScalarGridSpec(
        num_scalar_prefetch=1,
        grid=(nsteps,),
        in_specs=[
            row(D_MODEL), row(W_A), row(W_R),
            _const_spec((W_A + W_R, D_MODEL)), _const_spec((1, D_MODEL)), _const_spec((1, D_MODEL)),
            _const_spec((D_MODEL, D_FF)), _const_spec((D_MODEL, D_FF)), _const_spec((D_FF, D_MODEL)),
            _const_spec((1, D_MODEL)), _const_spec((1, D_MODEL)),
            pl.BlockSpec(memory_space=pl.ANY), pl.BlockSpec(memory_space=pl.ANY),
            per_seq(DH_A, H_A), per_seq(DH_A, H_A), per_seq(DH_A, H_A),
            bias_spec, pl.BlockSpec((1, H_A), lambda i, pt: (0, 0)), mask_spec, per_seq(1, LANES),
        ],
        out_specs=[row(D_MODEL), per_seq(DH_A, H_A)],
        scratch_shapes=[
            pltpu.VMEM((units, grp, H_A, DH_A, PAGE_SIZE), f32),
            pltpu.VMEM((units, grp, H_A, DH_A, PAGE_SIZE), f32),
            pltpu.SemaphoreType.DMA((units, 2)),
            pltpu.VMEM((H_A, 1, PAGE_SIZE), f32),
            pltpu.VMEM((H_A, 1, PAGE_SIZE), f32),
            pltpu.VMEM((H_A, DH_A, PAGE_SIZE), f32),
        ],
    )
    return pl.pallas_call(
        functools.partial(_post_stream_kernel, alpha, layer, npages, grp, units, steps_per_seq),
        grid_spec=grid_spec,
        out_shape=[jax.ShapeDtypeStruct((n, D_MODEL), f32), jax.ShapeDtypeStruct((db, DH_A, H_A), f32)],
        compiler_params=pltpu.CompilerParams(dimension_semantics=("arbitrary",), vmem_limit_bytes=58 * MIB),
        name=name,
    )(pt_flat, h, oa, yr, w["w_out"], w["ln2_g"], w["ln2_b"], w["ffn2_wg"], w["ffn2_wu"], w["ffn2_wd"],
      w["ln3_g"], w["ln3_b"], cache_k_t, cache_v_t, qt, knt, vnt, bias_pos, bias0, mask, snew)


def _post_call(h, oa, yr, w, alpha, tm, name):
    n = h.shape[0]
    row = lambda width: pl.BlockSpec((tm, width), lambda i: (i, 0))
    in_specs = [
        row(D_MODEL), row(W_A), row(W_R),
        _const_spec((W_A + W_R, D_MODEL)), _const_spec((1, D_MODEL)), _const_spec((1, D_MODEL)),
        _const_spec((D_MODEL, D_FF)), _const_spec((D_MODEL, D_FF)), _const_spec((D_FF, D_MODEL)),
        _const_spec((1, D_MODEL)), _const_spec((1, D_MODEL)),
    ]
    return pl.pallas_call(
        functools.partial(_post_kernel, alpha),
        grid=(pl.cdiv(n, tm),), in_specs=in_specs, out_specs=row(D_MODEL),
        out_shape=jax.ShapeDtypeStruct((n, D_MODEL), f32),
        compiler_params=pltpu.CompilerParams(dimension_semantics=("arbitrary",), vmem_limit_bytes=48 * MIB),
        name=name,
    )(h, oa, yr, w["w_out"], w["ln2_g"], w["ln2_b"], w["ffn2_wg"], w["ffn2_wu"], w["ffn2_wd"],
      w["ln3_g"], w["ln3_b"])


def _attn_kernel(topk, qa_ref, qi_ref, wi_ref, k_ref, v_ref, ki2_ref, bias_ref, oa_ref,
                    sc_ref, qap_ref, qip_ref, m_ref, l_ref, acc_ref):
    tq = qa_ref.shape[0]
    ngrp = tq // 8
    i = pl.program_id(1)
    nj = i + 1
    kf = float(topk)

    lane = lax.broadcasted_iota(jnp.int32, (tq, LANES), 1)
    lo_half = lane < DH_A
    for h in range(H_A):
        p, par = h // 2, h % 2
        keep = lo_half if par == 0 else jnp.logical_not(lo_half)
        blk = slice(LANES * p, LANES * (p + 1))
        qip_ref[h] = jnp.where(keep, qi_ref[:, blk], jnp.zeros((), qi_ref.dtype))
        qap_ref[p, par * tq:(par + 1) * tq, :] = jnp.where(keep, qa_ref[:, blk], jnp.zeros((), qa_ref.dtype))
    w_heads = wi_ref[...].T[0:H_IDX, :] * H_IDX ** -0.5

    def score_chunk(j, carry):
        off = pl.multiple_of(j * tq, tq)
        kj = ki2_ref[pl.ds(off, tq), :]
        acc = jnp.zeros((tq, tq), f32)
        for h in range(H_IDX):
            d = lax.dot_general(kj, qip_ref[h], NT_DIMS, preferred_element_type=f32)
            acc = acc + w_heads[h:h + 1, :] * jnp.maximum(d, 0.0)
        sc_ref[j] = acc
        return carry

    lax.fori_loop(0, nj, score_chunk, 0)

    krow = lax.broadcasted_iota(jnp.int32, (tq, tq), 0)
    qcol = lax.broadcasted_iota(jnp.int32, (tq, tq), 1)
    causal = krow <= qcol
    sd = sc_ref[i]
    mn_diag = jnp.min(jnp.where(causal, sd, jnp.inf), axis=0, keepdims=True)
    sc_ref[i] = jnp.where(causal, sd, -jnp.inf)

    def fold(fn, init, comb, n_chunks, t=None):
        t8 = None if t is None else jnp.broadcast_to(t, (8, tq))

        nacc = 4

        def body(j, accs):
            accs = list(accs)
            for r in range(ngrp):
                accs[r % nacc] = comb(accs[r % nacc], fn(sc_ref[j, 8 * r:8 * r + 8, :], t8))
            return tuple(accs)
        accs = lax.fori_loop(0, n_chunks, body, tuple(jnp.full((8, tq), init, f32) for _ in range(nacc)))
        acc = comb(comb(accs[0], accs[1]), comb(accs[2], accs[3]))
        if comb is jnp.add:
            return jnp.sum(acc, axis=0, keepdims=True)
        if comb is jnp.maximum:
            return jnp.max(acc, axis=0, keepdims=True)
        return jnp.min(acc, axis=0, keepdims=True)

    def count_ge(t):
        return fold(lambda s, t8: jnp.where(s >= t8, 1.0, 0.0), 0.0, jnp.add, nj, t)

    def count_gt(t):
        return fold(lambda s, t8: jnp.where(s > t8, 1.0, 0.0), 0.0, jnp.add, nj, t)

    def max_below(t):
        return fold(lambda s, t8: jnp.where(s < t8, s, -jnp.inf), -jnp.inf, jnp.maximum, nj, t)

    n_keys = (lax.broadcasted_iota(jnp.int32, (1, tq), 1) + (i * tq + 1)).astype(f32)
    take_all = n_keys <= kf

    @pl.when((i + 1) * tq <= topk)
    def _():
        def mk(j, carry):
            sc_ref[j] = jnp.zeros((tq, tq), f32)
            return carry
        lax.fori_loop(0, nj, mk, 0)

    @pl.when((i + 1) * tq > topk)
    def _():
        def minmax_body(j, c):
            mxs, mns = list(c[0]), list(c[1])
            for r in range(ngrp):
                v = sc_ref[j, 8 * r:8 * r + 8, :]
                mxs[r % 2] = jnp.maximum(mxs[r % 2], v)
                mns[r % 2] = jnp.minimum(mns[r % 2], v)
            return tuple(mxs), tuple(mns)
        full8 = lambda val: jnp.full((8, tq), val, f32)
        mxs, mns = lax.fori_loop(0, i, minmax_body, ((full8(-jnp.inf),) * 2, (full8(jnp.inf),) * 2))
        mx = jnp.maximum(jnp.max(jnp.maximum(mxs[0], mxs[1]), axis=0, keepdims=True),
                         jnp.max(sc_ref[i], axis=0, keepdims=True))
        mn = jnp.minimum(jnp.min(jnp.minimum(mns[0], mns[1]), axis=0, keepdims=True), mn_diag)
        c_max = count_ge(mx)
        done0 = jnp.logical_or(take_all, c_max >= kf)
        thr0 = jnp.where(take_all, -jnp.inf, mx)
        cge0 = jnp.where(take_all, n_keys, c_max)

        def bis(_, st):
            lo, hi = st
            mid = 0.5 * (lo + hi)
            ge = count_ge(mid) >= kf
            return jnp.where(ge, mid, lo), jnp.where(ge, hi, mid)

        lo, hi = lax.fori_loop(0, BISECT_ITERS, bis, (mn, mx))

        def snap_cond(st):
            return st[0] > 0.0

        def snap_body(st):
            _, hi, thr, cge, done = st
            m = max_below(hi)
            c = count_ge(m)
            ok = c >= kf
            newly = jnp.logical_and(ok, done < 0.5)
            thr = jnp.where(newly, m, thr)
            cge = jnp.where(newly, c, cge)
            hi = jnp.where(jnp.logical_or(done > 0.5, ok), hi, m)
            done = jnp.where(ok, 1.0, done)
            return jnp.sum(1.0 - done), hi, thr, cge, done

        done_f = jnp.where(done0, 1.0, 0.0)
        _, _, thr, cge, _ = lax.while_loop(snap_cond, snap_body, (jnp.sum(1.0 - done_f), hi, thr0, cge0, done_f))

        excess = jnp.sum(jnp.where(cge > kf, 1.0, 0.0)) > 0.0

        @pl.when(jnp.logical_not(excess))
        def _():
            def mk(j, carry):
                sc_ref[j] = jnp.where(sc_ref[j] >= thr, 0.0, NEG)
                return carry
            lax.fori_loop(0, nj, mk, 0)

        @pl.when(excess)
        def _():
            need = kf - count_gt(thr)
            lower = jnp.where(qcol <= krow, 1.0, 0.0).astype(bf16)

            def mk(j, seen):
                s = sc_ref[j]
                tie = s == thr
                rank = seen + jnp.dot(lower, jnp.where(tie, 1.0, 0.0).astype(bf16), preferred_element_type=f32)
                sel = jnp.logical_or(s > thr, jnp.logical_and(tie, rank <= need))
                sc_ref[j] = jnp.where(sel, 0.0, NEG)
                return rank[tq - 1:tq, :]
            lax.fori_loop(0, nj, mk, jnp.zeros((1, tq), f32))

    sc_ref[i] = jnp.where(causal, sc_ref[i], NEG)

    m_ref[...] = jnp.full(m_ref.shape, -jnp.inf, f32)
    l_ref[...] = jnp.zeros(l_ref.shape, f32)
    acc_ref[...] = jnp.zeros(acc_ref.shape, f32)
    ncb = tq // LANES

    def lane_blocks(x, comb):
        out = x[:, 0:LANES]
        for cb in range(1, x.shape[1] // LANES):
            out = comb(out, x[:, cb * LANES:(cb + 1) * LANES])
        return out

    def attend(j, width, bias_cols):
        off = pl.multiple_of(j * tq, tq)
        keys = pl.ds(off, width * tq)
        sel_mask = jnp.concatenate([sc_ref[j + c].T for c in range(width)], axis=1)
        for p in range(H_A // 2):
            blk = slice(LANES * p, LANES * (p + 1))
            s_pair = lax.dot_general(qap_ref[p], k_ref[keys, blk], NT_DIMS, preferred_element_type=f32)
            es = []
            for par in range(2):
                h = 2 * p + par
                s = s_pair[par * tq:(par + 1) * tq] + sel_mask
                if bias_cols is not None:
                    s = s + bias_ref[h, :, bias_cols]
                m_old = m_ref[h]
                row_max = jnp.max(lane_blocks(s, jnp.maximum), axis=1, keepdims=True)
                m_new = jnp.maximum(m_old, jnp.broadcast_to(row_max, (tq, LANES)))
                a = jnp.exp2(m_old - m_new)
                e = jnp.exp2(s - jnp.concatenate([m_new] * (width * ncb), axis=1))
                l_ref[h] = l_ref[h] * a + lane_blocks(e, jnp.add)
                acc_ref[h] = acc_ref[h] * a
                m_ref[h] = m_new
                es.append(e.astype(bf16))
            pv = jnp.dot(jnp.concatenate(es, axis=0), v_ref[keys, blk], preferred_element_type=f32)
            acc_ref[2 * p] += pv[0:tq]
            acc_ref[2 * p + 1] += pv[tq:2 * tq]

    n_far = jnp.maximum(i - 1, 0)
    if ATT_FAR_WIDTH == 2:
        def far_pair(jj, carry):
            attend(2 * jj, 2, None)
            return carry
        lax.fori_loop(0, n_far // 2, far_pair, 0)

        @pl.when(n_far % 2 == 1)
        def _():
            attend(n_far - 1, 1, None)
    else:
        def far_one(j, carry):
            attend(j, 1, None)
            return carry
        lax.fori_loop(0, n_far, far_one, 0)

    @pl.when(i >= 1)
    def _():
        attend(i - 1, 2, slice(0, 2 * tq))

    @pl.when(i == 0)
    def _():
        attend(0, 1, slice(tq, 2 * tq))

    for p in range(H_A // 2):
        l_even = jnp.broadcast_to(jnp.sum(l_ref[2 * p], axis=1, keepdims=True), (tq, LANES))
        l_odd = jnp.broadcast_to(jnp.sum(l_ref[2 * p + 1], axis=1, keepdims=True), (tq, LANES))
        o_pair = jnp.where(lo_half, acc_ref[2 * p] / l_even, acc_ref[2 * p + 1] / l_odd)
        oa_ref[:, LANES * p:LANES * (p + 1)] = o_pair.astype(oa_ref.dtype)


def _attn_call(qa, qi, wi, kab, vab, ki2, bias_tiles, topk, tq):
    b, s, _ = qa.shape
    nq = s // tq
    qspec = lambda width: pl.BlockSpec((None, tq, width), lambda bi, i: (bi, i, 0))
    whole = lambda shape: pl.BlockSpec((None,) + shape, lambda bi, i: (bi,) + (0,) * len(shape),
                                       pipeline_mode=pl.Buffered(1))
    scratch = [
        pltpu.VMEM((nq, tq, tq), f32),
        pltpu.VMEM((H_A // 2, 2 * tq, LANES), qa.dtype),
        pltpu.VMEM((H_IDX, tq, LANES), qi.dtype),
        pltpu.VMEM((H_A, tq, LANES), f32),
        pltpu.VMEM((H_A, tq, LANES), f32),
        pltpu.VMEM((H_A, tq, LANES), f32),
    ]
    return pl.pallas_call(
        functools.partial(_attn_kernel, topk),
        grid=(b, nq),
        in_specs=[qspec(W_A), qspec(W_A),
                  qspec(LANES),
                  whole((s, W_A)), whole((s, W_A)), whole((s, LANES)),
                  _const_spec((H_A, tq, 2 * tq))],
        out_specs=qspec(W_A),
        out_shape=jax.ShapeDtypeStruct((b, s, W_A), bf16),
        scratch_shapes=scratch,
        compiler_params=pltpu.CompilerParams(dimension_semantics=("arbitrary", "arbitrary"),
                                             vmem_limit_bytes=48 * MIB),
        name="attn_prompt",
    )(qa, qi, wi, kab, vab, ki2, bias_tiles)


def _ret_kernel(qr_ref, kr_ref, vr_ref, gr_ref, gng_ref, dmat_ref, cross_ref, kvd_ref, cdec_ref,
                yr_ref, st_ref):
    c = pl.program_id(1)

    @pl.when(c == 0)
    def _():
        st_ref[...] = jnp.zeros(st_ref.shape, f32)

    for h in range(H_R):
        sl = slice(DK_R * h, DK_R * (h + 1))
        q = qr_ref[:, sl]
        k = kr_ref[:, sl]
        v = vr_ref[:, sl]
        att = lax.dot_general(q, k, NT_DIMS, preferred_element_type=f32) * dmat_ref[h]
        st = st_ref[h]
        o = (jnp.dot(att.astype(bf16), v, preferred_element_type=f32)
             + jnp.dot(q, st.astype(bf16), preferred_element_type=f32) * cross_ref[h])
        kd = (k.astype(f32) * kvd_ref[h]).T.astype(bf16)
        st_ref[h] = cdec_ref[h, 0:1, :] * st + jnp.dot(kd, v, preferred_element_type=f32)
        mu = jnp.mean(o, axis=-1, keepdims=True)
        oc = o - mu
        var = jnp.mean(oc * oc, axis=-1, keepdims=True)
        yn = oc * lax.rsqrt(var + GN_EPS) * gng_ref[:, sl]
        yr_ref[:, sl] = (jax.nn.silu(gr_ref[:, sl]) * yn).astype(yr_ref.dtype)


def _ret_call(qr, kr, vr, gr, gng, dec, chunk):
    b, s, _ = qr.shape
    nc = s // chunk
    rspec = pl.BlockSpec((None, chunk, W_R), lambda bi, c: (bi, c, 0))
    return pl.pallas_call(
        _ret_kernel,
        grid=(b, nc),
        in_specs=[rspec, rspec, rspec, rspec, _const_spec((1, W_R)),
                  _const_spec((H_R, chunk, chunk)), _const_spec((H_R, chunk, LANES)),
                  _const_spec((H_R, chunk, LANES)), _const_spec((H_R, 8, LANES))],
        out_specs=[rspec, pl.BlockSpec((None, H_R, DK_R, DV_R), lambda bi, c: (bi, 0, 0, 0))],
        out_shape=[jax.ShapeDtypeStruct((b, s, W_R), bf16), jax.ShapeDtypeStruct((b, H_R, DK_R, DV_R), f32)],
        compiler_params=pltpu.CompilerParams(dimension_semantics=("arbitrary", "arbitrary"),
                                             vmem_limit_bytes=32 * MIB),
        name="ret_prompt",
    )(qr, kr, vr, gr, gng, dec["intra"], dec["cross"], dec["kv"], dec["chunk"])


def _sscore_kernel(layer, npages, pt_ref, ck_ref, q_ref, w_ref, kn_ref,
                   qr_ref, kr_ref, vr_ref, gr_ref, gng_ref, gam_ref, s0_ref,
                   sc_ref, snew_ref, yr_ref, sn_ref, kbuf, sem):
    b = pl.program_id(0)
    nb = pl.num_programs(0)
    slot = b % 2

    def page_copy(seq, p, sl):
        page = pt_ref[seq * npages + p]
        return pltpu.make_async_copy(ck_ref.at[layer, page], kbuf.at[sl, p], sem.at[sl])

    def start_all(seq, sl):
        def body(p, carry):
            page_copy(seq, p, sl).start()
            return carry
        lax.fori_loop(0, npages, body, 0)

    @pl.when(b == 0)
    def _():
        start_all(0, 0)

    @pl.when(b + 1 < nb)
    def _():
        start_all(b + 1, 1 - slot)

    _sample_retention_step(qr_ref, kr_ref, vr_ref, gr_ref, gng_ref, gam_ref, s0_ref, yr_ref, sn_ref)

    def wait_body(p, carry):
        page_copy(b, p, slot).wait()
        return carry
    lax.fori_loop(0, npages, wait_body, 0)

    q = q_ref[...].astype(bf16)
    w = w_ref[...]

    pages_per_dot = min(8, npages)
    for p0 in range(0, npages, pages_per_dot):
        kw = jnp.concatenate([kbuf[slot, p] for p in range(p0, p0 + pages_per_dot)], axis=1).astype(bf16)
        d = jnp.dot(q, kw, preferred_element_type=f32)
        sc_ref[:, p0 * PAGE_SIZE:(p0 + pages_per_dot) * PAGE_SIZE] = jnp.sum(
            w * jnp.maximum(d, 0.0), axis=0, keepdims=True)

    kn = kn_ref[...].astype(bf16).astype(f32)
    dn = jnp.sum(q.astype(f32) * kn, axis=1, keepdims=True)
    s_new = jnp.sum(w * jnp.maximum(dn, 0.0), axis=0, keepdims=True)
    snew_ref[...] = jnp.broadcast_to(s_new, (1, LANES))


def _sscore_call(page_table_flat, cache_kidx_t, q16, w16, kn, qr, kr, vr, gr, gng, gam, state, layer, npages):
    db = q16.shape[0]
    assert npages % min(8, npages) == 0
    per_seq = lambda d0, d1: pl.BlockSpec((None, d0, d1), lambda b, pt: (b, 0, 0))
    shared = pl.BlockSpec((H_R, DV_R), lambda b, pt: (0, 0))
    grid_spec = pltpu.PrefetchScalarGridSpec(
        num_scalar_prefetch=1,
        grid=(db,),
        in_specs=[
            pl.BlockSpec(memory_space=pl.ANY),
            per_seq(16, DH_IDX), per_seq(16, 1), per_seq(1, DH_IDX),
            per_seq(H_R, DK_R), per_seq(H_R, DK_R), per_seq(H_R, DV_R), per_seq(H_R, DV_R), shared, shared,
            pl.BlockSpec((None, None, H_R, DK_R, DV_R), lambda b, pt: (layer, b, 0, 0, 0)),
        ],
        out_specs=[per_seq(1, npages * PAGE_SIZE), per_seq(1, LANES), per_seq(H_R, DV_R),
                   pl.BlockSpec((None, H_R, DK_R, DV_R), lambda b, pt: (b, 0, 0, 0))],
        scratch_shapes=[
            pltpu.VMEM((2, npages, DH_IDX, PAGE_SIZE), f32),
            pltpu.SemaphoreType.DMA((2,)),
        ],
    )
    return pl.pallas_call(
        functools.partial(_sscore_kernel, layer, npages),
        grid_spec=grid_spec,
        out_shape=[jax.ShapeDtypeStruct((db, 1, npages * PAGE_SIZE), f32),
                   jax.ShapeDtypeStruct((db, 1, LANES), f32),
                   jax.ShapeDtypeStruct((db, H_R, DV_R), f32), jax.ShapeDtypeStruct((db, H_R, DK_R, DV_R), f32)],
        compiler_params=pltpu.CompilerParams(dimension_semantics=("arbitrary",), vmem_limit_bytes=32 * MIB),
        name="sample_scores",
    )(page_table_flat, cache_kidx_t, q16, w16, kn, qr, kr, vr, gr, gng, gam, state)


PREFIX_CHUNK = 256


def _ssel_kernel(topk, sc_ref, sn_ref, mask_ref, selnew_ref):
    kf = float(topk)
    db, length = sc_ref.shape
    sc = sc_ref[...]
    s_new = sn_ref[:, 0:1]

    def rsum(x):
        return jnp.sum(x, axis=1, keepdims=True)

    def count(cmp, t):
        return rsum(jnp.where(cmp(sc, t), 1.0, 0.0)) + jnp.where(cmp(s_new, t), 1.0, 0.0)

    ge = lambda a, t: a >= t
    gt = lambda a, t: a > t
    mx = jnp.maximum(jnp.max(sc, axis=1, keepdims=True), s_new)
    mn = jnp.minimum(jnp.min(sc, axis=1, keepdims=True), s_new)
    c_max = count(ge, mx)
    done0 = jnp.where(c_max >= kf, 1.0, 0.0)

    def bis(_, st):
        lo, hi = st
        mid = 0.5 * (lo + hi)
        ok = count(ge, mid) >= kf
        return jnp.where(ok, mid, lo), jnp.where(ok, hi, mid)
    lo, hi = lax.fori_loop(0, BISECT_ITERS, bis, (mn, mx))

    def snap_cond(st):
        return st[0] > 0.0

    def snap_body(st):
        _, hi, thr, done = st
        below = jnp.maximum(jnp.max(jnp.where(sc < hi, sc, -jnp.inf), axis=1, keepdims=True),
                            jnp.where(s_new < hi, s_new, -jnp.inf))
        ok = count(ge, below) >= kf
        newly = jnp.logical_and(ok, done < 0.5)
        thr = jnp.where(newly, below, thr)
        hi = jnp.where(jnp.logical_or(done > 0.5, ok), hi, below)
        done = jnp.where(ok, 1.0, done)
        return jnp.sum(1.0 - done), hi, thr, done
    _, _, thr, _ = lax.while_loop(snap_cond, snap_body, (jnp.sum(1.0 - done0), hi, mx, done0))

    need = kf - count(gt, thr)
    pc = min(PREFIX_CHUNK, length)
    r_i = lax.broadcasted_iota(jnp.int32, (pc, pc), 0)
    c_i = lax.broadcasted_iota(jnp.int32, (pc, pc), 1)
    upper = jnp.where(r_i <= c_i, 1.0, 0.0).astype(bf16)
    seen = jnp.zeros((db, 1), f32)
    for c0 in range(0, length, pc):
        s_c = sc[:, c0:c0 + pc]
        tie = s_c == thr
        rank = seen + jnp.dot(jnp.where(tie, 1.0, 0.0).astype(bf16), upper, preferred_element_type=f32)
        sel = jnp.logical_or(s_c > thr, jnp.logical_and(tie, rank <= need))
        mask_ref[:, c0:c0 + pc] = jnp.where(sel, 0.0, NEG)
        seen = rank[:, pc - 1:pc]
    sel_new = jnp.logical_or(s_new > thr, jnp.logical_and(s_new == thr, seen + 1.0 <= need))
    selnew_ref[...] = jnp.broadcast_to(jnp.where(sel_new, 1.0, 0.0), (db, LANES))


def _ssel_call(scores, s_new, topk):
    db, length = scores.shape
    assert length % min(PREFIX_CHUNK, length) == 0
    return pl.pallas_call(
        functools.partial(_ssel_kernel, topk),
        out_shape=[jax.ShapeDtypeStruct((db, length), f32), jax.ShapeDtypeStruct((db, LANES), f32)],
        compiler_params=pltpu.CompilerParams(vmem_limit_bytes=32 * MIB),
        name="sample_select",
    )(scores, s_new)


def _satt_kernel(layer, npages, grp, pt_ref,
                 ckk_ref, ckv_ref, qt_ref, knt_ref, vnt_ref, bias_ref, bias0_ref, mask_ref, snew_ref,
                 ot_ref, kbuf, vbuf, sem, m_ref, l_ref, acc_ref):
    b = pl.program_id(0)
    g = pl.program_id(1)
    nb = pl.num_programs(0)
    ng = pl.num_programs(1)
    t = b * ng + g
    slot = t % 2

    def page_copies(seq, gi, p, sl):
        page = pt_ref[seq * npages + gi * grp + p]
        return (pltpu.make_async_copy(ckk_ref.at[layer, page], kbuf.at[sl, p], sem.at[sl, 0]),
                pltpu.make_async_copy(ckv_ref.at[layer, page], vbuf.at[sl, p], sem.at[sl, 1]))

    def start_all(seq, gi, sl):
        def body(p, carry):
            ck, cv = page_copies(seq, gi, p, sl)
            ck.start()
            cv.start()
            return carry
        lax.fori_loop(0, grp, body, 0)

    @pl.when(t == 0)
    def _():
        start_all(0, 0, 0)

    @pl.when(t + 1 < nb * ng)
    def _():
        wrap = g + 1 == ng
        start_all(jnp.where(wrap, b + 1, b), jnp.where(wrap, 0, g + 1), 1 - slot)

    @pl.when(g == 0)
    def _():
        m_ref[...] = jnp.full(m_ref.shape, -jnp.inf, f32)
        l_ref[...] = jnp.zeros(l_ref.shape, f32)
        acc_ref[...] = jnp.zeros(acc_ref.shape, f32)

    def wait_body(p, carry):
        ck, cv = page_copies(b, g, p, slot)
        ck.wait()
        cv.wait()
        return carry
    lax.fori_loop(0, grp, wait_body, 0)

    _satt_fold(kbuf.at[slot], vbuf.at[slot], qt_ref, bias_ref, mask_ref, slice(0, grp), m_ref, l_ref, acc_ref)

    @pl.when(g == ng - 1)
    def _():
        _satt_finish(qt_ref, knt_ref, vnt_ref, bias0_ref, snew_ref, m_ref, l_ref, acc_ref, ot_ref)


def _satt_fold(kb_ref, vb_ref, qt_ref, bias_ref, mask_ref, rows, m_ref, l_ref, acc_ref):
    mask = mask_ref[rows]
    for h in range(H_A):
        qc = qt_ref[:, h:h + 1]
        kh = kb_ref[:, h]
        lg = jnp.sum(kh * qc[None], axis=1, keepdims=True) + bias_ref[h, rows] + mask
        m_old = m_ref[h]
        m_blk = jnp.max(jnp.max(lg, axis=0), axis=1, keepdims=True)
        m_new = jnp.maximum(m_old, m_blk)
        a = jnp.exp(m_old - m_new)
        e = jnp.exp(lg - m_new[None])
        l_ref[h] = l_ref[h] * a + jnp.sum(e, axis=0)
        acc_ref[h] = acc_ref[h] * a + jnp.sum(vb_ref[:, h] * e, axis=0)
        m_ref[h] = m_new


def _satt_finish(qt_ref, knt_ref, vnt_ref, bias0_ref, snew_ref, m_ref, l_ref, acc_ref, ot_ref):
    lg_new = jnp.sum(qt_ref[...] * knt_ref[...], axis=0, keepdims=True) + bias0_ref[...]
    lg_new = jnp.where(snew_ref[0:1, 0:1] > 0.5, lg_new, NEG)
    for h in range(H_A):
        m_c = m_ref[h][:, 0:1]
        lg_h = lg_new[:, h:h + 1]
        m_f = jnp.maximum(m_c, lg_h)
        a = jnp.exp(m_c - m_f)
        e_new = jnp.exp(lg_h - m_f)
        den = jnp.sum(l_ref[h], axis=1, keepdims=True) * a + e_new
        num = jnp.sum(acc_ref[h], axis=1, keepdims=True) * a + e_new * vnt_ref[:, h:h + 1]
        ot_ref[:, h:h + 1] = num / den


def _satt_reset(m_ref, l_ref, acc_ref):
    m_ref[...] = jnp.full(m_ref.shape, -jnp.inf, f32)
    l_ref[...] = jnp.zeros(l_ref.shape, f32)
    acc_ref[...] = jnp.zeros(acc_ref.shape, f32)


def _sample_retention_step(qr_ref, kr_ref, vr_ref, gr_ref, gng_ref, gam_ref, s0_ref, yr_ref, sn_ref):
    r_i = lax.broadcasted_iota(jnp.int32, (DK_R, DK_R), 0)
    c_i = lax.broadcasted_iota(jnp.int32, (DK_R, DK_R), 1)
    eye = jnp.where(r_i == c_i, 1.0, 0.0)
    for h in range(H_R):
        qrow = qr_ref[h:h + 1, :]
        krow = kr_ref[h:h + 1, :]
        vrow = vr_ref[h:h + 1, :]
        gam = gam_ref[h:h + 1, :]
        qcol = jnp.sum(eye * qrow, axis=1, keepdims=True)
        kcol = jnp.sum(eye * krow, axis=1, keepdims=True)
        st = s0_ref[h]
        qk = jnp.sum(qrow * krow, axis=1, keepdims=True)
        o = qk * vrow + gam * jnp.sum(qcol * st, axis=0, keepdims=True)
        sn_ref[h] = gam * st + kcol * vrow
        mu = jnp.mean(o, axis=1, keepdims=True)
        oc = o - mu
        var = jnp.mean(oc * oc, axis=1, keepdims=True)
        yn = oc * lax.rsqrt(var + GN_EPS) * gng_ref[h:h + 1, :]
        yr_ref[h:h + 1, :] = jax.nn.silu(gr_ref[h:h + 1, :]) * yn


def _satt_call(pt_flat, cache_k_t, cache_v_t, qt, knt, vnt, bias_pos, bias0, mask, snew, layer, npages, grp):
    db = qt.shape[0]
    ng = npages // grp
    per_seq = lambda d0, d1: pl.BlockSpec((None, d0, d1), lambda b, g, pt: (b, 0, 0))
    grid_spec = pltpu.PrefetchScalarGridSpec(
        num_scalar_prefetch=1,
        grid=(db, ng),
        in_specs=[
            pl.BlockSpec(memory_space=pl.ANY), pl.BlockSpec(memory_space=pl.ANY),
            per_seq(DH_A, H_A), per_seq(DH_A, H_A), per_seq(DH_A, H_A),
            pl.BlockSpec((H_A, grp, 1, PAGE_SIZE), lambda b, g, pt: (0, g, 0, 0)),
            pl.BlockSpec((1, H_A), lambda b, g, pt: (0, 0)),
            pl.BlockSpec((None, grp, 1, PAGE_SIZE), lambda b, g, pt: (b, g, 0, 0)),
            per_seq(1, LANES),
        ],
        out_specs=per_seq(DH_A, H_A),
        scratch_shapes=[
            pltpu.VMEM((2, grp, H_A, DH_A, PAGE_SIZE), f32),
            pltpu.VMEM((2, grp, H_A, DH_A, PAGE_SIZE), f32),
            pltpu.SemaphoreType.DMA((2, 2)),
            pltpu.VMEM((H_A, 1, PAGE_SIZE), f32),
            pltpu.VMEM((H_A, 1, PAGE_SIZE), f32),
            pltpu.VMEM((H_A, DH_A, PAGE_SIZE), f32),
        ],
    )
    return pl.pallas_call(
        functools.partial(_satt_kernel, layer, npages, grp),
        grid_spec=grid_spec,
        out_shape=jax.ShapeDtypeStruct((db, DH_A, H_A), f32),
        compiler_params=pltpu.CompilerParams(dimension_semantics=("arbitrary", "arbitrary"),
                                             vmem_limit_bytes=40 * MIB),
        name="sample_attend",
    )(pt_flat, cache_k_t, cache_v_t, qt, knt, vnt, bias_pos, bias0, mask, snew)


def _t5_bucket(rel):
    n = jnp.maximum(rel, 0)
    max_exact = NUM_BUCKETS // 2
    nf = jnp.maximum(n, 1).astype(f32)
    large = max_exact + (jnp.log(nf / max_exact) / math.log(MAX_DISTANCE / max_exact)
                         * (NUM_BUCKETS - max_exact)).astype(jnp.int32)
    large = jnp.minimum(large, NUM_BUCKETS - 1)
    return jnp.where(n < max_exact, n, large)


def _rotary_tables(pos):
    half = DK_R // 2
    freqs = ROPE_BASE ** (-jnp.arange(half, dtype=f32) / half)
    ang = pos.astype(f32)[:, None] * freqs[None, :]
    cos, sin = jnp.cos(ang), jnp.sin(ang)
    return jnp.concatenate([cos, cos], axis=1), jnp.concatenate([-sin, sin], axis=1)


def _decay_tables(chunk):
    lg = jnp.log1p(-jnp.exp2(-5.0 - jnp.arange(H_R, dtype=f32)))
    i = jnp.arange(chunk, dtype=f32)
    diff = i[:, None] - i[None, :]
    causal = diff >= 0
    intra = jnp.where(causal[None], jnp.exp(jnp.where(causal, diff, 0.0)[None] * lg[:, None, None]), 0.0)
    cross = jnp.exp((i[None, :] + 1.0) * lg[:, None])
    kv = jnp.exp((chunk - 1.0 - i)[None, :] * lg[:, None])
    cdec = jnp.exp(chunk * lg)
    bc = lambda a: jnp.broadcast_to(a[:, :, None], (H_R, chunk, LANES))
    return {"intra": intra, "cross": bc(cross), "kv": bc(kv),
            "chunk": jnp.broadcast_to(cdec[:, None, None], (H_R, 8, LANES))}


def _bias_tiles(rel_bias, tq):
    assert _last_bucket_from(tq + 1), "keys beyond the previous chunk must share the last bucket"
    rb = rel_bias.astype(f32) - rel_bias[NUM_BUCKETS - 1].astype(f32)[None, :]
    tab = rb[_t5_bucket(jnp.arange(2 * tq + 1, dtype=jnp.int32))].T
    m = 3 * tq
    g = jnp.concatenate([tab[:, tq::-1],
                         jnp.broadcast_to(tab[:, 0:1], (H_A, tq - 1)),
                         tab[:, 2 * tq:tq:-1]], axis=1)
    flat = jnp.tile(g, (1, tq))[:, :tq * (m - 1)]
    return flat.reshape(H_A, tq, m - 1)[:, :, :2 * tq]


def _last_bucket_from(n):
    large = NUM_BUCKETS // 2 + int(math.log(n / (NUM_BUCKETS // 2)) / math.log(MAX_DISTANCE / (NUM_BUCKETS // 2))
                                   * (NUM_BUCKETS - NUM_BUCKETS // 2) - 1e-3)
    return n >= NUM_BUCKETS // 2 and large >= NUM_BUCKETS - 1


def _layer_weights(l, ffn1_wg, ffn1_wu, ffn1_wd, ln1_g, ln1_b, w_in, ret_gn_g, w_out, ln2_g, ln2_b,
                   ffn2_wg, ffn2_wu, ffn2_wd, ln3_g, ln3_b):
    pts = np.cumsum((0,) + IN_SIZES)
    wi = w_in[l]
    col = lambda k: wi[:, pts[k]:pts[k + 1]]
    zeros = jnp.zeros((D_MODEL, LANES - H_IDX), wi.dtype)
    w_in2 = jnp.concatenate([col(0), col(1), col(2), col(3), col(4), col(4), col(5), zeros,
                             col(6), col(7), col(8), col(9)], axis=1).astype(bf16)
    r2 = lambda a: a[l].reshape(1, -1).astype(f32)
    return {
        "ffn1_wg": ffn1_wg[l].astype(bf16), "ffn1_wu": ffn1_wu[l].astype(bf16), "ffn1_wd": ffn1_wd[l].astype(bf16),
        "ln1_g": r2(ln1_g), "ln1_b": r2(ln1_b), "w_in": w_in2, "gng": r2(ret_gn_g),
        "w_out": w_out[l].astype(bf16), "ln2_g": r2(ln2_g), "ln2_b": r2(ln2_b),
        "ffn2_wg": ffn2_wg[l].astype(bf16), "ffn2_wu": ffn2_wu[l].astype(bf16), "ffn2_wd": ffn2_wd[l].astype(bf16),
        "ln3_g": r2(ln3_g), "ln3_b": r2(ln3_b),
    }


def kernel(x_prompt, x_sample, cache_k, cache_v, cache_kidx, state_ret, page_table, rel_bias,
           ffn1_wg, ffn1_wu, ffn1_wd, ln1_g, ln1_b, w_in, ret_gn_g, w_out,
           ln2_g, ln2_b, ffn2_wg, ffn2_wu, ffn2_wd, ln3_g, ln3_b):
    b, s, _ = x_prompt.shape
    db, ds, _ = x_sample.shape
    depth = w_in.shape[0]
    npages = page_table.shape[1]
    past = npages * PAGE_SIZE
    assert ds == 1, "the sample group decodes one token per sequence"
    alpha = (2 * depth) ** 0.25

    tq = min(ATT_TQ, s)
    chunk = min(RET_CHUNK, s)
    assert s % tq == 0 and s % chunk == 0 and tq % LANES == 0
    topk_p = min(TOPK_MAX, s // 4)
    topk_s = min(TOPK_MAX, (past + ds) // 4)

    rot_p = _rotary_tables(jnp.arange(s, dtype=jnp.int32))
    rot_s = _rotary_tables(jnp.full((db * ds,), past, jnp.int32))
    dec_p = _decay_tables(chunk)
    gam = jnp.broadcast_to(_decay_tables(1)["chunk"][:, 0, :], (H_R, LANES))
    bias_tiles = _bias_tiles(rel_bias, tq)
    near = min(past, 2 * PAGE_SIZE)
    assert _last_bucket_from(near + 1)
    bias_near = rel_bias.astype(f32)[_t5_bucket(jnp.arange(near, 0, -1, dtype=jnp.int32))].T
    bias_far = jnp.broadcast_to(rel_bias[NUM_BUCKETS - 1].astype(f32)[:, None], (H_A, past - near))
    bias_pos = jnp.concatenate([bias_far, bias_near], axis=1).reshape(H_A, npages, 1, PAGE_SIZE)
    bias0 = rel_bias[0:1].astype(f32)
    pt_flat = page_table.reshape(-1).astype(jnp.int32)
    grp = min(SAMPLE_PAGE_GROUP, npages)
    assert npages % grp == 0
    ckidx_t = jnp.transpose(cache_kidx, (0, 1, 3, 2))
    ck_t = jnp.transpose(cache_k, (0, 1, 3, 4, 2))
    cv_t = jnp.transpose(cache_v, (0, 1, 3, 4, 2))

    hp = x_prompt.reshape(b * s, D_MODEL)
    hs = x_sample.reshape(db * ds, D_MODEL)
    outs = {k: [] for k in ("kp", "vp", "kip", "sp", "ks", "vs", "kis", "ss")}
    for l in range(depth):
        w = _layer_weights(l, ffn1_wg, ffn1_wu, ffn1_wd, ln1_g, ln1_b, w_in, ret_gn_g, w_out, ln2_g, ln2_b,
                           ffn2_wg, ffn2_wu, ffn2_wd, ln3_g, ln3_b)
        ps = _pre_call(hs, w, rot_s[0], rot_s[1], alpha, DH_A ** -0.5, db * ds, f32, None, "pre_sample")
        q16 = jnp.pad(ps["qi"].reshape(db, H_IDX, DH_IDX), ((0, 0), (0, 16 - H_IDX), (0, 0)))
        w16 = jnp.pad((ps["wi"][:, :H_IDX] * H_IDX ** -0.5).reshape(db, H_IDX, 1), ((0, 0), (0, 16 - H_IDX), (0, 0)))
        t8 = lambda a: a.reshape(db, H_A, DH_A).transpose(0, 2, 1)
        r4 = lambda a: a.reshape(db, H_R, DK_R)
        ret_args = (r4(ps["qr"]), r4(ps["kr"]), r4(ps["vr"]), r4(ps["gr"]), w["gng"].reshape(H_R, DV_R), gam)
        sc_s, sn_s, yr_s, st_s = _sscore_call(pt_flat, ckidx_t, q16, w16, ps["ki"].reshape(db, 1, DH_IDX),
                                              *ret_args, state_ret, l, npages)
        mask, snew = _ssel_call(sc_s.reshape(db, past), sn_s.reshape(db, LANES), topk_s)
        mask = mask.reshape(db, npages, 1, PAGE_SIZE)
        snew = snew.reshape(db, 1, LANES)

        pp = _pre_call(hp, w, rot_p[0], rot_p[1], alpha, DH_A ** -0.5 * LOG2E, min(PRE_TM, s), bf16, s,
                       "pre_prompt")
        r3 = lambda a: a.reshape(b, s, a.shape[-1])
        oa = _attn_call(r3(pp["qa"]), r3(pp["qi"]), r3(pp["wi"]), r3(pp["kab"]), r3(pp["vab"]), r3(pp["ki2"]),
                        bias_tiles * LOG2E, topk_p, tq)
        yr, st_p = _ret_call(r3(pp["qr"]), r3(pp["kr"]), r3(pp["vr"]), r3(pp["gr"]), w["gng"], dec_p, chunk)
        steps = (b * s) // STREAM_POST_TM if (b * s) % STREAM_POST_TM == 0 else 0
        sps = steps // db if steps and steps % db == 0 else 0
        step_pages = npages // sps if sps and npages % sps == 0 else 0
        sgrp = min(STREAM_PAGE_GROUP, step_pages) if step_pages else 0
        if sgrp and step_pages % sgrp == 0:
            hp, ot_s = _post_stream_call(pp["h"], oa.reshape(b * s, W_A), yr.reshape(b * s, W_R), w, alpha,
                                         STREAM_POST_TM, pt_flat, ck_t, cv_t, t8(ps["qa"]), t8(ps["ka"]),
                                         t8(ps["va"]), bias_pos, bias0, mask, snew, l, npages, sgrp,
                                         step_pages // sgrp, "post_prompt")
        else:
            hp = _post_call(pp["h"], oa.reshape(b * s, W_A), yr.reshape(b * s, W_R), w, alpha,
                            min(POST_TM, b * s), "post_prompt")
            ot_s = _satt_call(pt_flat, ck_t, cv_t, t8(ps["qa"]), t8(ps["ka"]), t8(ps["va"]), bias_pos,
                              bias0, mask, snew, l, npages, grp)
        outs["kp"].append(pp["ka"].reshape(b, H_A, DH_A, s).transpose(0, 3, 1, 2))
        outs["vp"].append(pp["va"].reshape(b, H_A, DH_A, s).transpose(0, 3, 1, 2))
        outs["kip"].append(pp["ki"].transpose(0, 2, 1))
        outs["sp"].append(st_p)
        oa_s = ot_s.transpose(0, 2, 1).reshape(db, W_A)
        hs = _post_call(ps["h"], oa_s, yr_s.reshape(db, W_R), w, alpha, db * ds, "post_sample")
        outs["ks"].append(ps["ka"].reshape(db, ds, H_A, DH_A))
        outs["vs"].append(ps["va"].reshape(db, ds, H_A, DH_A))
        outs["kis"].append(ps["ki"].reshape(db, ds, DH_IDX))
        outs["ss"].append(st_s)

    stack = lambda k: jnp.stack(outs[k])
    return (hp.reshape(b, s, D_MODEL), hs.reshape(db, ds, D_MODEL),
            stack("kp"), stack("vp"), stack("kip"), stack("sp"),
            stack("ks"), stack("vs"), stack("kis"), stack("ss"))
```

```python
import functools
import math

import numpy as np
import jax
import jax.numpy as jnp
from jax import lax
from jax.experimental import pallas as pl
from jax.experimental.pallas import tpu as pltpu

D_MODEL = 1024
D_FF = 2816
PAGE_SIZE = 128
H_A = 8
DH_A = 64
W_A = H_A * DH_A
H_IDX = 8
DH_IDX = 64
TOPK_MAX = 256
NUM_BUCKETS = 32
MAX_DISTANCE = 128
H_R = 4
DK_R = 128
DV_R = 128
W_R = H_R * DV_R
ROPE_BASE = 10000.0
LN_EPS = 1e-5
GN_EPS = 1e-5
IN_SIZES = (W_A, W_A, W_A, H_IDX * DH_IDX, DH_IDX, H_IDX, H_R * DK_R, H_R * DK_R, W_R, W_R)

LANES = 128
MIB = 1024 * 1024
NEG = -1e30
LOG2E = math.log2(math.e)

FF_CHUNK = 512
PRE_TM = 512
POST_TM = 512
ATT_TQ = 256
RET_CHUNK = 512
BISECT_ITERS = 16
SAMPLE_PAGE_GROUP = 16
STREAM_POST_TM = 512
STREAM_PAGE_GROUP = 8
STREAM_ROUNDS = 4
ATT_FAR_WIDTH = 2

C_QA, C_KA, C_VA, C_QI, C_KI2, C_WI, C_QR, C_KR, C_VR, C_GR, C_END = (
    0, 512, 1024, 1536, 2048, 2176, 2304, 2816, 3328, 3840, 4352)

f32 = jnp.float32
bf16 = jnp.bfloat16
NT_DIMS = (((1,), (1,)), ((), ()))


def _const_spec(shape):
    nd = len(shape)
    return pl.BlockSpec(shape, lambda *_: (0,) * nd, pipeline_mode=pl.Buffered(1))


def _ln(x, g, b):
    mu = jnp.mean(x, axis=-1, keepdims=True)
    xc = x - mu
    var = jnp.mean(xc * xc, axis=-1, keepdims=True)
    return xc * lax.rsqrt(var + LN_EPS) * g + b


def _ffn(xb, wg_ref, wu_ref, wd_ref, hooks=()):
    acc = None
    starts = list(range(0, D_FF, FF_CHUNK))
    at = {-(-(k + 1) * len(starts) // (len(hooks) + 1)): hook for k, hook in enumerate(hooks)}
    assert len(at) == len(hooks) and all(0 < ci < len(starts) for ci in at)
    for ci, c0 in enumerate(starts):
        if ci in at:
            at[ci]()
        c1 = min(c0 + FF_CHUNK, D_FF)
        g = jnp.dot(xb, wg_ref[:, c0:c1], preferred_element_type=f32)
        u = jnp.dot(xb, wu_ref[:, c0:c1], preferred_element_type=f32)
        a = (jax.nn.silu(g) * u).astype(bf16)
        part = jnp.dot(a, wd_ref[c0:c1, :], preferred_element_type=f32)
        acc = part if acc is None else acc + part
    return acc


def _pre_kernel(alpha, q_scale, feature_major, x_ref, wg_ref, wu_ref, wd_ref, lng_ref, lnb_ref, win_ref,
                rc_ref, rs_ref,
                h_ref, qa_ref, ka_ref, va_ref, kab_ref, vab_ref, qi_ref, ki_ref, ki2_ref, wi_ref,
                qr_ref, kr_ref, vr_ref, gr_ref):
    act = qa_ref.dtype
    x = x_ref[...]
    f = _ffn(x.astype(bf16), wg_ref, wu_ref, wd_ref)
    h = _ln(alpha * x + 0.5 * f, lng_ref[...], lnb_ref[...])
    h_ref[...] = h
    hb = h.astype(bf16)

    def proj(c0, c1):
        return jnp.dot(hb, win_ref[:, c0:c1], preferred_element_type=f32)

    qa_ref[...] = (proj(C_QA, C_KA) * q_scale).astype(act)
    ka = proj(C_KA, C_VA)
    kab_ref[...] = ka.astype(act)
    va = proj(C_VA, C_QI)
    qi_ref[...] = (proj(C_QI, C_KI2) * DH_IDX ** -0.5).astype(act)
    kk = proj(C_KI2, C_WI)
    wi_ref[...] = proj(C_WI, C_QR)
    vab_ref[...] = va.astype(act)
    if feature_major:
        ka_ref[...] = ka.T
        va_ref[...] = va.T
        ki_ref[...] = kk.T[:DH_IDX, :]
    else:
        ka_ref[...] = ka
        va_ref[...] = va
        ki_ref[...] = kk[:, :DH_IDX]
    ki2_ref[...] = kk.astype(act)
    qr = proj(C_QR, C_KR)
    kr = proj(C_KR, C_VR)
    c = rc_ref[...]
    s = rs_ref[...]
    for hh in range(H_R):
        sl = slice(DK_R * hh, DK_R * (hh + 1))
        qh = qr[:, sl]
        kh = kr[:, sl]
        qr_ref[:, sl] = (qh * c + pltpu.roll(qh, DK_R // 2, 1) * s).astype(act)
        kr_ref[:, sl] = ((kh * c + pltpu.roll(kh, DK_R // 2, 1) * s) * DK_R ** -0.5).astype(act)
    vr_ref[...] = proj(C_VR, C_GR).astype(act)
    gr_ref[...] = proj(C_GR, C_END)


def _pre_call(x, w, rot_c, rot_s, alpha, q_scale, tm, act, seq_len, name):
    n = x.shape[0]
    grid = (pl.cdiv(n, tm),)
    row = lambda width: pl.BlockSpec((tm, width), lambda i: (i, 0))
    rot_blocks = rot_c.shape[0] // tm
    rot = pl.BlockSpec((tm, LANES), lambda i: (i % rot_blocks, 0))
    feature_major = seq_len is not None
    if feature_major:
        assert seq_len % tm == 0 and n % seq_len == 0
        seq_blocks = seq_len // tm
        kv_shape = lambda width: jax.ShapeDtypeStruct((n // seq_len, width, seq_len), f32)
        kv_spec = lambda width: pl.BlockSpec((None, width, tm), lambda i: (i // seq_blocks, 0, i % seq_blocks))
    else:
        kv_shape = lambda width: jax.ShapeDtypeStruct((n, width), f32)
        kv_spec = row
    in_specs = [
        row(D_MODEL),
        _const_spec((D_MODEL, D_FF)), _const_spec((D_MODEL, D_FF)), _const_spec((D_FF, D_MODEL)),
        _const_spec((1, D_MODEL)), _const_spec((1, D_MODEL)),
        _const_spec((D_MODEL, C_END)),
        rot, rot,
    ]
    outs = [
        ("h", D_MODEL, f32), ("qa", W_A, act), ("ka", W_A, f32), ("va", W_A, f32), ("kab", W_A, act),
        ("vab", W_A, act), ("qi", W_A, act), ("ki", DH_IDX, f32), ("ki2", LANES, act), ("wi", LANES, f32),
        ("qr", W_R, act), ("kr", W_R, act), ("vr", W_R, act), ("gr", W_R, f32),
    ]
    kv_names = ("ka", "va", "ki")
    out_shape = [kv_shape(wd) if k in kv_names else jax.ShapeDtypeStruct((n, wd), dt) for k, wd, dt in outs]
    out_specs = [kv_spec(wd) if k in kv_names else row(wd) for k, wd, _ in outs]
    res = pl.pallas_call(
        functools.partial(_pre_kernel, alpha, q_scale, feature_major),
        grid=grid, in_specs=in_specs, out_specs=out_specs, out_shape=out_shape,
        compiler_params=pltpu.CompilerParams(dimension_semantics=("arbitrary",), vmem_limit_bytes=58 * MIB),
        name=name,
    )(x, w["ffn1_wg"], w["ffn1_wu"], w["ffn1_wd"], w["ln1_g"], w["ln1_b"], w["w_in"], rot_c, rot_s)
    return {k: v for (k, _, _), v in zip(outs, res)}


def _post_kernel(alpha, h_ref, oa_ref, yr_ref, wo_ref, l2g_ref, l2b_ref, wg_ref, wu_ref, wd_ref,
                 l3g_ref, l3b_ref, out_ref):
    h = h_ref[...]
    mix = (jnp.dot(oa_ref[...].astype(bf16), wo_ref[0:W_A, :], preferred_element_type=f32)
           + jnp.dot(yr_ref[...].astype(bf16), wo_ref[W_A:W_A + W_R, :], preferred_element_type=f32))
    h2 = _ln(alpha * h + mix, l2g_ref[...], l2b_ref[...])
    f = _ffn(h2.astype(bf16), wg_ref, wu_ref, wd_ref)
    out_ref[...] = _ln(alpha * h2 + 0.5 * f, l3g_ref[...], l3b_ref[...])


def _post_stream_kernel(alpha, layer, npages, grp, units, parts, steps_per_seq, pt_ref,
                        h_ref, oa_ref, yr_ref, wo_ref, l2g_ref, l2b_ref, wg_ref, wu_ref, wd_ref, l3g_ref, l3b_ref,
                        ckk_ref, ckv_ref, qt_ref, knt_ref, vnt_ref, bias_ref, bias0_ref, mask_ref, snew_ref,
                        out_ref, ot_ref, kbuf, vbuf, sem, m_ref, l_ref, acc_ref):
    stream = _page_stream((layer, npages, grp, units, parts, steps_per_seq), pt_ref, ckk_ref, ckv_ref,
                          qt_ref, bias_ref, mask_ref, kbuf, vbuf, sem, m_ref, l_ref, acc_ref)

    @pl.when(stream.seq_part == 0)
    def _():
        _satt_reset(m_ref, l_ref, acc_ref)

    stream.begin()
    h = h_ref[...]
    mix = (jnp.dot(oa_ref[...].astype(bf16), wo_ref[0:W_A, :], preferred_element_type=f32)
           + jnp.dot(yr_ref[...].astype(bf16), wo_ref[W_A:W_A + W_R, :], preferred_element_type=f32))
    h2 = _ln(alpha * h + mix, l2g_ref[...], l2b_ref[...])
    stream.fold(0)
    f = _ffn(h2.astype(bf16), wg_ref, wu_ref, wd_ref, stream.hooks())
    out_ref[...] = _ln(alpha * h2 + 0.5 * f, l3g_ref[...], l3b_ref[...])

    @pl.when(stream.seq_part == steps_per_seq - 1)
    def _():
        _satt_finish(qt_ref, knt_ref, vnt_ref, bias0_ref, snew_ref, m_ref, l_ref, acc_ref, ot_ref)


class _page_stream:
    def __init__(self, cfg, pt_ref, ckk_ref, ckv_ref, qt_ref, bias_ref, mask_ref, kbuf, vbuf, sem,
                 m_ref, l_ref, acc_ref):
        self.layer, self.npages, self.grp, self.units, self.parts, self.steps_per_seq = cfg
        self.per_round = self.units // self.parts
        self.pt_ref, self.ckk_ref, self.ckv_ref = pt_ref, ckk_ref, ckv_ref
        self.qt_ref, self.bias_ref, self.mask_ref = qt_ref, bias_ref, mask_ref
        self.kbuf, self.vbuf, self.sem = kbuf, vbuf, sem
        self.state = (m_ref, l_ref, acc_ref)
        self.st = pl.program_id(0)
        self.nsteps = pl.num_programs(0)
        self.seq_part = self.st % self.steps_per_seq

    def _round(self, r):
        return range(r * self.per_round, (r + 1) * self.per_round)

    def _buf(self, u):
        return ((u // self.per_round) % 2) * self.per_round + u % self.per_round

    def _copies(self, step, u, p):
        seq, part = step // self.steps_per_seq, step % self.steps_per_seq
        page = self.pt_ref[seq * self.npages + (part * self.units + u) * self.grp + p]
        b = self._buf(u)
        return (pltpu.make_async_copy(self.ckk_ref.at[self.layer, page], self.kbuf.at[b, p], self.sem.at[b, 0]),
                pltpu.make_async_copy(self.ckv_ref.at[self.layer, page], self.vbuf.at[b, p], self.sem.at[b, 1]))

    def _start(self, step, r):
        for u in self._round(r):
            for p in range(self.grp):
                ck, cv = self._copies(step, u, p)
                ck.start()
                cv.start()

    def _wait(self, r):
        for u in self._round(r):
            for p in range(self.grp):
                ck, cv = self._copies(self.st, u, p)
                ck.wait()
                cv.wait()

    def fold(self, r):
        for u in self._round(r):
            b = self._buf(u)
            _satt_fold(self.kbuf.at[b], self.vbuf.at[b], self.qt_ref, self.bias_ref, self.mask_ref,
                       slice(u * self.grp, (u + 1) * self.grp), *self.state)

    def begin(self):
        if self.parts == 1:
            self._start(self.st, 0)
            self._wait(0)
            return
        assert self.parts % 2 == 0, "rounds alternate between two buffer sets"

        @pl.when(self.st == 0)
        def _():
            self._start(0, 0)
        self._wait(0)
        self._start(self.st, 1)

    def _hook(self, r):
        def run():
            self._wait(r)
            if r + 1 < self.parts:
                self._start(self.st, r + 1)
            else:
                @pl.when(self.st + 1 < self.nsteps)
                def _():
                    self._start(self.st + 1, 0)
            self.fold(r)
        return run

    def hooks(self):
        return tuple(self._hook(r) for r in range(1, self.parts))


def _post_stream_call(h, oa, yr, w, alpha, tm, pt_flat, cache_k_t, cache_v_t, qt, knt, vnt, bias_pos, bias0, mask,
                      snew, layer, npages, grp, units, parts, name):
    n = h.shape[0]
    db = qt.shape[0]
    nsteps = n // tm
    pages = units * grp
    steps_per_seq = npages // pages
    assert n % tm == 0 and npages % pages == 0 and nsteps == db * steps_per_seq and units % parts == 0
    nbuf = units if parts == 1 else 2 * (units // parts)
    row = lambda width: pl.BlockSpec((tm, width), lambda i, pt: (i, 0))
    per_seq = lambda d0, d1: pl.BlockSpec((None, d0, d1), lambda i, pt: (i // steps_per_seq, 0, 0))
    bias_spec = pl.BlockSpec((H_A, pages, 1, PAGE_SIZE), lambda i, pt: (0, i % steps_per_seq, 0, 0))
    mask_spec = pl.BlockSpec((None, pages, 1, PAGE_SIZE),
                             lambda i, pt: (i // steps_per_seq, i % steps_per_seq, 0, 0))
    grid_spec = pltpu.PrefetchScalarGridSpec(
        num_scalar_prefetch=1,
        grid=(nsteps,),
        in_specs=[
            row(D_MODEL), row(W_A), row(W_R),
            _const_spec((W_A + W_R, D_MODEL)), _const_spec((1, D_MODEL)), _const_spec((1, D_MODEL)),
            _const_spec((D_MODEL, D_FF)), _const_spec((D_MODEL, D_FF)), _const_spec((D_FF, D_MODEL)),
            _const_spec((1, D_MODEL)), _const_spec((1, D_MODEL)),
            pl.BlockSpec(memory_space=pl.ANY), pl.BlockSpec(memory_space=pl.ANY),
            per_seq(DH_A, H_A), per_seq(DH_A, H_A), per_seq(DH_A, H_A),
            bias_spec, pl.BlockSpec((1, H_A), lambda i, pt: (0, 0)), mask_spec, per_seq(1, LANES),
        ],
        out_specs=[row(D_MODEL), per_seq(DH_A, H_A)],
        scratch_shapes=[
            pltpu.VMEM((nbuf, grp, H_A, DH_A, PAGE_SIZE), f32),
            pltpu.VMEM((nbuf, grp, H_A, DH_A, PAGE_SIZE), f32),
            pltpu.SemaphoreType.DMA((nbuf, 2)),
            pltpu.VMEM((H_A, 1, PAGE_SIZE), f32),
            pltpu.VMEM((H_A, 1, PAGE_SIZE), f32),
            pltpu.VMEM((H_A, DH_A, PAGE_SIZE), f32),
        ],
    )
    return pl.pallas_call(
        functools.partial(_post_stream_kernel, alpha, layer, npages, grp, units, parts, steps_per_seq),
        grid_spec=grid_spec,
        out_shape=[jax.ShapeDtypeStruct((n, D_MODEL), f32), jax.ShapeDtypeStruct((db, DH_A, H_A), f32)],
        compiler_params=pltpu.CompilerParams(dimension_semantics=("arbitrary",), vmem_limit_bytes=58 * MIB),
        name=name,
    )(pt_flat, h, oa, yr, w["w_out"], w["ln2_g"], w["ln2_b"], w["ffn2_wg"], w["ffn2_wu"], w["ffn2_wd"],
      w["ln3_g"], w["ln3_b"], cache_k_t, cache_v_t, qt, knt, vnt, bias_pos, bias0, mask, snew)


def _post_call(h, oa, yr, w, alpha, tm, name):
    n = h.shape[0]
    row = lambda width: pl.BlockSpec((tm, width), lambda i: (i, 0))
    in_specs = [
        row(D_MODEL), row(W_A), row(W_R),
        _const_spec((W_A + W_R, D_MODEL)), _const_spec((1, D_MODEL)), _const_spec((1, D_MODEL)),
        _const_spec((D_MODEL, D_FF)), _const_spec((D_MODEL, D_FF)), _const_spec((D_FF, D_MODEL)),
        _const_spec((1, D_MODEL)), _const_spec((1, D_MODEL)),
    ]
    return pl.pallas_call(
        functools.partial(_post_kernel, alpha),
        grid=(pl.cdiv(n, tm),), in_specs=in_specs, out_specs=row(D_MODEL),
        out_shape=jax.ShapeDtypeStruct((n, D_MODEL), f32),
        compiler_params=pltpu.CompilerParams(dimension_semantics=("arbitrary",), vmem_limit_bytes=48 * MIB),
        name=name,
    )(h, oa, yr, w["w_out"], w["ln2_g"], w["ln2_b"], w["ffn2_wg"], w["ffn2_wu"], w["ffn2_wd"],
      w["ln3_g"], w["ln3_b"])


def _attn_kernel(topk, qa_ref, qi_ref, wi_ref, k_ref, v_ref, ki2_ref, bias_ref, oa_ref,
                    sc_ref, qap_ref, qip_ref, m_ref, l_ref, acc_ref):
    tq = qa_ref.shape[0]
    ngrp = tq // 8
    i = pl.program_id(1)
    nj = i + 1
    kf = float(topk)

    lane = lax.broadcasted_iota(jnp.int32, (tq, LANES), 1)
    lo_half = lane < DH_A
    for h in range(H_A):
        p, par = h // 2, h % 2
        keep = lo_half if par == 0 else jnp.logical_not(lo_half)
        blk = slice(LANES * p, LANES * (p + 1))
        qip_ref[h] = jnp.where(keep, qi_ref[:, blk], jnp.zeros((), qi_ref.dtype))
        qap_ref[p, par * tq:(par + 1) * tq, :] = jnp.where(keep, qa_ref[:, blk], jnp.zeros((), qa_ref.dtype))
    w_heads = wi_ref[...].T[0:H_IDX, :] * H_IDX ** -0.5

    def score_chunk(j, carry):
        off = pl.multiple_of(j * tq, tq)
        kj = ki2_ref[pl.ds(off, tq), :]
        acc = jnp.zeros((tq, tq), f32)
        for h in range(H_IDX):
            d = lax.dot_general(kj, qip_ref[h], NT_DIMS, preferred_element_type=f32)
            acc = acc + w_heads[h:h + 1, :] * jnp.maximum(d, 0.0)
        sc_ref[j] = acc
        return carry

    lax.fori_loop(0, nj, score_chunk, 0)

    krow = lax.broadcasted_iota(jnp.int32, (tq, tq), 0)
    qcol = lax.broadcasted_iota(jnp.int32, (tq, tq), 1)
    causal = krow <= qcol
    sd = sc_ref[i]
    mn_diag = jnp.min(jnp.where(causal, sd, jnp.inf), axis=0, keepdims=True)
    sc_ref[i] = jnp.where(causal, sd, -jnp.inf)

    def fold(fn, init, comb, n_chunks, t=None):
        t8 = None if t is None else jnp.broadcast_to(t, (8, tq))

        nacc = 4

        def body(j, accs):
            accs = list(accs)
            for r in range(ngrp):
                accs[r % nacc] = comb(accs[r % nacc], fn(sc_ref[j, 8 * r:8 * r + 8, :], t8))
            return tuple(accs)
        accs = lax.fori_loop(0, n_chunks, body, tuple(jnp.full((8, tq), init, f32) for _ in range(nacc)))
        acc = comb(comb(accs[0], accs[1]), comb(accs[2], accs[3]))
        if comb is jnp.add:
            return jnp.sum(acc, axis=0, keepdims=True)
        if comb is jnp.maximum:
            return jnp.max(acc, axis=0, keepdims=True)
        return jnp.min(acc, axis=0, keepdims=True)

    def count_ge(t):
        return fold(lambda s, t8: jnp.where(s >= t8, 1.0, 0.0), 0.0, jnp.add, nj, t)

    def count_gt(t):
        return fold(lambda s, t8: jnp.where(s > t8, 1.0, 0.0), 0.0, jnp.add, nj, t)

    def max_below(t):
        return fold(lambda s, t8: jnp.where(s < t8, s, -jnp.inf), -jnp.inf, jnp.maximum, nj, t)

    n_keys = (lax.broadcasted_iota(jnp.int32, (1, tq), 1) + (i * tq + 1)).astype(f32)
    take_all = n_keys <= kf

    @pl.when((i + 1) * tq <= topk)
    def _():
        def mk(j, carry):
            sc_ref[j] = jnp.zeros((tq, tq), f32)
            return carry
        lax.fori_loop(0, nj, mk, 0)

    @pl.when((i + 1) * tq > topk)
    def _():
        def minmax_body(j, c):
            mxs, mns = list(c[0]), list(c[1])
            for r in range(ngrp):
                v = sc_ref[j, 8 * r:8 * r + 8, :]
                mxs[r % 2] = jnp.maximum(mxs[r % 2], v)
                mns[r % 2] = jnp.minimum(mns[r % 2], v)
            return tuple(mxs), tuple(mns)
        full8 = lambda val: jnp.full((8, tq), val, f32)
        mxs, mns = lax.fori_loop(0, i, minmax_body, ((full8(-jnp.inf),) * 2, (full8(jnp.inf),) * 2))
        mx = jnp.maximum(jnp.max(jnp.maximum(mxs[0], mxs[1]), axis=0, keepdims=True),
                         jnp.max(sc_ref[i], axis=0, keepdims=True))
        mn = jnp.minimum(jnp.min(jnp.minimum(mns[0], mns[1]), axis=0, keepdims=True), mn_diag)
        c_max = count_ge(mx)
        done0 = jnp.logical_or(take_all, c_max >= kf)
        thr0 = jnp.where(take_all, -jnp.inf, mx)
        cge0 = jnp.where(take_all, n_keys, c_max)

        def bis(_, st):
            lo, hi = st
            mid = 0.5 * (lo + hi)
            ge = count_ge(mid) >= kf
            return jnp.where(ge, mid, lo), jnp.where(ge, hi, mid)

        lo, hi = lax.fori_loop(0, BISECT_ITERS, bis, (mn, mx))

        def snap_cond(st):
            return st[0] > 0.0

        def snap_body(st):
            _, hi, thr, cge, done = st
            m = max_below(hi)
            c = count_ge(m)
            ok = c >= kf
            newly = jnp.logical_and(ok, done < 0.5)
            thr = jnp.where(newly, m, thr)
            cge = jnp.where(newly, c, cge)
            hi = jnp.where(jnp.logical_or(done > 0.5, ok), hi, m)
            done = jnp.where(ok, 1.0, done)
            return jnp.sum(1.0 - done), hi, thr, cge, done

        done_f = jnp.where(done0, 1.0, 0.0)
        _, _, thr, cge, _ = lax.while_loop(snap_cond, snap_body, (jnp.sum(1.0 - done_f), hi, thr0, cge0, done_f))

        excess = jnp.sum(jnp.where(cge > kf, 1.0, 0.0)) > 0.0

        @pl.when(jnp.logical_not(excess))
        def _():
            def mk(j, carry):
                sc_ref[j] = jnp.where(sc_ref[j] >= thr, 0.0, NEG)
                return carry
            lax.fori_loop(0, nj, mk, 0)

        @pl.when(excess)
        def _():
            need = kf - count_gt(thr)
            lower = jnp.where(qcol <= krow, 1.0, 0.0).astype(bf16)

            def mk(j, seen):
                s = sc_ref[j]
                tie = s == thr
                rank = seen + jnp.dot(lower, jnp.where(tie, 1.0, 0.0).astype(bf16), preferred_element_type=f32)
                sel = jnp.logical_or(s > thr, jnp.logical_and(tie, rank <= need))
                sc_ref[j] = jnp.where(sel, 0.0, NEG)
                return rank[tq - 1:tq, :]
            lax.fori_loop(0, nj, mk, jnp.zeros((1, tq), f32))

    sc_ref[i] = jnp.where(causal, sc_ref[i], NEG)

    m_ref[...] = jnp.full(m_ref.shape, -jnp.inf, f32)
    l_ref[...] = jnp.zeros(l_ref.shape, f32)
    acc_ref[...] = jnp.zeros(acc_ref.shape, f32)
    ncb = tq // LANES

    def lane_blocks(x, comb):
        out = x[:, 0:LANES]
        for cb in range(1, x.shape[1] // LANES):
            out = comb(out, x[:, cb * LANES:(cb + 1) * LANES])
        return out

    def attend(j, width, bias_cols):
        off = pl.multiple_of(j * tq, tq)
        keys = pl.ds(off, width * tq)
        sel_mask = jnp.concatenate([sc_ref[j + c].T for c in range(width)], axis=1)
        for p in range(H_A // 2):
            blk = slice(LANES * p, LANES * (p + 1))
            s_pair = lax.dot_general(qap_ref[p], k_ref[keys, blk], NT_DIMS, preferred_element_type=f32)
            es = []
            for par in range(2):
                h = 2 * p + par
                s = s_pair[par * tq:(par + 1) * tq] + sel_mask
                if bias_cols is not None:
                    s = s + bias_ref[h, :, bias_cols]
                m_old = m_ref[h]
                row_max = jnp.max(lane_blocks(s, jnp.maximum), axis=1, keepdims=True)
                m_new = jnp.maximum(m_old, jnp.broadcast_to(row_max, (tq, LANES)))
                a = jnp.exp2(m_old - m_new)
                e = jnp.exp2(s - jnp.concatenate([m_new] * (width * ncb), axis=1))
                l_ref[h] = l_ref[h] * a + lane_blocks(e, jnp.add)
                acc_ref[h] = acc_ref[h] * a
                m_ref[h] = m_new
                es.append(e.astype(bf16))
            pv = jnp.dot(jnp.concatenate(es, axis=0), v_ref[keys, blk], preferred_element_type=f32)
            acc_ref[2 * p] += pv[0:tq]
            acc_ref[2 * p + 1] += pv[tq:2 * tq]

    n_far = jnp.maximum(i - 1, 0)
    if ATT_FAR_WIDTH == 2:
        def far_pair(jj, carry):
            attend(2 * jj, 2, None)
            return carry
        lax.fori_loop(0, n_far // 2, far_pair, 0)

        @pl.when(n_far % 2 == 1)
        def _():
            attend(n_far - 1, 1, None)
    else:
        def far_one(j, carry):
            attend(j, 1, None)
            return carry
        lax.fori_loop(0, n_far, far_one, 0)

    @pl.when(i >= 1)
    def _():
        attend(i - 1, 2, slice(0, 2 * tq))

    @pl.when(i == 0)
    def _():
        attend(0, 1, slice(tq, 2 * tq))

    for p in range(H_A // 2):
        l_even = jnp.broadcast_to(jnp.sum(l_ref[2 * p], axis=1, keepdims=True), (tq, LANES))
        l_odd = jnp.broadcast_to(jnp.sum(l_ref[2 * p + 1], axis=1, keepdims=True), (tq, LANES))
        o_pair = jnp.where(lo_half, acc_ref[2 * p] / l_even, acc_ref[2 * p + 1] / l_odd)
        oa_ref[:, LANES * p:LANES * (p + 1)] = o_pair.astype(oa_ref.dtype)


def _attn_call(qa, qi, wi, kab, vab, ki2, bias_tiles, topk, tq):
    b, s, _ = qa.shape
    nq = s // tq
    qspec = lambda width: pl.BlockSpec((None, tq, width), lambda bi, i: (bi, i, 0))
    whole = lambda shape: pl.BlockSpec((None,) + shape, lambda bi, i: (bi,) + (0,) * len(shape),
                                       pipeline_mode=pl.Buffered(1))
    scratch = [
        pltpu.VMEM((nq, tq, tq), f32),
        pltpu.VMEM((H_A // 2, 2 * tq, LANES), qa.dtype),
        pltpu.VMEM((H_IDX, tq, LANES), qi.dtype),
        pltpu.VMEM((H_A, tq, LANES), f32),
        pltpu.VMEM((H_A, tq, LANES), f32),
        pltpu.VMEM((H_A, tq, LANES), f32),
    ]
    return pl.pallas_call(
        functools.partial(_attn_kernel, topk),
        grid=(b, nq),
        in_specs=[qspec(W_A), qspec(W_A),
                  qspec(LANES),
                  whole((s, W_A)), whole((s, W_A)), whole((s, LANES)),
                  _const_spec((H_A, tq, 2 * tq))],
        out_specs=qspec(W_A),
        out_shape=jax.ShapeDtypeStruct((b, s, W_A), bf16),
        scratch_shapes=scratch,
        compiler_params=pltpu.CompilerParams(dimension_semantics=("arbitrary", "arbitrary"),
                                             vmem_limit_bytes=48 * MIB),
        name="attn_prompt",
    )(qa, qi, wi, kab, vab, ki2, bias_tiles)


def _ret_kernel(qr_ref, kr_ref, vr_ref, gr_ref, gng_ref, dmat_ref, cross_ref, kvd_ref, cdec_ref,
                yr_ref, st_ref):
    c = pl.program_id(1)

    @pl.when(c == 0)
    def _():
        st_ref[...] = jnp.zeros(st_ref.shape, f32)

    for h in range(H_R):
        sl = slice(DK_R * h, DK_R * (h + 1))
        q = qr_ref[:, sl]
        k = kr_ref[:, sl]
        v = vr_ref[:, sl]
        att = lax.dot_general(q, k, NT_DIMS, preferred_element_type=f32) * dmat_ref[h]
        st = st_ref[h]
        o = (jnp.dot(att.astype(bf16), v, preferred_element_type=f32)
             + jnp.dot(q, st.astype(bf16), preferred_element_type=f32) * cross_ref[h])
        kd = (k.astype(f32) * kvd_ref[h]).T.astype(bf16)
        st_ref[h] = cdec_ref[h, 0:1, :] * st + jnp.dot(kd, v, preferred_element_type=f32)
        mu = jnp.mean(o, axis=-1, keepdims=True)
        oc = o - mu
        var = jnp.mean(oc * oc, axis=-1, keepdims=True)
        yn = oc * lax.rsqrt(var + GN_EPS) * gng_ref[:, sl]
        yr_ref[:, sl] = (jax.nn.silu(gr_ref[:, sl]) * yn).astype(yr_ref.dtype)


def _ret_call(qr, kr, vr, gr, gng, dec, chunk):
    b, s, _ = qr.shape
    nc = s // chunk
    rspec = pl.BlockSpec((None, chunk, W_R), lambda bi, c: (bi, c, 0))
    return pl.pallas_call(
        _ret_kernel,
        grid=(b, nc),
        in_specs=[rspec, rspec, rspec, rspec, _const_spec((1, W_R)),
                  _const_spec((H_R, chunk, chunk)), _const_spec((H_R, chunk, LANES)),
                  _const_spec((H_R, chunk, LANES)), _const_spec((H_R, 8, LANES))],
        out_specs=[rspec, pl.BlockSpec((None, H_R, DK_R, DV_R), lambda bi, c: (bi, 0, 0, 0))],
        out_shape=[jax.ShapeDtypeStruct((b, s, W_R), bf16), jax.ShapeDtypeStruct((b, H_R, DK_R, DV_R), f32)],
        compiler_params=pltpu.CompilerParams(dimension_semantics=("arbitrary", "arbitrary"),
                                             vmem_limit_bytes=32 * MIB),
        name="ret_prompt",
    )(qr, kr, vr, gr, gng, dec["intra"], dec["cross"], dec["kv"], dec["chunk"])


def _sscore_kernel(layer, npages, pt_ref, ck_ref, q_ref, w_ref, kn_ref,
                   qr_ref, kr_ref, vr_ref, gr_ref, gng_ref, gam_ref, s0_ref,
                   sc_ref, snew_ref, yr_ref, sn_ref, kbuf, sem):
    b = pl.program_id(0)
    nb = pl.num_programs(0)
    slot = b % 2

    def page_copy(seq, p, sl):
        page = pt_ref[seq * npages + p]
        return pltpu.make_async_copy(ck_ref.at[layer, page], kbuf.at[sl, p], sem.at[sl])

    def start_all(seq, sl):
        def body(p, carry):
            page_copy(seq, p, sl).start()
            return carry
        lax.fori_loop(0, npages, body, 0)

    @pl.when(b == 0)
    def _():
        start_all(0, 0)

    @pl.when(b + 1 < nb)
    def _():
        start_all(b + 1, 1 - slot)

    _sample_retention_step(qr_ref, kr_ref, vr_ref, gr_ref, gng_ref, gam_ref, s0_ref, yr_ref, sn_ref)

    def wait_body(p, carry):
        page_copy(b, p, slot).wait()
        return carry
    lax.fori_loop(0, npages, wait_body, 0)

    q = q_ref[...].astype(bf16)
    w = w_ref[...]

    pages_per_dot = min(8, npages)
    for p0 in range(0, npages, pages_per_dot):
        kw = jnp.concatenate([kbuf[slot, p] for p in range(p0, p0 + pages_per_dot)], axis=1).astype(bf16)
        d = jnp.dot(q, kw, preferred_element_type=f32)
        sc_ref[:, p0 * PAGE_SIZE:(p0 + pages_per_dot) * PAGE_SIZE] = jnp.sum(
            w * jnp.maximum(d, 0.0), axis=0, keepdims=True)

    kn = kn_ref[...].astype(bf16).astype(f32)
    dn = jnp.sum(q.astype(f32) * kn, axis=1, keepdims=True)
    s_new = jnp.sum(w * jnp.maximum(dn, 0.0), axis=0, keepdims=True)
    snew_ref[...] = jnp.broadcast_to(s_new, (1, LANES))


def _sscore_call(page_table_flat, cache_kidx_t, q16, w16, kn, qr, kr, vr, gr, gng, gam, state, layer, npages):
    db = q16.shape[0]
    assert npages % min(8, npages) == 0
    per_seq = lambda d0, d1: pl.BlockSpec((None, d0, d1), lambda b, pt: (b, 0, 0))
    shared = pl.BlockSpec((H_R, DV_R), lambda b, pt: (0, 0))
    grid_spec = pltpu.PrefetchScalarGridSpec(
        num_scalar_prefetch=1,
        grid=(db,),
        in_specs=[
            pl.BlockSpec(memory_space=pl.ANY),
            per_seq(16, DH_IDX), per_seq(16, 1), per_seq(1, DH_IDX),
            per_seq(H_R, DK_R), per_seq(H_R, DK_R), per_seq(H_R, DV_R), per_seq(H_R, DV_R), shared, shared,
            pl.BlockSpec((None, None, H_R, DK_R, DV_R), lambda b, pt: (layer, b, 0, 0, 0)),
        ],
        out_specs=[per_seq(1, npages * PAGE_SIZE), per_seq(1, LANES), per_seq(H_R, DV_R),
                   pl.BlockSpec((None, H_R, DK_R, DV_R), lambda b, pt: (b, 0, 0, 0))],
        scratch_shapes=[
            pltpu.VMEM((2, npages, DH_IDX, PAGE_SIZE), f32),
            pltpu.SemaphoreType.DMA((2,)),
        ],
    )
    return pl.pallas_call(
        functools.partial(_sscore_kernel, layer, npages),
        grid_spec=grid_spec,
        out_shape=[jax.ShapeDtypeStruct((db, 1, npages * PAGE_SIZE), f32),
                   jax.ShapeDtypeStruct((db, 1, LANES), f32),
                   jax.ShapeDtypeStruct((db, H_R, DV_R), f32), jax.ShapeDtypeStruct((db, H_R, DK_R, DV_R), f32)],
        compiler_params=pltpu.CompilerParams(dimension_semantics=("arbitrary",), vmem_limit_bytes=32 * MIB),
        name="sample_scores",
    )(page_table_flat, cache_kidx_t, q16, w16, kn, qr, kr, vr, gr, gng, gam, state)


PREFIX_CHUNK = 256


def _ssel_kernel(topk, sc_ref, sn_ref, mask_ref, selnew_ref):
    kf = float(topk)
    db, length = sc_ref.shape
    sc = sc_ref[...]
    s_new = sn_ref[:, 0:1]

    def rsum(x):
        return jnp.sum(x, axis=1, keepdims=True)

    def count(cmp, t):
        return rsum(jnp.where(cmp(sc, t), 1.0, 0.0)) + jnp.where(cmp(s_new, t), 1.0, 0.0)

    ge = lambda a, t: a >= t
    gt = lambda a, t: a > t
    mx = jnp.maximum(jnp.max(sc, axis=1, keepdims=True), s_new)
    mn = jnp.minimum(jnp.min(sc, axis=1, keepdims=True), s_new)
    c_max = count(ge, mx)
    done0 = jnp.where(c_max >= kf, 1.0, 0.0)

    def bis(_, st):
        lo, hi = st
        mid = 0.5 * (lo + hi)
        ok = count(ge, mid) >= kf
        return jnp.where(ok, mid, lo), jnp.where(ok, hi, mid)
    lo, hi = lax.fori_loop(0, BISECT_ITERS, bis, (mn, mx))

    def snap_cond(st):
        return st[0] > 0.0

    def snap_body(st):
        _, hi, thr, done = st
        below = jnp.maximum(jnp.max(jnp.where(sc < hi, sc, -jnp.inf), axis=1, keepdims=True),
                            jnp.where(s_new < hi, s_new, -jnp.inf))
        ok = count(ge, below) >= kf
        newly = jnp.logical_and(ok, done < 0.5)
        thr = jnp.where(newly, below, thr)
        hi = jnp.where(jnp.logical_or(done > 0.5, ok), hi, below)
        done = jnp.where(ok, 1.0, done)
        return jnp.sum(1.0 - done), hi, thr, done
    _, _, thr, _ = lax.while_loop(snap_cond, snap_body, (jnp.sum(1.0 - done0), hi, mx, done0))

    need = kf - count(gt, thr)
    pc = min(PREFIX_CHUNK, length)
    r_i = lax.broadcasted_iota(jnp.int32, (pc, pc), 0)
    c_i = lax.broadcasted_iota(jnp.int32, (pc, pc), 1)
    upper = jnp.where(r_i <= c_i, 1.0, 0.0).astype(bf16)
    seen = jnp.zeros((db, 1), f32)
    for c0 in range(0, length, pc):
        s_c = sc[:, c0:c0 + pc]
        tie = s_c == thr
        rank = seen + jnp.dot(jnp.where(tie, 1.0, 0.0).astype(bf16), upper, preferred_element_type=f32)
        sel = jnp.logical_or(s_c > thr, jnp.logical_and(tie, rank <= need))
        mask_ref[:, c0:c0 + pc] = jnp.where(sel, 0.0, NEG)
        seen = rank[:, pc - 1:pc]
    sel_new = jnp.logical_or(s_new > thr, jnp.logical_and(s_new == thr, seen + 1.0 <= need))
    selnew_ref[...] = jnp.broadcast_to(jnp.where(sel_new, 1.0, 0.0), (db, LANES))


def _ssel_call(scores, s_new, topk):
    db, length = scores.shape
    assert length % min(PREFIX_CHUNK, length) == 0
    return pl.pallas_call(
        functools.partial(_ssel_kernel, topk),
        out_shape=[jax.ShapeDtypeStruct((db, length), f32), jax.ShapeDtypeStruct((db, LANES), f32)],
        compiler_params=pltpu.CompilerParams(vmem_limit_bytes=32 * MIB),
        name="sample_select",
    )(scores, s_new)


def _satt_kernel(layer, npages, grp, pt_ref,
                 ckk_ref, ckv_ref, qt_ref, knt_ref, vnt_ref, bias_ref, bias0_ref, mask_ref, snew_ref,
                 ot_ref, kbuf, vbuf, sem, m_ref, l_ref, acc_ref):
    b = pl.program_id(0)
    g = pl.program_id(1)
    nb = pl.num_programs(0)
    ng = pl.num_programs(1)
    t = b * ng + g
    slot = t % 2

    def page_copies(seq, gi, p, sl):
        page = pt_ref[seq * npages + gi * grp + p]
        return (pltpu.make_async_copy(ckk_ref.at[layer, page], kbuf.at[sl, p], sem.at[sl, 0]),
                pltpu.make_async_copy(ckv_ref.at[layer, page], vbuf.at[sl, p], sem.at[sl, 1]))

    def start_all(seq, gi, sl):
        def body(p, carry):
            ck, cv = page_copies(seq, gi, p, sl)
            ck.start()
            cv.start()
            return carry
        lax.fori_loop(0, grp, body, 0)

    @pl.when(t == 0)
    def _():
        start_all(0, 0, 0)

    @pl.when(t + 1 < nb * ng)
    def _():
        wrap = g + 1 == ng
        start_all(jnp.where(wrap, b + 1, b), jnp.where(wrap, 0, g + 1), 1 - slot)

    @pl.when(g == 0)
    def _():
        m_ref[...] = jnp.full(m_ref.shape, -jnp.inf, f32)
        l_ref[...] = jnp.zeros(l_ref.shape, f32)
        acc_ref[...] = jnp.zeros(acc_ref.shape, f32)

    def wait_body(p, carry):
        ck, cv = page_copies(b, g, p, slot)
        ck.wait()
        cv.wait()
        return carry
    lax.fori_loop(0, grp, wait_body, 0)

    _satt_fold(kbuf.at[slot], vbuf.at[slot], qt_ref, bias_ref, mask_ref, slice(0, grp), m_ref, l_ref, acc_ref)

    @pl.when(g == ng - 1)
    def _():
        _satt_finish(qt_ref, knt_ref, vnt_ref, bias0_ref, snew_ref, m_ref, l_ref, acc_ref, ot_ref)


def _satt_fold(kb_ref, vb_ref, qt_ref, bias_ref, mask_ref, rows, m_ref, l_ref, acc_ref):
    mask = mask_ref[rows]
    for h in range(H_A):
        qc = qt_ref[:, h:h + 1]
        kh = kb_ref[:, h]
        lg = jnp.sum(kh * qc[None], axis=1, keepdims=True) + bias_ref[h, rows] + mask
        m_old = m_ref[h]
        m_blk = jnp.max(jnp.max(lg, axis=0), axis=1, keepdims=True)
        m_new = jnp.maximum(m_old, m_blk)
        a = jnp.exp(m_old - m_new)
        e = jnp.exp(lg - m_new[None])
        l_ref[h] = l_ref[h] * a + jnp.sum(e, axis=0)
        acc_ref[h] = acc_ref[h] * a + jnp.sum(vb_ref[:, h] * e, axis=0)
        m_ref[h] = m_new


def _satt_finish(qt_ref, knt_ref, vnt_ref, bias0_ref, snew_ref, m_ref, l_ref, acc_ref, ot_ref):
    lg_new = jnp.sum(qt_ref[...] * knt_ref[...], axis=0, keepdims=True) + bias0_ref[...]
    lg_new = jnp.where(snew_ref[0:1, 0:1] > 0.5, lg_new, NEG)
    for h in range(H_A):
        m_c = m_ref[h][:, 0:1]
        lg_h = lg_new[:, h:h + 1]
        m_f = jnp.maximum(m_c, lg_h)
        a = jnp.exp(m_c - m_f)
        e_new = jnp.exp(lg_h - m_f)
        den = jnp.sum(l_ref[h], axis=1, keepdims=True) * a + e_new
        num = jnp.sum(acc_ref[h], axis=1, keepdims=True) * a + e_new * vnt_ref[:, h:h + 1]
        ot_ref[:, h:h + 1] = num / den


def _satt_reset(m_ref, l_ref, acc_ref):
    m_ref[...] = jnp.full(m_ref.shape, -jnp.inf, f32)
    l_ref[...] = jnp.zeros(l_ref.shape, f32)
    acc_ref[...] = jnp.zeros(acc_ref.shape, f32)


def _sample_retention_step(qr_ref, kr_ref, vr_ref, gr_ref, gng_ref, gam_ref, s0_ref, yr_ref, sn_ref):
    r_i = lax.broadcasted_iota(jnp.int32, (DK_R, DK_R), 0)
    c_i = lax.broadcasted_iota(jnp.int32, (DK_R, DK_R), 1)
    eye = jnp.where(r_i == c_i, 1.0, 0.0)
    for h in range(H_R):
        qrow = qr_ref[h:h + 1, :]
        krow = kr_ref[h:h + 1, :]
        vrow = vr_ref[h:h + 1, :]
        gam = gam_ref[h:h + 1, :]
        qcol = jnp.sum(eye * qrow, axis=1, keepdims=True)
        kcol = jnp.sum(eye * krow, axis=1, keepdims=True)
        st = s0_ref[h]
        qk = jnp.sum(qrow * krow, axis=1, keepdims=True)
        o = qk * vrow + gam * jnp.sum(qcol * st, axis=0, keepdims=True)
        sn_ref[h] = gam * st + kcol * vrow
        mu = jnp.mean(o, axis=1, keepdims=True)
        oc = o - mu
        var = jnp.mean(oc * oc, axis=1, keepdims=True)
        yn = oc * lax.rsqrt(var + GN_EPS) * gng_ref[h:h + 1, :]
        yr_ref[h:h + 1, :] = jax.nn.silu(gr_ref[h:h + 1, :]) * yn


def _satt_call(pt_flat, cache_k_t, cache_v_t, qt, knt, vnt, bias_pos, bias0, mask, snew, layer, npages, grp):
    db = qt.shape[0]
    ng = npages // grp
    per_seq = lambda d0, d1: pl.BlockSpec((None, d0, d1), lambda b, g, pt: (b, 0, 0))
    grid_spec = pltpu.PrefetchScalarGridSpec(
        num_scalar_prefetch=1,
        grid=(db, ng),
        in_specs=[
            pl.BlockSpec(memory_space=pl.ANY), pl.BlockSpec(memory_space=pl.ANY),
            per_seq(DH_A, H_A), per_seq(DH_A, H_A), per_seq(DH_A, H_A),
            pl.BlockSpec((H_A, grp, 1, PAGE_SIZE), lambda b, g, pt: (0, g, 0, 0)),
            pl.BlockSpec((1, H_A), lambda b, g, pt: (0, 0)),
            pl.BlockSpec((None, grp, 1, PAGE_SIZE), lambda b, g, pt: (b, g, 0, 0)),
            per_seq(1, LANES),
        ],
        out_specs=per_seq(DH_A, H_A),
        scratch_shapes=[
            pltpu.VMEM((2, grp, H_A, DH_A, PAGE_SIZE), f32),
            pltpu.VMEM((2, grp, H_A, DH_A, PAGE_SIZE), f32),
            pltpu.SemaphoreType.DMA((2, 2)),
            pltpu.VMEM((H_A, 1, PAGE_SIZE), f32),
            pltpu.VMEM((H_A, 1, PAGE_SIZE), f32),
            pltpu.VMEM((H_A, DH_A, PAGE_SIZE), f32),
        ],
    )
    return pl.pallas_call(
        functools.partial(_satt_kernel, layer, npages, grp),
        grid_spec=grid_spec,
        out_shape=jax.ShapeDtypeStruct((db, DH_A, H_A), f32),
        compiler_params=pltpu.CompilerParams(dimension_semantics=("arbitrary", "arbitrary"),
                                             vmem_limit_bytes=40 * MIB),
        name="sample_attend",
    )(pt_flat, cache_k_t, cache_v_t, qt, knt, vnt, bias_pos, bias0, mask, snew)


def _t5_bucket(rel):
    n = jnp.maximum(rel, 0)
    max_exact = NUM_BUCKETS // 2
    nf = jnp.maximum(n, 1).astype(f32)
    large = max_exact + (jnp.log(nf / max_exact) / math.log(MAX_DISTANCE / max_exact)
                         * (NUM_BUCKETS - max_exact)).astype(jnp.int32)
    large = jnp.minimum(large, NUM_BUCKETS - 1)
    return jnp.where(n < max_exact, n, large)


def _rotary_tables(pos):
    half = DK_R // 2
    freqs = ROPE_BASE ** (-jnp.arange(half, dtype=f32) / half)
    ang = pos.astype(f32)[:, None] * freqs[None, :]
    cos, sin = jnp.cos(ang), jnp.sin(ang)
    return jnp.concatenate([cos, cos], axis=1), jnp.concatenate([-sin, sin], axis=1)


def _decay_tables(chunk):
    lg = jnp.log1p(-jnp.exp2(-5.0 - jnp.arange(H_R, dtype=f32)))
    i = jnp.arange(chunk, dtype=f32)
    diff = i[:, None] - i[None, :]
    causal = diff >= 0
    intra = jnp.where(causal[None], jnp.exp(jnp.where(causal, diff, 0.0)[None] * lg[:, None, None]), 0.0)
    cross = jnp.exp((i[None, :] + 1.0) * lg[:, None])
    kv = jnp.exp((chunk - 1.0 - i)[None, :] * lg[:, None])
    cdec = jnp.exp(chunk * lg)
    bc = lambda a: jnp.broadcast_to(a[:, :, None], (H_R, chunk, LANES))
    return {"intra": intra, "cross": bc(cross), "kv": bc(kv),
            "chunk": jnp.broadcast_to(cdec[:, None, None], (H_R, 8, LANES))}


def _bias_tiles(rel_bias, tq):
    assert _last_bucket_from(tq + 1), "keys beyond the previous chunk must share the last bucket"
    rb = rel_bias.astype(f32) - rel_bias[NUM_BUCKETS - 1].astype(f32)[None, :]
    tab = rb[_t5_bucket(jnp.arange(2 * tq + 1, dtype=jnp.int32))].T
    m = 3 * tq
    g = jnp.concatenate([tab[:, tq::-1],
                         jnp.broadcast_to(tab[:, 0:1], (H_A, tq - 1)),
                         tab[:, 2 * tq:tq:-1]], axis=1)
    flat = jnp.tile(g, (1, tq))[:, :tq * (m - 1)]
    return flat.reshape(H_A, tq, m - 1)[:, :, :2 * tq]


def _last_bucket_from(n):
    large = NUM_BUCKETS // 2 + int(math.log(n / (NUM_BUCKETS // 2)) / math.log(MAX_DISTANCE / (NUM_BUCKETS // 2))
                                   * (NUM_BUCKETS - NUM_BUCKETS // 2) - 1e-3)
    return n >= NUM_BUCKETS // 2 and large >= NUM_BUCKETS - 1


def _layer_weights(l, ffn1_wg, ffn1_wu, ffn1_wd, ln1_g, ln1_b, w_in, ret_gn_g, w_out, ln2_g, ln2_b,
                   ffn2_wg, ffn2_wu, ffn2_wd, ln3_g, ln3_b):
    pts = np.cumsum((0,) + IN_SIZES)
    wi = w_in[l]
    col = lambda k: wi[:, pts[k]:pts[k + 1]]
    zeros = jnp.zeros((D_MODEL, LANES - H_IDX), wi.dtype)
    w_in2 = jnp.concatenate([col(0), col(1), col(2), col(3), col(4), col(4), col(5), zeros,
                             col(6), col(7), col(8), col(9)], axis=1).astype(bf16)
    r2 = lambda a: a[l].reshape(1, -1).astype(f32)
    return {
        "ffn1_wg": ffn1_wg[l].astype(bf16), "ffn1_wu": ffn1_wu[l].astype(bf16), "ffn1_wd": ffn1_wd[l].astype(bf16),
        "ln1_g": r2(ln1_g), "ln1_b": r2(ln1_b), "w_in": w_in2, "gng": r2(ret_gn_g),
        "w_out": w_out[l].astype(bf16), "ln2_g": r2(ln2_g), "ln2_b": r2(ln2_b),
        "ffn2_wg": ffn2_wg[l].astype(bf16), "ffn2_wu": ffn2_wu[l].astype(bf16), "ffn2_wd": ffn2_wd[l].astype(bf16),
        "ln3_g": r2(ln3_g), "ln3_b": r2(ln3_b),
    }


def kernel(x_prompt, x_sample, cache_k, cache_v, cache_kidx, state_ret, page_table, rel_bias,
           ffn1_wg, ffn1_wu, ffn1_wd, ln1_g, ln1_b, w_in, ret_gn_g, w_out,
           ln2_g, ln2_b, ffn2_wg, ffn2_wu, ffn2_wd, ln3_g, ln3_b):
    b, s, _ = x_prompt.shape
    db, ds, _ = x_sample.shape
    depth = w_in.shape[0]
    npages = page_table.shape[1]
    past = npages * PAGE_SIZE
    assert ds == 1, "the sample group decodes one token per sequence"
    alpha = (2 * depth) ** 0.25

    tq = min(ATT_TQ, s)
    chunk = min(RET_CHUNK, s)
    assert s % tq == 0 and s % chunk == 0 and tq % LANES == 0
    topk_p = min(TOPK_MAX, s // 4)
    topk_s = min(TOPK_MAX, (past + ds) // 4)

    rot_p = _rotary_tables(jnp.arange(s, dtype=jnp.int32))
    rot_s = _rotary_tables(jnp.full((db * ds,), past, jnp.int32))
    dec_p = _decay_tables(chunk)
    gam = jnp.broadcast_to(_decay_tables(1)["chunk"][:, 0, :], (H_R, LANES))
    bias_tiles = _bias_tiles(rel_bias, tq)
    near = min(past, 2 * PAGE_SIZE)
    assert _last_bucket_from(near + 1)
    bias_near = rel_bias.astype(f32)[_t5_bucket(jnp.arange(near, 0, -1, dtype=jnp.int32))].T
    bias_far = jnp.broadcast_to(rel_bias[NUM_BUCKETS - 1].astype(f32)[:, None], (H_A, past - near))
    bias_pos = jnp.concatenate([bias_far, bias_near], axis=1).reshape(H_A, npages, 1, PAGE_SIZE)
    bias0 = rel_bias[0:1].astype(f32)
    pt_flat = page_table.reshape(-1).astype(jnp.int32)
    grp = min(SAMPLE_PAGE_GROUP, npages)
    assert npages % grp == 0
    ckidx_t = jnp.transpose(cache_kidx, (0, 1, 3, 2))
    ck_t = jnp.transpose(cache_k, (0, 1, 3, 4, 2))
    cv_t = jnp.transpose(cache_v, (0, 1, 3, 4, 2))

    hp = x_prompt.reshape(b * s, D_MODEL)
    hs = x_sample.reshape(db * ds, D_MODEL)
    outs = {k: [] for k in ("kp", "vp", "kip", "sp", "ks", "vs", "kis", "ss")}
    for l in range(depth):
        w = _layer_weights(l, ffn1_wg, ffn1_wu, ffn1_wd, ln1_g, ln1_b, w_in, ret_gn_g, w_out, ln2_g, ln2_b,
                           ffn2_wg, ffn2_wu, ffn2_wd, ln3_g, ln3_b)
        ps = _pre_call(hs, w, rot_s[0], rot_s[1], alpha, DH_A ** -0.5, db * ds, f32, None, "pre_sample")
        q16 = jnp.pad(ps["qi"].reshape(db, H_IDX, DH_IDX), ((0, 0), (0, 16 - H_IDX), (0, 0)))
        w16 = jnp.pad((ps["wi"][:, :H_IDX] * H_IDX ** -0.5).reshape(db, H_IDX, 1), ((0, 0), (0, 16 - H_IDX), (0, 0)))
        t8 = lambda a: a.reshape(db, H_A, DH_A).transpose(0, 2, 1)
        r4 = lambda a: a.reshape(db, H_R, DK_R)
        ret_args = (r4(ps["qr"]), r4(ps["kr"]), r4(ps["vr"]), r4(ps["gr"]), w["gng"].reshape(H_R, DV_R), gam)
        sc_s, sn_s, yr_s, st_s = _sscore_call(pt_flat, ckidx_t, q16, w16, ps["ki"].reshape(db, 1, DH_IDX),
                                              *ret_args, state_ret, l, npages)
        mask, snew = _ssel_call(sc_s.reshape(db, past), sn_s.reshape(db, LANES), topk_s)
        mask = mask.reshape(db, npages, 1, PAGE_SIZE)
        snew = snew.reshape(db, 1, LANES)

        pp = _pre_call(hp, w, rot_p[0], rot_p[1], alpha, DH_A ** -0.5 * LOG2E, min(PRE_TM, s), bf16, s,
                       "pre_prompt")
        r3 = lambda a: a.reshape(b, s, a.shape[-1])
        oa = _attn_call(r3(pp["qa"]), r3(pp["qi"]), r3(pp["wi"]), r3(pp["kab"]), r3(pp["vab"]), r3(pp["ki2"]),
                        bias_tiles * LOG2E, topk_p, tq)
        yr, st_p = _ret_call(r3(pp["qr"]), r3(pp["kr"]), r3(pp["vr"]), r3(pp["gr"]), w["gng"], dec_p, chunk)
        steps = (b * s) // STREAM_POST_TM if (b * s) % STREAM_POST_TM == 0 else 0
        sps = steps // db if steps and steps % db == 0 else 0
        step_pages = npages // sps if sps and npages % sps == 0 else 0
        sgrp = min(STREAM_PAGE_GROUP, step_pages) if step_pages else 0
        if sgrp and step_pages % sgrp == 0:
            units = step_pages // sgrp
            parts = max(p for p in (1, 2, STREAM_ROUNDS) if units % p == 0)
            hp, ot_s = _post_stream_call(pp["h"], oa.reshape(b * s, W_A), yr.reshape(b * s, W_R), w, alpha,
                                         STREAM_POST_TM, pt_flat, ck_t, cv_t, t8(ps["qa"]), t8(ps["ka"]),
                                         t8(ps["va"]), bias_pos, bias0, mask, snew, l, npages, sgrp,
                                         units, parts, "post_prompt")
        else:
            hp = _post_call(pp["h"], oa.reshape(b * s, W_A), yr.reshape(b * s, W_R), w, alpha,
                            min(POST_TM, b * s), "post_prompt")
            ot_s = _satt_call(pt_flat, ck_t, cv_t, t8(ps["qa"]), t8(ps["ka"]), t8(ps["va"]), bias_pos,
                              bias0, mask, snew, l, npages, grp)
        outs["kp"].append(pp["ka"].reshape(b, H_A, DH_A, s).transpose(0, 3, 1, 2))
        outs["vp"].append(pp["va"].reshape(b, H_A, DH_A, s).transpose(0, 3, 1, 2))
        outs["kip"].append(pp["ki"].transpose(0, 2, 1))
        outs["sp"].append(st_p)
        oa_s = ot_s.transpose(0, 2, 1).reshape(db, W_A)
        hs = _post_call(ps["h"], oa_s, yr_s.reshape(db, W_R), w, alpha, db * ds, "post_sample")
        outs["ks"].append(ps["ka"].reshape(db, ds, H_A, DH_A))
        outs["vs"].append(ps["va"].reshape(db, ds, H_A, DH_A))
        outs["kis"].append(ps["ki"].reshape(db, ds, DH_IDX))
        outs["ss"].append(st_s)

    stack = lambda k: jnp.stack(outs[k])
    return (hp.reshape(b, s, D_MODEL), hs.reshape(db, ds, D_MODEL),
            stack("kp"), stack("vp"), stack("kip"), stack("sp"),
            stack("ks"), stack("vs"), stack("kis"), stack("ss"))
```

```python
import functools
import math

import numpy as np
import jax
import jax.numpy as jnp
from jax import lax
from jax.experimental import pallas as pl
from jax.experimental.pallas import tpu as pltpu

D_MODEL = 1024
D_FF = 2816
PAGE_SIZE = 128
H_A = 8
DH_A = 64
W_A = H_A * DH_A
H_IDX = 8
DH_IDX = 64
TOPK_MAX = 256
NUM_BUCKETS = 32
MAX_DISTANCE = 128
H_R = 4
DK_R = 128
DV_R = 128
W_R = H_R * DV_R
ROPE_BASE = 10000.0
LN_EPS = 1e-5
GN_EPS = 1e-5
IN_SIZES = (W_A, W_A, W_A, H_IDX * DH_IDX, DH_IDX, H_IDX, H_R * DK_R, H_R * DK_R, W_R, W_R)

LANES = 128
MIB = 1024 * 1024
NEG = -1e30
LOG2E = math.log2(math.e)

FF_CHUNK = 512
PRE_TM = 512
POST_TM = 512
ATT_TQ = 256
RET_CHUNK = 512
BISECT_ITERS = 16
SAMPLE_PAGE_GROUP = 16
STREAM_POST_TM = 256
STREAM_PAGE_GROUP = 8
STREAM_ROUNDS = 2
ATT_FAR_WIDTH = 2

C_QA, C_KA, C_VA, C_QI, C_KI2, C_WI, C_QR, C_KR, C_VR, C_GR, C_END = (
    0, 512, 1024, 1536, 2048, 2176, 2304, 2816, 3328, 3840, 4352)

f32 = jnp.float32
bf16 = jnp.bfloat16
NT_DIMS = (((1,), (1,)), ((), ()))


def _const_spec(shape):
    nd = len(shape)
    return pl.BlockSpec(shape, lambda *_: (0,) * nd, pipeline_mode=pl.Buffered(1))


def _ln(x, g, b):
    mu = jnp.mean(x, axis=-1, keepdims=True)
    xc = x - mu
    var = jnp.mean(xc * xc, axis=-1, keepdims=True)
    return xc * lax.rsqrt(var + LN_EPS) * g + b


def _ffn(xb, wg_ref, wu_ref, wd_ref, hooks=()):
    acc = None
    starts = list(range(0, D_FF, FF_CHUNK))
    at = {-(-(k + 1) * len(starts) // (len(hooks) + 1)): hook for k, hook in enumerate(hooks)}
    assert len(at) == len(hooks) and all(0 < ci < len(starts) for ci in at)
    for ci, c0 in enumerate(starts):
        if ci in at:
            at[ci]()
        c1 = min(c0 + FF_CHUNK, D_FF)
        g = jnp.dot(xb, wg_ref[:, c0:c1], preferred_element_type=f32)
        u = jnp.dot(xb, wu_ref[:, c0:c1], preferred_element_type=f32)
        a = (jax.nn.silu(g) * u).astype(bf16)
        part = jnp.dot(a, wd_ref[c0:c1, :], preferred_element_type=f32)
        acc = part if acc is None else acc + part
    return acc


def _pre_kernel(alpha, q_scale, feature_major, x_ref, wg_ref, wu_ref, wd_ref, lng_ref, lnb_ref, win_ref,
                rc_ref, rs_ref,
                h_ref, qa_ref, ka_ref, va_ref, kab_ref, vab_ref, qi_ref, ki_ref, ki2_ref, wi_ref,
                qr_ref, kr_ref, vr_ref, gr_ref):
    act = qa_ref.dtype
    x = x_ref[...]
    f = _ffn(x.astype(bf16), wg_ref, wu_ref, wd_ref)
    h = _ln(alpha * x + 0.5 * f, lng_ref[...], lnb_ref[...])
    h_ref[...] = h
    hb = h.astype(bf16)

    def proj(c0, c1):
        return jnp.dot(hb, win_ref[:, c0:c1], preferred_element_type=f32)

    qa_ref[...] = (proj(C_QA, C_KA) * q_scale).astype(act)
    ka = proj(C_KA, C_VA)
    kab_ref[...] = ka.astype(act)
    va = proj(C_VA, C_QI)
    qi_ref[...] = (proj(C_QI, C_KI2) * DH_IDX ** -0.5).astype(act)
    kk = proj(C_KI2, C_WI)
    wi_ref[...] = proj(C_WI, C_QR)
    vab_ref[...] = va.astype(act)
    if feature_major:
        ka_ref[...] = ka.T
        va_ref[...] = va.T
        ki_ref[...] = kk.T[:DH_IDX, :]
    else:
        ka_ref[...] = ka
        va_ref[...] = va
        ki_ref[...] = kk[:, :DH_IDX]
    ki2_ref[...] = kk.astype(act)
    qr = proj(C_QR, C_KR)
    kr = proj(C_KR, C_VR)
    c = rc_ref[...]
    s = rs_ref[...]
    for hh in range(H_R):
        sl = slice(DK_R * hh, DK_R * (hh + 1))
        qh = qr[:, sl]
        kh = kr[:, sl]
        qr_ref[:, sl] = (qh * c + pltpu.roll(qh, DK_R // 2, 1) * s).astype(act)
        kr_ref[:, sl] = ((kh * c + pltpu.roll(kh, DK_R // 2, 1) * s) * DK_R ** -0.5).astype(act)
    vr_ref[...] = proj(C_VR, C_GR).astype(act)
    gr_ref[...] = proj(C_GR, C_END)


def _pre_call(x, w, rot_c, rot_s, alpha, q_scale, tm, act, seq_len, name):
    n = x.shape[0]
    grid = (pl.cdiv(n, tm),)
    row = lambda width: pl.BlockSpec((tm, width), lambda i: (i, 0))
    rot_blocks = rot_c.shape[0] // tm
    rot = pl.BlockSpec((tm, LANES), lambda i: (i % rot_blocks, 0))
    feature_major = seq_len is not None
    if feature_major:
        assert seq_len % tm == 0 and n % seq_len == 0
        seq_blocks = seq_len // tm
        kv_shape = lambda width: jax.ShapeDtypeStruct((n // seq_len, width, seq_len), f32)
        kv_spec = lambda width: pl.BlockSpec((None, width, tm), lambda i: (i // seq_blocks, 0, i % seq_blocks))
    else:
        kv_shape = lambda width: jax.ShapeDtypeStruct((n, width), f32)
        kv_spec = row
    in_specs = [
        row(D_MODEL),
        _const_spec((D_MODEL, D_FF)), _const_spec((D_MODEL, D_FF)), _const_spec((D_FF, D_MODEL)),
        _const_spec((1, D_MODEL)), _const_spec((1, D_MODEL)),
        _const_spec((D_MODEL, C_END)),
        rot, rot,
    ]
    outs = [
        ("h", D_MODEL, f32), ("qa", W_A, act), ("ka", W_A, f32), ("va", W_A, f32), ("kab", W_A, act),
        ("vab", W_A, act), ("qi", W_A, act), ("ki", DH_IDX, f32), ("ki2", LANES, act), ("wi", LANES, f32),
        ("qr", W_R, act), ("kr", W_R, act), ("vr", W_R, act), ("gr", W_R, f32),
    ]
    kv_names = ("ka", "va", "ki")
    out_shape = [kv_shape(wd) if k in kv_names else jax.ShapeDtypeStruct((n, wd), dt) for k, wd, dt in outs]
    out_specs = [kv_spec(wd) if k in kv_names else row(wd) for k, wd, _ in outs]
    res = pl.pallas_call(
        functools.partial(_pre_kernel, alpha, q_scale, feature_major),
        grid=grid, in_specs=in_specs, out_specs=out_specs, out_shape=out_shape,
        compiler_params=pltpu.CompilerParams(dimension_semantics=("arbitrary",), vmem_limit_bytes=58 * MIB),
        name=name,
    )(x, w["ffn1_wg"], w["ffn1_wu"], w["ffn1_wd"], w["ln1_g"], w["ln1_b"], w["w_in"], rot_c, rot_s)
    return {k: v for (k, _, _), v in zip(outs, res)}


def _post_kernel(alpha, h_ref, oa_ref, yr_ref, wo_ref, l2g_ref, l2b_ref, wg_ref, wu_ref, wd_ref,
                 l3g_ref, l3b_ref, out_ref):
    h = h_ref[...]
    mix = (jnp.dot(oa_ref[...].astype(bf16), wo_ref[0:W_A, :], preferred_element_type=f32)
           + jnp.dot(yr_ref[...].astype(bf16), wo_ref[W_A:W_A + W_R, :], preferred_element_type=f32))
    h2 = _ln(alpha * h + mix, l2g_ref[...], l2b_ref[...])
    f = _ffn(h2.astype(bf16), wg_ref, wu_ref, wd_ref)
    out_ref[...] = _ln(alpha * h2 + 0.5 * f, l3g_ref[...], l3b_ref[...])


def _post_stream_kernel(alpha, layer, npages, grp, units, parts, steps_per_seq, pt_ref,
                        h_ref, oa_ref, yr_ref, wo_ref, l2g_ref, l2b_ref, wg_ref, wu_ref, wd_ref, l3g_ref, l3b_ref,
                        ckk_ref, ckv_ref, qt_ref, knt_ref, vnt_ref, bias_ref, bias0_ref, mask_ref, snew_ref,
                        out_ref, ot_ref, kbuf, vbuf, sem, m_ref, l_ref, acc_ref):
    stream = _page_stream((layer, npages, grp, units, parts, steps_per_seq), pt_ref, ckk_ref, ckv_ref,
                          qt_ref, bias_ref, mask_ref, kbuf, vbuf, sem, m_ref, l_ref, acc_ref)

    @pl.when(stream.seq_part == 0)
    def _():
        _satt_reset(m_ref, l_ref, acc_ref)

    stream.begin()
    h = h_ref[...]
    mix = (jnp.dot(oa_ref[...].astype(bf16), wo_ref[0:W_A, :], preferred_element_type=f32)
           + jnp.dot(yr_ref[...].astype(bf16), wo_ref[W_A:W_A + W_R, :], preferred_element_type=f32))
    h2 = _ln(alpha * h + mix, l2g_ref[...], l2b_ref[...])
    stream.fold(0)
    f = _ffn(h2.astype(bf16), wg_ref, wu_ref, wd_ref, stream.hooks())
    out_ref[...] = _ln(alpha * h2 + 0.5 * f, l3g_ref[...], l3b_ref[...])

    @pl.when(stream.seq_part == steps_per_seq - 1)
    def _():
        _satt_finish(qt_ref, knt_ref, vnt_ref, bias0_ref, snew_ref, m_ref, l_ref, acc_ref, ot_ref)


class _page_stream:
    def __init__(self, cfg, pt_ref, ckk_ref, ckv_ref, qt_ref, bias_ref, mask_ref, kbuf, vbuf, sem,
                 m_ref, l_ref, acc_ref):
        self.layer, self.npages, self.grp, self.units, self.parts, self.steps_per_seq = cfg
        self.per_round = self.units // self.parts
        self.pt_ref, self.ckk_ref, self.ckv_ref = pt_ref, ckk_ref, ckv_ref
        self.qt_ref, self.bias_ref, self.mask_ref = qt_ref, bias_ref, mask_ref
        self.kbuf, self.vbuf, self.sem = kbuf, vbuf, sem
        self.state = (m_ref, l_ref, acc_ref)
        self.st = pl.program_id(0)
        self.nsteps = pl.num_programs(0)
        self.seq_part = self.st % self.steps_per_seq

    def _round(self, r):
        return range(r * self.per_round, (r + 1) * self.per_round)

    def _buf(self, u):
        return ((u // self.per_round) % 2) * self.per_round + u % self.per_round

    def _copies(self, step, u, p):
        seq, part = step // self.steps_per_seq, step % self.steps_per_seq
        page = self.pt_ref[seq * self.npages + (part * self.units + u) * self.grp + p]
        b = self._buf(u)
        return (pltpu.make_async_copy(self.ckk_ref.at[self.layer, page], self.kbuf.at[b, p], self.sem.at[b, 0]),
                pltpu.make_async_copy(self.ckv_ref.at[self.layer, page], self.vbuf.at[b, p], self.sem.at[b, 1]))

    def _start(self, step, r):
        for u in self._round(r):
            for p in range(self.grp):
                ck, cv = self._copies(step, u, p)
                ck.start(priority=0)
                cv.start(priority=1)

    def _wait(self, r):
        for u in self._round(r):
            for p in range(self.grp):
                ck, cv = self._copies(self.st, u, p)
                ck.wait()
                cv.wait()

    def fold(self, r):
        for u in self._round(r):
            b = self._buf(u)
            _satt_fold(self.kbuf.at[b], self.vbuf.at[b], self.qt_ref, self.bias_ref, self.mask_ref,
                       slice(u * self.grp, (u + 1) * self.grp), *self.state)

    def begin(self):
        if self.parts == 1:
            self._start(self.st, 0)
            self._wait(0)
            return
        assert self.parts % 2 == 0, "rounds alternate between two buffer sets"

        @pl.when(self.st == 0)
        def _():
            self._start(0, 0)
        self._wait(0)
        self._start(self.st, 1)

    def _hook(self, r):
        def run():
            self._wait(r)
            if r + 1 < self.parts:
                self._start(self.st, r + 1)
            else:
                @pl.when(self.st + 1 < self.nsteps)
                def _():
                    self._start(self.st + 1, 0)
            self.fold(r)
        return run

    def hooks(self):
        return tuple(self._hook(r) for r in range(1, self.parts))


def _post_stream_call(h, oa, yr, w, alpha, tm, pt_flat, cache_k_t, cache_v_t, qt, knt, vnt, bias_pos, bias0, mask,
                      snew, layer, npages, grp, units, parts, name):
    n = h.shape[0]
    db = qt.shape[0]
    nsteps = n // tm
    pages = units * grp
    steps_per_seq = npages // pages
    assert n % tm == 0 and npages % pages == 0 and nsteps == db * steps_per_seq and units % parts == 0
    nbuf = units if parts == 1 else 2 * (units // parts)
    row = lambda width: pl.BlockSpec((tm, width), lambda i, pt: (i, 0))
    per_seq = lambda d0, d1: pl.BlockSpec((None, d0, d1), lambda i, pt: (i // steps_per_seq, 0, 0))
    bias_spec = pl.BlockSpec((H_A, pages, 1, PAGE_SIZE), lambda i, pt: (0, i % steps_per_seq, 0, 0))
    mask_spec = pl.BlockSpec((None, pages, 1, PAGE_SIZE),
                             lambda i, pt: (i // steps_per_seq, i % steps_per_seq, 0, 0))
    grid_spec = pltpu.PrefetchScalarGridSpec(
        num_scalar_prefetch=1,
        grid=(nsteps,),
        in_specs=[
            row(D_MODEL), row(W_A), row(W_R),
            _const_spec((W_A + W_R, D_MODEL)), _const_spec((1, D_MODEL)), _const_spec((1, D_MODEL)),
            _const_spec((D_MODEL, D_FF)), _const_spec((D_MODEL, D_FF)), _const_spec((D_FF, D_MODEL)),
            _const_spec((1, D_MODEL)), _const_spec((1, D_MODEL)),
            pl.BlockSpec(memory_space=pl.ANY), pl.BlockSpec(memory_space=pl.ANY),
            per_seq(DH_A, H_A), per_seq(DH_A, H_A), per_seq(DH_A, H_A),
            bias_spec, pl.BlockSpec((1, H_A), lambda i, pt: (0, 0)), mask_spec, per_seq(1, LANES),
        ],
        out_specs=[row(D_MODEL), per_seq(DH_A, H_A)],
        scratch_shapes=[
            pltpu.VMEM((nbuf, grp, H_A, DH_A, PAGE_SIZE), f32),
            pltpu.VMEM((nbuf, grp, H_A, DH_A, PAGE_SIZE), f32),
            pltpu.SemaphoreType.DMA((nbuf, 2)),
            pltpu.VMEM((H_A, 1, PAGE_SIZE), f32),
            pltpu.VMEM((H_A, 1, PAGE_SIZE), f32),
            pltpu.VMEM((H_A, DH_A, PAGE_SIZE), f32),
        ],
    )
    return pl.pallas_call(
        functools.partial(_post_stream_kernel, alpha, layer, npages, grp, units, parts, steps_per_seq),
        grid_spec=grid_spec,
        out_shape=[jax.ShapeDtypeStruct((n, D_MODEL), f32), jax.ShapeDtypeStruct((db, DH_A, H_A), f32)],
        compiler_params=pltpu.CompilerParams(dimension_semantics=("arbitrary",), vmem_limit_bytes=58 * MIB),
        name=name,
    )(pt_flat, h, oa, yr, w["w_out"], w["ln2_g"], w["ln2_b"], w["ffn2_wg"], w["ffn2_wu"], w["ffn2_wd"],
      w["ln3_g"], w["ln3_b"], cache_k_t, cache_v_t, qt, knt, vnt, bias_pos, bias0, mask, snew)


def _post_call(h, oa, yr, w, alpha, tm, name):
    n = h.shape[0]
    row = lambda width: pl.BlockSpec((tm, width), lambda i: (i, 0))
    in_specs = [
        row(D_MODEL), row(W_A), row(W_R),
        _const_spec((W_A + W_R, D_MODEL)), _const_spec((1, D_MODEL)), _const_spec((1, D_MODEL)),
        _const_spec((D_MODEL, D_FF)), _const_spec((D_MODEL, D_FF)), _const_spec((D_FF, D_MODEL)),
        _const_spec((1, D_MODEL)), _const_spec((1, D_MODEL)),
    ]
    return pl.pallas_call(
        functools.partial(_post_kernel, alpha),
        grid=(pl.cdiv(n, tm),), in_specs=in_specs, out_specs=row(D_MODEL),
        out_shape=jax.ShapeDtypeStruct((n, D_MODEL), f32),
        compiler_params=pltpu.CompilerParams(dimension_semantics=("arbitrary",), vmem_limit_bytes=48 * MIB),
        name=name,
    )(h, oa, yr, w["w_out"], w["ln2_g"], w["ln2_b"], w["ffn2_wg"], w["ffn2_wu"], w["ffn2_wd"],
      w["ln3_g"], w["ln3_b"])


def _attn_kernel(topk, qa_ref, qi_ref, wi_ref, k_ref, v_ref, ki2_ref, bias_ref, oa_ref,
                    sc_ref, qap_ref, qip_ref, m_ref, l_ref, acc_ref):
    tq = qa_ref.shape[0]
    ngrp = tq // 8
    i = pl.program_id(1)
    nj = i + 1
    kf = float(topk)

    lane = lax.broadcasted_iota(jnp.int32, (tq, LANES), 1)
    lo_half = lane < DH_A
    for h in range(H_A):
        p, par = h // 2, h % 2
        keep = lo_half if par == 0 else jnp.logical_not(lo_half)
        blk = slice(LANES * p, LANES * (p + 1))
        qip_ref[h] = jnp.where(keep, qi_ref[:, blk], jnp.zeros((), qi_ref.dtype))
        qap_ref[p, par * tq:(par + 1) * tq, :] = jnp.where(keep, qa_ref[:, blk], jnp.zeros((), qa_ref.dtype))
    w_heads = wi_ref[...].T[0:H_IDX, :] * H_IDX ** -0.5

    def score_chunk(j, carry):
        off = pl.multiple_of(j * tq, tq)
        kj = ki2_ref[pl.ds(off, tq), :]
        acc = jnp.zeros((tq, tq), f32)
        for h in range(H_IDX):
            d = lax.dot_general(kj, qip_ref[h], NT_DIMS, preferred_element_type=f32)
            acc = acc + w_heads[h:h + 1, :] * jnp.maximum(d, 0.0)
        sc_ref[j] = acc
        return carry

    lax.fori_loop(0, nj, score_chunk, 0)

    krow = lax.broadcasted_iota(jnp.int32, (tq, tq), 0)
    qcol = lax.broadcasted_iota(jnp.int32, (tq, tq), 1)
    causal = krow <= qcol
    sd = sc_ref[i]
    mn_diag = jnp.min(jnp.where(causal, sd, jnp.inf), axis=0, keepdims=True)
    sc_ref[i] = jnp.where(causal, sd, -jnp.inf)

    def fold(fn, init, comb, n_chunks, t=None):
        t8 = None if t is None else jnp.broadcast_to(t, (8, tq))

        nacc = 4

        def body(j, accs):
            accs = list(accs)
            for r in range(ngrp):
                accs[r % nacc] = comb(accs[r % nacc], fn(sc_ref[j, 8 * r:8 * r + 8, :], t8))
            return tuple(accs)
        accs = lax.fori_loop(0, n_chunks, body, tuple(jnp.full((8, tq), init, f32) for _ in range(nacc)))
        acc = comb(comb(accs[0], accs[1]), comb(accs[2], accs[3]))
        if comb is jnp.add:
            return jnp.sum(acc, axis=0, keepdims=True)
        if comb is jnp.maximum:
            return jnp.max(acc, axis=0, keepdims=True)
        return jnp.min(acc, axis=0, keepdims=True)

    def count_ge(t):
        return fold(lambda s, t8: jnp.where(s >= t8, 1.0, 0.0), 0.0, jnp.add, nj, t)

    def count_gt(t):
        return fold(lambda s, t8: jnp.where(s > t8, 1.0, 0.0), 0.0, jnp.add, nj, t)

    def max_below(t):
        return fold(lambda s, t8: jnp.where(s < t8, s, -jnp.inf), -jnp.inf, jnp.maximum, nj, t)

    n_keys = (lax.broadcasted_iota(jnp.int32, (1, tq), 1) + (i * tq + 1)).astype(f32)
    take_all = n_keys <= kf

    @pl.when((i + 1) * tq <= topk)
    def _():
        def mk(j, carry):
            sc_ref[j] = jnp.zeros((tq, tq), f32)
            return carry
        lax.fori_loop(0, nj, mk, 0)

    @pl.when((i + 1) * tq > topk)
    def _():
        def minmax_body(j, c):
            mxs, mns = list(c[0]), list(c[1])
            for r in range(ngrp):
                v = sc_ref[j, 8 * r:8 * r + 8, :]
                mxs[r % 2] = jnp.maximum(mxs[r % 2], v)
                mns[r % 2] = jnp.minimum(mns[r % 2], v)
            return tuple(mxs), tuple(mns)
        full8 = lambda val: jnp.full((8, tq), val, f32)
        mxs, mns = lax.fori_loop(0, i, minmax_body, ((full8(-jnp.inf),) * 2, (full8(jnp.inf),) * 2))
        mx = jnp.maximum(jnp.max(jnp.maximum(mxs[0], mxs[1]), axis=0, keepdims=True),
                         jnp.max(sc_ref[i], axis=0, keepdims=True))
        mn = jnp.minimum(jnp.min(jnp.minimum(mns[0], mns[1]), axis=0, keepdims=True), mn_diag)
        c_max = count_ge(mx)
        done0 = jnp.logical_or(take_all, c_max >= kf)
        thr0 = jnp.where(take_all, -jnp.inf, mx)
        cge0 = jnp.where(take_all, n_keys, c_max)

        def bis(_, st):
            lo, hi = st
            mid = 0.5 * (lo + hi)
            ge = count_ge(mid) >= kf
            return jnp.where(ge, mid, lo), jnp.where(ge, hi, mid)

        lo, hi = lax.fori_loop(0, BISECT_ITERS, bis, (mn, mx))

        def snap_cond(st):
            return st[0] > 0.0

        def snap_body(st):
            _, hi, thr, cge, done = st
            m = max_below(hi)
            c = count_ge(m)
            ok = c >= kf
            newly = jnp.logical_and(ok, done < 0.5)
            thr = jnp.where(newly, m, thr)
            cge = jnp.where(newly, c, cge)
            hi = jnp.where(jnp.logical_or(done > 0.5, ok), hi, m)
            done = jnp.where(ok, 1.0, done)
            return jnp.sum(1.0 - done), hi, thr, cge, done

        done_f = jnp.where(done0, 1.0, 0.0)
        _, _, thr, cge, _ = lax.while_loop(snap_cond, snap_body, (jnp.sum(1.0 - done_f), hi, thr0, cge0, done_f))

        excess = jnp.sum(jnp.where(cge > kf, 1.0, 0.0)) > 0.0

        @pl.when(jnp.logical_not(excess))
        def _():
            def mk(j, carry):
                sc_ref[j] = jnp.where(sc_ref[j] >= thr, 0.0, NEG)
                return carry
            lax.fori_loop(0, nj, mk, 0)

        @pl.when(excess)
        def _():
            need = kf - count_gt(thr)
            lower = jnp.where(qcol <= krow, 1.0, 0.0).astype(bf16)

            def mk(j, seen):
                s = sc_ref[j]
                tie = s == thr
                rank = seen + jnp.dot(lower, jnp.where(tie, 1.0, 0.0).astype(bf16), preferred_element_type=f32)
                sel = jnp.logical_or(s > thr, jnp.logical_and(tie, rank <= need))
                sc_ref[j] = jnp.where(sel, 0.0, NEG)
                return rank[tq - 1:tq, :]
            lax.fori_loop(0, nj, mk, jnp.zeros((1, tq), f32))

    sc_ref[i] = jnp.where(causal, sc_ref[i], NEG)

    m_ref[...] = jnp.full(m_ref.shape, -jnp.inf, f32)
    l_ref[...] = jnp.zeros(l_ref.shape, f32)
    acc_ref[...] = jnp.zeros(acc_ref.shape, f32)
    ncb = tq // LANES

    def lane_blocks(x, comb):
        out = x[:, 0:LANES]
        for cb in range(1, x.shape[1] // LANES):
            out = comb(out, x[:, cb * LANES:(cb + 1) * LANES])
        return out

    def attend(j, width, bias_cols):
        off = pl.multiple_of(j * tq, tq)
        keys = pl.ds(off, width * tq)
        sel_mask = jnp.concatenate([sc_ref[j + c].T for c in range(width)], axis=1)
        for p in range(H_A // 2):
            blk = slice(LANES * p, LANES * (p + 1))
            s_pair = lax.dot_general(qap_ref[p], k_ref[keys, blk], NT_DIMS, preferred_element_type=f32)
            es = []
            for par in range(2):
                h = 2 * p + par
                s = s_pair[par * tq:(par + 1) * tq] + sel_mask
                if bias_cols is not None:
                    s = s + bias_ref[h, :, bias_cols]
                m_old = m_ref[h]
                row_max = jnp.max(lane_blocks(s, jnp.maximum), axis=1, keepdims=True)
                m_new = jnp.maximum(m_old, jnp.broadcast_to(row_max, (tq, LANES)))
                a = jnp.exp2(m_old - m_new)
                e = jnp.exp2(s - jnp.concatenate([m_new] * (width * ncb), axis=1))
                l_ref[h] = l_ref[h] * a + lane_blocks(e, jnp.add)
                acc_ref[h] = acc_ref[h] * a
                m_ref[h] = m_new
                es.append(e.astype(bf16))
            pv = jnp.dot(jnp.concatenate(es, axis=0), v_ref[keys, blk], preferred_element_type=f32)
            acc_ref[2 * p] += pv[0:tq]
            acc_ref[2 * p + 1] += pv[tq:2 * tq]

    n_far = jnp.maximum(i - 1, 0)
    if ATT_FAR_WIDTH == 2:
        def far_pair(jj, carry):
            attend(2 * jj, 2, None)
            return carry
        lax.fori_loop(0, n_far // 2, far_pair, 0)

        @pl.when(n_far % 2 == 1)
        def _():
            attend(n_far - 1, 1, None)
    else:
        def far_one(j, carry):
            attend(j, 1, None)
            return carry
        lax.fori_loop(0, n_far, far_one, 0)

    @pl.when(i >= 1)
    def _():
        attend(i - 1, 2, slice(0, 2 * tq))

    @pl.when(i == 0)
    def _():
        attend(0, 1, slice(tq, 2 * tq))

    for p in range(H_A // 2):
        l_even = jnp.broadcast_to(jnp.sum(l_ref[2 * p], axis=1, keepdims=True), (tq, LANES))
        l_odd = jnp.broadcast_to(jnp.sum(l_ref[2 * p + 1], axis=1, keepdims=True), (tq, LANES))
        o_pair = jnp.where(lo_half, acc_ref[2 * p] / l_even, acc_ref[2 * p + 1] / l_odd)
        oa_ref[:, LANES * p:LANES * (p + 1)] = o_pair.astype(oa_ref.dtype)


def _attn_call(qa, qi, wi, kab, vab, ki2, bias_tiles, topk, tq):
    b, s, _ = qa.shape
    nq = s // tq
    qspec = lambda width: pl.BlockSpec((None, tq, width), lambda bi, i: (bi, i, 0))
    whole = lambda shape: pl.BlockSpec((None,) + shape, lambda bi, i: (bi,) + (0,) * len(shape),
                                       pipeline_mode=pl.Buffered(1))
    scratch = [
        pltpu.VMEM((nq, tq, tq), f32),
        pltpu.VMEM((H_A // 2, 2 * tq, LANES), qa.dtype),
        pltpu.VMEM((H_IDX, tq, LANES), qi.dtype),
        pltpu.VMEM((H_A, tq, LANES), f32),
        pltpu.VMEM((H_A, tq, LANES), f32),
        pltpu.VMEM((H_A, tq, LANES), f32),
    ]
    return pl.pallas_call(
        functools.partial(_attn_kernel, topk),
        grid=(b, nq),
        in_specs=[qspec(W_A), qspec(W_A),
                  qspec(LANES),
                  whole((s, W_A)), whole((s, W_A)), whole((s, LANES)),
                  _const_spec((H_A, tq, 2 * tq))],
        out_specs=qspec(W_A),
        out_shape=jax.ShapeDtypeStruct((b, s, W_A), bf16),
        scratch_shapes=scratch,
        compiler_params=pltpu.CompilerParams(dimension_semantics=("arbitrary", "arbitrary"),
                                             vmem_limit_bytes=48 * MIB),
        name="attn_prompt",
    )(qa, qi, wi, kab, vab, ki2, bias_tiles)


def _ret_kernel(qr_ref, kr_ref, vr_ref, gr_ref, gng_ref, dmat_ref, cross_ref, kvd_ref, cdec_ref,
                yr_ref, st_ref):
    c = pl.program_id(1)

    @pl.when(c == 0)
    def _():
        st_ref[...] = jnp.zeros(st_ref.shape, f32)

    for h in range(H_R):
        sl = slice(DK_R * h, DK_R * (h + 1))
        q = qr_ref[:, sl]
        k = kr_ref[:, sl]
        v = vr_ref[:, sl]
        att = lax.dot_general(q, k, NT_DIMS, preferred_element_type=f32) * dmat_ref[h]
        st = st_ref[h]
        o = (jnp.dot(att.astype(bf16), v, preferred_element_type=f32)
             + jnp.dot(q, st.astype(bf16), preferred_element_type=f32) * cross_ref[h])
        kd = (k.astype(f32) * kvd_ref[h]).T.astype(bf16)
        st_ref[h] = cdec_ref[h, 0:1, :] * st + jnp.dot(kd, v, preferred_element_type=f32)
        mu = jnp.mean(o, axis=-1, keepdims=True)
        oc = o - mu
        var = jnp.mean(oc * oc, axis=-1, keepdims=True)
        yn = oc * lax.rsqrt(var + GN_EPS) * gng_ref[:, sl]
        yr_ref[:, sl] = (jax.nn.silu(gr_ref[:, sl]) * yn).astype(yr_ref.dtype)


def _ret_call(qr, kr, vr, gr, gng, dec, chunk):
    b, s, _ = qr.shape
    nc = s // chunk
    rspec = pl.BlockSpec((None, chunk, W_R), lambda bi, c: (bi, c, 0))
    return pl.pallas_call(
        _ret_kernel,
        grid=(b, nc),
        in_specs=[rspec, rspec, rspec, rspec, _const_spec((1, W_R)),
                  _const_spec((H_R, chunk, chunk)), _const_spec((H_R, chunk, LANES)),
                  _const_spec((H_R, chunk, LANES)), _const_spec((H_R, 8, LANES))],
        out_specs=[rspec, pl.BlockSpec((None, H_R, DK_R, DV_R), lambda bi, c: (bi, 0, 0, 0))],
        out_shape=[jax.ShapeDtypeStruct((b, s, W_R), bf16), jax.ShapeDtypeStruct((b, H_R, DK_R, DV_R), f32)],
        compiler_params=pltpu.CompilerParams(dimension_semantics=("arbitrary", "arbitrary"),
                                             vmem_limit_bytes=32 * MIB),
        name="ret_prompt",
    )(qr, kr, vr, gr, gng, dec["intra"], dec["cross"], dec["kv"], dec["chunk"])


def _sscore_kernel(layer, npages, pt_ref, ck_ref, q_ref, w_ref, kn_ref,
                   qr_ref, kr_ref, vr_ref, gr_ref, gng_ref, gam_ref, s0_ref,
                   sc_ref, snew_ref, yr_ref, sn_ref, kbuf, sem):
    b = pl.program_id(0)
    nb = pl.num_programs(0)
    slot = b % 2

    def page_copy(seq, p, sl):
        page = pt_ref[seq * npages + p]
        return pltpu.make_async_copy(ck_ref.at[layer, page], kbuf.at[sl, p], sem.at[sl])

    def start_all(seq, sl):
        def body(p, carry):
            page_copy(seq, p, sl).start()
            return carry
        lax.fori_loop(0, npages, body, 0)

    @pl.when(b == 0)
    def _():
        start_all(0, 0)

    @pl.when(b + 1 < nb)
    def _():
        start_all(b + 1, 1 - slot)

    _sample_retention_step(qr_ref, kr_ref, vr_ref, gr_ref, gng_ref, gam_ref, s0_ref, yr_ref, sn_ref)

    def wait_body(p, carry):
        page_copy(b, p, slot).wait()
        return carry
    lax.fori_loop(0, npages, wait_body, 0)

    q = q_ref[...].astype(bf16)
    w = w_ref[...]

    pages_per_dot = min(8, npages)
    for p0 in range(0, npages, pages_per_dot):
        kw = jnp.concatenate([kbuf[slot, p] for p in range(p0, p0 + pages_per_dot)], axis=1).astype(bf16)
        d = jnp.dot(q, kw, preferred_element_type=f32)
        sc_ref[:, p0 * PAGE_SIZE:(p0 + pages_per_dot) * PAGE_SIZE] = jnp.sum(
            w * jnp.maximum(d, 0.0), axis=0, keepdims=True)

    kn = kn_ref[...].astype(bf16).astype(f32)
    dn = jnp.sum(q.astype(f32) * kn, axis=1, keepdims=True)
    s_new = jnp.sum(w * jnp.maximum(dn, 0.0), axis=0, keepdims=True)
    snew_ref[...] = jnp.broadcast_to(s_new, (1, LANES))


def _sscore_call(page_table_flat, cache_kidx_t, q16, w16, kn, qr, kr, vr, gr, gng, gam, state, layer, npages):
    db = q16.shape[0]
    assert npages % min(8, npages) == 0
    per_seq = lambda d0, d1: pl.BlockSpec((None, d0, d1), lambda b, pt: (b, 0, 0))
    shared = pl.BlockSpec((H_R, DV_R), lambda b, pt: (0, 0))
    grid_spec = pltpu.PrefetchScalarGridSpec(
        num_scalar_prefetch=1,
        grid=(db,),
        in_specs=[
            pl.BlockSpec(memory_space=pl.ANY),
            per_seq(16, DH_IDX), per_seq(16, 1), per_seq(1, DH_IDX),
            per_seq(H_R, DK_R), per_seq(H_R, DK_R), per_seq(H_R, DV_R), per_seq(H_R, DV_R), shared, shared,
            pl.BlockSpec((None, None, H_R, DK_R, DV_R), lambda b, pt: (layer, b, 0, 0, 0)),
        ],
        out_specs=[per_seq(1, npages * PAGE_SIZE), per_seq(1, LANES), per_seq(H_R, DV_R),
                   pl.BlockSpec((None, H_R, DK_R, DV_R), lambda b, pt: (b, 0, 0, 0))],
        scratch_shapes=[
            pltpu.VMEM((2, npages, DH_IDX, PAGE_SIZE), f32),
            pltpu.SemaphoreType.DMA((2,)),
        ],
    )
    return pl.pallas_call(
        functools.partial(_sscore_kernel, layer, npages),
        grid_spec=grid_spec,
        out_shape=[jax.ShapeDtypeStruct((db, 1, npages * PAGE_SIZE), f32),
                   jax.ShapeDtypeStruct((db, 1, LANES), f32),
                   jax.ShapeDtypeStruct((db, H_R, DV_R), f32), jax.ShapeDtypeStruct((db, H_R, DK_R, DV_R), f32)],
        compiler_params=pltpu.CompilerParams(dimension_semantics=("arbitrary",), vmem_limit_bytes=32 * MIB),
        name="sample_scores",
    )(page_table_flat, cache_kidx_t, q16, w16, kn, qr, kr, vr, gr, gng, gam, state)


PREFIX_CHUNK = 256


def _ssel_kernel(topk, sc_ref, sn_ref, mask_ref, selnew_ref):
    kf = float(topk)
    db, length = sc_ref.shape
    sc = sc_ref[...]
    s_new = sn_ref[:, 0:1]

    def rsum(x):
        return jnp.sum(x, axis=1, keepdims=True)

    def count(cmp, t):
        return rsum(jnp.where(cmp(sc, t), 1.0, 0.0)) + jnp.where(cmp(s_new, t), 1.0, 0.0)

    ge = lambda a, t: a >= t
    gt = lambda a, t: a > t
    mx = jnp.maximum(jnp.max(sc, axis=1, keepdims=True), s_new)
    mn = jnp.minimum(jnp.min(sc, axis=1, keepdims=True), s_new)
    c_max = count(ge, mx)
    done0 = jnp.where(c_max >= kf, 1.0, 0.0)

    def bis(_, st):
        lo, hi = st
        mid = 0.5 * (lo + hi)
        ok = count(ge, mid) >= kf
        return jnp.where(ok, mid, lo), jnp.where(ok, hi, mid)
    lo, hi = lax.fori_loop(0, BISECT_ITERS, bis, (mn, mx))

    def snap_cond(st):
        return st[0] > 0.0

    def snap_body(st):
        _, hi, thr, done = st
        below = jnp.maximum(jnp.max(jnp.where(sc < hi, sc, -jnp.inf), axis=1, keepdims=True),
                            jnp.where(s_new < hi, s_new, -jnp.inf))
        ok = count(ge, below) >= kf
        newly = jnp.logical_and(ok, done < 0.5)
        thr = jnp.where(newly, below, thr)
        hi = jnp.where(jnp.logical_or(done > 0.5, ok), hi, below)
        done = jnp.where(ok, 1.0, done)
        return jnp.sum(1.0 - done), hi, thr, done
    _, _, thr, _ = lax.while_loop(snap_cond, snap_body, (jnp.sum(1.0 - done0), hi, mx, done0))

    need = kf - count(gt, thr)
    pc = min(PREFIX_CHUNK, length)
    r_i = lax.broadcasted_iota(jnp.int32, (pc, pc), 0)
    c_i = lax.broadcasted_iota(jnp.int32, (pc, pc), 1)
    upper = jnp.where(r_i <= c_i, 1.0, 0.0).astype(bf16)
    seen = jnp.zeros((db, 1), f32)
    for c0 in range(0, length, pc):
        s_c = sc[:, c0:c0 + pc]
        tie = s_c == thr
        rank = seen + jnp.dot(jnp.where(tie, 1.0, 0.0).astype(bf16), upper, preferred_element_type=f32)
        sel = jnp.logical_or(s_c > thr, jnp.logical_and(tie, rank <= need))
        mask_ref[:, c0:c0 + pc] = jnp.where(sel, 0.0, NEG)
        seen = rank[:, pc - 1:pc]
    sel_new = jnp.logical_or(s_new > thr, jnp.logical_and(s_new == thr, seen + 1.0 <= need))
    selnew_ref[...] = jnp.broadcast_to(jnp.where(sel_new, 1.0, 0.0), (db, LANES))


def _ssel_call(scores, s_new, topk):
    db, length = scores.shape
    assert length % min(PREFIX_CHUNK, length) == 0
    return pl.pallas_call(
        functools.partial(_ssel_kernel, topk),
        out_shape=[jax.ShapeDtypeStruct((db, length), f32), jax.ShapeDtypeStruct((db, LANES), f32)],
        compiler_params=pltpu.CompilerParams(vmem_limit_bytes=32 * MIB),
        name="sample_select",
    )(scores, s_new)


def _satt_kernel(layer, npages, grp, pt_ref,
                 ckk_ref, ckv_ref, qt_ref, knt_ref, vnt_ref, bias_ref, bias0_ref, mask_ref, snew_ref,
                 ot_ref, kbuf, vbuf, sem, m_ref, l_ref, acc_ref):
    b = pl.program_id(0)
    g = pl.program_id(1)
    nb = pl.num_programs(0)
    ng = pl.num_programs(1)
    t = b * ng + g
    slot = t % 2

    def page_copies(seq, gi, p, sl):
        page = pt_ref[seq * npages + gi * grp + p]
        return (pltpu.make_async_copy(ckk_ref.at[layer, page], kbuf.at[sl, p], sem.at[sl, 0]),
                pltpu.make_async_copy(ckv_ref.at[layer, page], vbuf.at[sl, p], sem.at[sl, 1]))

    def start_all(seq, gi, sl):
        def body(p, carry):
            ck, cv = page_copies(seq, gi, p, sl)
            ck.start()
            cv.start()
            return carry
        lax.fori_loop(0, grp, body, 0)

    @pl.when(t == 0)
    def _():
        start_all(0, 0, 0)

    @pl.when(t + 1 < nb * ng)
    def _():
        wrap = g + 1 == ng
        start_all(jnp.where(wrap, b + 1, b), jnp.where(wrap, 0, g + 1), 1 - slot)

    @pl.when(g == 0)
    def _():
        m_ref[...] = jnp.full(m_ref.shape, -jnp.inf, f32)
        l_ref[...] = jnp.zeros(l_ref.shape, f32)
        acc_ref[...] = jnp.zeros(acc_ref.shape, f32)

    def wait_body(p, carry):
        ck, cv = page_copies(b, g, p, slot)
        ck.wait()
        cv.wait()
        return carry
    lax.fori_loop(0, grp, wait_body, 0)

    _satt_fold(kbuf.at[slot], vbuf.at[slot], qt_ref, bias_ref, mask_ref, slice(0, grp), m_ref, l_ref, acc_ref)

    @pl.when(g == ng - 1)
    def _():
        _satt_finish(qt_ref, knt_ref, vnt_ref, bias0_ref, snew_ref, m_ref, l_ref, acc_ref, ot_ref)


def _satt_fold(kb_ref, vb_ref, qt_ref, bias_ref, mask_ref, rows, m_ref, l_ref, acc_ref):
    mask = mask_ref[rows]
    for h in range(H_A):
        qc = qt_ref[:, h:h + 1]
        kh = kb_ref[:, h]
        lg = jnp.sum(kh * qc[None], axis=1, keepdims=True) + bias_ref[h, rows] + mask
        m_old = m_ref[h]
        m_blk = jnp.max(jnp.max(lg, axis=0), axis=1, keepdims=True)
        m_new = jnp.maximum(m_old, m_blk)
        a = jnp.exp(m_old - m_new)
        e = jnp.exp(lg - m_new[None])
        l_ref[h] = l_ref[h] * a + jnp.sum(e, axis=0)
        acc_ref[h] = acc_ref[h] * a + jnp.sum(vb_ref[:, h] * e, axis=0)
        m_ref[h] = m_new


def _satt_finish(qt_ref, knt_ref, vnt_ref, bias0_ref, snew_ref, m_ref, l_ref, acc_ref, ot_ref):
    lg_new = jnp.sum(qt_ref[...] * knt_ref[...], axis=0, keepdims=True) + bias0_ref[...]
    lg_new = jnp.where(snew_ref[0:1, 0:1] > 0.5, lg_new, NEG)
    for h in range(H_A):
        m_c = m_ref[h][:, 0:1]
        lg_h = lg_new[:, h:h + 1]
        m_f = jnp.maximum(m_c, lg_h)
        a = jnp.exp(m_c - m_f)
        e_new = jnp.exp(lg_h - m_f)
        den = jnp.sum(l_ref[h], axis=1, keepdims=True) * a + e_new
        num = jnp.sum(acc_ref[h], axis=1, keepdims=True) * a + e_new * vnt_ref[:, h:h + 1]
        ot_ref[:, h:h + 1] = num / den


def _satt_reset(m_ref, l_ref, acc_ref):
    m_ref[...] = jnp.full(m_ref.shape, -jnp.inf, f32)
    l_ref[...] = jnp.zeros(l_ref.shape, f32)
    acc_ref[...] = jnp.zeros(acc_ref.shape, f32)


def _sample_retention_step(qr_ref, kr_ref, vr_ref, gr_ref, gng_ref, gam_ref, s0_ref, yr_ref, sn_ref):
    r_i = lax.broadcasted_iota(jnp.int32, (DK_R, DK_R), 0)
    c_i = lax.broadcasted_iota(jnp.int32, (DK_R, DK_R), 1)
    eye = jnp.where(r_i == c_i, 1.0, 0.0)
    for h in range(H_R):
        qrow = qr_ref[h:h + 1, :]
        krow = kr_ref[h:h + 1, :]
        vrow = vr_ref[h:h + 1, :]
        gam = gam_ref[h:h + 1, :]
        qcol = jnp.sum(eye * qrow, axis=1, keepdims=True)
        kcol = jnp.sum(eye * krow, axis=1, keepdims=True)
        st = s0_ref[h]
        qk = jnp.sum(qrow * krow, axis=1, keepdims=True)
        o = qk * vrow + gam * jnp.sum(qcol * st, axis=0, keepdims=True)
        sn_ref[h] = gam * st + kcol * vrow
        mu = jnp.mean(o, axis=1, keepdims=True)
        oc = o - mu
        var = jnp.mean(oc * oc, axis=1, keepdims=True)
        yn = oc * lax.rsqrt(var + GN_EPS) * gng_ref[h:h + 1, :]
        yr_ref[h:h + 1, :] = jax.nn.silu(gr_ref[h:h + 1, :]) * yn


def _satt_call(pt_flat, cache_k_t, cache_v_t, qt, knt, vnt, bias_pos, bias0, mask, snew, layer, npages, grp):
    db = qt.shape[0]
    ng = npages // grp
    per_seq = lambda d0, d1: pl.BlockSpec((None, d0, d1), lambda b, g, pt: (b, 0, 0))
    grid_spec = pltpu.PrefetchScalarGridSpec(
        num_scalar_prefetch=1,
        grid=(db, ng),
        in_specs=[
            pl.BlockSpec(memory_space=pl.ANY), pl.BlockSpec(memory_space=pl.ANY),
            per_seq(DH_A, H_A), per_seq(DH_A, H_A), per_seq(DH_A, H_A),
            pl.BlockSpec((H_A, grp, 1, PAGE_SIZE), lambda b, g, pt: (0, g, 0, 0)),
            pl.BlockSpec((1, H_A), lambda b, g, pt: (0, 0)),
            pl.BlockSpec((None, grp, 1, PAGE_SIZE), lambda b, g, pt: (b, g, 0, 0)),
            per_seq(1, LANES),
        ],
        out_specs=per_seq(DH_A, H_A),
        scratch_shapes=[
            pltpu.VMEM((2, grp, H_A, DH_A, PAGE_SIZE), f32),
            pltpu.VMEM((2, grp, H_A, DH_A, PAGE_SIZE), f32),
            pltpu.SemaphoreType.DMA((2, 2)),
            pltpu.VMEM((H_A, 1, PAGE_SIZE), f32),
            pltpu.VMEM((H_A, 1, PAGE_SIZE), f32),
            pltpu.VMEM((H_A, DH_A, PAGE_SIZE), f32),
        ],
    )
    return pl.pallas_call(
        functools.partial(_satt_kernel, layer, npages, grp),
        grid_spec=grid_spec,
        out_shape=jax.ShapeDtypeStruct((db, DH_A, H_A), f32),
        compiler_params=pltpu.CompilerParams(dimension_semantics=("arbitrary", "arbitrary"),
                                             vmem_limit_bytes=40 * MIB),
        name="sample_attend",
    )(pt_flat, cache_k_t, cache_v_t, qt, knt, vnt, bias_pos, bias0, mask, snew)


def _t5_bucket(rel):
    n = jnp.maximum(rel, 0)
    max_exact = NUM_BUCKETS // 2
    nf = jnp.maximum(n, 1).astype(f32)
    large = max_exact + (jnp.log(nf / max_exact) / math.log(MAX_DISTANCE / max_exact)
                         * (NUM_BUCKETS - max_exact)).astype(jnp.int32)
    large = jnp.minimum(large, NUM_BUCKETS - 1)
    return jnp.where(n < max_exact, n, large)


def _rotary_tables(pos):
    half = DK_R // 2
    freqs = ROPE_BASE ** (-jnp.arange(half, dtype=f32) / half)
    ang = pos.astype(f32)[:, None] * freqs[None, :]
    cos, sin = jnp.cos(ang), jnp.sin(ang)
    return jnp.concatenate([cos, cos], axis=1), jnp.concatenate([-sin, sin], axis=1)


def _decay_tables(chunk):
    lg = jnp.log1p(-jnp.exp2(-5.0 - jnp.arange(H_R, dtype=f32)))
    i = jnp.arange(chunk, dtype=f32)
    diff = i[:, None] - i[None, :]
    causal = diff >= 0
    intra = jnp.where(causal[None], jnp.exp(jnp.where(causal, diff, 0.0)[None] * lg[:, None, None]), 0.0)
    cross = jnp.exp((i[None, :] + 1.0) * lg[:, None])
    kv = jnp.exp((chunk - 1.0 - i)[None, :] * lg[:, None])
    cdec = jnp.exp(chunk * lg)
    bc = lambda a: jnp.broadcast_to(a[:, :, None], (H_R, chunk, LANES))
    return {"intra": intra, "cross": bc(cross), "kv": bc(kv),
            "chunk": jnp.broadcast_to(cdec[:, None, None], (H_R, 8, LANES))}


def _bias_tiles(rel_bias, tq):
    assert _last_bucket_from(tq + 1), "keys beyond the previous chunk must share the last bucket"
    rb = rel_bias.astype(f32) - rel_bias[NUM_BUCKETS - 1].astype(f32)[None, :]
    tab = rb[_t5_bucket(jnp.arange(2 * tq + 1, dtype=jnp.int32))].T
    m = 3 * tq
    g = jnp.concatenate([tab[:, tq::-1],
                         jnp.broadcast_to(tab[:, 0:1], (H_A, tq - 1)),
                         tab[:, 2 * tq:tq:-1]], axis=1)
    flat = jnp.tile(g, (1, tq))[:, :tq * (m - 1)]
    return flat.reshape(H_A, tq, m - 1)[:, :, :2 * tq]


def _last_bucket_from(n):
    large = NUM_BUCKETS // 2 + int(math.log(n / (NUM_BUCKETS // 2)) / math.log(MAX_DISTANCE / (NUM_BUCKETS // 2))
                                   * (NUM_BUCKETS - NUM_BUCKETS // 2) - 1e-3)
    return n >= NUM_BUCKETS // 2 and large >= NUM_BUCKETS - 1


def _layer_weights(l, ffn1_wg, ffn1_wu, ffn1_wd, ln1_g, ln1_b, w_in, ret_gn_g, w_out, ln2_g, ln2_b,
                   ffn2_wg, ffn2_wu, ffn2_wd, ln3_g, ln3_b):
    pts = np.cumsum((0,) + IN_SIZES)
    wi = w_in[l]
    col = lambda k: wi[:, pts[k]:pts[k + 1]]
    zeros = jnp.zeros((D_MODEL, LANES - H_IDX), wi.dtype)
    w_in2 = jnp.concatenate([col(0), col(1), col(2), col(3), col(4), col(4), col(5), zeros,
                             col(6), col(7), col(8), col(9)], axis=1).astype(bf16)
    r2 = lambda a: a[l].reshape(1, -1).astype(f32)
    return {
        "ffn1_wg": ffn1_wg[l].astype(bf16), "ffn1_wu": ffn1_wu[l].astype(bf16), "ffn1_wd": ffn1_wd[l].astype(bf16),
        "ln1_g": r2(ln1_g), "ln1_b": r2(ln1_b), "w_in": w_in2, "gng": r2(ret_gn_g),
        "w_out": w_out[l].astype(bf16), "ln2_g": r2(ln2_g), "ln2_b": r2(ln2_b),
        "ffn2_wg": ffn2_wg[l].astype(bf16), "ffn2_wu": ffn2_wu[l].astype(bf16), "ffn2_wd": ffn2_wd[l].astype(bf16),
        "ln3_g": r2(ln3_g), "ln3_b": r2(ln3_b),
    }


def kernel(x_prompt, x_sample, cache_k, cache_v, cache_kidx, state_ret, page_table, rel_bias,
           ffn1_wg, ffn1_wu, ffn1_wd, ln1_g, ln1_b, w_in, ret_gn_g, w_out,
           ln2_g, ln2_b, ffn2_wg, ffn2_wu, ffn2_wd, ln3_g, ln3_b):
    b, s, _ = x_prompt.shape
    db, ds, _ = x_sample.shape
    depth = w_in.shape[0]
    npages = page_table.shape[1]
    past = npages * PAGE_SIZE
    assert ds == 1, "the sample group decodes one token per sequence"
    alpha = (2 * depth) ** 0.25

    tq = min(ATT_TQ, s)
    chunk = min(RET_CHUNK, s)
    assert s % tq == 0 and s % chunk == 0 and tq % LANES == 0
    topk_p = min(TOPK_MAX, s // 4)
    topk_s = min(TOPK_MAX, (past + ds) // 4)

    rot_p = _rotary_tables(jnp.arange(s, dtype=jnp.int32))
    rot_s = _rotary_tables(jnp.full((db * ds,), past, jnp.int32))
    dec_p = _decay_tables(chunk)
    gam = jnp.broadcast_to(_decay_tables(1)["chunk"][:, 0, :], (H_R, LANES))
    bias_tiles = _bias_tiles(rel_bias, tq)
    near = min(past, 2 * PAGE_SIZE)
    assert _last_bucket_from(near + 1)
    bias_near = rel_bias.astype(f32)[_t5_bucket(jnp.arange(near, 0, -1, dtype=jnp.int32))].T
    bias_far = jnp.broadcast_to(rel_bias[NUM_BUCKETS - 1].astype(f32)[:, None], (H_A, past - near))
    bias_pos = jnp.concatenate([bias_far, bias_near], axis=1).reshape(H_A, npages, 1, PAGE_SIZE)
    bias0 = rel_bias[0:1].astype(f32)
    pt_flat = page_table.reshape(-1).astype(jnp.int32)
    grp = min(SAMPLE_PAGE_GROUP, npages)
    assert npages % grp == 0
    ckidx_t = jnp.transpose(cache_kidx, (0, 1, 3, 2))
    ck_t = jnp.transpose(cache_k, (0, 1, 3, 4, 2))
    cv_t = jnp.transpose(cache_v, (0, 1, 3, 4, 2))

    hp = x_prompt.reshape(b * s, D_MODEL)
    hs = x_sample.reshape(db * ds, D_MODEL)
    outs = {k: [] for k in ("kp", "vp", "kip", "sp", "ks", "vs", "kis", "ss")}
    for l in range(depth):
        w = _layer_weights(l, ffn1_wg, ffn1_wu, ffn1_wd, ln1_g, ln1_b, w_in, ret_gn_g, w_out, ln2_g, ln2_b,
                           ffn2_wg, ffn2_wu, ffn2_wd, ln3_g, ln3_b)
        ps = _pre_call(hs, w, rot_s[0], rot_s[1], alpha, DH_A ** -0.5, db * ds, f32, None, "pre_sample")
        q16 = jnp.pad(ps["qi"].reshape(db, H_IDX, DH_IDX), ((0, 0), (0, 16 - H_IDX), (0, 0)))
        w16 = jnp.pad((ps["wi"][:, :H_IDX] * H_IDX ** -0.5).reshape(db, H_IDX, 1), ((0, 0), (0, 16 - H_IDX), (0, 0)))
        t8 = lambda a: a.reshape(db, H_A, DH_A).transpose(0, 2, 1)
        r4 = lambda a: a.reshape(db, H_R, DK_R)
        ret_args = (r4(ps["qr"]), r4(ps["kr"]), r4(ps["vr"]), r4(ps["gr"]), w["gng"].reshape(H_R, DV_R), gam)
        sc_s, sn_s, yr_s, st_s = _sscore_call(pt_flat, ckidx_t, q16, w16, ps["ki"].reshape(db, 1, DH_IDX),
                                              *ret_args, state_ret, l, npages)
        mask, snew = _ssel_call(sc_s.reshape(db, past), sn_s.reshape(db, LANES), topk_s)
        mask = mask.reshape(db, npages, 1, PAGE_SIZE)
        snew = snew.reshape(db, 1, LANES)

        pp = _pre_call(hp, w, rot_p[0], rot_p[1], alpha, DH_A ** -0.5 * LOG2E, min(PRE_TM, s), bf16, s,
                       "pre_prompt")
        r3 = lambda a: a.reshape(b, s, a.shape[-1])
        oa = _attn_call(r3(pp["qa"]), r3(pp["qi"]), r3(pp["wi"]), r3(pp["kab"]), r3(pp["vab"]), r3(pp["ki2"]),
                        bias_tiles * LOG2E, topk_p, tq)
        yr, st_p = _ret_call(r3(pp["qr"]), r3(pp["kr"]), r3(pp["vr"]), r3(pp["gr"]), w["gng"], dec_p, chunk)
        steps = (b * s) // STREAM_POST_TM if (b * s) % STREAM_POST_TM == 0 else 0
        sps = steps // db if steps and steps % db == 0 else 0
        step_pages = npages // sps if sps and npages % sps == 0 else 0
        sgrp = min(STREAM_PAGE_GROUP, step_pages) if step_pages else 0
        if sgrp and step_pages % sgrp == 0:
            units = step_pages // sgrp
            parts = max(p for p in (1, 2, STREAM_ROUNDS) if units % p == 0)
            hp, ot_s = _post_stream_call(pp["h"], oa.reshape(b * s, W_A), yr.reshape(b * s, W_R), w, alpha,
                                         STREAM_POST_TM, pt_flat, ck_t, cv_t, t8(ps["qa"]), t8(ps["ka"]),
                                         t8(ps["va"]), bias_pos, bias0, mask, snew, l, npages, sgrp,
                                         units, parts, "post_prompt")
        else:
            hp = _post_call(pp["h"], oa.reshape(b * s, W_A), yr.reshape(b * s, W_R), w, alpha,
                            min(POST_TM, b * s), "post_prompt")
            ot_s = _satt_call(pt_flat, ck_t, cv_t, t8(ps["qa"]), t8(ps["ka"]), t8(ps["va"]), bias_pos,
                              bias0, mask, snew, l, npages, grp)
        outs["kp"].append(pp["ka"].reshape(b, H_A, DH_A, s).transpose(0, 3, 1, 2))
        outs["vp"].append(pp["va"].reshape(b, H_A, DH_A, s).transpose(0, 3, 1, 2))
        outs["kip"].append(pp["ki"].transpose(0, 2, 1))
        outs["sp"].append(st_p)
        oa_s = ot_s.transpose(0, 2, 1).reshape(db, W_A)
        hs = _post_call(ps["h"], oa_s, yr_s.reshape(db, W_R), w, alpha, db * ds, "post_sample")
        outs["ks"].append(ps["ka"].reshape(db, ds, H_A, DH_A))
        outs["vs"].append(ps["va"].reshape(db, ds, H_A, DH_A))
        outs["kis"].append(ps["ki"].reshape(db, ds, DH_IDX))
        outs["ss"].append(st_s)

    stack = lambda k: jnp.stack(outs[k])
    return (hp.reshape(b, s, D_MODEL), hs.reshape(db, ds, D_MODEL),
            stack("kp"), stack("vp"), stack("kip"), stack("sp"),
            stack("ks"), stack("vs"), stack("kis"), stack("ss"))
```
